```python
import math
import jax, jax.numpy as jnp
from jax import lax
import numpy as np

D_MODEL = 1024
BATCH = 4
SEQ = 4096
DEPTH = 1

GRID_W = 64
CTX_LEN = 256
ATTN_HEADS = 4
ATTN_QK_DIM = 64
ATTN_V_DIM = 2 * ATTN_QK_DIM
ATTN_WIDTH = ATTN_HEADS * ATTN_V_DIM
QKV_WIDTH = 3 * ATTN_WIDTH
AXIS_DIM = ATTN_QK_DIM // 2
ROPE_THETA = 10000.0
Q_BLOCK = 128
POOL_WIDTH = D_MODEL - ATTN_WIDTH
POOL_WINDOWS = (2, 4, 8, 16)
N_POOL_GROUPS = len(POOL_WINDOWS)
POOL_GROUP = POOL_WIDTH // N_POOL_GROUPS
IN_WIDTH = QKV_WIDTH + POOL_WIDTH
N_EXPERTS = 64
TOP_K = 8
N_GROUPS = 8
TOPK_GROUPS = 4
EXPERT_DIM = 256
SHARED_DIM = 256
ROUTED_SCALE = 2.5
EXPERT_BLOCK = 256
EPS = 1e-6

kernel_name = "hybrid_diffattn_pool_moe_dit_layer"


def rmsnorm(x, g):
    xf = x.astype(jnp.float32)
    y = xf * lax.rsqrt(jnp.mean(xf * xf, axis=-1, keepdims=True) + EPS)
    return (y * g.astype(jnp.float32)).astype(x.dtype)


def adaln(cvec, w, b):
    m = jax.nn.silu(cvec) @ w + b
    return jnp.split(m, 6, axis=-1)


def modulate(h, shift, scale):
    return h * (1.0 + scale) + shift


def axial_rope_tables(rows):
    row = jnp.broadcast_to(jnp.arange(rows)[:, None], (rows, GRID_W)).reshape(-1).astype(jnp.float32)
    col = jnp.broadcast_to(jnp.arange(GRID_W)[None, :], (rows, GRID_W)).reshape(-1).astype(jnp.float32)
    freqs = ROPE_THETA ** (-jnp.arange(0, AXIS_DIM, 2, dtype=jnp.float32) / AXIS_DIM)
    ar = row[:, None] * freqs
    ac = col[:, None] * freqs
    return (jnp.cos(ar), jnp.sin(ar), jnp.cos(ac), jnp.sin(ac))


def rope_axis(x, cos, sin):
    x1, x2 = jnp.split(x, 2, axis=-1)
    cos = cos[None, :, None, :]
    sin = sin[None, :, None, :]
    return jnp.concatenate([x1 * cos - x2 * sin, x2 * cos + x1 * sin], axis=-1)


def rope2d(x, tabs):
    cr, sr, cc, sc = tabs
    out = jnp.concatenate([rope_axis(x[..., :AXIS_DIM], cr, sr), rope_axis(x[..., AXIS_DIM:], cc, sc)], axis=-1)
    return out.astype(x.dtype)


def split_q(pq):
    b, n, _ = pq.shape
    q = pq.reshape(b, n, ATTN_HEADS, 2, ATTN_QK_DIM)
    return q[..., 0, :], q[..., 1, :]


def split_kv(pkv):
    b, n, _ = pkv.shape
    k = pkv[..., :ATTN_WIDTH].reshape(b, n, ATTN_HEADS, 2, ATTN_QK_DIM)
    v = pkv[..., ATTN_WIDTH:].reshape(b, n, ATTN_HEADS, ATTN_V_DIM)
    return k[..., 0, :], k[..., 1, :], v


def bhnd(x):
    return jnp.swapaxes(x, 1, 2)


def diff_lambda(lq1, lk1, lq2, lk2, lam_init):
    f = jnp.float32
    return (jnp.exp(jnp.sum(lq1.astype(f) * lk1.astype(f))) - jnp.exp(jnp.sum(lq2.astype(f) * lk2.astype(f)))
            + lam_init)


def diff_attn(q1, q2, k1, k2, v, lam):
    scale = ATTN_QK_DIM ** -0.5
    s1 = jnp.einsum('bhqd,bhkd->bhqk', q1, k1, preferred_element_type=jnp.float32) * scale
    s2 = jnp.einsum('bhqd,bhkd->bhqk', q2, k2, preferred_element_type=jnp.float32) * scale
    a = jax.nn.softmax(s1, axis=-1) - lam * jax.nn.softmax(s2, axis=-1)
    return jnp.einsum('bhqk,bhkd->bhqd', a.astype(v.dtype), v)


def blocked_diff_attn(q1, q2, k1, k2, v, lam):
    b, h, n, d = q1.shape
    nb = n // Q_BLOCK
    to_blocks = lambda q: jnp.moveaxis(q.reshape(b, h, nb, Q_BLOCK, d), 2, 0)
    o = lax.map(lambda qq: diff_attn(qq[0], qq[1], k1, k2, v, lam), (to_blocks(q1), to_blocks(q2)))
    return jnp.moveaxis(o, 0, 2).reshape(b, h, n, ATTN_V_DIM)


def attn_heads_out(o, g_subln, lam_init):
    o = rmsnorm(o, g_subln) * (1.0 - lam_init)
    b, h, n, dv = o.shape
    return bhnd(o).reshape(b, n, h * dv)


def pool_mix(p, w_pool, b_pool, pool_scale):
    b, n, _ = p.shape
    pf = p.astype(jnp.float32)
    cs = jnp.concatenate([jnp.zeros((b, 1, POOL_WIDTH), jnp.float32), jnp.cumsum(pf, axis=1)], axis=1)
    t = jnp.arange(n)
    outs = []
    for gi, w in enumerate(POOL_WINDOWS):
        lo = jnp.clip(t - w // 2, 0, n)
        hi = jnp.clip(t + (w - w // 2), 0, n)
        sl = slice(gi * POOL_GROUP, (gi + 1) * POOL_GROUP)
        csg = cs[..., sl]
        mean = (csg[:, hi] - csg[:, lo]) / (hi - lo).astype(jnp.float32)[None, :, None]
        outs.append(mean - pf[..., sl])
    d = jnp.stack(outs, axis=2)
    y = jnp.einsum('bngc,gce->bnge', d, w_pool.astype(jnp.float32)).reshape(b, n, POOL_WIDTH) + b_pool
    return (y * pool_scale).astype(p.dtype)


def swiglu(t, wg, wu, wd):
    return (jax.nn.silu(t @ wg) * (t @ wu)) @ wd


def moe_ffn(h, w_router, router_bias, w_e_gate, w_e_up, w_e_down, w_sh_gate, w_sh_up, w_sh_down):
    b, n, d = h.shape
    t = h.reshape(-1, d)
    T = t.shape[0]
    scores = jax.nn.sigmoid((t @ w_router).astype(jnp.float32))
    biased = scores + router_bias.astype(jnp.float32)
    grp = biased.reshape(T, N_GROUPS, N_EXPERTS // N_GROUPS)
    grp_score = lax.top_k(grp, 2)[0].sum(-1)
    _, top_groups = lax.top_k(grp_score, TOPK_GROUPS)
    group_mask = jax.nn.one_hot(top_groups, N_GROUPS, dtype=jnp.float32).sum(1) > 0
    expert_mask = jnp.repeat(group_mask, N_EXPERTS // N_GROUPS, axis=1)
    _, top_e = lax.top_k(jnp.where(expert_mask, biased, -jnp.inf), TOP_K)
    wts = jnp.take_along_axis(scores, top_e, axis=1)
    wts = wts / jnp.sum(wts, axis=-1, keepdims=True) * ROUTED_SCALE
    A = T * TOP_K
    flat_e = top_e.reshape(-1)
    flat_tok = jnp.repeat(jnp.arange(T, dtype=jnp.int32), TOP_K)
    flat_w = wts.reshape(-1)
    order = jnp.argsort(flat_e)
    sorted_e = flat_e[order]
    counts = jnp.bincount(flat_e, length=N_EXPERTS)
    starts = jnp.cumsum(counts) - counts
    padded = ((counts + EXPERT_BLOCK - 1) // EXPERT_BLOCK) * EXPERT_BLOCK
    pends = jnp.cumsum(padded)
    pstarts = pends - padded
    dest = pstarts[sorted_e] + (jnp.arange(A) - starts[sorted_e])
    P = A + N_EXPERTS * EXPERT_BLOCK
    nblk = P // EXPERT_BLOCK
    slot_tok = jnp.full((P,), T, jnp.int32).at[dest].set(flat_tok[order])
    slot_w = jnp.zeros((P,), jnp.float32).at[dest].set(flat_w[order])
    block_e = jnp.minimum(jnp.searchsorted(pends, jnp.arange(nblk) * EXPERT_BLOCK, side='right'), N_EXPERTS - 1)
    t_pad = jnp.concatenate([t, jnp.zeros((1, d), t.dtype)], axis=0)

    def expert_block(args):
        tok, wb, e = args
        xb = t_pad[tok]
        yb = swiglu(xb, w_e_gate[e], w_e_up[e], w_e_down[e])
        return yb * wb[:, None]

    yb = lax.map(expert_block, (slot_tok.reshape(nblk, EXPERT_BLOCK), slot_w.reshape(nblk, EXPERT_BLOCK), block_e))
    routed = jax.ops.segment_sum(yb.reshape(P, d), slot_tok, num_segments=T + 1)[:T]
    shared = swiglu(t, w_sh_gate, w_sh_up, w_sh_down)
    return (routed + shared).astype(h.dtype).reshape(b, n, d)


def setup_inputs(seed: int = 0) -> dict:
    key = jax.random.key(seed)
    ks = jax.random.split(key, 32)
    f32 = jnp.float32
    L, D, E, F, FS = DEPTH, D_MODEL, N_EXPERTS, EXPERT_DIM, SHARED_DIM

    def nrm(k, shape, scale=1.0):
        return jax.random.normal(k, shape, f32) * scale

    def gain(k, shape, s=0.05):
        return 1.0 + s * jax.random.normal(k, shape, f32)

    return {
        "x": nrm(ks[0], (BATCH, SEQ, D)),
        "c": nrm(ks[1], (BATCH, D)),
        "ctx": nrm(ks[2], (BATCH, CTX_LEN, D)),
        "c_ctx": nrm(ks[3], (D,)),
        "w_ada": nrm(ks[4], (L, D, 6 * D), D ** -0.5),
        "b_ada": nrm(ks[5], (L, 6 * D), 0.01),
        "g_pre_mix": gain(ks[6], (L, D)),
        "g_post_mix": gain(ks[7], (L, D)),
        "g_pre_ffn": gain(ks[8], (L, D)),
        "g_post_ffn": gain(ks[9], (L, D)),
        "w_in": nrm(ks[10], (L, D, IN_WIDTH), D ** -0.5),
        "lambda_q1": nrm(ks[11], (L, ATTN_QK_DIM), 0.1),
        "lambda_k1": nrm(ks[12], (L, ATTN_QK_DIM), 0.1),
        "lambda_q2": nrm(ks[13], (L, ATTN_QK_DIM), 0.1),
        "lambda_k2": nrm(ks[14], (L, ATTN_QK_DIM), 0.1),
        "g_subln": gain(ks[15], (L, ATTN_V_DIM)),
        "w_pool": nrm(ks[16], (L, N_POOL_GROUPS, POOL_GROUP, POOL_GROUP), POOL_GROUP ** -0.5),
        "b_pool": nrm(ks[17], (L, POOL_WIDTH), 0.01),
        "pool_scale": gain(ks[18], (L, POOL_WIDTH), 0.1),
        "w_out": nrm(ks[19], (L, D, D), D ** -0.5),
        "w_router": nrm(ks[20], (L, D, E), D ** -0.5),
        "router_bias": nrm(ks[21], (L, E), 0.01),
        "w_e_gate": nrm(ks[22], (L, E, D, F), D ** -0.5),
        "w_e_up": nrm(ks[23], (L, E, D, F), D ** -0.5),
        "w_e_down": nrm(ks[24], (L, E, F, D), F ** -0.5),
        "w_sh_gate": nrm(ks[25], (L, D, FS), D ** -0.5),
        "w_sh_up": nrm(ks[26], (L, D, FS), D ** -0.5),
        "w_sh_down": nrm(ks[27], (L, FS, D), FS ** -0.5),
    }


def reference(x, c, ctx, c_ctx, w_ada, b_ada, g_pre_mix, g_post_mix, g_pre_ffn, g_post_ffn, w_in,
              lambda_q1, lambda_k1, lambda_q2, lambda_k2, g_subln, w_pool, b_pool, pool_scale, w_out,
              w_router, router_bias, w_e_gate, w_e_up, w_e_down, w_sh_gate, w_sh_up, w_sh_down):
    n = x.shape[1]
    rows = n // GRID_W
    tabs = axial_rope_tables(rows)
    ex = lambda m: m[:, None, :]
    for l in range(DEPTH):
        last = l == DEPTH - 1
        lam_init = 0.8 - 0.6 * math.exp(-0.3 * l)
        sh_a, sc_a, g_a, sh_f, sc_f, g_f = adaln(c, w_ada[l], b_ada[l])
        csh_a, csc_a, cg_a, csh_f, csc_f, cg_f = adaln(c_ctx, w_ada[l], b_ada[l])
        lam = diff_lambda(lambda_q1[l], lambda_k1[l], lambda_q2[l], lambda_k2[l], lam_init)

        hx = modulate(rmsnorm(x, g_pre_mix[l]), ex(sh_a), ex(sc_a))
        hc = modulate(rmsnorm(ctx, g_pre_mix[l]), csh_a, csc_a)
        px = hx @ w_in[l]
        q1x, q2x = split_q(px[..., :ATTN_WIDTH])
        k1x, k2x, vx = split_kv(px[..., ATTN_WIDTH:QKV_WIDTH])
        q1x, q2x, k1x, k2x = (rope2d(t, tabs) for t in (q1x, q2x, k1x, k2x))
        k1c, k2c, vc = split_kv(hc @ w_in[l][:, ATTN_WIDTH:QKV_WIDTH])
        k1_all = jnp.concatenate([bhnd(k1c), bhnd(k1x)], axis=2)
        k2_all = jnp.concatenate([bhnd(k2c), bhnd(k2x)], axis=2)
        v_all = jnp.concatenate([bhnd(vc), bhnd(vx)], axis=2)
        att_x = blocked_diff_attn(bhnd(q1x), bhnd(q2x), k1_all, k2_all, v_all, lam)
        att_x = attn_heads_out(att_x, g_subln[l], lam_init)
        pool_x = pool_mix(px[..., QKV_WIDTH:], w_pool[l], b_pool[l], pool_scale[l])
        yx = jnp.concatenate([att_x, pool_x], axis=-1) @ w_out[l]
        x = x + ex(g_a) * rmsnorm(yx, g_post_mix[l])
        if not last:
            q1c, q2c = split_q(hc @ w_in[l][:, :ATTN_WIDTH])
            att_c = diff_attn(bhnd(q1c), bhnd(q2c), bhnd(k1c), bhnd(k2c), bhnd(vc), lam)
            att_c = attn_heads_out(att_c, g_subln[l], lam_init)
            pool_c = pool_mix(hc @ w_in[l][:, QKV_WIDTH:], w_pool[l], b_pool[l], pool_scale[l])
            yc = jnp.concatenate([att_c, pool_c], axis=-1) @ w_out[l]
            ctx = ctx + cg_a * rmsnorm(yc, g_post_mix[l])

        moe_w = (w_router[l], router_bias[l], w_e_gate[l], w_e_up[l], w_e_down[l],
                 w_sh_gate[l], w_sh_up[l], w_sh_down[l])
        hx = modulate(rmsnorm(x, g_pre_ffn[l]), ex(sh_f), ex(sc_f))
        x = x + ex(g_f) * rmsnorm(moe_ffn(hx, *moe_w), g_post_ffn[l])
        if not last:
            hc = modulate(rmsnorm(ctx, g_pre_ffn[l]), csh_f, csc_f)
            ctx = ctx + cg_f * rmsnorm(moe_ffn(hc, *moe_w), g_post_ffn[l])
    return x
```

```python
import functools
import math

import jax
import jax.numpy as jnp
import numpy as np
from jax import lax
from jax.experimental import pallas as pl
from jax.experimental.pallas import tpu as pltpu

F32 = jnp.float32
BF16 = jnp.bfloat16

D_MODEL = 1024
GRID_W = 64
N_HEADS = 4
QK_DIM = 64
V_DIM = 128
ATTN_WIDTH = N_HEADS * V_DIM
POOL_WIDTH = D_MODEL - ATTN_WIDTH
POOL_WINDOWS = (2, 4, 8, 16)
POOL_GROUP = POOL_WIDTH // len(POOL_WINDOWS)
AXIS_DIM = QK_DIM // 2
ROPE_THETA = 10000.0
N_EXPERTS = 64
TOP_K = 8
N_GROUPS = 8
GROUP_SIZE = N_EXPERTS // N_GROUPS
TOPK_GROUPS = 4
EXPERT_DIM = 256
ROUTED_SCALE = 2.5
EPS = 1e-6
LAM_INIT = 0.8 - 0.6 * math.exp(-0.3 * 0)

LANES = 128
SLOT_TILE = 256
HALO = 16
VMEM_LIMIT = 56 * 1024 * 1024


def _cparams(sem, vmem=VMEM_LIMIT):
    return pltpu.CompilerParams(dimension_semantics=sem, vmem_limit_bytes=vmem)


def _split(a):
    hi = a.astype(BF16)
    lo = (a - hi.astype(F32)).astype(BF16)
    return hi, lo


def _dot(a, b):
    return jnp.dot(a, b, preferred_element_type=F32)


def _dot_nt(a, b):
    return lax.dot_general(a, b, (((1,), (1,)), ((), ())), preferred_element_type=F32)


def _dot3(a, b, nt=False):
    d = _dot_nt if nt else _dot
    ah, al = _split(a)
    bh, bl = _split(b)
    return d(ah, bh) + d(ah, bl) + d(al, bh)


def _rms(x, g):
    return x * lax.rsqrt(jnp.mean(x * x, axis=-1, keepdims=True) + EPS) * g


def _silu(x):
    return x * (1.0 / (1.0 + jnp.exp(-x)))


def _ada_kernel(c_ref, w_ref, b_ref, o_ref):
    o_ref[...] = _dot3(_silu(c_ref[...]), w_ref[...]) + b_ref[...]


def _ada(cin, w_ada, b_ada):
    rows, d = cin.shape
    n = w_ada.shape[1]
    tn = 1024
    return pl.pallas_call(
        _ada_kernel,
        grid=(n // tn,),
        in_specs=[pl.BlockSpec((rows, d), lambda j: (0, 0)),
                  pl.BlockSpec((d, tn), lambda j: (0, j)),
                  pl.BlockSpec((1, tn), lambda j: (0, j))],
        out_specs=pl.BlockSpec((rows, tn), lambda j: (0, j)),
        out_shape=jax.ShapeDtypeStruct((rows, n), F32),
        compiler_params=_cparams(("arbitrary",)),
        name="ada",
    )(cin, w_ada, b_ada.reshape(1, n))


def _rope(t, cos, sa, sb):
    return t * cos + pltpu.roll(t, LANES - AXIS_DIM // 2, 1) * sa + pltpu.roll(t, AXIS_DIM // 2, 1) * sb


def _inproj_kernel(x_ref, mod_ref, g_ref, w_ref, cq_ref, ck_ref, sa_ref, sb_ref,
                   q_ref, k_ref, v_ref, p_ref):
    x = x_ref[0]
    h = _rms(x, g_ref[...]) * (1.0 + mod_ref[0, 1:2, :]) + mod_ref[0, 0:1, :]
    px = _dot(h.astype(BF16), w_ref[...])
    ck = ck_ref[...]
    sa = sa_ref[...]
    sb = sb_ref[...]
    cq = cq_ref[...]
    scale = QK_DIM ** -0.5
    saq = sa * scale
    sbq = sb * scale
    for hh in range(N_HEADS):
        lo = hh * LANES
        q = px[:, lo:lo + LANES]
        k = px[:, ATTN_WIDTH + lo:ATTN_WIDTH + lo + LANES]
        q_ref[0, :, lo:lo + LANES] = _rope(q, cq, saq, sbq).astype(BF16)
        k_ref[0, :, lo:lo + LANES] = _rope(k, ck, sa, sb).astype(BF16)
    v_ref[0] = px[:, 2 * ATTN_WIDTH:3 * ATTN_WIDTH].astype(BF16)
    p_ref[0] = px[:, 3 * ATTN_WIDTH:].astype(BF16)


def _inproj(x, mod3, g, w_in_bf, cq, ck, sa, sb, tm=512):
    b, n, d = x.shape
    width = w_in_bf.shape[1]
    row = lambda bi, i: (bi, i, 0)
    tab = pl.BlockSpec((tm, LANES), lambda bi, i: (i, 0))
    out = jax.ShapeDtypeStruct((b, n, ATTN_WIDTH), BF16)
    return pl.pallas_call(
        _inproj_kernel,
        grid=(b, n // tm),
        in_specs=[pl.BlockSpec((1, tm, d), row),
                  pl.BlockSpec((1, 6, d), lambda bi, i: (bi, 0, 0)),
                  pl.BlockSpec((1, d), lambda bi, i: (0, 0)),
                  pl.BlockSpec((d, width), lambda bi, i: (0, 0)),
                  tab, tab, tab, tab],
        out_specs=[pl.BlockSpec((1, tm, ATTN_WIDTH), row)] * 4,
        out_shape=[out] * 4,
        compiler_params=_cparams(("arbitrary", "arbitrary")),
        name="inproj",
    )(x, mod3, g, w_in_bf, cq, ck, sa, sb)


def _ctxkv_kernel(x_ref, mod_ref, g_ref, wk_ref, wv_ref, k_ref, v_ref):
    h = _rms(x_ref[0], g_ref[...]) * (1.0 + mod_ref[0, 1:2, :]) + mod_ref[0, 0:1, :]
    hb = h.astype(BF16)
    k_ref[0] = _dot(hb, wk_ref[...]).astype(BF16)
    v_ref[0] = _dot(hb, wv_ref[...]).astype(BF16)


def _ctxkv(ctx, mod3, g, w_in_bf, ctx_row):
    b, n, d = ctx.shape
    out = jax.ShapeDtypeStruct((b, n, ATTN_WIDTH), BF16)
    return pl.pallas_call(
        _ctxkv_kernel,
        grid=(b,),
        in_specs=[pl.BlockSpec((1, n, d), lambda bi: (bi, 0, 0)),
                  pl.BlockSpec((1, 6, d), lambda bi: (ctx_row, 0, 0)),
                  pl.BlockSpec((1, d), lambda bi: (0, 0)),
                  pl.BlockSpec((d, ATTN_WIDTH), lambda bi: (0, 1)),
                  pl.BlockSpec((d, ATTN_WIDTH), lambda bi: (0, 2))],
        out_specs=[pl.BlockSpec((1, n, ATTN_WIDTH), lambda bi: (bi, 0, 0))] * 2,
        out_shape=[out] * 2,
        compiler_params=_cparams(("arbitrary",)),
        name="ctxkv",
    )(ctx, mod3, g, w_in_bf, w_in_bf)


def _attn_kernel(q_ref, kx_ref, vx_ref, kc_ref, vc_ref, lam_ref, g_ref, o_ref):
    q = q_ref[0]
    lane = lax.broadcasted_iota(jnp.int32, q.shape, 1)
    zero = jnp.zeros_like(q)
    q1 = jnp.where(lane < QK_DIM, q, zero)
    q2 = jnp.where(lane >= QK_DIM, q, zero)
    kx = kx_ref[0]
    kc = kc_ref[0]
    lv = lam_ref[...]
    lam = (jnp.exp(jnp.sum(lv[0:1] * lv[1:2], axis=-1, keepdims=True))
           - jnp.exp(jnp.sum(lv[2:3] * lv[3:4], axis=-1, keepdims=True)) + LAM_INIT)

    def softmax_parts(qm):
        sc = _dot_nt(qm, kc)
        sx = _dot_nt(qm, kx)
        m = jnp.maximum(jnp.max(sc, axis=-1, keepdims=True), jnp.max(sx, axis=-1, keepdims=True))
        pc = jnp.exp(sc - m)
        px = jnp.exp(sx - m)
        l = jnp.sum(pc, axis=-1, keepdims=True) + jnp.sum(px, axis=-1, keepdims=True)
        return pc, px, l

    p1c, p1x, l1 = softmax_parts(q1)
    p2c, p2x, l2 = softmax_parts(q2)
    r1 = 1.0 / l1
    r2 = lam / l2
    ac = (p1c * r1 - p2c * r2).astype(BF16)
    ax = (p1x * r1 - p2x * r2).astype(BF16)
    o = _dot(ac, vc_ref[0]) + _dot(ax, vx_ref[0])
    o_ref[0] = (_rms(o, g_ref[...]) * (1.0 - LAM_INIT)).astype(BF16)


def _attn(q, k, v, kc, vc, lamv, g_subln, tq=256):
    b, n, _ = q.shape
    nc = kc.shape[1]
    return pl.pallas_call(
        _attn_kernel,
        grid=(b, N_HEADS, n // tq),
        in_specs=[pl.BlockSpec((1, tq, LANES), lambda bi, h, i: (bi, i, h)),
                  pl.BlockSpec((1, n, LANES), lambda bi, h, i: (bi, 0, h)),
                  pl.BlockSpec((1, n, LANES), lambda bi, h, i: (bi, 0, h)),
                  pl.BlockSpec((1, nc, LANES), lambda bi, h, i: (bi, 0, h)),
                  pl.BlockSpec((1, nc, LANES), lambda bi, h, i: (bi, 0, h)),
                  pl.BlockSpec((4, QK_DIM), lambda bi, h, i: (0, 0)),
                  pl.BlockSpec((1, V_DIM), lambda bi, h, i: (0, 0))],
        out_specs=pl.BlockSpec((1, tq, LANES), lambda bi, h, i: (bi, i, h)),
        out_shape=jax.ShapeDtypeStruct((b, n, ATTN_WIDTH), BF16),
        compiler_params=_cparams(("arbitrary", "arbitrary", "arbitrary")),
        name="attn",
    )(q, k, v, kc, vc, lamv, g_subln)


def _outproj_kernel(att_ref, p_ref, pprev_ref, pnext_ref, x_ref, mod_ref, wpool_ref, bpool_ref,
                    pscale_ref, wout_ref, gpost_ref, gpre_ref, wr_ref,
                    x1_ref, h2_ref, lg_ref, *, tm, n):
    i = pl.program_id(1)
    p = p_ref[0].astype(F32)
    prev = jnp.where(i > 0, pprev_ref[0].astype(F32), 0.0)
    nxt = jnp.where(i < pl.num_programs(1) - 1, pnext_ref[0].astype(F32), 0.0)
    ext = jnp.concatenate([prev, p, nxt], axis=0)
    rows = tm + 2 * HALO
    trow = (i * tm + lax.broadcasted_iota(jnp.int32, (tm, 1), 0))

    pooled = []
    sums = {}
    acc = ext + pltpu.roll(ext, 1, 0)
    sums[2] = acc
    w = 2
    while w < POOL_WINDOWS[-1]:
        acc = pltpu.roll(acc, rows - w // 2, 0) + pltpu.roll(acc, w // 2, 0)
        w = 2 * w
        sums[w] = acc
    for gi, win in enumerate(POOL_WINDOWS):
        sl = slice(gi * POOL_GROUP, (gi + 1) * POOL_GROUP)
        wsum = sums[win][HALO:HALO + tm, sl]
        hi_c = jnp.minimum(trow + (win - win // 2), n)
        lo_c = jnp.maximum(trow - win // 2, 0)
        cnt = (hi_c - lo_c).astype(F32)
        d = wsum / cnt - p[:, sl]
        y = _dot(d.astype(BF16), wpool_ref[gi]) + bpool_ref[:, sl]
        pooled.append((y * pscale_ref[:, sl]).astype(BF16))
    pool = jnp.concatenate(pooled, axis=1)
    yx = _dot(att_ref[0], wout_ref[0:ATTN_WIDTH, :]) + _dot(pool, wout_ref[ATTN_WIDTH:, :])
    x1 = x_ref[0] + mod_ref[0, 2:3, :] * _rms(yx, gpost_ref[...])
    x1_ref[0] = x1
    h2 = _rms(x1, gpre_ref[...]) * (1.0 + mod_ref[0, 4:5, :]) + mod_ref[0, 3:4, :]
    h2_ref[0] = h2
    lg_ref[...] = _dot3(wr_ref[...], h2, nt=True)


def _outproj(att, p, x, mod3, w_pool_bf, b_pool, pool_scale, w_out_bf, g_post, g_pre, w_router_t, tm=256):
    b, n, d = x.shape
    hb = tm // HALO
    nhb = n // HALO
    row = lambda bi, i: (bi, i, 0)
    vec = lambda bi, i: (0, 0)
    kern = functools.partial(_outproj_kernel, tm=tm, n=n)
    return pl.pallas_call(
        kern,
        grid=(b, n // tm),
        in_specs=[pl.BlockSpec((1, tm, ATTN_WIDTH), row),
                  pl.BlockSpec((1, tm, POOL_WIDTH), row),
                  pl.BlockSpec((1, HALO, POOL_WIDTH), lambda bi, i: (bi, jnp.maximum(i * hb - 1, 0), 0)),
                  pl.BlockSpec((1, HALO, POOL_WIDTH), lambda bi, i: (bi, jnp.minimum((i + 1) * hb, nhb - 1), 0)),
                  pl.BlockSpec((1, tm, d), row),
                  pl.BlockSpec((1, 6, d), lambda bi, i: (bi, 0, 0)),
                  pl.BlockSpec((len(POOL_WINDOWS), POOL_GROUP, POOL_GROUP), lambda bi, i: (0, 0, 0)),
                  pl.BlockSpec((1, POOL_WIDTH), vec),
                  pl.BlockSpec((1, POOL_WIDTH), vec),
                  pl.BlockSpec((d, d), vec),
                  pl.BlockSpec((1, d), vec),
                  pl.BlockSpec((1, d), vec),
                  pl.BlockSpec((N_EXPERTS, d), vec)],
        out_specs=[pl.BlockSpec((1, tm, d), row),
                   pl.BlockSpec((1, tm, d), row),
                   pl.BlockSpec((N_EXPERTS, tm), lambda bi, i: (0, bi * (n // tm) + i))],
        out_shape=[jax.ShapeDtypeStruct((b, n, d), F32),
                   jax.ShapeDtypeStruct((b, n, d), F32),
                   jax.ShapeDtypeStruct((N_EXPERTS, b * n), F32)],
        compiler_params=_cparams(("arbitrary", "arbitrary")),
        name="outproj",
    )(att, p, p, p, x, mod3, w_pool_bf, b_pool, pool_scale, w_out_bf, g_post, g_pre, w_router_t)


def _beats(a, b, a_first):
    return jnp.where((a >= b) if a_first else (a > b), 1.0, 0.0)


def _route_kernel(lg_ref, bias_ref, w_ref):
    scores = [1.0 / (1.0 + jnp.exp(-lg_ref[e])) for e in range(N_EXPERTS)]
    biased = [scores[e] + bias_ref[e] for e in range(N_EXPERTS)]
    gscore = []
    for g in range(N_GROUPS):
        vals = biased[g * GROUP_SIZE:(g + 1) * GROUP_SIZE]
        m1 = vals[0]
        m2 = jnp.full_like(m1, -jnp.inf)
        for v in vals[1:]:
            m2 = jnp.maximum(m2, jnp.minimum(m1, v))
            m1 = jnp.maximum(m1, v)
        gscore.append(m1 + m2)
    gsel = []
    for g in range(N_GROUPS):
        rank = jnp.zeros_like(gscore[g])
        for g2 in range(N_GROUPS):
            if g2 != g:
                rank = rank + _beats(gscore[g2], gscore[g], g2 < g)
        gsel.append(rank < TOPK_GROUPS)
    masked = [jnp.where(gsel[e // GROUP_SIZE], biased[e], -jnp.inf) for e in range(N_EXPERTS)]
    picked = []
    for e in range(N_EXPERTS):
        rank = jnp.zeros_like(masked[e])
        for e2 in range(N_EXPERTS):
            if e2 != e:
                rank = rank + _beats(masked[e2], masked[e], e2 < e)
        picked.append(jnp.where(rank < TOP_K, scores[e], 0.0))
    denom = picked[0]
    for e in range(1, N_EXPERTS):
        denom = denom + picked[e]
    for e in range(N_EXPERTS):
        w_ref[e] = picked[e] / denom * ROUTED_SCALE


def _route(logits3, bias3, rows=8):
    e, r, l = logits3.shape
    return pl.pallas_call(
        _route_kernel,
        grid=(r // rows,),
        in_specs=[pl.BlockSpec((e, rows, l), lambda i: (0, i, 0)),
                  pl.BlockSpec((e, 1, l), lambda i: (0, 0, 0))],
        out_specs=pl.BlockSpec((e, rows, l), lambda i: (0, i, 0)),
        out_shape=jax.ShapeDtypeStruct((e, r, l), F32),
        compiler_params=_cparams(("arbitrary",)),
        name="route",
    )(logits3, bias3)


def _experts_kernel(be_ref, nt_ref, tok_ref, sw_ref, x_hbm, wg_ref, wu_ref, wd_ref, o_hbm,
                    xs_ref, acc_ref, xg_ref, ys_ref, sem, *, ntile, nrows):
    c = pl.program_id(0)
    j = pl.program_id(1)
    trash = xs_ref.shape[0] - nrows

    @pl.when(j == 0)
    def _():
        cp = pltpu.make_async_copy(x_hbm.at[c], xs_ref.at[pl.ds(0, nrows)], sem.at[0])
        cp.start()
        xs_ref[pl.ds(nrows, trash), :] = jnp.zeros((trash, xs_ref.shape[1]), F32)
        acc_ref[...] = jnp.zeros_like(acc_ref)
        cp.wait()

    @pl.when(j < nt_ref[c])
    def _():
        def gather(gi, carry):
            for r in range(8):
                s = gi * 8 + r
                t = tok_ref[0, 0, s]
                xg_ref[pl.ds(s, 1), :] = xs_ref[pl.ds(t, 1), :]
            return carry

        lax.fori_loop(0, SLOT_TILE // 8, gather, 0)
        xb = xg_ref[...].astype(BF16)
        g = _dot(xb, wg_ref[0])
        u = _dot(xb, wu_ref[0])
        h = (_silu(g) * u).astype(BF16)
        ys_ref[...] = _dot(h, wd_ref[0]) * sw_ref[0]

        def scatter(gi, carry):
            for r in range(8):
                s = gi * 8 + r
                t = tok_ref[0, 0, s]
                acc_ref[pl.ds(t, 1), :] = acc_ref[pl.ds(t, 1), :] + ys_ref[pl.ds(s, 1), :]
            return carry

        lax.fori_loop(0, SLOT_TILE // 8, scatter, 0)

    @pl.when(j == ntile - 1)
    def _():
        cp = pltpu.make_async_copy(acc_ref.at[pl.ds(0, nrows)], o_hbm.at[c], sem.at[1])
        cp.start()
        cp.wait()


def _experts(block_e, ntiles, slot_tok, slot_w, h2, wg_bf, wu_bf, wd_bf, ntile):
    b, n, d = h2.shape
    f = wg_bf.shape[2]
    trash = 8
    widx = lambda c, j, be, nt: (be[c * ntile + j], 0, 0)
    sidx = lambda c, j, be, nt: (c * ntile + j, 0, 0)
    kern = functools.partial(_experts_kernel, ntile=ntile, nrows=n)
    grid_spec = pltpu.PrefetchScalarGridSpec(
        num_scalar_prefetch=2,
        grid=(b, ntile),
        in_specs=[pl.BlockSpec((1, 1, SLOT_TILE), sidx, memory_space=pltpu.SMEM),
                  pl.BlockSpec((1, SLOT_TILE, 1), sidx),
                  pl.BlockSpec(memory_space=pl.ANY),
                  pl.BlockSpec((1, d, f), widx),
                  pl.BlockSpec((1, d, f), widx),
                  pl.BlockSpec((1, f, d), widx)],
        out_specs=pl.BlockSpec(memory_space=pl.ANY),
        scratch_shapes=[pltpu.VMEM((n + trash, d), F32),
                        pltpu.VMEM((n + trash, d), F32),
                        pltpu.VMEM((SLOT_TILE, d), F32),
                        pltpu.VMEM((SLOT_TILE, d), F32),
                        pltpu.SemaphoreType.DMA((2,))],
    )
    return pl.pallas_call(
        kern,
        grid_spec=grid_spec,
        out_shape=jax.ShapeDtypeStruct((b, n, d), F32),
        compiler_params=_cparams(("arbitrary", "arbitrary")),
        name="experts",
    )(block_e, ntiles, slot_tok, slot_w, h2, wg_bf, wu_bf, wd_bf)


def _dispatch_plan(selw, b, n, ntile):
    w = selw.reshape(N_EXPERTS, b, n).transpose(1, 0, 2)
    sel = w > 0.0
    cnt = jnp.sum(sel, axis=2).astype(jnp.int32)
    pos = jnp.cumsum(sel, axis=2, dtype=jnp.int32) - sel.astype(jnp.int32)
    nt_e = (cnt + SLOT_TILE - 1) // SLOT_TILE
    tend = jnp.cumsum(nt_e, axis=1)
    tstart = tend - nt_e
    ntiles = tend[:, -1]
    dest = tstart[:, :, None] * SLOT_TILE + pos
    nslots = ntile * SLOT_TILE
    dest = jnp.where(sel, dest, nslots)
    tok = jnp.broadcast_to(jnp.arange(n, dtype=jnp.int32)[None, None, :], sel.shape)

    def per_batch(dst, tk, wt):
        st = jnp.full((nslots,), n, jnp.int32).at[dst.reshape(-1)].set(tk.reshape(-1), mode="drop")
        sw = jnp.zeros((nslots,), F32).at[dst.reshape(-1)].set(wt.reshape(-1), mode="drop")
        return st, sw

    slot_tok, slot_w = jax.vmap(per_batch)(dest, tok, w)
    tile_id = jnp.arange(ntile, dtype=jnp.int32)
    be = jnp.sum(tend[:, None, :] <= tile_id[None, :, None], axis=2).astype(jnp.int32)
    last = jnp.sum(tend <= (ntiles - 1)[:, None], axis=1).astype(jnp.int32)
    be = jnp.where(tile_id[None, :] < ntiles[:, None], be, last[:, None])
    be = jnp.minimum(be, N_EXPERTS - 1)
    return (be.reshape(-1), ntiles.astype(jnp.int32),
            slot_tok.reshape(b * ntile, 1, SLOT_TILE), slot_w.reshape(b * ntile, SLOT_TILE, 1))


def _final_kernel(h2_ref, r_ref, x1_ref, mod_ref, wg_ref, wu_ref, wd_ref, g_ref, o_ref):
    hb = h2_ref[0].astype(BF16)
    sh = _dot((_silu(_dot(hb, wg_ref[...])) * _dot(hb, wu_ref[...])).astype(BF16), wd_ref[...])
    moe = r_ref[0] + sh
    o_ref[0] = x1_ref[0] + mod_ref[0, 5:6, :] * _rms(moe, g_ref[...])


def _final(h2, routed, x1, mod3, wsg_bf, wsu_bf, wsd_bf, g_post, tm=512):
    b, n, d = x1.shape
    f = wsg_bf.shape[1]
    row = lambda bi, i: (bi, i, 0)
    vec = lambda bi, i: (0, 0)
    return pl.pallas_call(
        _final_kernel,
        grid=(b, n // tm),
        in_specs=[pl.BlockSpec((1, tm, d), row),
                  pl.BlockSpec((1, tm, d), row),
                  pl.BlockSpec((1, tm, d), row),
                  pl.BlockSpec((1, 6, d), lambda bi, i: (bi, 0, 0)),
                  pl.BlockSpec((d, f), vec),
                  pl.BlockSpec((d, f), vec),
                  pl.BlockSpec((f, d), vec),
                  pl.BlockSpec((1, d), vec)],
        out_specs=pl.BlockSpec((1, tm, d), row),
        out_shape=jax.ShapeDtypeStruct((b, n, d), F32),
        compiler_params=_cparams(("arbitrary", "arbitrary")),
        name="final",
    )(h2, routed, x1, mod3, wsg_bf, wsu_bf, wsd_bf, g_post)


def _rope_tables(n):
    t = np.arange(n)
    row = (t // GRID_W).astype(np.float32)
    col = (t % GRID_W).astype(np.float32)
    freqs = jnp.asarray(ROPE_THETA, F32) ** (-jnp.arange(0, AXIS_DIM, 2, dtype=F32) / AXIS_DIM)
    ar = jnp.asarray(row)[:, None] * freqs
    ac = jnp.asarray(col)[:, None] * freqs
    zeros = jnp.zeros_like(ar)
    cos64 = jnp.concatenate([jnp.cos(ar), jnp.cos(ar), jnp.cos(ac), jnp.cos(ac)], axis=1)
    sa64 = jnp.concatenate([-jnp.sin(ar), zeros, -jnp.sin(ac), zeros], axis=1)
    sb64 = jnp.concatenate([zeros, jnp.sin(ar), zeros, jnp.sin(ac)], axis=1)
    two = lambda a: jnp.concatenate([a, a], axis=1)
    return two(cos64), two(sa64), two(sb64)


def kernel(x, c, ctx, c_ctx, w_ada, b_ada, g_pre_mix, g_post_mix, g_pre_ffn, g_post_ffn, w_in, lambda_q1, lambda_k1, lambda_q2, lambda_k2, g_subln, w_pool, b_pool, pool_scale, w_out, w_router, router_bias, w_e_gate, w_e_up, w_e_down, w_sh_gate, w_sh_up, w_sh_down):
    b, n, d = x.shape
    l = 0
    mod_rows = 8
    cin = jnp.concatenate([c, c_ctx[None, :], jnp.zeros((mod_rows - b - 1, d), F32)], axis=0)
    mod3 = _ada(cin, w_ada[l], b_ada[l]).reshape(mod_rows, 6, d)

    w_in_bf = w_in[l].astype(BF16)
    cos, sa, sb = _rope_tables(n)
    q, k, v, p = _inproj(x, mod3, g_pre_mix[l][None, :], w_in_bf, cos * (QK_DIM ** -0.5), cos, sa, sb)
    kc, vc = _ctxkv(ctx, mod3, g_pre_mix[l][None, :], w_in_bf, b)

    lamv = jnp.stack([lambda_q1[l], lambda_k1[l], lambda_q2[l], lambda_k2[l]], axis=0)
    att = _attn(q, k, v, kc, vc, lamv, g_subln[l][None, :])

    x1, h2, logits_t = _outproj(
        att, p, x, mod3, w_pool[l].astype(BF16), b_pool[l][None, :], pool_scale[l][None, :],
        w_out[l].astype(BF16), g_post_mix[l][None, :], g_pre_ffn[l][None, :], w_router[l].T)

    t = b * n
    selw = _route(logits_t.reshape(N_EXPERTS, t // LANES, LANES),
                  jnp.broadcast_to(router_bias[l][:, None, None], (N_EXPERTS, 1, LANES)))
    selw = selw.reshape(N_EXPERTS, t)

    ntile = (n * TOP_K) // SLOT_TILE + N_EXPERTS
    block_e, ntiles, slot_tok, slot_w = _dispatch_plan(selw, b, n, ntile)
    routed = _experts(block_e, ntiles, slot_tok, slot_w, h2,
                      w_e_gate[l].astype(BF16), w_e_up[l].astype(BF16), w_e_down[l].astype(BF16), ntile)

    return _final(h2, routed, x1, mod3, w_sh_gate[l].astype(BF16), w_sh_up[l].astype(BF16),
                  w_sh_down[l].astype(BF16), g_post_ffn[l][None, :])
```

```python
import functools
import math

import jax
import jax.numpy as jnp
import numpy as np
from jax import lax
from jax.experimental import pallas as pl
from jax.experimental.pallas import tpu as pltpu

F32 = jnp.float32
BF16 = jnp.bfloat16

D_MODEL = 1024
GRID_W = 64
N_HEADS = 4
QK_DIM = 64
V_DIM = 128
ATTN_WIDTH = N_HEADS * V_DIM
POOL_WIDTH = D_MODEL - ATTN_WIDTH
POOL_WINDOWS = (2, 4, 8, 16)
POOL_GROUP = POOL_WIDTH // len(POOL_WINDOWS)
AXIS_DIM = QK_DIM // 2
ROPE_THETA = 10000.0
N_EXPERTS = 64
TOP_K = 8
N_GROUPS = 8
GROUP_SIZE = N_EXPERTS // N_GROUPS
TOPK_GROUPS = 4
EXPERT_DIM = 256
ROUTED_SCALE = 2.5
EPS = 1e-6
LAM_INIT = 0.8 - 0.6 * math.exp(-0.3 * 0)

LANES = 128
SLOT_TILE = 256
HALO = 16
VMEM_LIMIT = 56 * 1024 * 1024


def _cparams(sem, vmem=VMEM_LIMIT):
    return pltpu.CompilerParams(dimension_semantics=sem, vmem_limit_bytes=vmem)


def _split(a):
    hi = a.astype(BF16)
    lo = (a - hi.astype(F32)).astype(BF16)
    return hi, lo


def _dot(a, b):
    return jnp.dot(a, b, preferred_element_type=F32)


def _dot_nt(a, b):
    return lax.dot_general(a, b, (((1,), (1,)), ((), ())), preferred_element_type=F32)


def _dot3(a, b, nt=False):
    d = _dot_nt if nt else _dot
    ah, al = _split(a)
    bh, bl = _split(b)
    return d(ah, bh) + d(ah, bl) + d(al, bh)


def _rms(x, g):
    return x * lax.rsqrt(jnp.mean(x * x, axis=-1, keepdims=True) + EPS) * g


def _silu(x):
    return x * (1.0 / (1.0 + jnp.exp(-x)))


def _ada_kernel(c_ref, w_ref, b_ref, o_ref):
    o_ref[...] = _dot3(_silu(c_ref[...]), w_ref[...]) + b_ref[...]


def _ada(cin, w_ada, b_ada):
    rows, d = cin.shape
    n = w_ada.shape[1]
    tn = 1024
    return pl.pallas_call(
        _ada_kernel,
        grid=(n // tn,),
        in_specs=[pl.BlockSpec((rows, d), lambda j: (0, 0)),
                  pl.BlockSpec((d, tn), lambda j: (0, j)),
                  pl.BlockSpec((1, tn), lambda j: (0, j))],
        out_specs=pl.BlockSpec((rows, tn), lambda j: (0, j)),
        out_shape=jax.ShapeDtypeStruct((rows, n), F32),
        compiler_params=_cparams(("arbitrary",)),
        name="ada",
    )(cin, w_ada, b_ada.reshape(1, n))


def _rope(t, cos, sa, sb):
    return t * cos + pltpu.roll(t, LANES - AXIS_DIM // 2, 1) * sa + pltpu.roll(t, AXIS_DIM // 2, 1) * sb


def _inproj_kernel(x_ref, mod_ref, g_ref, w_ref, cq_ref, ck_ref, sa_ref, sb_ref,
                   q_ref, k_ref, v_ref, p_ref):
    x = x_ref[0]
    h = _rms(x, g_ref[...]) * (1.0 + mod_ref[0, 1:2, :]) + mod_ref[0, 0:1, :]
    px = _dot(h.astype(BF16), w_ref[...])
    ck = ck_ref[...]
    sa = sa_ref[...]
    sb = sb_ref[...]
    cq = cq_ref[...]
    scale = QK_DIM ** -0.5
    saq = sa * scale
    sbq = sb * scale
    for hh in range(N_HEADS):
        lo = hh * LANES
        q = px[:, lo:lo + LANES]
        k = px[:, ATTN_WIDTH + lo:ATTN_WIDTH + lo + LANES]
        q_ref[0, :, lo:lo + LANES] = _rope(q, cq, saq, sbq).astype(BF16)
        k_ref[0, :, lo:lo + LANES] = _rope(k, ck, sa, sb).astype(BF16)
    v_ref[0] = px[:, 2 * ATTN_WIDTH:3 * ATTN_WIDTH].astype(BF16)
    p_ref[0] = px[:, 3 * ATTN_WIDTH:].astype(BF16)


def _inproj(x, mod3, g, w_in_bf, cq, ck, sa, sb, tm=512):
    b, n, d = x.shape
    width = w_in_bf.shape[1]
    row = lambda bi, i: (bi, i, 0)
    tab = pl.BlockSpec((tm, LANES), lambda bi, i: (i, 0))
    out = jax.ShapeDtypeStruct((b, n, ATTN_WIDTH), BF16)
    return pl.pallas_call(
        _inproj_kernel,
        grid=(b, n // tm),
        in_specs=[pl.BlockSpec((1, tm, d), row),
                  pl.BlockSpec((1, 6, d), lambda bi, i: (bi, 0, 0)),
                  pl.BlockSpec((1, d), lambda bi, i: (0, 0)),
                  pl.BlockSpec((d, width), lambda bi, i: (0, 0)),
                  tab, tab, tab, tab],
        out_specs=[pl.BlockSpec((1, tm, ATTN_WIDTH), row)] * 4,
        out_shape=[out] * 4,
        compiler_params=_cparams(("arbitrary", "arbitrary")),
        name="inproj",
    )(x, mod3, g, w_in_bf, cq, ck, sa, sb)


def _ctxkv_kernel(x_ref, mod_ref, g_ref, wk_ref, wv_ref, k_ref, v_ref):
    h = _rms(x_ref[0], g_ref[...]) * (1.0 + mod_ref[0, 1:2, :]) + mod_ref[0, 0:1, :]
    hb = h.astype(BF16)
    k_ref[0] = _dot(hb, wk_ref[...]).astype(BF16)
    v_ref[0] = _dot(hb, wv_ref[...]).astype(BF16)


def _ctxkv(ctx, mod3, g, w_in_bf, ctx_row):
    b, n, d = ctx.shape
    out = jax.ShapeDtypeStruct((b, n, ATTN_WIDTH), BF16)
    return pl.pallas_call(
        _ctxkv_kernel,
        grid=(b,),
        in_specs=[pl.BlockSpec((1, n, d), lambda bi: (bi, 0, 0)),
                  pl.BlockSpec((1, 6, d), lambda bi: (ctx_row, 0, 0)),
                  pl.BlockSpec((1, d), lambda bi: (0, 0)),
                  pl.BlockSpec((d, ATTN_WIDTH), lambda bi: (0, 1)),
                  pl.BlockSpec((d, ATTN_WIDTH), lambda bi: (0, 2))],
        out_specs=[pl.BlockSpec((1, n, ATTN_WIDTH), lambda bi: (bi, 0, 0))] * 2,
        out_shape=[out] * 2,
        compiler_params=_cparams(("arbitrary",)),
        name="ctxkv",
    )(ctx, mod3, g, w_in_bf, w_in_bf)


def _attn_kernel(q_ref, kx_ref, vx_ref, kc_ref, vc_ref, lam_ref, g_ref, o_ref):
    q = q_ref[0]
    lane = lax.broadcasted_iota(jnp.int32, q.shape, 1)
    zero = jnp.zeros_like(q)
    q1 = jnp.where(lane < QK_DIM, q, zero)
    q2 = jnp.where(lane >= QK_DIM, q, zero)
    kx = kx_ref[0]
    kc = kc_ref[0]
    lv = lam_ref[...]
    lam = (jnp.exp(jnp.sum(lv[0:1] * lv[1:2], axis=-1, keepdims=True))
           - jnp.exp(jnp.sum(lv[2:3] * lv[3:4], axis=-1, keepdims=True)) + LAM_INIT)

    def softmax_parts(qm):
        sc = _dot_nt(qm, kc)
        sx = _dot_nt(qm, kx)
        m = jnp.maximum(jnp.max(sc, axis=-1, keepdims=True), jnp.max(sx, axis=-1, keepdims=True))
        pc = jnp.exp(sc - m)
        px = jnp.exp(sx - m)
        l = jnp.sum(pc, axis=-1, keepdims=True) + jnp.sum(px, axis=-1, keepdims=True)
        return pc, px, l

    p1c, p1x, l1 = softmax_parts(q1)
    p2c, p2x, l2 = softmax_parts(q2)
    r1 = 1.0 / l1
    r2 = lam / l2
    ac = (p1c * r1 - p2c * r2).astype(BF16)
    ax = (p1x * r1 - p2x * r2).astype(BF16)
    o = _dot(ac, vc_ref[0]) + _dot(ax, vx_ref[0])
    o_ref[0] = (_rms(o, g_ref[...]) * (1.0 - LAM_INIT)).astype(BF16)


def _attn(q, k, v, kc, vc, lamv, g_subln, tq=256):
    b, n, _ = q.shape
    nc = kc.shape[1]
    return pl.pallas_call(
        _attn_kernel,
        grid=(b, N_HEADS, n // tq),
        in_specs=[pl.BlockSpec((1, tq, LANES), lambda bi, h, i: (bi, i, h)),
                  pl.BlockSpec((1, n, LANES), lambda bi, h, i: (bi, 0, h)),
                  pl.BlockSpec((1, n, LANES), lambda bi, h, i: (bi, 0, h)),
                  pl.BlockSpec((1, nc, LANES), lambda bi, h, i: (bi, 0, h)),
                  pl.BlockSpec((1, nc, LANES), lambda bi, h, i: (bi, 0, h)),
                  pl.BlockSpec((4, QK_DIM), lambda bi, h, i: (0, 0)),
                  pl.BlockSpec((1, V_DIM), lambda bi, h, i: (0, 0))],
        out_specs=pl.BlockSpec((1, tq, LANES), lambda bi, h, i: (bi, i, h)),
        out_shape=jax.ShapeDtypeStruct((b, n, ATTN_WIDTH), BF16),
        compiler_params=_cparams(("arbitrary", "arbitrary", "arbitrary")),
        name="attn",
    )(q, k, v, kc, vc, lamv, g_subln)


def _outproj_kernel(att_ref, p_ref, pprev_ref, pnext_ref, x_ref, mod_ref, wpool_ref, bpool_ref,
                    pscale_ref, wout_ref, gpost_ref, gpre_ref, wr_ref,
                    x1_ref, h2_ref, lg_ref, *, tm, n):
    i = pl.program_id(1)
    p = p_ref[0].astype(F32)
    prev = jnp.where(i > 0, pprev_ref[0].astype(F32), 0.0)
    nxt = jnp.where(i < pl.num_programs(1) - 1, pnext_ref[0].astype(F32), 0.0)
    ext = jnp.concatenate([prev, p, nxt], axis=0)
    rows = tm + 2 * HALO
    trow = (i * tm + lax.broadcasted_iota(jnp.int32, (tm, 1), 0))

    pooled = []
    sums = {}
    acc = ext + pltpu.roll(ext, 1, 0)
    sums[2] = acc
    w = 2
    while w < POOL_WINDOWS[-1]:
        acc = pltpu.roll(acc, rows - w // 2, 0) + pltpu.roll(acc, w // 2, 0)
        w = 2 * w
        sums[w] = acc
    for gi, win in enumerate(POOL_WINDOWS):
        sl = slice(gi * POOL_GROUP, (gi + 1) * POOL_GROUP)
        wsum = sums[win][HALO:HALO + tm, sl]
        hi_c = jnp.minimum(trow + (win - win // 2), n)
        lo_c = jnp.maximum(trow - win // 2, 0)
        cnt = (hi_c - lo_c).astype(F32)
        d = wsum / cnt - p[:, sl]
        y = _dot(d.astype(BF16), wpool_ref[gi]) + bpool_ref[:, sl]
        pooled.append((y * pscale_ref[:, sl]).astype(BF16))
    pool = jnp.concatenate(pooled, axis=1)
    yx = _dot(att_ref[0], wout_ref[0:ATTN_WIDTH, :]) + _dot(pool, wout_ref[ATTN_WIDTH:, :])
    x1 = x_ref[0] + mod_ref[0, 2:3, :] * _rms(yx, gpost_ref[...])
    x1_ref[0] = x1
    h2 = _rms(x1, gpre_ref[...]) * (1.0 + mod_ref[0, 4:5, :]) + mod_ref[0, 3:4, :]
    h2_ref[0] = h2
    lg_ref[...] = _dot3(wr_ref[...], h2, nt=True)


def _outproj(att, p, x, mod3, w_pool_bf, b_pool, pool_scale, w_out_bf, g_post, g_pre, w_router_t, tm=256):
    b, n, d = x.shape
    hb = tm // HALO
    nhb = n // HALO
    row = lambda bi, i: (bi, i, 0)
    vec = lambda bi, i: (0, 0)
    kern = functools.partial(_outproj_kernel, tm=tm, n=n)
    return pl.pallas_call(
        kern,
        grid=(b, n // tm),
        in_specs=[pl.BlockSpec((1, tm, ATTN_WIDTH), row),
                  pl.BlockSpec((1, tm, POOL_WIDTH), row),
                  pl.BlockSpec((1, HALO, POOL_WIDTH), lambda bi, i: (bi, jnp.maximum(i * hb - 1, 0), 0)),
                  pl.BlockSpec((1, HALO, POOL_WIDTH), lambda bi, i: (bi, jnp.minimum((i + 1) * hb, nhb - 1), 0)),
                  pl.BlockSpec((1, tm, d), row),
                  pl.BlockSpec((1, 6, d), lambda bi, i: (bi, 0, 0)),
                  pl.BlockSpec((len(POOL_WINDOWS), POOL_GROUP, POOL_GROUP), lambda bi, i: (0, 0, 0)),
                  pl.BlockSpec((1, POOL_WIDTH), vec),
                  pl.BlockSpec((1, POOL_WIDTH), vec),
                  pl.BlockSpec((d, d), vec),
                  pl.BlockSpec((1, d), vec),
                  pl.BlockSpec((1, d), vec),
                  pl.BlockSpec((N_EXPERTS, d), vec)],
        out_specs=[pl.BlockSpec((1, tm, d), row),
                   pl.BlockSpec((1, tm, d), row),
                   pl.BlockSpec((N_EXPERTS, tm), lambda bi, i: (0, bi * (n // tm) + i))],
        out_shape=[jax.ShapeDtypeStruct((b, n, d), F32),
                   jax.ShapeDtypeStruct((b, n, d), F32),
                   jax.ShapeDtypeStruct((N_EXPERTS, b * n), F32)],
        compiler_params=_cparams(("arbitrary", "arbitrary")),
        name="outproj",
    )(att, p, p, p, x, mod3, w_pool_bf, b_pool, pool_scale, w_out_bf, g_post, g_pre, w_router_t)


def _beats(a, b, a_first):
    return jnp.where((a >= b) if a_first else (a > b), 1.0, 0.0)


def _route_kernel(lg_ref, bias_ref, w_ref):
    scores = [1.0 / (1.0 + jnp.exp(-lg_ref[e])) for e in range(N_EXPERTS)]
    biased = [scores[e] + bias_ref[e] for e in range(N_EXPERTS)]
    gscore = []
    for g in range(N_GROUPS):
        vals = biased[g * GROUP_SIZE:(g + 1) * GROUP_SIZE]
        m1 = vals[0]
        m2 = jnp.full_like(m1, -jnp.inf)
        for v in vals[1:]:
            m2 = jnp.maximum(m2, jnp.minimum(m1, v))
            m1 = jnp.maximum(m1, v)
        gscore.append(m1 + m2)
    gsel = []
    for g in range(N_GROUPS):
        rank = jnp.zeros_like(gscore[g])
        for g2 in range(N_GROUPS):
            if g2 != g:
                rank = rank + _beats(gscore[g2], gscore[g], g2 < g)
        gsel.append(rank < TOPK_GROUPS)
    masked = [jnp.where(gsel[e // GROUP_SIZE], biased[e], -jnp.inf) for e in range(N_EXPERTS)]
    picked = []
    for e in range(N_EXPERTS):
        rank = jnp.zeros_like(masked[e])
        for e2 in range(N_EXPERTS):
            if e2 != e:
                rank = rank + _beats(masked[e2], masked[e], e2 < e)
        picked.append(jnp.where(rank < TOP_K, scores[e], 0.0))
    denom = picked[0]
    for e in range(1, N_EXPERTS):
        denom = denom + picked[e]
    for e in range(N_EXPERTS):
        w_ref[e] = picked[e] / denom * ROUTED_SCALE


def _route(logits3, bias3, rows=8):
    e, r, l = logits3.shape
    return pl.pallas_call(
        _route_kernel,
        grid=(r // rows,),
        in_specs=[pl.BlockSpec((e, rows, l), lambda i: (0, i, 0)),
                  pl.BlockSpec((e, 1, l), lambda i: (0, 0, 0))],
        out_specs=pl.BlockSpec((e, rows, l), lambda i: (0, i, 0)),
        out_shape=jax.ShapeDtypeStruct((e, r, l), F32),
        compiler_params=_cparams(("arbitrary",)),
        name="route",
    )(logits3, bias3)


def _experts_kernel(be_ref, nt_ref, tok_ref, sw_ref, x_hbm, wg_ref, wu_ref, wd_ref, o_hbm,
                    xs_ref, acc_ref, xg_ref, ys_ref, sem, *, ntile, nrows):
    c = pl.program_id(0)
    j = pl.program_id(1)
    trash = xs_ref.shape[0] - nrows

    @pl.when(j == 0)
    def _():
        cp = pltpu.make_async_copy(x_hbm.at[c], xs_ref.at[pl.ds(0, nrows)], sem.at[0])
        cp.start()
        xs_ref[pl.ds(nrows, trash), :] = jnp.zeros((trash, xs_ref.shape[1]), F32)
        acc_ref[...] = jnp.zeros_like(acc_ref)
        cp.wait()

    @pl.when(j < nt_ref[c])
    def _():
        def gather(gi, carry):
            for r in range(8):
                s = gi * 8 + r
                t = tok_ref[0, 0, s]
                xg_ref[pl.ds(s, 1), :] = xs_ref[pl.ds(t, 1), :]
            return carry

        lax.fori_loop(0, SLOT_TILE // 8, gather, 0)
        xb = xg_ref[...].astype(BF16)
        g = _dot(xb, wg_ref[0])
        u = _dot(xb, wu_ref[0])
        h = (_silu(g) * u).astype(BF16)
        ys_ref[...] = _dot(h, wd_ref[0]) * sw_ref[0]

        def scatter(gi, carry):
            for r in range(8):
                s = gi * 8 + r
                t = tok_ref[0, 0, s]
                acc_ref[pl.ds(t, 1), :] = acc_ref[pl.ds(t, 1), :] + ys_ref[pl.ds(s, 1), :]
            return carry

        lax.fori_loop(0, SLOT_TILE // 8, scatter, 0)

    @pl.when(j == ntile - 1)
    def _():
        cp = pltpu.make_async_copy(acc_ref.at[pl.ds(0, nrows)], o_hbm.at[c], sem.at[1])
        cp.start()
        cp.wait()


def _experts(block_e, ntiles, slot_tok, slot_w, h2, wg_bf, wu_bf, wd_bf, ntile):
    b, n, d = h2.shape
    f = wg_bf.shape[2]
    trash = 8
    widx = lambda c, j, be, nt: (be[c * ntile + j], 0, 0)
    sidx = lambda c, j, be, nt: (c * ntile + j, 0, 0)
    kern = functools.partial(_experts_kernel, ntile=ntile, nrows=n)
    grid_spec = pltpu.PrefetchScalarGridSpec(
        num_scalar_prefetch=2,
        grid=(b, ntile),
        in_specs=[pl.BlockSpec((1, 1, SLOT_TILE), sidx, memory_space=pltpu.SMEM),
                  pl.BlockSpec((1, SLOT_TILE, 1), sidx),
                  pl.BlockSpec(memory_space=pl.ANY),
                  pl.BlockSpec((1, d, f), widx),
                  pl.BlockSpec((1, d, f), widx),
                  pl.BlockSpec((1, f, d), widx)],
        out_specs=pl.BlockSpec(memory_space=pl.ANY),
        scratch_shapes=[pltpu.VMEM((n + trash, d), F32),
                        pltpu.VMEM((n + trash, d), F32),
                        pltpu.VMEM((SLOT_TILE, d), F32),
                        pltpu.VMEM((SLOT_TILE, d), F32),
                        pltpu.SemaphoreType.DMA((2,))],
    )
    return pl.pallas_call(
        kern,
        grid_spec=grid_spec,
        out_shape=jax.ShapeDtypeStruct((b, n, d), F32),
        compiler_params=_cparams(("arbitrary", "arbitrary")),
        name="experts",
    )(block_e, ntiles, slot_tok, slot_w, h2, wg_bf, wu_bf, wd_bf)


EXPERTS_PER_LANE_GROUP = LANES // 32


def _plan_tables_kernel(w_ref, re_ref, ro_ref, int_ref, wt_ref, *, rows):
    ne = w_ref.shape[0]
    w2 = w_ref[...].reshape(ne * rows, LANES)
    selb = jnp.where(w2 > 0.0, 1.0, 0.0).astype(BF16)
    i0 = lax.broadcasted_iota(jnp.int32, (LANES, LANES), 0)
    i1 = lax.broadcasted_iota(jnp.int32, (LANES, LANES), 1)
    rowtot = _dot(selb, jnp.ones((LANES, LANES), BF16))
    ridx = lax.broadcasted_iota(jnp.int32, rowtot.shape, 0) & (rows - 1)
    acc = rowtot
    sh = 1
    while sh < rows:
        acc = acc + jnp.where(ridx >= sh, pltpu.roll(acc, sh, 0), 0.0)
        sh *= 2
    re_ref[0] = acc
    ro_ref[0] = acc - rowtot
    incl_t = _dot_nt(jnp.where(i1 <= i0, 1.0, 0.0).astype(BF16), selb)
    eye = jnp.where(i0 == i1, 1.0, 0.0).astype(BF16)
    w_hi = w2.astype(BF16)
    r1 = w2 - w_hi.astype(F32)
    w_mid = r1.astype(BF16)
    w_lo = (r1 - w_mid.astype(F32)).astype(BF16)
    parts = [_dot_nt(eye, p) for p in (w_hi, w_mid, w_lo)]
    for g in range(ne * rows // LANES):
        sl = slice(g * LANES, (g + 1) * LANES)
        int_ref[0, g] = incl_t[:, sl].astype(BF16)
        for k in range(3):
            wt_ref[0, k, g] = parts[k][:, sl].astype(BF16)


def _plan_tables(selw3, b):
    ne, r_all, _ = selw3.shape
    rows = r_all // b
    groups = ne * rows // LANES
    kern = functools.partial(_plan_tables_kernel, rows=rows)
    return pl.pallas_call(
        kern,
        grid=(b,),
        in_specs=[pl.BlockSpec((ne, rows, LANES), lambda c: (0, c, 0))],
        out_specs=[pl.BlockSpec((1, ne * rows, LANES), lambda c: (c, 0, 0)),
                   pl.BlockSpec((1, ne * rows, LANES), lambda c: (c, 0, 0)),
                   pl.BlockSpec((1, groups, LANES, LANES), lambda c: (c, 0, 0, 0)),
                   pl.BlockSpec((1, 3, groups, LANES, LANES), lambda c: (c, 0, 0, 0, 0))],
        out_shape=[jax.ShapeDtypeStruct((b, ne * rows, LANES), F32),
                   jax.ShapeDtypeStruct((b, ne * rows, LANES), F32),
                   jax.ShapeDtypeStruct((b, groups, LANES, LANES), BF16),
                   jax.ShapeDtypeStruct((b, 3, groups, LANES, LANES), BF16)],
        compiler_params=_cparams(("arbitrary",)),
        name="plan_tables",
    )(selw3)


def _plan_slots_kernel(be_ref, q0_ref, nt_ref, re_ref, ro_ref, int_ref, wt_ref, tok_ref, sw_ref,
                       *, ntile, rows, n):
    c = pl.program_id(0)
    tok_ref[...] = jnp.full(tok_ref.shape, n, jnp.int32)
    sw_ref[...] = jnp.zeros(sw_ref.shape, F32)
    lane = lax.broadcasted_iota(jnp.int32, (1, SLOT_TILE), 1).astype(F32)
    sub_r = lax.broadcasted_iota(jnp.int32, (rows, SLOT_TILE), 0).astype(F32)
    sub_l = lax.broadcasted_iota(jnp.int32, (LANES, SLOT_TILE), 0).astype(F32)
    reps = SLOT_TILE // LANES

    def tile_body(j, carry):
        e = be_ref[c * ntile + j]
        q = q0_ref[c * ntile + j].astype(F32) + lane
        base = pl.multiple_of(e * rows, rows)
        row_end = jnp.concatenate([re_ref[0, pl.ds(base, rows), :]] * reps, axis=1)
        row_off = jnp.concatenate([ro_ref[0, pl.ds(base, rows), :]] * reps, axis=1)
        r = jnp.sum(jnp.where(row_end <= q, 1.0, 0.0), axis=0, keepdims=True)
        row_start = jnp.sum(jnp.where(sub_r == r, row_off, 0.0), axis=0, keepdims=True)
        g = lax.shift_right_logical(e, EXPERTS_PER_LANE_GROUP.bit_length() - 1)
        col = r + ((e & (EXPERTS_PER_LANE_GROUP - 1)) * rows).astype(F32)
        onehot = jnp.where(sub_l == col, 1.0, 0.0).astype(BF16)
        incl = _dot(int_ref[0, g], onehot)
        lane_idx = jnp.sum(jnp.where(incl <= q - row_start, 1.0, 0.0), axis=0, keepdims=True)
        wrow = (_dot(wt_ref[0, 0, g], onehot) + _dot(wt_ref[0, 1, g], onehot)) + _dot(wt_ref[0, 2, g], onehot)
        wsel = jnp.sum(jnp.where(sub_l == lane_idx, wrow, 0.0), axis=0, keepdims=True)
        valid = q < row_end[rows - 1:rows, :]
        tok = jnp.where(valid, r * LANES + lane_idx, float(n)).astype(jnp.int32)
        tok_ref[0, pl.ds(j, 1), :] = tok
        sw_ref[0, pl.ds(j, 1), :] = jnp.where(valid, wsel, 0.0)
        return carry

    lax.fori_loop(0, nt_ref[c], tile_body, 0)


def _plan_slots(block_e, tile_q0, ntiles, row_end, row_off, incl_t, w_t, ntile, n):
    b, er, _ = row_end.shape
    rows = n // LANES
    groups = incl_t.shape[1]
    kern = functools.partial(_plan_slots_kernel, ntile=ntile, rows=rows, n=n)
    grid_spec = pltpu.PrefetchScalarGridSpec(
        num_scalar_prefetch=3,
        grid=(b,),
        in_specs=[pl.BlockSpec((1, er, LANES), lambda c, *_: (c, 0, 0)),
                  pl.BlockSpec((1, er, LANES), lambda c, *_: (c, 0, 0)),
                  pl.BlockSpec((1, groups, LANES, LANES), lambda c, *_: (c, 0, 0, 0)),
                  pl.BlockSpec((1, 3, groups, LANES, LANES), lambda c, *_: (c, 0, 0, 0, 0))],
        out_specs=[pl.BlockSpec((1, ntile, SLOT_TILE), lambda c, *_: (c, 0, 0)),
                   pl.BlockSpec((1, ntile, SLOT_TILE), lambda c, *_: (c, 0, 0))],
    )
    return pl.pallas_call(
        kern,
        grid_spec=grid_spec,
        out_shape=[jax.ShapeDtypeStruct((b, ntile, SLOT_TILE), jnp.int32),
                   jax.ShapeDtypeStruct((b, ntile, SLOT_TILE), F32)],
        compiler_params=_cparams(("arbitrary",)),
        name="plan_slots",
    )(block_e, tile_q0, ntiles, row_end, row_off, incl_t, w_t)


def _tile_meta(row_end, b, n, ntile):
    rows = n // LANES
    cnt = row_end.reshape(b, N_EXPERTS, rows, LANES)[:, :, rows - 1, 0].astype(jnp.int32)
    nt_e = (cnt + SLOT_TILE - 1) // SLOT_TILE
    tend = jnp.cumsum(nt_e, axis=1)
    tstart = tend - nt_e
    ntiles = tend[:, -1]
    tile_id = jnp.arange(ntile, dtype=jnp.int32)
    tid = jnp.minimum(tile_id[None, :], ntiles[:, None] - 1)
    be = jnp.minimum(jnp.sum(tend[:, None, :] <= tid[:, :, None], axis=2).astype(jnp.int32), N_EXPERTS - 1)
    q0 = (tid - jnp.take_along_axis(tstart, be, axis=1)) * SLOT_TILE
    return be.reshape(-1), q0.reshape(-1).astype(jnp.int32), ntiles.astype(jnp.int32)


def _dispatch_plan(selw3, b, n, ntile):
    row_end, row_off, incl_t, w_t = _plan_tables(selw3, b)
    block_e, tile_q0, ntiles = _tile_meta(row_end, b, n, ntile)
    slot_tok, slot_w = _plan_slots(block_e, tile_q0, ntiles, row_end, row_off, incl_t, w_t, ntile, n)
    return (block_e, ntiles,
            slot_tok.reshape(b * ntile, 1, SLOT_TILE), slot_w.reshape(b * ntile, SLOT_TILE, 1))


def _final_kernel(h2_ref, r_ref, x1_ref, mod_ref, wg_ref, wu_ref, wd_ref, g_ref, o_ref):
    hb = h2_ref[0].astype(BF16)
    sh = _dot((_silu(_dot(hb, wg_ref[...])) * _dot(hb, wu_ref[...])).astype(BF16), wd_ref[...])
    moe = r_ref[0] + sh
    o_ref[0] = x1_ref[0] + mod_ref[0, 5:6, :] * _rms(moe, g_ref[...])


def _final(h2, routed, x1, mod3, wsg_bf, wsu_bf, wsd_bf, g_post, tm=512):
    b, n, d = x1.shape
    f = wsg_bf.shape[1]
    row = lambda bi, i: (bi, i, 0)
    vec = lambda bi, i: (0, 0)
    return pl.pallas_call(
        _final_kernel,
        grid=(b, n // tm),
        in_specs=[pl.BlockSpec((1, tm, d), row),
                  pl.BlockSpec((1, tm, d), row),
                  pl.BlockSpec((1, tm, d), row),
                  pl.BlockSpec((1, 6, d), lambda bi, i: (bi, 0, 0)),
                  pl.BlockSpec((d, f), vec),
                  pl.BlockSpec((d, f), vec),
                  pl.BlockSpec((f, d), vec),
                  pl.BlockSpec((1, d), vec)],
        out_specs=pl.BlockSpec((1, tm, d), row),
        out_shape=jax.ShapeDtypeStruct((b, n, d), F32),
        compiler_params=_cparams(("arbitrary", "arbitrary")),
        name="final",
    )(h2, routed, x1, mod3, wsg_bf, wsu_bf, wsd_bf, g_post)


def _rope_tables(n):
    t = np.arange(n)
    row = (t // GRID_W).astype(np.float32)
    col = (t % GRID_W).astype(np.float32)
    freqs = jnp.asarray(ROPE_THETA, F32) ** (-jnp.arange(0, AXIS_DIM, 2, dtype=F32) / AXIS_DIM)
    ar = jnp.asarray(row)[:, None] * freqs
    ac = jnp.asarray(col)[:, None] * freqs
    zeros = jnp.zeros_like(ar)
    cos64 = jnp.concatenate([jnp.cos(ar), jnp.cos(ar), jnp.cos(ac), jnp.cos(ac)], axis=1)
    sa64 = jnp.concatenate([-jnp.sin(ar), zeros, -jnp.sin(ac), zeros], axis=1)
    sb64 = jnp.concatenate([zeros, jnp.sin(ar), zeros, jnp.sin(ac)], axis=1)
    two = lambda a: jnp.concatenate([a, a], axis=1)
    return two(cos64), two(sa64), two(sb64)


def kernel(x, c, ctx, c_ctx, w_ada, b_ada, g_pre_mix, g_post_mix, g_pre_ffn, g_post_ffn, w_in, lambda_q1, lambda_k1, lambda_q2, lambda_k2, g_subln, w_pool, b_pool, pool_scale, w_out, w_router, router_bias, w_e_gate, w_e_up, w_e_down, w_sh_gate, w_sh_up, w_sh_down):
    b, n, d = x.shape
    l = 0
    mod_rows = 8
    cin = jnp.concatenate([c, c_ctx[None, :], jnp.zeros((mod_rows - b - 1, d), F32)], axis=0)
    mod3 = _ada(cin, w_ada[l], b_ada[l]).reshape(mod_rows, 6, d)

    w_in_bf = w_in[l].astype(BF16)
    cos, sa, sb = _rope_tables(n)
    q, k, v, p = _inproj(x, mod3, g_pre_mix[l][None, :], w_in_bf, cos * (QK_DIM ** -0.5), cos, sa, sb)
    kc, vc = _ctxkv(ctx, mod3, g_pre_mix[l][None, :], w_in_bf, b)

    lamv = jnp.stack([lambda_q1[l], lambda_k1[l], lambda_q2[l], lambda_k2[l]], axis=0)
    att = _attn(q, k, v, kc, vc, lamv, g_subln[l][None, :])

    x1, h2, logits_t = _outproj(
        att, p, x, mod3, w_pool[l].astype(BF16), b_pool[l][None, :], pool_scale[l][None, :],
        w_out[l].astype(BF16), g_post_mix[l][None, :], g_pre_ffn[l][None, :], w_router[l].T)

    t = b * n
    selw3 = _route(logits_t.reshape(N_EXPERTS, t // LANES, LANES),
                   jnp.broadcast_to(router_bias[l][:, None, None], (N_EXPERTS, 1, LANES)))

    ntile = (n * TOP_K) // SLOT_TILE + N_EXPERTS
    block_e, ntiles, slot_tok, slot_w = _dispatch_plan(selw3, b, n, ntile)
    routed = _experts(block_e, ntiles, slot_tok, slot_w, h2,
                      w_e_gate[l].astype(BF16), w_e_up[l].astype(BF16), w_e_down[l].astype(BF16), ntile)

    return _final(h2, routed, x1, mod3, w_sh_gate[l].astype(BF16), w_sh_up[l].astype(BF16),
                  w_sh_down[l].astype(BF16), g_post_ffn[l][None, :])
```

```python
import functools
import math

import jax
import jax.numpy as jnp
import numpy as np
from jax import lax
from jax.experimental import pallas as pl
from jax.experimental.pallas import tpu as pltpu

F32 = jnp.float32
BF16 = jnp.bfloat16

D_MODEL = 1024
GRID_W = 64
N_HEADS = 4
QK_DIM = 64
V_DIM = 128
ATTN_WIDTH = N_HEADS * V_DIM
POOL_WIDTH = D_MODEL - ATTN_WIDTH
POOL_WINDOWS = (2, 4, 8, 16)
POOL_GROUP = POOL_WIDTH // len(POOL_WINDOWS)
AXIS_DIM = QK_DIM // 2
ROPE_THETA = 10000.0
N_EXPERTS = 64
TOP_K = 8
N_GROUPS = 8
GROUP_SIZE = N_EXPERTS // N_GROUPS
TOPK_GROUPS = 4
EXPERT_DIM = 256
ROUTED_SCALE = 2.5
EPS = 1e-6
LAM_INIT = 0.8 - 0.6 * math.exp(-0.3 * 0)

LANES = 128
SLOT_TILE = 256
HALO = 16
VMEM_LIMIT = 56 * 1024 * 1024


def _cparams(sem, vmem=VMEM_LIMIT):
    return pltpu.CompilerParams(dimension_semantics=sem, vmem_limit_bytes=vmem)


def _split(a):
    hi = a.astype(BF16)
    lo = (a - hi.astype(F32)).astype(BF16)
    return hi, lo


def _dot(a, b):
    return jnp.dot(a, b, preferred_element_type=F32)


def _dot_nt(a, b):
    return lax.dot_general(a, b, (((1,), (1,)), ((), ())), preferred_element_type=F32)


def _dot3(a, b, nt=False):
    d = _dot_nt if nt else _dot
    ah, al = _split(a)
    bh, bl = _split(b)
    return d(ah, bh) + d(ah, bl) + d(al, bh)


def _rms(x, g):
    return x * lax.rsqrt(jnp.mean(x * x, axis=-1, keepdims=True) + EPS) * g


def _silu(x):
    return x * (1.0 / (1.0 + jnp.exp(-x)))


def _ada_kernel(c_ref, w_ref, b_ref, o_ref):
    o_ref[...] = _dot3(_silu(c_ref[...]), w_ref[...]) + b_ref[...]


def _ada(cin, w_ada, b_ada):
    rows, d = cin.shape
    n = w_ada.shape[1]
    tn = 1024
    return pl.pallas_call(
        _ada_kernel,
        grid=(n // tn,),
        in_specs=[pl.BlockSpec((rows, d), lambda j: (0, 0)),
                  pl.BlockSpec((d, tn), lambda j: (0, j)),
                  pl.BlockSpec((1, tn), lambda j: (0, j))],
        out_specs=pl.BlockSpec((rows, tn), lambda j: (0, j)),
        out_shape=jax.ShapeDtypeStruct((rows, n), F32),
        compiler_params=_cparams(("arbitrary",)),
        name="ada",
    )(cin, w_ada, b_ada.reshape(1, n))


def _rope(t, cos, sa, sb):
    return t * cos + pltpu.roll(t, LANES - AXIS_DIM // 2, 1) * sa + pltpu.roll(t, AXIS_DIM // 2, 1) * sb


def _inproj_kernel(x_ref, mod_ref, g_ref, w_ref, cq_ref, ck_ref, sa_ref, sb_ref,
                   q_ref, k_ref, v_ref, p_ref):
    x = x_ref[0]
    h = _rms(x, g_ref[...]) * (1.0 + mod_ref[0, 1:2, :]) + mod_ref[0, 0:1, :]
    px = _dot(h.astype(BF16), w_ref[...])
    ck = ck_ref[...]
    sa = sa_ref[...]
    sb = sb_ref[...]
    cq = cq_ref[...]
    scale = QK_DIM ** -0.5
    saq = sa * scale
    sbq = sb * scale
    for hh in range(N_HEADS):
        lo = hh * LANES
        q = px[:, lo:lo + LANES]
        k = px[:, ATTN_WIDTH + lo:ATTN_WIDTH + lo + LANES]
        q_ref[0, :, lo:lo + LANES] = _rope(q, cq, saq, sbq).astype(BF16)
        k_ref[0, :, lo:lo + LANES] = _rope(k, ck, sa, sb).astype(BF16)
    v_ref[0] = px[:, 2 * ATTN_WIDTH:3 * ATTN_WIDTH].astype(BF16)
    p_ref[0] = px[:, 3 * ATTN_WIDTH:].astype(BF16)


def _inproj(x, mod3, g, w_in_bf, cq, ck, sa, sb, tm=512):
    b, n, d = x.shape
    width = w_in_bf.shape[1]
    row = lambda bi, i: (bi, i, 0)
    tab = pl.BlockSpec((tm, LANES), lambda bi, i: (i, 0))
    out = jax.ShapeDtypeStruct((b, n, ATTN_WIDTH), BF16)
    return pl.pallas_call(
        _inproj_kernel,
        grid=(b, n // tm),
        in_specs=[pl.BlockSpec((1, tm, d), row),
                  pl.BlockSpec((1, 6, d), lambda bi, i: (bi, 0, 0)),
                  pl.BlockSpec((1, d), lambda bi, i: (0, 0)),
                  pl.BlockSpec((d, width), lambda bi, i: (0, 0)),
                  tab, tab, tab, tab],
        out_specs=[pl.BlockSpec((1, tm, ATTN_WIDTH), row)] * 4,
        out_shape=[out] * 4,
        compiler_params=_cparams(("arbitrary", "arbitrary")),
        name="inproj",
    )(x, mod3, g, w_in_bf, cq, ck, sa, sb)


def _ctxkv_kernel(x_ref, mod_ref, g_ref, wk_ref, wv_ref, k_ref, v_ref):
    h = _rms(x_ref[0], g_ref[...]) * (1.0 + mod_ref[0, 1:2, :]) + mod_ref[0, 0:1, :]
    hb = h.astype(BF16)
    k_ref[0] = _dot(hb, wk_ref[...]).astype(BF16)
    v_ref[0] = _dot(hb, wv_ref[...]).astype(BF16)


def _ctxkv(ctx, mod3, g, w_in_bf, ctx_row):
    b, n, d = ctx.shape
    out = jax.ShapeDtypeStruct((b, n, ATTN_WIDTH), BF16)
    return pl.pallas_call(
        _ctxkv_kernel,
        grid=(b,),
        in_specs=[pl.BlockSpec((1, n, d), lambda bi: (bi, 0, 0)),
                  pl.BlockSpec((1, 6, d), lambda bi: (ctx_row, 0, 0)),
                  pl.BlockSpec((1, d), lambda bi: (0, 0)),
                  pl.BlockSpec((d, ATTN_WIDTH), lambda bi: (0, 1)),
                  pl.BlockSpec((d, ATTN_WIDTH), lambda bi: (0, 2))],
        out_specs=[pl.BlockSpec((1, n, ATTN_WIDTH), lambda bi: (bi, 0, 0))] * 2,
        out_shape=[out] * 2,
        compiler_params=_cparams(("arbitrary",)),
        name="ctxkv",
    )(ctx, mod3, g, w_in_bf, w_in_bf)


def _attn_kernel(q_ref, kx_ref, vx_ref, kc_ref, vc_ref, lam_ref, g_ref, o_ref):
    q = q_ref[0]
    lane = lax.broadcasted_iota(jnp.int32, q.shape, 1)
    zero = jnp.zeros_like(q)
    q1 = jnp.where(lane < QK_DIM, q, zero)
    q2 = jnp.where(lane >= QK_DIM, q, zero)
    kx = kx_ref[0]
    kc = kc_ref[0]
    lv = lam_ref[...]
    lam = (jnp.exp(jnp.sum(lv[0:1] * lv[1:2], axis=-1, keepdims=True))
           - jnp.exp(jnp.sum(lv[2:3] * lv[3:4], axis=-1, keepdims=True)) + LAM_INIT)

    def softmax_parts(qm):
        sc = _dot_nt(qm, kc)
        sx = _dot_nt(qm, kx)
        m = jnp.maximum(jnp.max(sc, axis=-1, keepdims=True), jnp.max(sx, axis=-1, keepdims=True))
        pc = jnp.exp(sc - m)
        px = jnp.exp(sx - m)
        l = jnp.sum(pc, axis=-1, keepdims=True) + jnp.sum(px, axis=-1, keepdims=True)
        return pc, px, l

    p1c, p1x, l1 = softmax_parts(q1)
    p2c, p2x, l2 = softmax_parts(q2)
    r1 = 1.0 / l1
    r2 = lam / l2
    ac = (p1c * r1 - p2c * r2).astype(BF16)
    ax = (p1x * r1 - p2x * r2).astype(BF16)
    o = _dot(ac, vc_ref[0]) + _dot(ax, vx_ref[0])
    o_ref[0] = (_rms(o, g_ref[...]) * (1.0 - LAM_INIT)).astype(BF16)


def _attn(q, k, v, kc, vc, lamv, g_subln, tq=256):
    b, n, _ = q.shape
    nc = kc.shape[1]
    return pl.pallas_call(
        _attn_kernel,
        grid=(b, N_HEADS, n // tq),
        in_specs=[pl.BlockSpec((1, tq, LANES), lambda bi, h, i: (bi, i, h)),
                  pl.BlockSpec((1, n, LANES), lambda bi, h, i: (bi, 0, h)),
                  pl.BlockSpec((1, n, LANES), lambda bi, h, i: (bi, 0, h)),
                  pl.BlockSpec((1, nc, LANES), lambda bi, h, i: (bi, 0, h)),
                  pl.BlockSpec((1, nc, LANES), lambda bi, h, i: (bi, 0, h)),
                  pl.BlockSpec((4, QK_DIM), lambda bi, h, i: (0, 0)),
                  pl.BlockSpec((1, V_DIM), lambda bi, h, i: (0, 0))],
        out_specs=pl.BlockSpec((1, tq, LANES), lambda bi, h, i: (bi, i, h)),
        out_shape=jax.ShapeDtypeStruct((b, n, ATTN_WIDTH), BF16),
        compiler_params=_cparams(("arbitrary", "arbitrary", "arbitrary")),
        name="attn",
    )(q, k, v, kc, vc, lamv, g_subln)


def _outproj_kernel(att_ref, p_ref, pprev_ref, pnext_ref, x_ref, mod_ref, wpool_ref, bpool_ref,
                    pscale_ref, wout_ref, gpost_ref, gpre_ref, wr_ref,
                    x1_ref, h2_ref, h2t_ref, lg_ref, *, tm, n):
    i = pl.program_id(1)
    p = p_ref[0].astype(F32)
    prev = jnp.where(i > 0, pprev_ref[0].astype(F32), 0.0)
    nxt = jnp.where(i < pl.num_programs(1) - 1, pnext_ref[0].astype(F32), 0.0)
    ext = jnp.concatenate([prev, p, nxt], axis=0)
    rows = tm + 2 * HALO
    trow = (i * tm + lax.broadcasted_iota(jnp.int32, (tm, 1), 0))

    pooled = []
    sums = {}
    acc = ext + pltpu.roll(ext, 1, 0)
    sums[2] = acc
    w = 2
    while w < POOL_WINDOWS[-1]:
        acc = pltpu.roll(acc, rows - w // 2, 0) + pltpu.roll(acc, w // 2, 0)
        w = 2 * w
        sums[w] = acc
    for gi, win in enumerate(POOL_WINDOWS):
        sl = slice(gi * POOL_GROUP, (gi + 1) * POOL_GROUP)
        wsum = sums[win][HALO:HALO + tm, sl]
        hi_c = jnp.minimum(trow + (win - win // 2), n)
        lo_c = jnp.maximum(trow - win // 2, 0)
        cnt = (hi_c - lo_c).astype(F32)
        d = wsum / cnt - p[:, sl]
        y = _dot(d.astype(BF16), wpool_ref[gi]) + bpool_ref[:, sl]
        pooled.append((y * pscale_ref[:, sl]).astype(BF16))
    pool = jnp.concatenate(pooled, axis=1)
    yx = _dot(att_ref[0], wout_ref[0:ATTN_WIDTH, :]) + _dot(pool, wout_ref[ATTN_WIDTH:, :])
    x1 = x_ref[0] + mod_ref[0, 2:3, :] * _rms(yx, gpost_ref[...])
    x1_ref[0] = x1
    h2 = _rms(x1, gpre_ref[...]) * (1.0 + mod_ref[0, 4:5, :]) + mod_ref[0, 3:4, :]
    h2_ref[0] = h2.astype(BF16)
    for k in range(CHUNKS):
        h2t_ref[0, pl.ds(k, tm, stride=CHUNKS), :] = h2[:, k * LANES:(k + 1) * LANES]
    lg_ref[...] = _dot3(wr_ref[...], h2, nt=True)


def _outproj(att, p, x, mod3, w_pool_bf, b_pool, pool_scale, w_out_bf, g_post, g_pre, w_router_t, tm=256):
    b, n, d = x.shape
    hb = tm // HALO
    nhb = n // HALO
    row = lambda bi, i: (bi, i, 0)
    vec = lambda bi, i: (0, 0)
    kern = functools.partial(_outproj_kernel, tm=tm, n=n)
    return pl.pallas_call(
        kern,
        grid=(b, n // tm),
        in_specs=[pl.BlockSpec((1, tm, ATTN_WIDTH), row),
                  pl.BlockSpec((1, tm, POOL_WIDTH), row),
                  pl.BlockSpec((1, HALO, POOL_WIDTH), lambda bi, i: (bi, jnp.maximum(i * hb - 1, 0), 0)),
                  pl.BlockSpec((1, HALO, POOL_WIDTH), lambda bi, i: (bi, jnp.minimum((i + 1) * hb, nhb - 1), 0)),
                  pl.BlockSpec((1, tm, d), row),
                  pl.BlockSpec((1, 6, d), lambda bi, i: (bi, 0, 0)),
                  pl.BlockSpec((len(POOL_WINDOWS), POOL_GROUP, POOL_GROUP), lambda bi, i: (0, 0, 0)),
                  pl.BlockSpec((1, POOL_WIDTH), vec),
                  pl.BlockSpec((1, POOL_WIDTH), vec),
                  pl.BlockSpec((d, d), vec),
                  pl.BlockSpec((1, d), vec),
                  pl.BlockSpec((1, d), vec),
                  pl.BlockSpec((N_EXPERTS, d), vec)],
        out_specs=[pl.BlockSpec((1, tm, d), row),
                   pl.BlockSpec((1, tm, d), row),
                   pl.BlockSpec((1, tm * CHUNKS, LANES), row),
                   pl.BlockSpec((N_EXPERTS, tm), lambda bi, i: (0, bi * (n // tm) + i))],
        out_shape=[jax.ShapeDtypeStruct((b, n, d), F32),
                   jax.ShapeDtypeStruct((b, n, d), BF16),
                   jax.ShapeDtypeStruct((b, n * CHUNKS, LANES), F32),
                   jax.ShapeDtypeStruct((N_EXPERTS, b * n), F32)],
        compiler_params=_cparams(("arbitrary", "arbitrary")),
        name="outproj",
    )(att, p, p, p, x, mod3, w_pool_bf, b_pool, pool_scale, w_out_bf, g_post, g_pre, w_router_t)


def _beats(a, b, a_first):
    return jnp.where((a >= b) if a_first else (a > b), 1.0, 0.0)


def _route_kernel(lg_ref, bias_ref, w_ref):
    scores = [1.0 / (1.0 + jnp.exp(-lg_ref[e])) for e in range(N_EXPERTS)]
    biased = [scores[e] + bias_ref[e] for e in range(N_EXPERTS)]
    gscore = []
    for g in range(N_GROUPS):
        vals = biased[g * GROUP_SIZE:(g + 1) * GROUP_SIZE]
        m1 = vals[0]
        m2 = jnp.full_like(m1, -jnp.inf)
        for v in vals[1:]:
            m2 = jnp.maximum(m2, jnp.minimum(m1, v))
            m1 = jnp.maximum(m1, v)
        gscore.append(m1 + m2)
    gsel = []
    for g in range(N_GROUPS):
        rank = jnp.zeros_like(gscore[g])
        for g2 in range(N_GROUPS):
            if g2 != g:
                rank = rank + _beats(gscore[g2], gscore[g], g2 < g)
        gsel.append(rank < TOPK_GROUPS)
    masked = [jnp.where(gsel[e // GROUP_SIZE], biased[e], -jnp.inf) for e in range(N_EXPERTS)]
    picked = []
    for e in range(N_EXPERTS):
        rank = jnp.zeros_like(masked[e])
        for e2 in range(N_EXPERTS):
            if e2 != e:
                rank = rank + _beats(masked[e2], masked[e], e2 < e)
        picked.append(jnp.where(rank < TOP_K, scores[e], 0.0))
    denom = picked[0]
    for e in range(1, N_EXPERTS):
        denom = denom + picked[e]
    for e in range(N_EXPERTS):
        w_ref[e] = picked[e] / denom * ROUTED_SCALE


def _route(logits3, bias3, rows=8):
    e, r, l = logits3.shape
    return pl.pallas_call(
        _route_kernel,
        grid=(r // rows,),
        in_specs=[pl.BlockSpec((e, rows, l), lambda i: (0, i, 0)),
                  pl.BlockSpec((e, 1, l), lambda i: (0, 0, 0))],
        out_specs=pl.BlockSpec((e, rows, l), lambda i: (0, i, 0)),
        out_shape=jax.ShapeDtypeStruct((e, r, l), F32),
        compiler_params=_cparams(("arbitrary",)),
        name="route",
    )(logits3, bias3)


CHUNKS = D_MODEL // LANES
SLAB = SLOT_TILE + 8
ROW_BATCH = 16


def _experts_kernel(be_ref, nt_ref, ng_ref, row_ref, sw_ref, x_hbm, wg_ref, wu_ref, wd_ref, o_hbm,
                    xs_ref, acc_ref, tx_ref, ty_ref, sem, *, ntile, n):
    c = pl.program_id(0)
    j = pl.program_id(1)
    rows = n * CHUNKS

    @pl.when(j == 0)
    def _():
        cp = pltpu.make_async_copy(x_hbm.at[c], xs_ref.at[pl.ds(0, rows)], sem.at[0])
        cp.start()
        xs_ref[pl.ds(rows, CHUNKS), :] = jnp.zeros((CHUNKS, LANES), F32)
        acc_ref[...] = jnp.zeros_like(acc_ref)
        tx_ref[...] = jnp.zeros_like(tx_ref)
        cp.wait()

    @pl.when(j < nt_ref[c])
    def _():
        ngroups = ng_ref[c * ntile + j]

        def gather(gi, carry):
            for r in range(ROW_BATCH):
                s = gi * ROW_BATCH + r
                row = pl.multiple_of(row_ref[0, 0, s], CHUNKS)
                tx_ref[pl.ds(s, CHUNKS, stride=SLAB), :] = xs_ref[pl.ds(row, CHUNKS), :]
            return carry

        lax.fori_loop(0, ngroups, gather, 0)
        xb = jnp.concatenate([tx_ref[pl.ds(k * SLAB, SLOT_TILE), :] for k in range(CHUNKS)],
                             axis=1).astype(BF16)
        g = _dot(xb, wg_ref[0])
        u = _dot(xb, wu_ref[0])
        h = (_silu(g) * u).astype(BF16)
        y = _dot(h, wd_ref[0])
        for k in range(CHUNKS):
            ty_ref[pl.ds(k * SLAB, SLOT_TILE), :] = y[:, k * LANES:(k + 1) * LANES]

        def scatter(gi, carry):
            new = []
            for r in range(ROW_BATCH):
                s = gi * ROW_BATCH + r
                row = pl.multiple_of(row_ref[0, 0, s], CHUNKS)
                new.append((row, acc_ref[pl.ds(row, CHUNKS), :]
                            + sw_ref[0, 0, s] * ty_ref[pl.ds(s, CHUNKS, stride=SLAB), :]))
            for row, v in new:
                acc_ref[pl.ds(row, CHUNKS), :] = v
            return carry

        lax.fori_loop(0, ngroups, scatter, 0)

    @pl.when(j == ntile - 1)
    def _():
        cp = pltpu.make_async_copy(acc_ref.at[pl.ds(0, rows)], o_hbm.at[c], sem.at[1])
        cp.start()
        cp.wait()


def _experts(block_e, ntiles, ngroups, slot_tok, slot_w, h2t, wg_bf, wu_bf, wd_bf, ntile, n):
    b = h2t.shape[0]
    d, f = wg_bf.shape[1], wg_bf.shape[2]
    widx = lambda c, j, be, nt, ng: (be[c * ntile + j], 0, 0)
    sidx = lambda c, j, be, nt, ng: (c * ntile + j, 0, 0)
    kern = functools.partial(_experts_kernel, ntile=ntile, n=n)
    grid_spec = pltpu.PrefetchScalarGridSpec(
        num_scalar_prefetch=3,
        grid=(b, ntile),
        in_specs=[pl.BlockSpec((1, 1, SLOT_TILE), sidx, memory_space=pltpu.SMEM),
                  pl.BlockSpec((1, 1, SLOT_TILE), sidx, memory_space=pltpu.SMEM),
                  pl.BlockSpec(memory_space=pl.ANY),
                  pl.BlockSpec((1, d, f), widx),
                  pl.BlockSpec((1, d, f), widx),
                  pl.BlockSpec((1, f, d), widx)],
        out_specs=pl.BlockSpec(memory_space=pl.ANY),
        scratch_shapes=[pltpu.VMEM(((n + 1) * CHUNKS, LANES), F32),
                        pltpu.VMEM(((n + 1) * CHUNKS, LANES), F32),
                        pltpu.VMEM((CHUNKS * SLAB, LANES), F32),
                        pltpu.VMEM((CHUNKS * SLAB, LANES), F32),
                        pltpu.SemaphoreType.DMA((2,))],
    )
    return pl.pallas_call(
        kern,
        grid_spec=grid_spec,
        out_shape=jax.ShapeDtypeStruct((b, n * CHUNKS, LANES), F32),
        compiler_params=_cparams(("arbitrary", "arbitrary")),
        name="experts",
    )(block_e, ntiles, ngroups, slot_tok, slot_w, h2t, wg_bf, wu_bf, wd_bf)


EXPERTS_PER_LANE_GROUP = LANES // 32


def _plan_tables_kernel(w_ref, re_ref, ro_ref, int_ref, wt_ref, *, rows):
    ne = w_ref.shape[0]
    w2 = w_ref[...].reshape(ne * rows, LANES)
    selb = jnp.where(w2 > 0.0, 1.0, 0.0).astype(BF16)
    i0 = lax.broadcasted_iota(jnp.int32, (LANES, LANES), 0)
    i1 = lax.broadcasted_iota(jnp.int32, (LANES, LANES), 1)
    rowtot = _dot(selb, jnp.ones((LANES, LANES), BF16))
    ridx = lax.broadcasted_iota(jnp.int32, rowtot.shape, 0) & (rows - 1)
    acc = rowtot
    sh = 1
    while sh < rows:
        acc = acc + jnp.where(ridx >= sh, pltpu.roll(acc, sh, 0), 0.0)
        sh *= 2
    re_ref[0] = acc
    ro_ref[0] = acc - rowtot
    incl_t = _dot_nt(jnp.where(i1 <= i0, 1.0, 0.0).astype(BF16), selb)
    eye = jnp.where(i0 == i1, 1.0, 0.0).astype(BF16)
    w_hi = w2.astype(BF16)
    r1 = w2 - w_hi.astype(F32)
    w_mid = r1.astype(BF16)
    w_lo = (r1 - w_mid.astype(F32)).astype(BF16)
    parts = [_dot_nt(eye, p) for p in (w_hi, w_mid, w_lo)]
    for g in range(ne * rows // LANES):
        sl = slice(g * LANES, (g + 1) * LANES)
        int_ref[0, g] = incl_t[:, sl].astype(BF16)
        for k in range(3):
            wt_ref[0, k, g] = parts[k][:, sl].astype(BF16)


def _plan_tables(selw3, b):
    ne, r_all, _ = selw3.shape
    rows = r_all // b
    groups = ne * rows // LANES
    kern = functools.partial(_plan_tables_kernel, rows=rows)
    return pl.pallas_call(
        kern,
        grid=(b,),
        in_specs=[pl.BlockSpec((ne, rows, LANES), lambda c: (0, c, 0))],
        out_specs=[pl.BlockSpec((1, ne * rows, LANES), lambda c: (c, 0, 0)),
                   pl.BlockSpec((1, ne * rows, LANES), lambda c: (c, 0, 0)),
                   pl.BlockSpec((1, groups, LANES, LANES), lambda c: (c, 0, 0, 0)),
                   pl.BlockSpec((1, 3, groups, LANES, LANES), lambda c: (c, 0, 0, 0, 0))],
        out_shape=[jax.ShapeDtypeStruct((b, ne * rows, LANES), F32),
                   jax.ShapeDtypeStruct((b, ne * rows, LANES), F32),
                   jax.ShapeDtypeStruct((b, groups, LANES, LANES), BF16),
                   jax.ShapeDtypeStruct((b, 3, groups, LANES, LANES), BF16)],
        compiler_params=_cparams(("arbitrary",)),
        name="plan_tables",
    )(selw3)


def _plan_slots_kernel(be_ref, q0_ref, nt_ref, re_ref, ro_ref, int_ref, wt_ref, tok_ref, sw_ref,
                       *, ntile, rows, n):
    c = pl.program_id(0)
    tok_ref[...] = jnp.full(tok_ref.shape, n * CHUNKS, jnp.int32)
    sw_ref[...] = jnp.zeros(sw_ref.shape, F32)
    lane = lax.broadcasted_iota(jnp.int32, (1, SLOT_TILE), 1).astype(F32)
    sub_r = lax.broadcasted_iota(jnp.int32, (rows, SLOT_TILE), 0).astype(F32)
    sub_l = lax.broadcasted_iota(jnp.int32, (LANES, SLOT_TILE), 0).astype(F32)
    reps = SLOT_TILE // LANES

    def tile_body(j, carry):
        e = be_ref[c * ntile + j]
        q = q0_ref[c * ntile + j].astype(F32) + lane
        base = pl.multiple_of(e * rows, rows)
        row_end = jnp.concatenate([re_ref[0, pl.ds(base, rows), :]] * reps, axis=1)
        row_off = jnp.concatenate([ro_ref[0, pl.ds(base, rows), :]] * reps, axis=1)
        r = jnp.sum(jnp.where(row_end <= q, 1.0, 0.0), axis=0, keepdims=True)
        row_start = jnp.sum(jnp.where(sub_r == r, row_off, 0.0), axis=0, keepdims=True)
        g = lax.shift_right_logical(e, EXPERTS_PER_LANE_GROUP.bit_length() - 1)
        col = r + ((e & (EXPERTS_PER_LANE_GROUP - 1)) * rows).astype(F32)
        onehot = jnp.where(sub_l == col, 1.0, 0.0).astype(BF16)
        incl = _dot(int_ref[0, g], onehot)
        lane_idx = jnp.sum(jnp.where(incl <= q - row_start, 1.0, 0.0), axis=0, keepdims=True)
        wrow = (_dot(wt_ref[0, 0, g], onehot) + _dot(wt_ref[0, 1, g], onehot)) + _dot(wt_ref[0, 2, g], onehot)
        wsel = jnp.sum(jnp.where(sub_l == lane_idx, wrow, 0.0), axis=0, keepdims=True)
        valid = q < row_end[rows - 1:rows, :]
        tok = jnp.where(valid, r * LANES + lane_idx, float(n)).astype(jnp.int32)
        tok_ref[0, pl.ds(j, 1), :] = tok * CHUNKS
        sw_ref[0, pl.ds(j, 1), :] = jnp.where(valid, wsel, 0.0)
        return carry

    lax.fori_loop(0, nt_ref[c], tile_body, 0)


def _plan_slots(block_e, tile_q0, ntiles, row_end, row_off, incl_t, w_t, ntile, n):
    b, er, _ = row_end.shape
    rows = n // LANES
    groups = incl_t.shape[1]
    kern = functools.partial(_plan_slots_kernel, ntile=ntile, rows=rows, n=n)
    grid_spec = pltpu.PrefetchScalarGridSpec(
        num_scalar_prefetch=3,
        grid=(b,),
        in_specs=[pl.BlockSpec((1, er, LANES), lambda c, *_: (c, 0, 0)),
                  pl.BlockSpec((1, er, LANES), lambda c, *_: (c, 0, 0)),
                  pl.BlockSpec((1, groups, LANES, LANES), lambda c, *_: (c, 0, 0, 0)),
                  pl.BlockSpec((1, 3, groups, LANES, LANES), lambda c, *_: (c, 0, 0, 0, 0))],
        out_specs=[pl.BlockSpec((1, ntile, SLOT_TILE), lambda c, *_: (c, 0, 0)),
                   pl.BlockSpec((1, ntile, SLOT_TILE), lambda c, *_: (c, 0, 0))],
    )
    return pl.pallas_call(
        kern,
        grid_spec=grid_spec,
        out_shape=[jax.ShapeDtypeStruct((b, ntile, SLOT_TILE), jnp.int32),
                   jax.ShapeDtypeStruct((b, ntile, SLOT_TILE), F32)],
        compiler_params=_cparams(("arbitrary",)),
        name="plan_slots",
    )(block_e, tile_q0, ntiles, row_end, row_off, incl_t, w_t)


def _tile_meta(row_end, b, n, ntile):
    rows = n // LANES
    cnt = row_end.reshape(b, N_EXPERTS, rows, LANES)[:, :, rows - 1, 0].astype(jnp.int32)
    nt_e = (cnt + SLOT_TILE - 1) // SLOT_TILE
    tend = jnp.cumsum(nt_e, axis=1)
    tstart = tend - nt_e
    ntiles = tend[:, -1]
    tile_id = jnp.arange(ntile, dtype=jnp.int32)
    tid = jnp.minimum(tile_id[None, :], ntiles[:, None] - 1)
    be = jnp.minimum(jnp.sum(tend[:, None, :] <= tid[:, :, None], axis=2).astype(jnp.int32), N_EXPERTS - 1)
    q0 = (tid - jnp.take_along_axis(tstart, be, axis=1)) * SLOT_TILE
    used = jnp.clip(jnp.take_along_axis(cnt, be, axis=1) - q0, 0, SLOT_TILE)
    ngroups = (used + ROW_BATCH - 1) // ROW_BATCH
    return (be.reshape(-1), q0.reshape(-1).astype(jnp.int32), ntiles.astype(jnp.int32),
            ngroups.reshape(-1).astype(jnp.int32))


def _dispatch_plan(selw3, b, n, ntile):
    row_end, row_off, incl_t, w_t = _plan_tables(selw3, b)
    block_e, tile_q0, ntiles, ngroups = _tile_meta(row_end, b, n, ntile)
    slot_tok, slot_w = _plan_slots(block_e, tile_q0, ntiles, row_end, row_off, incl_t, w_t, ntile, n)
    return (block_e, ntiles, ngroups,
            slot_tok.reshape(b * ntile, 1, SLOT_TILE), slot_w.reshape(b * ntile, 1, SLOT_TILE))


def _final_kernel(h2_ref, r_ref, x1_ref, mod_ref, wg_ref, wu_ref, wd_ref, g_ref, o_ref):
    hb = h2_ref[0]
    tm = hb.shape[0]
    sh = _dot((_silu(_dot(hb, wg_ref[...])) * _dot(hb, wu_ref[...])).astype(BF16), wd_ref[...])
    routed = jnp.concatenate([r_ref[0, pl.ds(k, tm, stride=CHUNKS), :] for k in range(CHUNKS)], axis=1)
    moe = routed + sh
    o_ref[0] = x1_ref[0] + mod_ref[0, 5:6, :] * _rms(moe, g_ref[...])


def _final(h2, routed, x1, mod3, wsg_bf, wsu_bf, wsd_bf, g_post, tm=512):
    b, n, d = x1.shape
    f = wsg_bf.shape[1]
    row = lambda bi, i: (bi, i, 0)
    vec = lambda bi, i: (0, 0)
    return pl.pallas_call(
        _final_kernel,
        grid=(b, n // tm),
        in_specs=[pl.BlockSpec((1, tm, d), row),
                  pl.BlockSpec((1, tm * CHUNKS, LANES), row),
                  pl.BlockSpec((1, tm, d), row),
                  pl.BlockSpec((1, 6, d), lambda bi, i: (bi, 0, 0)),
                  pl.BlockSpec((d, f), vec),
                  pl.BlockSpec((d, f), vec),
                  pl.BlockSpec((f, d), vec),
                  pl.BlockSpec((1, d), vec)],
        out_specs=pl.BlockSpec((1, tm, d), row),
        out_shape=jax.ShapeDtypeStruct((b, n, d), F32),
        compiler_params=_cparams(("arbitrary", "arbitrary")),
        name="final",
    )(h2, routed, x1, mod3, wsg_bf, wsu_bf, wsd_bf, g_post)


def _rope_tables(n):
    t = np.arange(n)
    row = (t // GRID_W).astype(np.float32)
    col = (t % GRID_W).astype(np.float32)
    freqs = np.float32(ROPE_THETA) ** (-np.arange(0, AXIS_DIM, 2, dtype=np.float32) / np.float32(AXIS_DIM))
    ar = row[:, None] * freqs
    ac = col[:, None] * freqs
    zeros = np.zeros_like(ar)
    cos64 = np.concatenate([np.cos(ar), np.cos(ar), np.cos(ac), np.cos(ac)], axis=1)
    sa64 = np.concatenate([-np.sin(ar), zeros, -np.sin(ac), zeros], axis=1)
    sb64 = np.concatenate([zeros, np.sin(ar), zeros, np.sin(ac)], axis=1)
    two = lambda a: jnp.asarray(np.concatenate([a, a], axis=1).astype(np.float32))
    return two(cos64), two(sa64), two(sb64)


def kernel(x, c, ctx, c_ctx, w_ada, b_ada, g_pre_mix, g_post_mix, g_pre_ffn, g_post_ffn, w_in, lambda_q1, lambda_k1, lambda_q2, lambda_k2, g_subln, w_pool, b_pool, pool_scale, w_out, w_router, router_bias, w_e_gate, w_e_up, w_e_down, w_sh_gate, w_sh_up, w_sh_down):
    b, n, d = x.shape
    l = 0
    mod_rows = 8
    cin = jnp.concatenate([c, c_ctx[None, :], jnp.zeros((mod_rows - b - 1, d), F32)], axis=0)
    mod3 = _ada(cin, w_ada[l], b_ada[l]).reshape(mod_rows, 6, d)

    w_in_bf = w_in[l].astype(BF16)
    cos, sa, sb = _rope_tables(n)
    q, k, v, p = _inproj(x, mod3, g_pre_mix[l][None, :], w_in_bf, cos * (QK_DIM ** -0.5), cos, sa, sb)
    kc, vc = _ctxkv(ctx, mod3, g_pre_mix[l][None, :], w_in_bf, b)

    lamv = jnp.stack([lambda_q1[l], lambda_k1[l], lambda_q2[l], lambda_k2[l]], axis=0)
    att = _attn(q, k, v, kc, vc, lamv, g_subln[l][None, :])

    x1, h2, h2t, logits_t = _outproj(
        att, p, x, mod3, w_pool[l].astype(BF16), b_pool[l][None, :], pool_scale[l][None, :],
        w_out[l].astype(BF16), g_post_mix[l][None, :], g_pre_ffn[l][None, :], w_router[l].T)

    t = b * n
    selw3 = _route(logits_t.reshape(N_EXPERTS, t // LANES, LANES),
                   jnp.broadcast_to(router_bias[l][:, None, None], (N_EXPERTS, 1, LANES)))

    ntile = (n * TOP_K) // SLOT_TILE + N_EXPERTS
    block_e, ntiles, ngroups, slot_tok, slot_w = _dispatch_plan(selw3, b, n, ntile)
    routed = _experts(block_e, ntiles, ngroups, slot_tok, slot_w, h2t,
                      w_e_gate[l].astype(BF16), w_e_up[l].astype(BF16), w_e_down[l].astype(BF16), ntile, n)

    return _final(h2, routed, x1, mod3, w_sh_gate[l].astype(BF16), w_sh_up[l].astype(BF16),
                  w_sh_down[l].astype(BF16), g_post_ffn[l][None, :])
```

```python
import functools
import math

import jax
import jax.numpy as jnp
import numpy as np
from jax import lax
from jax.experimental import pallas as pl
from jax.experimental.pallas import tpu as pltpu

F32 = jnp.float32
BF16 = jnp.bfloat16

D_MODEL = 1024
GRID_W = 64
N_HEADS = 4
QK_DIM = 64
V_DIM = 128
ATTN_WIDTH = N_HEADS * V_DIM
POOL_WIDTH = D_MODEL - ATTN_WIDTH
POOL_WINDOWS = (2, 4, 8, 16)
POOL_GROUP = POOL_WIDTH // len(POOL_WINDOWS)
AXIS_DIM = QK_DIM // 2
ROPE_THETA = 10000.0
N_EXPERTS = 64
TOP_K = 8
N_GROUPS = 8
GROUP_SIZE = N_EXPERTS // N_GROUPS
TOPK_GROUPS = 4
EXPERT_DIM = 256
ROUTED_SCALE = 2.5
EPS = 1e-6
LAM_INIT = 0.8 - 0.6 * math.exp(-0.3 * 0)

LANES = 128
SLOT_TILE = 256
HALO = 16
VMEM_LIMIT = 56 * 1024 * 1024


def _cparams(sem, vmem=VMEM_LIMIT):
    return pltpu.CompilerParams(dimension_semantics=sem, vmem_limit_bytes=vmem)


def _split(a):
    hi = a.astype(BF16)
    lo = (a - hi.astype(F32)).astype(BF16)
    return hi, lo


def _dot(a, b):
    return jnp.dot(a, b, preferred_element_type=F32)


def _dot_nt(a, b):
    return lax.dot_general(a, b, (((1,), (1,)), ((), ())), preferred_element_type=F32)


def _dot3(a, b, nt=False):
    d = _dot_nt if nt else _dot
    ah, al = _split(a)
    bh, bl = _split(b)
    return d(ah, bh) + d(ah, bl) + d(al, bh)


def _rms(x, g):
    return x * lax.rsqrt(jnp.mean(x * x, axis=-1, keepdims=True) + EPS) * g


def _silu(x):
    return x * (1.0 / (1.0 + jnp.exp(-x)))


def _ada_kernel(c_ref, w_ref, b_ref, o_ref):
    o_ref[...] = _dot3(_silu(c_ref[...]), w_ref[...]) + b_ref[...]


def _ada(cin, w_ada, b_ada):
    rows, d = cin.shape
    n = w_ada.shape[1]
    tn = 1024
    return pl.pallas_call(
        _ada_kernel,
        grid=(n // tn,),
        in_specs=[pl.BlockSpec((rows, d), lambda j: (0, 0)),
                  pl.BlockSpec((d, tn), lambda j: (0, j)),
                  pl.BlockSpec((1, tn), lambda j: (0, j))],
        out_specs=pl.BlockSpec((rows, tn), lambda j: (0, j)),
        out_shape=jax.ShapeDtypeStruct((rows, n), F32),
        compiler_params=_cparams(("arbitrary",)),
        name="ada",
    )(cin, w_ada, b_ada.reshape(1, n))


def _rope(t, cos, sa, sb):
    return t * cos + pltpu.roll(t, LANES - AXIS_DIM // 2, 1) * sa + pltpu.roll(t, AXIS_DIM // 2, 1) * sb


def _inproj_kernel(x_ref, mod_ref, g_ref, w_ref, cq_ref, ck_ref, sa_ref, sb_ref,
                   q_ref, k_ref, vt_ref, p_ref):
    x = x_ref[0]
    h = _rms(x, g_ref[...]) * (1.0 + mod_ref[0, 1:2, :]) + mod_ref[0, 0:1, :]
    px = _dot(h.astype(BF16), w_ref[...])
    ck = ck_ref[...]
    sa = sa_ref[...]
    sb = sb_ref[...]
    cq = cq_ref[...]
    scale = QK_DIM ** -0.5 * LOG2E
    saq = sa * scale
    sbq = sb * scale
    for hh in range(N_HEADS):
        lo = hh * LANES
        q = px[:, lo:lo + LANES]
        k = px[:, ATTN_WIDTH + lo:ATTN_WIDTH + lo + LANES]
        q_ref[0, :, lo:lo + LANES] = _rope(q, cq, saq, sbq).astype(BF16)
        k_ref[0, :, lo:lo + LANES] = _rope(k, ck, sa, sb).astype(BF16)
    vt_ref[0] = jnp.transpose(px[:, 2 * ATTN_WIDTH:3 * ATTN_WIDTH]).astype(BF16)
    p_ref[0] = px[:, 3 * ATTN_WIDTH:].astype(BF16)


def _inproj(x, mod3, g, w_in_bf, cq, ck, sa, sb, tm=512):
    b, n, d = x.shape
    width = w_in_bf.shape[1]
    row = lambda bi, i: (bi, i, 0)
    tab = pl.BlockSpec((tm, LANES), lambda bi, i: (i, 0))
    out = jax.ShapeDtypeStruct((b, n, ATTN_WIDTH), BF16)
    return pl.pallas_call(
        _inproj_kernel,
        grid=(b, n // tm),
        in_specs=[pl.BlockSpec((1, tm, d), row),
                  pl.BlockSpec((1, 6, d), lambda bi, i: (bi, 0, 0)),
                  pl.BlockSpec((1, d), lambda bi, i: (0, 0)),
                  pl.BlockSpec((d, width), lambda bi, i: (0, 0)),
                  tab, tab, tab, tab],
        out_specs=[pl.BlockSpec((1, tm, ATTN_WIDTH), row),
                   pl.BlockSpec((1, tm, ATTN_WIDTH), row),
                   pl.BlockSpec((1, ATTN_WIDTH, tm), lambda bi, i: (bi, 0, i)),
                   pl.BlockSpec((1, tm, ATTN_WIDTH), row)],
        out_shape=[out, out, jax.ShapeDtypeStruct((b, ATTN_WIDTH, n), BF16), out],
        compiler_params=_cparams(("arbitrary", "arbitrary")),
        name="inproj",
    )(x, mod3, g, w_in_bf, cq, ck, sa, sb)


def _ctxkv_kernel(x_ref, mod_ref, g_ref, wk_ref, wv_ref, k_ref, vt_ref):
    h = _rms(x_ref[0], g_ref[...]) * (1.0 + mod_ref[0, 1:2, :]) + mod_ref[0, 0:1, :]
    hb = h.astype(BF16)
    k_ref[0] = _dot(hb, wk_ref[...]).astype(BF16)
    vt_ref[0] = jnp.transpose(_dot(hb, wv_ref[...])).astype(BF16)


def _ctxkv(ctx, mod3, g, w_in_bf, ctx_row):
    b, n, d = ctx.shape
    out = jax.ShapeDtypeStruct((b, n, ATTN_WIDTH), BF16)
    return pl.pallas_call(
        _ctxkv_kernel,
        grid=(b,),
        in_specs=[pl.BlockSpec((1, n, d), lambda bi: (bi, 0, 0)),
                  pl.BlockSpec((1, 6, d), lambda bi: (ctx_row, 0, 0)),
                  pl.BlockSpec((1, d), lambda bi: (0, 0)),
                  pl.BlockSpec((d, ATTN_WIDTH), lambda bi: (0, 1)),
                  pl.BlockSpec((d, ATTN_WIDTH), lambda bi: (0, 2))],
        out_specs=[pl.BlockSpec((1, n, ATTN_WIDTH), lambda bi: (bi, 0, 0)),
                   pl.BlockSpec((1, ATTN_WIDTH, n), lambda bi: (bi, 0, 0))],
        out_shape=[out, jax.ShapeDtypeStruct((b, ATTN_WIDTH, n), BF16)],
        compiler_params=_cparams(("arbitrary",)),
        name="ctxkv",
    )(ctx, mod3, g, w_in_bf, w_in_bf)


LOG2E = math.log2(math.e)
KEY_CHUNK = 256


def _attn_step(q_ref, kx_ref, kc_ref, vt_ref, vct_ref, lam_ref, g_ref, o_ref, s_new, m_new, s_old, m_old):
    nc = kc_ref.shape[1]
    tq = q_ref.shape[1]
    nk = s_old.shape[1]
    groups = KEY_CHUNK // 8
    q = q_ref[0]
    lane = lax.broadcasted_iota(jnp.int32, q.shape, 1)
    zero = jnp.zeros_like(q)
    qms = (jnp.where(lane < QK_DIM, q, zero), jnp.where(lane >= QK_DIM, q, zero))
    m_prev = (m_old[0], m_old[1])
    top = [jnp.full((8, tq), -jnp.inf, F32) for _ in range(2)]
    den = [jnp.zeros((8, tq), F32) for _ in range(2)]
    acc = [jnp.zeros((V_DIM, tq), F32) for _ in range(2)]
    for c in range(nk // KEY_CHUNK):
        lo = c * KEY_CHUNK
        if lo < nc:
            keys, vt = kc_ref[0, lo:lo + KEY_CHUNK, :], vct_ref[0, :, lo:lo + KEY_CHUNK]
        else:
            keys, vt = kx_ref[0, lo - nc:lo - nc + KEY_CHUNK, :], vt_ref[0, :, lo - nc:lo - nc + KEY_CHUNK]
        for mp in range(2):
            s = _dot_nt(keys, qms[mp])
            s_new[mp, lo:lo + KEY_CHUNK, :] = s
            top[mp] = jnp.maximum(top[mp], jnp.max(s.reshape(groups, 8, tq), axis=0))
            p = jnp.exp2(s_old[mp, lo:lo + KEY_CHUNK, :] - m_prev[mp])
            den[mp] = den[mp] + jnp.sum(p.reshape(groups, 8, tq), axis=0)
            acc[mp] = acc[mp] + _dot(vt, p.astype(BF16))
    for mp in range(2):
        m_new[mp] = jnp.max(top[mp], axis=0, keepdims=True)
    lv = lam_ref[...]
    lam = (jnp.exp(jnp.sum(lv[0:1] * lv[1:2], axis=-1, keepdims=True))
           - jnp.exp(jnp.sum(lv[2:3] * lv[3:4], axis=-1, keepdims=True)) + LAM_INIT)
    l1 = jnp.sum(den[0], axis=0, keepdims=True)
    l2 = jnp.sum(den[1], axis=0, keepdims=True)
    o = acc[0] * (1.0 / l1) - acc[1] * (lam / l2)
    o = o * lax.rsqrt(jnp.mean(o * o, axis=0, keepdims=True) + EPS) * (g_ref[...] * (1.0 - LAM_INIT))
    o_ref[0] = jnp.transpose(o).astype(BF16)


def _attn_kernel(q_ref, kx_ref, kc_ref, vt_ref, vct_ref, lam_ref, g_ref, o_ref, s0_ref, m0_ref, s1_ref, m1_ref):
    t = pl.program_id(0)
    io = (q_ref, kx_ref, kc_ref, vt_ref, vct_ref, lam_ref, g_ref, o_ref)

    @pl.when(t == 0)
    def _():
        s1_ref[...] = jnp.zeros_like(s1_ref)
        m1_ref[...] = jnp.zeros_like(m1_ref)

    @pl.when(lax.rem(t, 2) == 0)
    def _():
        _attn_step(*io, s0_ref, m0_ref, s1_ref, m1_ref)

    @pl.when(lax.rem(t, 2) == 1)
    def _():
        _attn_step(*io, s1_ref, m1_ref, s0_ref, m0_ref)


def _attn(q, k, kc, vt, vct, lamv, g_col, tq=256):
    b, n, _ = q.shape
    nc = kc.shape[1]
    nq = n // tq
    total = b * N_HEADS * nq

    def cur(t):
        ta = jnp.minimum(t, total - 1)
        bh = ta // nq
        return bh // N_HEADS, bh % N_HEADS, ta % nq

    def prv(t):
        tb = jnp.maximum(t - 1, 0)
        bh = tb // nq
        return bh // N_HEADS, bh % N_HEADS, tb % nq

    def at(fn, order):
        return lambda t: tuple((fn(t) + (0,))[j] for j in order)

    return pl.pallas_call(
        _attn_kernel,
        grid=(total + 1,),
        in_specs=[pl.BlockSpec((1, tq, LANES), at(cur, (0, 2, 1))),
                  pl.BlockSpec((1, n, LANES), at(cur, (0, 3, 1))),
                  pl.BlockSpec((1, nc, LANES), at(cur, (0, 3, 1))),
                  pl.BlockSpec((1, V_DIM, n), at(prv, (0, 1, 3))),
                  pl.BlockSpec((1, V_DIM, nc), at(prv, (0, 1, 3))),
                  pl.BlockSpec((4, QK_DIM), lambda t: (0, 0)),
                  pl.BlockSpec((V_DIM, 1), lambda t: (0, 0))],
        out_specs=pl.BlockSpec((1, tq, LANES), at(prv, (0, 2, 1))),
        out_shape=jax.ShapeDtypeStruct((b, n, ATTN_WIDTH), BF16),
        scratch_shapes=[pltpu.VMEM((2, nc + n, tq), F32), pltpu.VMEM((2, 1, tq), F32),
                        pltpu.VMEM((2, nc + n, tq), F32), pltpu.VMEM((2, 1, tq), F32)],
        compiler_params=_cparams(("arbitrary",)),
        name="attn",
    )(q, k, kc, vt, vct, lamv, g_col)


def _outproj_kernel(att_ref, p_ref, pprev_ref, pnext_ref, x_ref, mod_ref, wpool_ref, bpool_ref,
                    pscale_ref, wout_ref, gpost_ref, gpre_ref, wr_ref,
                    x1_ref, h2_ref, h2t_ref, lg_ref, *, tm, n):
    i = pl.program_id(1)
    p = p_ref[0].astype(F32)
    prev = jnp.where(i > 0, pprev_ref[0].astype(F32), 0.0)
    nxt = jnp.where(i < pl.num_programs(1) - 1, pnext_ref[0].astype(F32), 0.0)
    ext = jnp.concatenate([prev, p, nxt], axis=0)
    rows = tm + 2 * HALO
    trow = (i * tm + lax.broadcasted_iota(jnp.int32, (tm, 1), 0))

    pooled = []
    sums = {}
    acc = ext + pltpu.roll(ext, 1, 0)
    sums[2] = acc
    w = 2
    while w < POOL_WINDOWS[-1]:
        acc = pltpu.roll(acc, rows - w // 2, 0) + pltpu.roll(acc, w // 2, 0)
        w = 2 * w
        sums[w] = acc
    for gi, win in enumerate(POOL_WINDOWS):
        sl = slice(gi * POOL_GROUP, (gi + 1) * POOL_GROUP)
        wsum = sums[win][HALO:HALO + tm, sl]
        hi_c = jnp.minimum(trow + (win - win // 2), n)
        lo_c = jnp.maximum(trow - win // 2, 0)
        cnt = (hi_c - lo_c).astype(F32)
        d = wsum / cnt - p[:, sl]
        y = _dot(d.astype(BF16), wpool_ref[gi]) + bpool_ref[:, sl]
        pooled.append((y * pscale_ref[:, sl]).astype(BF16))
    pool = jnp.concatenate(pooled, axis=1)
    yx = _dot(att_ref[0], wout_ref[0:ATTN_WIDTH, :]) + _dot(pool, wout_ref[ATTN_WIDTH:, :])
    x1 = x_ref[0] + mod_ref[0, 2:3, :] * _rms(yx, gpost_ref[...])
    x1_ref[0] = x1
    h2 = _rms(x1, gpre_ref[...]) * (1.0 + mod_ref[0, 4:5, :]) + mod_ref[0, 3:4, :]
    h2_ref[0] = h2.astype(BF16)
    for k in range(CHUNKS):
        h2t_ref[0, pl.ds(k, tm, stride=CHUNKS), :] = h2[:, k * LANES:(k + 1) * LANES]
    lg_ref[...] = _dot3(wr_ref[...], h2, nt=True)


def _outproj(att, p, x, mod3, w_pool_bf, b_pool, pool_scale, w_out_bf, g_post, g_pre, w_router_t, tm=256):
    b, n, d = x.shape
    hb = tm // HALO
    nhb = n // HALO
    row = lambda bi, i: (bi, i, 0)
    vec = lambda bi, i: (0, 0)
    kern = functools.partial(_outproj_kernel, tm=tm, n=n)
    return pl.pallas_call(
        kern,
        grid=(b, n // tm),
        in_specs=[pl.BlockSpec((1, tm, ATTN_WIDTH), row),
                  pl.BlockSpec((1, tm, POOL_WIDTH), row),
                  pl.BlockSpec((1, HALO, POOL_WIDTH), lambda bi, i: (bi, jnp.maximum(i * hb - 1, 0), 0)),
                  pl.BlockSpec((1, HALO, POOL_WIDTH), lambda bi, i: (bi, jnp.minimum((i + 1) * hb, nhb - 1), 0)),
                  pl.BlockSpec((1, tm, d), row),
                  pl.BlockSpec((1, 6, d), lambda bi, i: (bi, 0, 0)),
                  pl.BlockSpec((len(POOL_WINDOWS), POOL_GROUP, POOL_GROUP), lambda bi, i: (0, 0, 0)),
                  pl.BlockSpec((1, POOL_WIDTH), vec),
                  pl.BlockSpec((1, POOL_WIDTH), vec),
                  pl.BlockSpec((d, d), vec),
                  pl.BlockSpec((1, d), vec),
                  pl.BlockSpec((1, d), vec),
                  pl.BlockSpec((N_EXPERTS, d), vec)],
        out_specs=[pl.BlockSpec((1, tm, d), row),
                   pl.BlockSpec((1, tm, d), row),
                   pl.BlockSpec((1, tm * CHUNKS, LANES), row),
                   pl.BlockSpec((N_EXPERTS, tm), lambda bi, i: (0, bi * (n // tm) + i))],
        out_shape=[jax.ShapeDtypeStruct((b, n, d), F32),
                   jax.ShapeDtypeStruct((b, n, d), BF16),
                   jax.ShapeDtypeStruct((b, n * CHUNKS, LANES), F32),
                   jax.ShapeDtypeStruct((N_EXPERTS, b * n), F32)],
        compiler_params=_cparams(("arbitrary", "arbitrary")),
        name="outproj",
    )(att, p, p, p, x, mod3, w_pool_bf, b_pool, pool_scale, w_out_bf, g_post, g_pre, w_router_t)


def _beats(a, b, a_first):
    return jnp.where((a >= b) if a_first else (a > b), 1.0, 0.0)


def _route_kernel(lg_ref, bias_ref, w_ref):
    scores = [1.0 / (1.0 + jnp.exp(-lg_ref[e])) for e in range(N_EXPERTS)]
    biased = [scores[e] + bias_ref[e] for e in range(N_EXPERTS)]
    gscore = []
    for g in range(N_GROUPS):
        vals = biased[g * GROUP_SIZE:(g + 1) * GROUP_SIZE]
        m1 = vals[0]
        m2 = jnp.full_like(m1, -jnp.inf)
        for v in vals[1:]:
            m2 = jnp.maximum(m2, jnp.minimum(m1, v))
            m1 = jnp.maximum(m1, v)
        gscore.append(m1 + m2)
    gsel = []
    for g in range(N_GROUPS):
        rank = jnp.zeros_like(gscore[g])
        for g2 in range(N_GROUPS):
            if g2 != g:
                rank = rank + _beats(gscore[g2], gscore[g], g2 < g)
        gsel.append(rank < TOPK_GROUPS)
    masked = [jnp.where(gsel[e // GROUP_SIZE], biased[e], -jnp.inf) for e in range(N_EXPERTS)]
    picked = []
    for e in range(N_EXPERTS):
        rank = jnp.zeros_like(masked[e])
        for e2 in range(N_EXPERTS):
            if e2 != e:
                rank = rank + _beats(masked[e2], masked[e], e2 < e)
        picked.append(jnp.where(rank < TOP_K, scores[e], 0.0))
    denom = picked[0]
    for e in range(1, N_EXPERTS):
        denom = denom + picked[e]
    for e in range(N_EXPERTS):
        w_ref[e] = picked[e] / denom * ROUTED_SCALE


def _route(logits3, bias3, rows=8):
    e, r, l = logits3.shape
    return pl.pallas_call(
        _route_kernel,
        grid=(r // rows,),
        in_specs=[pl.BlockSpec((e, rows, l), lambda i: (0, i, 0)),
                  pl.BlockSpec((e, 1, l), lambda i: (0, 0, 0))],
        out_specs=pl.BlockSpec((e, rows, l), lambda i: (0, i, 0)),
        out_shape=jax.ShapeDtypeStruct((e, r, l), F32),
        compiler_params=_cparams(("arbitrary",)),
        name="route",
    )(logits3, bias3)


CHUNKS = D_MODEL // LANES
SLAB = SLOT_TILE + 8
ROW_BATCH = 16


def _experts_kernel(be_ref, nt_ref, ng_ref, row_ref, sw_ref, x_hbm, wg_ref, wu_ref, wd_ref, o_hbm,
                    xs_ref, acc_ref, tx_ref, ty_ref, sem, *, ntile, n):
    c = pl.program_id(0)
    j = pl.program_id(1)
    rows = n * CHUNKS

    @pl.when(j == 0)
    def _():
        cp = pltpu.make_async_copy(x_hbm.at[c], xs_ref.at[pl.ds(0, rows)], sem.at[0])
        cp.start()
        xs_ref[pl.ds(rows, CHUNKS), :] = jnp.zeros((CHUNKS, LANES), F32)
        acc_ref[...] = jnp.zeros_like(acc_ref)
        tx_ref[...] = jnp.zeros_like(tx_ref)
        cp.wait()

    @pl.when(j < nt_ref[c])
    def _():
        ngroups = ng_ref[c * ntile + j]

        def gather(gi, carry):
            for r in range(ROW_BATCH):
                s = gi * ROW_BATCH + r
                row = pl.multiple_of(row_ref[0, 0, s], CHUNKS)
                tx_ref[pl.ds(s, CHUNKS, stride=SLAB), :] = xs_ref[pl.ds(row, CHUNKS), :]
            return carry

        lax.fori_loop(0, ngroups, gather, 0)
        xb = jnp.concatenate([tx_ref[pl.ds(k * SLAB, SLOT_TILE), :] for k in range(CHUNKS)],
                             axis=1).astype(BF16)
        g = _dot(xb, wg_ref[0])
        u = _dot(xb, wu_ref[0])
        h = (_silu(g) * u).astype(BF16)
        y = _dot(h, wd_ref[0])
        for k in range(CHUNKS):
            ty_ref[pl.ds(k * SLAB, SLOT_TILE), :] = y[:, k * LANES:(k + 1) * LANES]

        def scatter(gi, carry):
            new = []
            for r in range(ROW_BATCH):
                s = gi * ROW_BATCH + r
                row = pl.multiple_of(row_ref[0, 0, s], CHUNKS)
                new.append((row, acc_ref[pl.ds(row, CHUNKS), :]
                            + sw_ref[0, 0, s] * ty_ref[pl.ds(s, CHUNKS, stride=SLAB), :]))
            for row, v in new:
                acc_ref[pl.ds(row, CHUNKS), :] = v
            return carry

        lax.fori_loop(0, ngroups, scatter, 0)

    @pl.when(j == ntile - 1)
    def _():
        cp = pltpu.make_async_copy(acc_ref.at[pl.ds(0, rows)], o_hbm.at[c], sem.at[1])
        cp.start()
        cp.wait()


def _experts(block_e, ntiles, ngroups, slot_tok, slot_w, h2t, wg_bf, wu_bf, wd_bf, ntile, n):
    b = h2t.shape[0]
    d, f = wg_bf.shape[1], wg_bf.shape[2]
    widx = lambda c, j, be, nt, ng: (be[c * ntile + j], 0, 0)
    sidx = lambda c, j, be, nt, ng: (c * ntile + j, 0, 0)
    kern = functools.partial(_experts_kernel, ntile=ntile, n=n)
    grid_spec = pltpu.PrefetchScalarGridSpec(
        num_scalar_prefetch=3,
        grid=(b, ntile),
        in_specs=[pl.BlockSpec((1, 1, SLOT_TILE), sidx, memory_space=pltpu.SMEM),
                  pl.BlockSpec((1, 1, SLOT_TILE), sidx, memory_space=pltpu.SMEM),
                  pl.BlockSpec(memory_space=pl.ANY),
                  pl.BlockSpec((1, d, f), widx),
                  pl.BlockSpec((1, d, f), widx),
                  pl.BlockSpec((1, f, d), widx)],
        out_specs=pl.BlockSpec(memory_space=pl.ANY),
        scratch_shapes=[pltpu.VMEM(((n + 1) * CHUNKS, LANES), F32),
                        pltpu.VMEM(((n + 1) * CHUNKS, LANES), F32),
                        pltpu.VMEM((CHUNKS * SLAB, LANES), F32),
                        pltpu.VMEM((CHUNKS * SLAB, LANES), F32),
                        pltpu.SemaphoreType.DMA((2,))],
    )
    return pl.pallas_call(
        kern,
        grid_spec=grid_spec,
        out_shape=jax.ShapeDtypeStruct((b, n * CHUNKS, LANES), F32),
        compiler_params=_cparams(("arbitrary", "arbitrary")),
        name="experts",
    )(block_e, ntiles, ngroups, slot_tok, slot_w, h2t, wg_bf, wu_bf, wd_bf)


EXPERTS_PER_LANE_GROUP = LANES // 32


def _plan_tables_kernel(w_ref, re_ref, ro_ref, int_ref, wt_ref, *, rows):
    ne = w_ref.shape[0]
    w2 = w_ref[...].reshape(ne * rows, LANES)
    selb = jnp.where(w2 > 0.0, 1.0, 0.0).astype(BF16)
    i0 = lax.broadcasted_iota(jnp.int32, (LANES, LANES), 0)
    i1 = lax.broadcasted_iota(jnp.int32, (LANES, LANES), 1)
    rowtot = _dot(selb, jnp.ones((LANES, LANES), BF16))
    ridx = lax.broadcasted_iota(jnp.int32, rowtot.shape, 0) & (rows - 1)
    acc = rowtot
    sh = 1
    while sh < rows:
        acc = acc + jnp.where(ridx >= sh, pltpu.roll(acc, sh, 0), 0.0)
        sh *= 2
    re_ref[0] = acc
    ro_ref[0] = acc - rowtot
    incl_t = _dot_nt(jnp.where(i1 <= i0, 1.0, 0.0).astype(BF16), selb)
    eye = jnp.where(i0 == i1, 1.0, 0.0).astype(BF16)
    w_hi = w2.astype(BF16)
    r1 = w2 - w_hi.astype(F32)
    w_mid = r1.astype(BF16)
    w_lo = (r1 - w_mid.astype(F32)).astype(BF16)
    parts = [_dot_nt(eye, p) for p in (w_hi, w_mid, w_lo)]
    for g in range(ne * rows // LANES):
        sl = slice(g * LANES, (g + 1) * LANES)
        int_ref[0, g] = incl_t[:, sl].astype(BF16)
        for k in range(3):
            wt_ref[0, k, g] = parts[k][:, sl].astype(BF16)


def _plan_tables(selw3, b):
    ne, r_all, _ = selw3.shape
    rows = r_all // b
    groups = ne * rows // LANES
    kern = functools.partial(_plan_tables_kernel, rows=rows)
    return pl.pallas_call(
        kern,
        grid=(b,),
        in_specs=[pl.BlockSpec((ne, rows, LANES), lambda c: (0, c, 0))],
        out_specs=[pl.BlockSpec((1, ne * rows, LANES), lambda c: (c, 0, 0)),
                   pl.BlockSpec((1, ne * rows, LANES), lambda c: (c, 0, 0)),
                   pl.BlockSpec((1, groups, LANES, LANES), lambda c: (c, 0, 0, 0)),
                   pl.BlockSpec((1, 3, groups, LANES, LANES), lambda c: (c, 0, 0, 0, 0))],
        out_shape=[jax.ShapeDtypeStruct((b, ne * rows, LANES), F32),
                   jax.ShapeDtypeStruct((b, ne * rows, LANES), F32),
                   jax.ShapeDtypeStruct((b, groups, LANES, LANES), BF16),
                   jax.ShapeDtypeStruct((b, 3, groups, LANES, LANES), BF16)],
        compiler_params=_cparams(("arbitrary",)),
        name="plan_tables",
    )(selw3)


def _plan_slots_kernel(be_ref, q0_ref, nt_ref, re_ref, ro_ref, int_ref, wt_ref, tok_ref, sw_ref,
                       *, ntile, rows, n):
    c = pl.program_id(0)
    tok_ref[...] = jnp.full(tok_ref.shape, n * CHUNKS, jnp.int32)
    sw_ref[...] = jnp.zeros(sw_ref.shape, F32)
    lane = lax.broadcasted_iota(jnp.int32, (1, SLOT_TILE), 1).astype(F32)
    sub_r = lax.broadcasted_iota(jnp.int32, (rows, SLOT_TILE), 0).astype(F32)
    sub_l = lax.broadcasted_iota(jnp.int32, (LANES, SLOT_TILE), 0).astype(F32)
    reps = SLOT_TILE // LANES

    def tile_body(j, carry):
        e = be_ref[c * ntile + j]
        q = q0_ref[c * ntile + j].astype(F32) + lane
        base = pl.multiple_of(e * rows, rows)
        row_end = jnp.concatenate([re_ref[0, pl.ds(base, rows), :]] * reps, axis=1)
        row_off = jnp.concatenate([ro_ref[0, pl.ds(base, rows), :]] * reps, axis=1)
        r = jnp.sum(jnp.where(row_end <= q, 1.0, 0.0), axis=0, keepdims=True)
        row_start = jnp.sum(jnp.where(sub_r == r, row_off, 0.0), axis=0, keepdims=True)
        g = lax.shift_right_logical(e, EXPERTS_PER_LANE_GROUP.bit_length() - 1)
        col = r + ((e & (EXPERTS_PER_LANE_GROUP - 1)) * rows).astype(F32)
        onehot = jnp.where(sub_l == col, 1.0, 0.0).astype(BF16)
        incl = _dot(int_ref[0, g], onehot)
        lane_idx = jnp.sum(jnp.where(incl <= q - row_start, 1.0, 0.0), axis=0, keepdims=True)
        wrow = (_dot(wt_ref[0, 0, g], onehot) + _dot(wt_ref[0, 1, g], onehot)) + _dot(wt_ref[0, 2, g], onehot)
        wsel = jnp.sum(jnp.where(sub_l == lane_idx, wrow, 0.0), axis=0, keepdims=True)
        valid = q < row_end[rows - 1:rows, :]
        tok = jnp.where(valid, r * LANES + lane_idx, float(n)).astype(jnp.int32)
        tok_ref[0, pl.ds(j, 1), :] = tok * CHUNKS
        sw_ref[0, pl.ds(j, 1), :] = jnp.where(valid, wsel, 0.0)
        return carry

    lax.fori_loop(0, nt_ref[c], tile_body, 0)


def _plan_slots(block_e, tile_q0, ntiles, row_end, row_off, incl_t, w_t, ntile, n):
    b, er, _ = row_end.shape
    rows = n // LANES
    groups = incl_t.shape[1]
    kern = functools.partial(_plan_slots_kernel, ntile=ntile, rows=rows, n=n)
    grid_spec = pltpu.PrefetchScalarGridSpec(
        num_scalar_prefetch=3,
        grid=(b,),
        in_specs=[pl.BlockSpec((1, er, LANES), lambda c, *_: (c, 0, 0)),
                  pl.BlockSpec((1, er, LANES), lambda c, *_: (c, 0, 0)),
                  pl.BlockSpec((1, groups, LANES, LANES), lambda c, *_: (c, 0, 0, 0)),
                  pl.BlockSpec((1, 3, groups, LANES, LANES), lambda c, *_: (c, 0, 0, 0, 0))],
        out_specs=[pl.BlockSpec((1, ntile, SLOT_TILE), lambda c, *_: (c, 0, 0)),
                   pl.BlockSpec((1, ntile, SLOT_TILE), lambda c, *_: (c, 0, 0))],
    )
    return pl.pallas_call(
        kern,
        grid_spec=grid_spec,
        out_shape=[jax.ShapeDtypeStruct((b, ntile, SLOT_TILE), jnp.int32),
                   jax.ShapeDtypeStruct((b, ntile, SLOT_TILE), F32)],
        compiler_params=_cparams(("arbitrary",)),
        name="plan_slots",
    )(block_e, tile_q0, ntiles, row_end, row_off, incl_t, w_t)


def _tile_meta(row_end, b, n, ntile):
    rows = n // LANES
    cnt = row_end.reshape(b, N_EXPERTS, rows, LANES)[:, :, rows - 1, 0].astype(jnp.int32)
    nt_e = (cnt + SLOT_TILE - 1) // SLOT_TILE
    tend = jnp.cumsum(nt_e, axis=1)
    tstart = tend - nt_e
    ntiles = tend[:, -1]
    tile_id = jnp.arange(ntile, dtype=jnp.int32)
    tid = jnp.minimum(tile_id[None, :], ntiles[:, None] - 1)
    be = jnp.minimum(jnp.sum(tend[:, None, :] <= tid[:, :, None], axis=2).astype(jnp.int32), N_EXPERTS - 1)
    q0 = (tid - jnp.take_along_axis(tstart, be, axis=1)) * SLOT_TILE
    used = jnp.clip(jnp.take_along_axis(cnt, be, axis=1) - q0, 0, SLOT_TILE)
    ngroups = (used + ROW_BATCH - 1) // ROW_BATCH
    return (be.reshape(-1), q0.reshape(-1).astype(jnp.int32), ntiles.astype(jnp.int32),
            ngroups.reshape(-1).astype(jnp.int32))


def _dispatch_plan(selw3, b, n, ntile):
    row_end, row_off, incl_t, w_t = _plan_tables(selw3, b)
    block_e, tile_q0, ntiles, ngroups = _tile_meta(row_end, b, n, ntile)
    slot_tok, slot_w = _plan_slots(block_e, tile_q0, ntiles, row_end, row_off, incl_t, w_t, ntile, n)
    return (block_e, ntiles, ngroups,
            slot_tok.reshape(b * ntile, 1, SLOT_TILE), slot_w.reshape(b * ntile, 1, SLOT_TILE))


def _final_kernel(h2_ref, r_ref, x1_ref, mod_ref, wg_ref, wu_ref, wd_ref, g_ref, o_ref):
    hb = h2_ref[0]
    tm = hb.shape[0]
    sh = _dot((_silu(_dot(hb, wg_ref[...])) * _dot(hb, wu_ref[...])).astype(BF16), wd_ref[...])
    routed = jnp.concatenate([r_ref[0, pl.ds(k, tm, stride=CHUNKS), :] for k in range(CHUNKS)], axis=1)
    moe = routed + sh
    o_ref[0] = x1_ref[0] + mod_ref[0, 5:6, :] * _rms(moe, g_ref[...])


def _final(h2, routed, x1, mod3, wsg_bf, wsu_bf, wsd_bf, g_post, tm=512):
    b, n, d = x1.shape
    f = wsg_bf.shape[1]
    row = lambda bi, i: (bi, i, 0)
    vec = lambda bi, i: (0, 0)
    return pl.pallas_call(
        _final_kernel,
        grid=(b, n // tm),
        in_specs=[pl.BlockSpec((1, tm, d), row),
                  pl.BlockSpec((1, tm * CHUNKS, LANES), row),
                  pl.BlockSpec((1, tm, d), row),
                  pl.BlockSpec((1, 6, d), lambda bi, i: (bi, 0, 0)),
                  pl.BlockSpec((d, f), vec),
                  pl.BlockSpec((d, f), vec),
                  pl.BlockSpec((f, d), vec),
                  pl.BlockSpec((1, d), vec)],
        out_specs=pl.BlockSpec((1, tm, d), row),
        out_shape=jax.ShapeDtypeStruct((b, n, d), F32),
        compiler_params=_cparams(("arbitrary", "arbitrary")),
        name="final",
    )(h2, routed, x1, mod3, wsg_bf, wsu_bf, wsd_bf, g_post)


def _rope_tables(n):
    t = np.arange(n)
    row = (t // GRID_W).astype(np.float32)
    col = (t % GRID_W).astype(np.float32)
    freqs = np.float32(ROPE_THETA) ** (-np.arange(0, AXIS_DIM, 2, dtype=np.float32) / np.float32(AXIS_DIM))
    ar = row[:, None] * freqs
    ac = col[:, None] * freqs
    zeros = np.zeros_like(ar)
    cos64 = np.concatenate([np.cos(ar), np.cos(ar), np.cos(ac), np.cos(ac)], axis=1)
    sa64 = np.concatenate([-np.sin(ar), zeros, -np.sin(ac), zeros], axis=1)
    sb64 = np.concatenate([zeros, np.sin(ar), zeros, np.sin(ac)], axis=1)
    two = lambda a: jnp.asarray(np.concatenate([a, a], axis=1).astype(np.float32))
    return two(cos64), two(sa64), two(sb64)


def kernel(x, c, ctx, c_ctx, w_ada, b_ada, g_pre_mix, g_post_mix, g_pre_ffn, g_post_ffn, w_in, lambda_q1, lambda_k1, lambda_q2, lambda_k2, g_subln, w_pool, b_pool, pool_scale, w_out, w_router, router_bias, w_e_gate, w_e_up, w_e_down, w_sh_gate, w_sh_up, w_sh_down):
    b, n, d = x.shape
    l = 0
    mod_rows = 8
    cin = jnp.concatenate([c, c_ctx[None, :], jnp.zeros((mod_rows - b - 1, d), F32)], axis=0)
    mod3 = _ada(cin, w_ada[l], b_ada[l]).reshape(mod_rows, 6, d)

    w_in_bf = w_in[l].astype(BF16)
    cos, sa, sb = _rope_tables(n)
    q, k, vt, p = _inproj(x, mod3, g_pre_mix[l][None, :], w_in_bf, cos * (QK_DIM ** -0.5 * LOG2E), cos, sa, sb)
    kc, vct = _ctxkv(ctx, mod3, g_pre_mix[l][None, :], w_in_bf, b)

    lamv = jnp.stack([lambda_q1[l], lambda_k1[l], lambda_q2[l], lambda_k2[l]], axis=0)
    att = _attn(q, k, kc, vt, vct, lamv, g_subln[l][:, None])

    x1, h2, h2t, logits_t = _outproj(
        att, p, x, mod3, w_pool[l].astype(BF16), b_pool[l][None, :], pool_scale[l][None, :],
        w_out[l].astype(BF16), g_post_mix[l][None, :], g_pre_ffn[l][None, :], w_router[l].T)

    t = b * n
    selw3 = _route(logits_t.reshape(N_EXPERTS, t // LANES, LANES),
                   jnp.broadcast_to(router_bias[l][:, None, None], (N_EXPERTS, 1, LANES)))

    ntile = (n * TOP_K) // SLOT_TILE + N_EXPERTS
    block_e, ntiles, ngroups, slot_tok, slot_w = _dispatch_plan(selw3, b, n, ntile)
    routed = _experts(block_e, ntiles, ngroups, slot_tok, slot_w, h2t,
                      w_e_gate[l].astype(BF16), w_e_up[l].astype(BF16), w_e_down[l].astype(BF16), ntile, n)

    return _final(h2, routed, x1, mod3, w_sh_gate[l].astype(BF16), w_sh_up[l].astype(BF16),
                  w_sh_down[l].astype(BF16), g_post_ffn[l][None, :])
```

```python
import functools
import math

import jax
import jax.numpy as jnp
import numpy as np
from jax import lax
from jax.experimental import pallas as pl
from jax.experimental.pallas import tpu as pltpu

F32 = jnp.float32
BF16 = jnp.bfloat16

D_MODEL = 1024
GRID_W = 64
N_HEADS = 4
QK_DIM = 64
V_DIM = 128
ATTN_WIDTH = N_HEADS * V_DIM
POOL_WIDTH = D_MODEL - ATTN_WIDTH
POOL_WINDOWS = (2, 4, 8, 16)
POOL_GROUP = POOL_WIDTH // len(POOL_WINDOWS)
AXIS_DIM = QK_DIM // 2
ROPE_THETA = 10000.0
N_EXPERTS = 64
TOP_K = 8
N_GROUPS = 8
GROUP_SIZE = N_EXPERTS // N_GROUPS
TOPK_GROUPS = 4
EXPERT_DIM = 256
ROUTED_SCALE = 2.5
EPS = 1e-6
LAM_INIT = 0.8 - 0.6 * math.exp(-0.3 * 0)

LANES = 128
SLOT_TILE = 256
HALO = 16
VMEM_LIMIT = 56 * 1024 * 1024


def _cparams(sem, vmem=VMEM_LIMIT):
    return pltpu.CompilerParams(dimension_semantics=sem, vmem_limit_bytes=vmem)


def _split(a):
    hi = a.astype(BF16)
    lo = (a - hi.astype(F32)).astype(BF16)
    return hi, lo


def _dot(a, b):
    return jnp.dot(a, b, preferred_element_type=F32)


def _dot_nt(a, b):
    return lax.dot_general(a, b, (((1,), (1,)), ((), ())), preferred_element_type=F32)


def _dot3(a, b, nt=False):
    d = _dot_nt if nt else _dot
    ah, al = _split(a)
    bh, bl = _split(b)
    return d(ah, bh) + d(ah, bl) + d(al, bh)


def _rms(x, g):
    return x * lax.rsqrt(jnp.mean(x * x, axis=-1, keepdims=True) + EPS) * g


def _silu(x):
    return x * (1.0 / (1.0 + jnp.exp(-x)))


def _ada_kernel(c_ref, w_ref, b_ref, o_ref):
    o_ref[...] = _dot3(_silu(c_ref[...]), w_ref[...]) + b_ref[...]


def _ada(cin, w_ada, b_ada):
    rows, d = cin.shape
    n = w_ada.shape[1]
    tn = 1024
    return pl.pallas_call(
        _ada_kernel,
        grid=(n // tn,),
        in_specs=[pl.BlockSpec((rows, d), lambda j: (0, 0)),
                  pl.BlockSpec((d, tn), lambda j: (0, j)),
                  pl.BlockSpec((1, tn), lambda j: (0, j))],
        out_specs=pl.BlockSpec((rows, tn), lambda j: (0, j)),
        out_shape=jax.ShapeDtypeStruct((rows, n), F32),
        compiler_params=_cparams(("arbitrary",)),
        name="ada",
    )(cin, w_ada, b_ada.reshape(1, n))


def _rope(t, cos, sa, sb):
    return t * cos + pltpu.roll(t, LANES - AXIS_DIM // 2, 1) * sa + pltpu.roll(t, AXIS_DIM // 2, 1) * sb


def _inproj_kernel(x_ref, mod_ref, g_ref, w_ref, cq_ref, ck_ref, sa_ref, sb_ref,
                   q_ref, k_ref, vt_ref, p_ref):
    x = x_ref[0]
    h = _rms(x, g_ref[...]) * (1.0 + mod_ref[0, 1:2, :]) + mod_ref[0, 0:1, :]
    px = _dot(h.astype(BF16), w_ref[...])
    ck = ck_ref[...]
    sa = sa_ref[...]
    sb = sb_ref[...]
    cq = cq_ref[...]
    scale = QK_DIM ** -0.5 * LOG2E
    saq = sa * scale
    sbq = sb * scale
    for hh in range(N_HEADS):
        lo = hh * LANES
        q = px[:, lo:lo + LANES]
        k = px[:, ATTN_WIDTH + lo:ATTN_WIDTH + lo + LANES]
        q_ref[0, :, lo:lo + LANES] = _rope(q, cq, saq, sbq).astype(BF16)
        k_ref[0, :, lo:lo + LANES] = _rope(k, ck, sa, sb).astype(BF16)
    vt_ref[0] = jnp.transpose(px[:, 2 * ATTN_WIDTH:3 * ATTN_WIDTH]).astype(BF16)
    p_ref[0] = px[:, 3 * ATTN_WIDTH:].astype(BF16)


def _inproj(x, mod3, g, w_in_bf, cq, ck, sa, sb, tm=512):
    b, n, d = x.shape
    width = w_in_bf.shape[1]
    row = lambda bi, i: (bi, i, 0)
    tab = pl.BlockSpec((tm, LANES), lambda bi, i: (i, 0))
    out = jax.ShapeDtypeStruct((b, n, ATTN_WIDTH), BF16)
    return pl.pallas_call(
        _inproj_kernel,
        grid=(b, n // tm),
        in_specs=[pl.BlockSpec((1, tm, d), row),
                  pl.BlockSpec((1, 6, d), lambda bi, i: (bi, 0, 0)),
                  pl.BlockSpec((1, d), lambda bi, i: (0, 0)),
                  pl.BlockSpec((d, width), lambda bi, i: (0, 0)),
                  tab, tab, tab, tab],
        out_specs=[pl.BlockSpec((1, tm, ATTN_WIDTH), row),
                   pl.BlockSpec((1, tm, ATTN_WIDTH), row),
                   pl.BlockSpec((1, ATTN_WIDTH, tm), lambda bi, i: (bi, 0, i)),
                   pl.BlockSpec((1, tm, ATTN_WIDTH), row)],
        out_shape=[out, out, jax.ShapeDtypeStruct((b, ATTN_WIDTH, n), BF16), out],
        compiler_params=_cparams(("arbitrary", "arbitrary")),
        name="inproj",
    )(x, mod3, g, w_in_bf, cq, ck, sa, sb)


def _ctxkv_kernel(x_ref, mod_ref, g_ref, wk_ref, wv_ref, k_ref, vt_ref):
    h = _rms(x_ref[0], g_ref[...]) * (1.0 + mod_ref[0, 1:2, :]) + mod_ref[0, 0:1, :]
    hb = h.astype(BF16)
    k_ref[0] = _dot(hb, wk_ref[...]).astype(BF16)
    vt_ref[0] = jnp.transpose(_dot(hb, wv_ref[...])).astype(BF16)


def _ctxkv(ctx, mod3, g, w_in_bf, ctx_row):
    b, n, d = ctx.shape
    out = jax.ShapeDtypeStruct((b, n, ATTN_WIDTH), BF16)
    return pl.pallas_call(
        _ctxkv_kernel,
        grid=(b,),
        in_specs=[pl.BlockSpec((1, n, d), lambda bi: (bi, 0, 0)),
                  pl.BlockSpec((1, 6, d), lambda bi: (ctx_row, 0, 0)),
                  pl.BlockSpec((1, d), lambda bi: (0, 0)),
                  pl.BlockSpec((d, ATTN_WIDTH), lambda bi: (0, 1)),
                  pl.BlockSpec((d, ATTN_WIDTH), lambda bi: (0, 2))],
        out_specs=[pl.BlockSpec((1, n, ATTN_WIDTH), lambda bi: (bi, 0, 0)),
                   pl.BlockSpec((1, ATTN_WIDTH, n), lambda bi: (bi, 0, 0))],
        out_shape=[out, jax.ShapeDtypeStruct((b, ATTN_WIDTH, n), BF16)],
        compiler_params=_cparams(("arbitrary",)),
        name="ctxkv",
    )(ctx, mod3, g, w_in_bf, w_in_bf)


LOG2E = math.log2(math.e)
KEY_CHUNK = 256


def _attn_step(q_ref, kx_ref, kc_ref, vt_ref, vct_ref, lam_ref, g_ref, o_ref, s_new, m_new, s_old, m_old):
    nc = kc_ref.shape[1]
    tq = q_ref.shape[1]
    nk = s_old.shape[1]
    groups = KEY_CHUNK // 8
    q = q_ref[0]
    lane = lax.broadcasted_iota(jnp.int32, q.shape, 1)
    zero = jnp.zeros_like(q)
    qms = (jnp.where(lane < QK_DIM, q, zero), jnp.where(lane >= QK_DIM, q, zero))
    m_prev = (m_old[0], m_old[1])
    top = [jnp.full((8, tq), -jnp.inf, F32) for _ in range(2)]
    den = [jnp.zeros((8, tq), F32) for _ in range(2)]
    acc = [jnp.zeros((V_DIM, tq), F32) for _ in range(2)]
    for c in range(nk // KEY_CHUNK):
        lo = c * KEY_CHUNK
        if lo < nc:
            keys, vt = kc_ref[0, lo:lo + KEY_CHUNK, :], vct_ref[0, :, lo:lo + KEY_CHUNK]
        else:
            keys, vt = kx_ref[0, lo - nc:lo - nc + KEY_CHUNK, :], vt_ref[0, :, lo - nc:lo - nc + KEY_CHUNK]
        for mp in range(2):
            s = _dot_nt(keys, qms[mp])
            s_new[mp, lo:lo + KEY_CHUNK, :] = s
            top[mp] = jnp.maximum(top[mp], jnp.max(s.reshape(groups, 8, tq), axis=0))
            p = jnp.exp2(s_old[mp, lo:lo + KEY_CHUNK, :] - m_prev[mp])
            den[mp] = den[mp] + jnp.sum(p.reshape(groups, 8, tq), axis=0)
            acc[mp] = acc[mp] + _dot(vt, p.astype(BF16))
    for mp in range(2):
        m_new[mp] = jnp.max(top[mp], axis=0, keepdims=True)
    lv = lam_ref[...]
    lam = (jnp.exp(jnp.sum(lv[0:1] * lv[1:2], axis=-1, keepdims=True))
           - jnp.exp(jnp.sum(lv[2:3] * lv[3:4], axis=-1, keepdims=True)) + LAM_INIT)
    l1 = jnp.sum(den[0], axis=0, keepdims=True)
    l2 = jnp.sum(den[1], axis=0, keepdims=True)
    o = acc[0] * (1.0 / l1) - acc[1] * (lam / l2)
    o = o * lax.rsqrt(jnp.mean(o * o, axis=0, keepdims=True) + EPS) * (g_ref[...] * (1.0 - LAM_INIT))
    o_ref[0] = jnp.transpose(o).astype(BF16)


def _attn_kernel(q_ref, kx_ref, kc_ref, vt_ref, vct_ref, lam_ref, g_ref, o_ref, s0_ref, m0_ref, s1_ref, m1_ref):
    t = pl.program_id(0)
    io = (q_ref, kx_ref, kc_ref, vt_ref, vct_ref, lam_ref, g_ref, o_ref)

    @pl.when(t == 0)
    def _():
        s1_ref[...] = jnp.zeros_like(s1_ref)
        m1_ref[...] = jnp.zeros_like(m1_ref)

    @pl.when(lax.rem(t, 2) == 0)
    def _():
        _attn_step(*io, s0_ref, m0_ref, s1_ref, m1_ref)

    @pl.when(lax.rem(t, 2) == 1)
    def _():
        _attn_step(*io, s1_ref, m1_ref, s0_ref, m0_ref)


def _attn(q, k, kc, vt, vct, lamv, g_col, tq=256):
    b, n, _ = q.shape
    nc = kc.shape[1]
    nq = n // tq
    total = b * N_HEADS * nq

    def cur(t):
        ta = jnp.minimum(t, total - 1)
        bh = ta // nq
        return bh // N_HEADS, bh % N_HEADS, ta % nq

    def prv(t):
        tb = jnp.maximum(t - 1, 0)
        bh = tb // nq
        return bh // N_HEADS, bh % N_HEADS, tb % nq

    def at(fn, order):
        return lambda t: tuple((fn(t) + (0,))[j] for j in order)

    return pl.pallas_call(
        _attn_kernel,
        grid=(total + 1,),
        in_specs=[pl.BlockSpec((1, tq, LANES), at(cur, (0, 2, 1))),
                  pl.BlockSpec((1, n, LANES), at(cur, (0, 3, 1))),
                  pl.BlockSpec((1, nc, LANES), at(cur, (0, 3, 1))),
                  pl.BlockSpec((1, V_DIM, n), at(prv, (0, 1, 3))),
                  pl.BlockSpec((1, V_DIM, nc), at(prv, (0, 1, 3))),
                  pl.BlockSpec((4, QK_DIM), lambda t: (0, 0)),
                  pl.BlockSpec((V_DIM, 1), lambda t: (0, 0))],
        out_specs=pl.BlockSpec((1, tq, LANES), at(prv, (0, 2, 1))),
        out_shape=jax.ShapeDtypeStruct((b, n, ATTN_WIDTH), BF16),
        scratch_shapes=[pltpu.VMEM((2, nc + n, tq), F32), pltpu.VMEM((2, 1, tq), F32),
                        pltpu.VMEM((2, nc + n, tq), F32), pltpu.VMEM((2, 1, tq), F32)],
        compiler_params=_cparams(("arbitrary",)),
        name="attn",
    )(q, k, kc, vt, vct, lamv, g_col)


def _outproj_kernel(att_ref, p_ref, pprev_ref, pnext_ref, x_ref, mod_ref, wpool_ref, bpool_ref,
                    pscale_ref, wout_ref, gpost_ref, gpre_ref, wr_ref,
                    x1_ref, h2_ref, h2t_ref, lg_ref, *, tm, n):
    i = pl.program_id(1)
    p = p_ref[0].astype(F32)
    prev = jnp.where(i > 0, pprev_ref[0].astype(F32), 0.0)
    nxt = jnp.where(i < pl.num_programs(1) - 1, pnext_ref[0].astype(F32), 0.0)
    ext = jnp.concatenate([prev, p, nxt], axis=0)
    rows = tm + 2 * HALO
    trow = (i * tm + lax.broadcasted_iota(jnp.int32, (tm, 1), 0))

    pooled = []
    sums = {}
    acc = ext + pltpu.roll(ext, 1, 0)
    sums[2] = acc
    w = 2
    while w < POOL_WINDOWS[-1]:
        acc = pltpu.roll(acc, rows - w // 2, 0) + pltpu.roll(acc, w // 2, 0)
        w = 2 * w
        sums[w] = acc
    for gi, win in enumerate(POOL_WINDOWS):
        sl = slice(gi * POOL_GROUP, (gi + 1) * POOL_GROUP)
        wsum = sums[win][HALO:HALO + tm, sl]
        hi_c = jnp.minimum(trow + (win - win // 2), n)
        lo_c = jnp.maximum(trow - win // 2, 0)
        cnt = (hi_c - lo_c).astype(F32)
        d = wsum / cnt - p[:, sl]
        y = _dot(d.astype(BF16), wpool_ref[gi]) + bpool_ref[:, sl]
        pooled.append((y * pscale_ref[:, sl]).astype(BF16))
    pool = jnp.concatenate(pooled, axis=1)
    yx = _dot(att_ref[0], wout_ref[0:ATTN_WIDTH, :]) + _dot(pool, wout_ref[ATTN_WIDTH:, :])
    x1 = x_ref[0] + mod_ref[0, 2:3, :] * _rms(yx, gpost_ref[...])
    x1_ref[0] = x1
    h2 = _rms(x1, gpre_ref[...]) * (1.0 + mod_ref[0, 4:5, :]) + mod_ref[0, 3:4, :]
    h2_ref[0] = h2.astype(BF16)
    for k in range(CHUNKS):
        h2t_ref[0, pl.ds(k, tm, stride=CHUNKS), :] = h2[:, k * LANES:(k + 1) * LANES]
    lg_ref[...] = _dot3(wr_ref[...], h2, nt=True)


def _outproj(att, p, x, mod3, w_pool_bf, b_pool, pool_scale, w_out_bf, g_post, g_pre, w_router_t, tm=256):
    b, n, d = x.shape
    hb = tm // HALO
    nhb = n // HALO
    row = lambda bi, i: (bi, i, 0)
    vec = lambda bi, i: (0, 0)
    kern = functools.partial(_outproj_kernel, tm=tm, n=n)
    return pl.pallas_call(
        kern,
        grid=(b, n // tm),
        in_specs=[pl.BlockSpec((1, tm, ATTN_WIDTH), row),
                  pl.BlockSpec((1, tm, POOL_WIDTH), row),
                  pl.BlockSpec((1, HALO, POOL_WIDTH), lambda bi, i: (bi, jnp.maximum(i * hb - 1, 0), 0)),
                  pl.BlockSpec((1, HALO, POOL_WIDTH), lambda bi, i: (bi, jnp.minimum((i + 1) * hb, nhb - 1), 0)),
                  pl.BlockSpec((1, tm, d), row),
                  pl.BlockSpec((1, 6, d), lambda bi, i: (bi, 0, 0)),
                  pl.BlockSpec((len(POOL_WINDOWS), POOL_GROUP, POOL_GROUP), lambda bi, i: (0, 0, 0)),
                  pl.BlockSpec((1, POOL_WIDTH), vec),
                  pl.BlockSpec((1, POOL_WIDTH), vec),
                  pl.BlockSpec((d, d), vec),
                  pl.BlockSpec((1, d), vec),
                  pl.BlockSpec((1, d), vec),
                  pl.BlockSpec((N_EXPERTS, d), vec)],
        out_specs=[pl.BlockSpec((1, tm, d), row),
                   pl.BlockSpec((1, tm, d), row),
                   pl.BlockSpec((1, tm * CHUNKS, LANES), row),
                   pl.BlockSpec((N_EXPERTS, tm), lambda bi, i: (0, bi * (n // tm) + i))],
        out_shape=[jax.ShapeDtypeStruct((b, n, d), F32),
                   jax.ShapeDtypeStruct((b, n, d), BF16),
                   jax.ShapeDtypeStruct((b, n * CHUNKS, LANES), F32),
                   jax.ShapeDtypeStruct((N_EXPERTS, b * n), F32)],
        compiler_params=_cparams(("arbitrary", "arbitrary")),
        name="outproj",
    )(att, p, p, p, x, mod3, w_pool_bf, b_pool, pool_scale, w_out_bf, g_post, g_pre, w_router_t)


def _beats(a, b, a_first):
    return jnp.where((a >= b) if a_first else (a > b), 1.0, 0.0)


def _route_kernel(lg_ref, bias_ref, w_ref):
    scores = [1.0 / (1.0 + jnp.exp(-lg_ref[e])) for e in range(N_EXPERTS)]
    biased = [scores[e] + bias_ref[e] for e in range(N_EXPERTS)]
    gscore = []
    for g in range(N_GROUPS):
        vals = biased[g * GROUP_SIZE:(g + 1) * GROUP_SIZE]
        m1 = vals[0]
        m2 = jnp.full_like(m1, -jnp.inf)
        for v in vals[1:]:
            m2 = jnp.maximum(m2, jnp.minimum(m1, v))
            m1 = jnp.maximum(m1, v)
        gscore.append(m1 + m2)
    gsel = []
    for g in range(N_GROUPS):
        rank = jnp.zeros_like(gscore[g])
        for g2 in range(N_GROUPS):
            if g2 != g:
                rank = rank + _beats(gscore[g2], gscore[g], g2 < g)
        gsel.append(rank < TOPK_GROUPS)
    masked = [jnp.where(gsel[e // GROUP_SIZE], biased[e], -jnp.inf) for e in range(N_EXPERTS)]
    picked = []
    for e in range(N_EXPERTS):
        rank = jnp.zeros_like(masked[e])
        for e2 in range(N_EXPERTS):
            if e2 != e:
                rank = rank + _beats(masked[e2], masked[e], e2 < e)
        picked.append(jnp.where(rank < TOP_K, scores[e], 0.0))
    denom = picked[0]
    for e in range(1, N_EXPERTS):
        denom = denom + picked[e]
    for e in range(N_EXPERTS):
        w_ref[e] = picked[e] / denom * ROUTED_SCALE


def _route(logits3, bias3, rows=8):
    e, r, l = logits3.shape
    return pl.pallas_call(
        _route_kernel,
        grid=(r // rows,),
        in_specs=[pl.BlockSpec((e, rows, l), lambda i: (0, i, 0)),
                  pl.BlockSpec((e, 1, l), lambda i: (0, 0, 0))],
        out_specs=pl.BlockSpec((e, rows, l), lambda i: (0, i, 0)),
        out_shape=jax.ShapeDtypeStruct((e, r, l), F32),
        compiler_params=_cparams(("arbitrary",)),
        name="route",
    )(logits3, bias3)


CHUNKS = D_MODEL // LANES
SLAB = SLOT_TILE + 4
ROW_BATCH = 16


def _gather_tile(row_ref, xs_ref, tx_ref):
    for s in range(SLOT_TILE):
        row = pl.multiple_of(row_ref[0, 0, s], CHUNKS)
        tx_ref[pl.ds(s, CHUNKS, stride=SLAB), :] = xs_ref[pl.ds(row, CHUNKS), :]


def _experts_step(row_nxt, row_prv, sw_ref, wg_ref, wu_ref, wd_ref, xs_ref, acc_ref,
                  tx_cur, tx_nxt, ty_cur, ty_prv):
    _gather_tile(row_nxt, xs_ref, tx_nxt)
    xb = jnp.concatenate([tx_cur[pl.ds(k * SLAB, SLOT_TILE), :] for k in range(CHUNKS)], axis=1).astype(BF16)
    g = _dot(xb, wg_ref[0])
    u = _dot(xb, wu_ref[0])
    h = (_silu(g) * u).astype(BF16)
    y = _dot(h, wd_ref[0])
    w_rows = jnp.transpose(jnp.broadcast_to(sw_ref[0], (LANES, SLOT_TILE)))
    for k in range(CHUNKS):
        ty_cur[pl.ds(k * SLAB, SLOT_TILE), :] = y[:, k * LANES:(k + 1) * LANES] * w_rows
    for s0 in range(0, SLOT_TILE, ROW_BATCH):
        new = []
        for s in range(s0, s0 + ROW_BATCH):
            row = pl.multiple_of(row_prv[0, 0, s], CHUNKS)
            new.append((row, acc_ref[pl.ds(row, CHUNKS), :] + ty_prv[pl.ds(s, CHUNKS, stride=SLAB), :]))
        for row, v in new:
            acc_ref[pl.ds(row, CHUNKS), :] = v


def _experts_kernel(be_ref, nt_ref, row_nxt, row_prv, sw_ref, x_hbm, wg_ref, wu_ref, wd_ref, o_hbm,
                    xs_ref, acc_ref, tx0_ref, tx1_ref, ty0_ref, ty1_ref, sem, *, ntile, n):
    c = pl.program_id(0)
    j = pl.program_id(1)
    rows = n * CHUNKS

    @pl.when(j == 0)
    def _():
        cp = pltpu.make_async_copy(x_hbm.at[c], xs_ref.at[pl.ds(0, rows)], sem.at[0])
        cp.start()
        xs_ref[pl.ds(rows, CHUNKS), :] = jnp.zeros((CHUNKS, LANES), F32)
        acc_ref[...] = jnp.zeros_like(acc_ref)
        ty1_ref[...] = jnp.zeros_like(ty1_ref)
        cp.wait()
        _gather_tile(row_prv, xs_ref, tx0_ref)

    live = j <= nt_ref[c]
    args = (row_nxt, row_prv, sw_ref, wg_ref, wu_ref, wd_ref, xs_ref, acc_ref)

    @pl.when(live & (lax.rem(j, 2) == 0))
    def _():
        _experts_step(*args, tx0_ref, tx1_ref, ty0_ref, ty1_ref)

    @pl.when(live & (lax.rem(j, 2) == 1))
    def _():
        _experts_step(*args, tx1_ref, tx0_ref, ty1_ref, ty0_ref)

    @pl.when(j == ntile - 1)
    def _():
        cp = pltpu.make_async_copy(acc_ref.at[pl.ds(0, rows)], o_hbm.at[c], sem.at[1])
        cp.start()
        cp.wait()


def _experts(block_e, ntiles, slot_row, slot_w, h2t, wg_bf, wu_bf, wd_bf, ntile, n):
    b = h2t.shape[0]
    d, f = wg_bf.shape[1], wg_bf.shape[2]
    widx = lambda c, j, be, nt: (be[c * ntile + j], 0, 0)
    cur = lambda c, j, be, nt: (c * ntile + j, 0, 0)
    nxt = lambda c, j, be, nt: (c * ntile + jnp.minimum(j + 1, ntile - 1), 0, 0)
    prv = lambda c, j, be, nt: (c * ntile + jnp.maximum(j - 1, 0), 0, 0)
    kern = functools.partial(_experts_kernel, ntile=ntile, n=n)
    slab = pltpu.VMEM((CHUNKS * SLAB, LANES), F32)
    grid_spec = pltpu.PrefetchScalarGridSpec(
        num_scalar_prefetch=2,
        grid=(b, ntile),
        in_specs=[pl.BlockSpec((1, 1, SLOT_TILE), nxt, memory_space=pltpu.SMEM),
                  pl.BlockSpec((1, 1, SLOT_TILE), prv, memory_space=pltpu.SMEM),
                  pl.BlockSpec((1, 1, SLOT_TILE), cur),
                  pl.BlockSpec(memory_space=pl.ANY),
                  pl.BlockSpec((1, d, f), widx),
                  pl.BlockSpec((1, d, f), widx),
                  pl.BlockSpec((1, f, d), widx)],
        out_specs=pl.BlockSpec(memory_space=pl.ANY),
        scratch_shapes=[pltpu.VMEM(((n + 1) * CHUNKS, LANES), F32),
                        pltpu.VMEM(((n + 1) * CHUNKS, LANES), F32),
                        slab, slab, slab, slab,
                        pltpu.SemaphoreType.DMA((2,))],
    )
    return pl.pallas_call(
        kern,
        grid_spec=grid_spec,
        out_shape=jax.ShapeDtypeStruct((b, n * CHUNKS, LANES), F32),
        compiler_params=_cparams(("arbitrary", "arbitrary")),
        name="experts",
    )(block_e, ntiles, slot_row, slot_row, slot_w, h2t, wg_bf, wu_bf, wd_bf)


EXPERTS_PER_LANE_GROUP = LANES // 32


def _plan_tables_kernel(w_ref, re_ref, ro_ref, int_ref, wt_ref, *, rows):
    ne = w_ref.shape[0]
    w2 = w_ref[...].reshape(ne * rows, LANES)
    selb = jnp.where(w2 > 0.0, 1.0, 0.0).astype(BF16)
    i0 = lax.broadcasted_iota(jnp.int32, (LANES, LANES), 0)
    i1 = lax.broadcasted_iota(jnp.int32, (LANES, LANES), 1)
    rowtot = _dot(selb, jnp.ones((LANES, LANES), BF16))
    ridx = lax.broadcasted_iota(jnp.int32, rowtot.shape, 0) & (rows - 1)
    acc = rowtot
    sh = 1
    while sh < rows:
        acc = acc + jnp.where(ridx >= sh, pltpu.roll(acc, sh, 0), 0.0)
        sh *= 2
    re_ref[0] = acc
    ro_ref[0] = acc - rowtot
    incl_t = _dot_nt(jnp.where(i1 <= i0, 1.0, 0.0).astype(BF16), selb)
    eye = jnp.where(i0 == i1, 1.0, 0.0).astype(BF16)
    w_hi = w2.astype(BF16)
    r1 = w2 - w_hi.astype(F32)
    w_mid = r1.astype(BF16)
    w_lo = (r1 - w_mid.astype(F32)).astype(BF16)
    parts = [_dot_nt(eye, p) for p in (w_hi, w_mid, w_lo)]
    for g in range(ne * rows // LANES):
        sl = slice(g * LANES, (g + 1) * LANES)
        int_ref[0, g] = incl_t[:, sl].astype(BF16)
        for k in range(3):
            wt_ref[0, k, g] = parts[k][:, sl].astype(BF16)


def _plan_tables(selw3, b):
    ne, r_all, _ = selw3.shape
    rows = r_all // b
    groups = ne * rows // LANES
    kern = functools.partial(_plan_tables_kernel, rows=rows)
    return pl.pallas_call(
        kern,
        grid=(b,),
        in_specs=[pl.BlockSpec((ne, rows, LANES), lambda c: (0, c, 0))],
        out_specs=[pl.BlockSpec((1, ne * rows, LANES), lambda c: (c, 0, 0)),
                   pl.BlockSpec((1, ne * rows, LANES), lambda c: (c, 0, 0)),
                   pl.BlockSpec((1, groups, LANES, LANES), lambda c: (c, 0, 0, 0)),
                   pl.BlockSpec((1, 3, groups, LANES, LANES), lambda c: (c, 0, 0, 0, 0))],
        out_shape=[jax.ShapeDtypeStruct((b, ne * rows, LANES), F32),
                   jax.ShapeDtypeStruct((b, ne * rows, LANES), F32),
                   jax.ShapeDtypeStruct((b, groups, LANES, LANES), BF16),
                   jax.ShapeDtypeStruct((b, 3, groups, LANES, LANES), BF16)],
        compiler_params=_cparams(("arbitrary",)),
        name="plan_tables",
    )(selw3)


def _plan_slots_kernel(be_ref, q0_ref, nt_ref, re_ref, ro_ref, int_ref, wt_ref, tok_ref, sw_ref,
                       *, ntile, rows, n):
    c = pl.program_id(0)
    tok_ref[...] = jnp.full(tok_ref.shape, n * CHUNKS, jnp.int32)
    sw_ref[...] = jnp.zeros(sw_ref.shape, F32)
    lane = lax.broadcasted_iota(jnp.int32, (1, SLOT_TILE), 1).astype(F32)
    sub_r = lax.broadcasted_iota(jnp.int32, (rows, SLOT_TILE), 0).astype(F32)
    sub_l = lax.broadcasted_iota(jnp.int32, (LANES, SLOT_TILE), 0).astype(F32)
    reps = SLOT_TILE // LANES

    def tile_body(j, carry):
        e = be_ref[c * ntile + j]
        q = q0_ref[c * ntile + j].astype(F32) + lane
        base = pl.multiple_of(e * rows, rows)
        row_end = jnp.concatenate([re_ref[0, pl.ds(base, rows), :]] * reps, axis=1)
        row_off = jnp.concatenate([ro_ref[0, pl.ds(base, rows), :]] * reps, axis=1)
        r = jnp.sum(jnp.where(row_end <= q, 1.0, 0.0), axis=0, keepdims=True)
        row_start = jnp.sum(jnp.where(sub_r == r, row_off, 0.0), axis=0, keepdims=True)
        g = lax.shift_right_logical(e, EXPERTS_PER_LANE_GROUP.bit_length() - 1)
        col = r + ((e & (EXPERTS_PER_LANE_GROUP - 1)) * rows).astype(F32)
        onehot = jnp.where(sub_l == col, 1.0, 0.0).astype(BF16)
        incl = _dot(int_ref[0, g], onehot)
        lane_idx = jnp.sum(jnp.where(incl <= q - row_start, 1.0, 0.0), axis=0, keepdims=True)
        wrow = (_dot(wt_ref[0, 0, g], onehot) + _dot(wt_ref[0, 1, g], onehot)) + _dot(wt_ref[0, 2, g], onehot)
        wsel = jnp.sum(jnp.where(sub_l == lane_idx, wrow, 0.0), axis=0, keepdims=True)
        valid = q < row_end[rows - 1:rows, :]
        tok = jnp.where(valid, r * LANES + lane_idx, float(n)).astype(jnp.int32)
        tok_ref[0, pl.ds(j, 1), :] = tok * CHUNKS
        sw_ref[0, pl.ds(j, 1), :] = jnp.where(valid, wsel, 0.0)
        return carry

    lax.fori_loop(0, nt_ref[c], tile_body, 0)


def _plan_slots(block_e, tile_q0, ntiles, row_end, row_off, incl_t, w_t, ntile, n):
    b, er, _ = row_end.shape
    rows = n // LANES
    groups = incl_t.shape[1]
    kern = functools.partial(_plan_slots_kernel, ntile=ntile, rows=rows, n=n)
    grid_spec = pltpu.PrefetchScalarGridSpec(
        num_scalar_prefetch=3,
        grid=(b,),
        in_specs=[pl.BlockSpec((1, er, LANES), lambda c, *_: (c, 0, 0)),
                  pl.BlockSpec((1, er, LANES), lambda c, *_: (c, 0, 0)),
                  pl.BlockSpec((1, groups, LANES, LANES), lambda c, *_: (c, 0, 0, 0)),
                  pl.BlockSpec((1, 3, groups, LANES, LANES), lambda c, *_: (c, 0, 0, 0, 0))],
        out_specs=[pl.BlockSpec((1, ntile, SLOT_TILE), lambda c, *_: (c, 0, 0)),
                   pl.BlockSpec((1, ntile, SLOT_TILE), lambda c, *_: (c, 0, 0))],
    )
    return pl.pallas_call(
        kern,
        grid_spec=grid_spec,
        out_shape=[jax.ShapeDtypeStruct((b, ntile, SLOT_TILE), jnp.int32),
                   jax.ShapeDtypeStruct((b, ntile, SLOT_TILE), F32)],
        compiler_params=_cparams(("arbitrary",)),
        name="plan_slots",
    )(block_e, tile_q0, ntiles, row_end, row_off, incl_t, w_t)


def _tile_meta(row_end, b, n, ntile):
    rows = n // LANES
    cnt = row_end.reshape(b, N_EXPERTS, rows, LANES)[:, :, rows - 1, 0].astype(jnp.int32)
    nt_e = (cnt + SLOT_TILE - 1) // SLOT_TILE
    tend = jnp.cumsum(nt_e, axis=1)
    tstart = tend - nt_e
    ntiles = tend[:, -1]
    tile_id = jnp.arange(ntile, dtype=jnp.int32)
    tid = jnp.minimum(tile_id[None, :], ntiles[:, None] - 1)
    be = jnp.minimum(jnp.sum(tend[:, None, :] <= tid[:, :, None], axis=2).astype(jnp.int32), N_EXPERTS - 1)
    q0 = (tid - jnp.take_along_axis(tstart, be, axis=1)) * SLOT_TILE
    return be.reshape(-1), q0.reshape(-1).astype(jnp.int32), ntiles.astype(jnp.int32)


def _dispatch_plan(selw3, b, n, ntile):
    row_end, row_off, incl_t, w_t = _plan_tables(selw3, b)
    block_e, tile_q0, ntiles = _tile_meta(row_end, b, n, ntile)
    slot_tok, slot_w = _plan_slots(block_e, tile_q0, ntiles, row_end, row_off, incl_t, w_t, ntile, n)
    return (block_e, ntiles,
            slot_tok.reshape(b * ntile, 1, SLOT_TILE), slot_w.reshape(b * ntile, 1, SLOT_TILE))


def _final_kernel(h2_ref, r_ref, x1_ref, mod_ref, wg_ref, wu_ref, wd_ref, g_ref, o_ref):
    hb = h2_ref[0]
    tm = hb.shape[0]
    sh = _dot((_silu(_dot(hb, wg_ref[...])) * _dot(hb, wu_ref[...])).astype(BF16), wd_ref[...])
    routed = jnp.concatenate([r_ref[0, pl.ds(k, tm, stride=CHUNKS), :] for k in range(CHUNKS)], axis=1)
    moe = routed + sh
    o_ref[0] = x1_ref[0] + mod_ref[0, 5:6, :] * _rms(moe, g_ref[...])


def _final(h2, routed, x1, mod3, wsg_bf, wsu_bf, wsd_bf, g_post, tm=512):
    b, n, d = x1.shape
    f = wsg_bf.shape[1]
    row = lambda bi, i: (bi, i, 0)
    vec = lambda bi, i: (0, 0)
    return pl.pallas_call(
        _final_kernel,
        grid=(b, n // tm),
        in_specs=[pl.BlockSpec((1, tm, d), row),
                  pl.BlockSpec((1, tm * CHUNKS, LANES), row),
                  pl.BlockSpec((1, tm, d), row),
                  pl.BlockSpec((1, 6, d), lambda bi, i: (bi, 0, 0)),
                  pl.BlockSpec((d, f), vec),
                  pl.BlockSpec((d, f), vec),
                  pl.BlockSpec((f, d), vec),
                  pl.BlockSpec((1, d), vec)],
        out_specs=pl.BlockSpec((1, tm, d), row),
        out_shape=jax.ShapeDtypeStruct((b, n, d), F32),
        compiler_params=_cparams(("arbitrary", "arbitrary")),
        name="final",
    )(h2, routed, x1, mod3, wsg_bf, wsu_bf, wsd_bf, g_post)


def _rope_tables(n):
    t = np.arange(n)
    row = (t // GRID_W).astype(np.float32)
    col = (t % GRID_W).astype(np.float32)
    freqs = np.float32(ROPE_THETA) ** (-np.arange(0, AXIS_DIM, 2, dtype=np.float32) / np.float32(AXIS_DIM))
    ar = row[:, None] * freqs
    ac = col[:, None] * freqs
    zeros = np.zeros_like(ar)
    cos64 = np.concatenate([np.cos(ar), np.cos(ar), np.cos(ac), np.cos(ac)], axis=1)
    sa64 = np.concatenate([-np.sin(ar), zeros, -np.sin(ac), zeros], axis=1)
    sb64 = np.concatenate([zeros, np.sin(ar), zeros, np.sin(ac)], axis=1)
    two = lambda a: jnp.asarray(np.concatenate([a, a], axis=1).astype(np.float32))
    return two(cos64), two(sa64), two(sb64)


def kernel(x, c, ctx, c_ctx, w_ada, b_ada, g_pre_mix, g_post_mix, g_pre_ffn, g_post_ffn, w_in, lambda_q1, lambda_k1, lambda_q2, lambda_k2, g_subln, w_pool, b_pool, pool_scale, w_out, w_router, router_bias, w_e_gate, w_e_up, w_e_down, w_sh_gate, w_sh_up, w_sh_down):
    b, n, d = x.shape
    l = 0
    mod_rows = 8
    cin = jnp.concatenate([c, c_ctx[None, :], jnp.zeros((mod_rows - b - 1, d), F32)], axis=0)
    mod3 = _ada(cin, w_ada[l], b_ada[l]).reshape(mod_rows, 6, d)

    w_in_bf = w_in[l].astype(BF16)
    cos, sa, sb = _rope_tables(n)
    q, k, vt, p = _inproj(x, mod3, g_pre_mix[l][None, :], w_in_bf, cos * (QK_DIM ** -0.5 * LOG2E), cos, sa, sb)
    kc, vct = _ctxkv(ctx, mod3, g_pre_mix[l][None, :], w_in_bf, b)

    lamv = jnp.stack([lambda_q1[l], lambda_k1[l], lambda_q2[l], lambda_k2[l]], axis=0)
    att = _attn(q, k, kc, vt, vct, lamv, g_subln[l][:, None])

    x1, h2, h2t, logits_t = _outproj(
        att, p, x, mod3, w_pool[l].astype(BF16), b_pool[l][None, :], pool_scale[l][None, :],
        w_out[l].astype(BF16), g_post_mix[l][None, :], g_pre_ffn[l][None, :], w_router[l].T)

    t = b * n
    selw3 = _route(logits_t.reshape(N_EXPERTS, t // LANES, LANES),
                   jnp.broadcast_to(router_bias[l][:, None, None], (N_EXPERTS, 1, LANES)))

    ntile = (n * TOP_K) // SLOT_TILE + N_EXPERTS
    block_e, ntiles, slot_row, slot_w = _dispatch_plan(selw3, b, n, ntile)
    routed = _experts(block_e, ntiles, slot_row, slot_w, h2t,
                      w_e_gate[l].astype(BF16), w_e_up[l].astype(BF16), w_e_down[l].astype(BF16), ntile, n)

    return _final(h2, routed, x1, mod3, w_sh_gate[l].astype(BF16), w_sh_up[l].astype(BF16),
                  w_sh_down[l].astype(BF16), g_post_ffn[l][None, :])
```

```python
import functools
import math

import jax
import jax.numpy as jnp
import numpy as np
from jax import lax
from jax.experimental import pallas as pl
from jax.experimental.pallas import tpu as pltpu

F32 = jnp.float32
BF16 = jnp.bfloat16

D_MODEL = 1024
GRID_W = 64
N_HEADS = 4
QK_DIM = 64
V_DIM = 128
ATTN_WIDTH = N_HEADS * V_DIM
POOL_WIDTH = D_MODEL - ATTN_WIDTH
POOL_WINDOWS = (2, 4, 8, 16)
POOL_GROUP = POOL_WIDTH // len(POOL_WINDOWS)
AXIS_DIM = QK_DIM // 2
ROPE_THETA = 10000.0
N_EXPERTS = 64
TOP_K = 8
N_GROUPS = 8
GROUP_SIZE = N_EXPERTS // N_GROUPS
TOPK_GROUPS = 4
EXPERT_DIM = 256
ROUTED_SCALE = 2.5
EPS = 1e-6
LAM_INIT = 0.8 - 0.6 * math.exp(-0.3 * 0)

LANES = 128
SLOT_TILE = 256
HALO = 16
VMEM_LIMIT = 56 * 1024 * 1024


def _cparams(sem, vmem=VMEM_LIMIT):
    return pltpu.CompilerParams(dimension_semantics=sem, vmem_limit_bytes=vmem)


def _split(a):
    hi = a.astype(BF16)
    lo = (a - hi.astype(F32)).astype(BF16)
    return hi, lo


def _dot(a, b):
    return jnp.dot(a, b, preferred_element_type=F32)


def _dot_nt(a, b):
    return lax.dot_general(a, b, (((1,), (1,)), ((), ())), preferred_element_type=F32)


def _dot3(a, b, nt=False):
    d = _dot_nt if nt else _dot
    ah, al = _split(a)
    bh, bl = _split(b)
    return d(ah, bh) + d(ah, bl) + d(al, bh)


def _rms(x, g):
    return x * lax.rsqrt(jnp.mean(x * x, axis=-1, keepdims=True) + EPS) * g


def _silu(x):
    return x * (1.0 / (1.0 + jnp.exp(-x)))


def _ada_kernel(c_ref, w_ref, b_ref, o_ref):
    o_ref[...] = _dot3(_silu(c_ref[...]), w_ref[...]) + b_ref[...]


def _ada(cin, w_ada, b_ada):
    rows, d = cin.shape
    n = w_ada.shape[1]
    tn = 1024
    return pl.pallas_call(
        _ada_kernel,
        grid=(n // tn,),
        in_specs=[pl.BlockSpec((rows, d), lambda j: (0, 0)),
                  pl.BlockSpec((d, tn), lambda j: (0, j)),
                  pl.BlockSpec((1, tn), lambda j: (0, j))],
        out_specs=pl.BlockSpec((rows, tn), lambda j: (0, j)),
        out_shape=jax.ShapeDtypeStruct((rows, n), F32),
        compiler_params=_cparams(("arbitrary",)),
        name="ada",
    )(cin, w_ada, b_ada.reshape(1, n))


def _rope(t, cos, sa, sb):
    return t * cos + pltpu.roll(t, LANES - AXIS_DIM // 2, 1) * sa + pltpu.roll(t, AXIS_DIM // 2, 1) * sb


def _inproj_kernel(x_ref, mod_ref, g_ref, w_ref, cq_ref, ck_ref, sa_ref, sb_ref,
                   q_ref, k_ref, vt_ref, p_ref):
    x = x_ref[0]
    h = _rms(x, g_ref[...]) * (1.0 + mod_ref[0, 1:2, :]) + mod_ref[0, 0:1, :]
    px = _dot(h.astype(BF16), w_ref[...])
    ck = ck_ref[...]
    sa = sa_ref[...]
    sb = sb_ref[...]
    cq = cq_ref[...]
    scale = QK_DIM ** -0.5 * LOG2E
    saq = sa * scale
    sbq = sb * scale
    for hh in range(N_HEADS):
        lo = hh * LANES
        q = px[:, lo:lo + LANES]
        k = px[:, ATTN_WIDTH + lo:ATTN_WIDTH + lo + LANES]
        q_ref[0, :, lo:lo + LANES] = _rope(q, cq, saq, sbq).astype(BF16)
        k_ref[0, :, lo:lo + LANES] = _rope(k, ck, sa, sb).astype(BF16)
    vt_ref[0] = jnp.transpose(px[:, 2 * ATTN_WIDTH:3 * ATTN_WIDTH]).astype(BF16)
    p_ref[0] = px[:, 3 * ATTN_WIDTH:].astype(BF16)


def _inproj(x, mod3, g, w_in_bf, cq, ck, sa, sb, tm=512):
    b, n, d = x.shape
    width = w_in_bf.shape[1]
    row = lambda bi, i: (bi, i, 0)
    tab = pl.BlockSpec((tm, LANES), lambda bi, i: (i, 0))
    out = jax.ShapeDtypeStruct((b, n, ATTN_WIDTH), BF16)
    return pl.pallas_call(
        _inproj_kernel,
        grid=(b, n // tm),
        in_specs=[pl.BlockSpec((1, tm, d), row),
                  pl.BlockSpec((1, 6, d), lambda bi, i: (bi, 0, 0)),
                  pl.BlockSpec((1, d), lambda bi, i: (0, 0)),
                  pl.BlockSpec((d, width), lambda bi, i: (0, 0)),
                  tab, tab, tab, tab],
        out_specs=[pl.BlockSpec((1, tm, ATTN_WIDTH), row),
                   pl.BlockSpec((1, tm, ATTN_WIDTH), row),
                   pl.BlockSpec((1, ATTN_WIDTH, tm), lambda bi, i: (bi, 0, i)),
                   pl.BlockSpec((1, tm, ATTN_WIDTH), row)],
        out_shape=[out, out, jax.ShapeDtypeStruct((b, ATTN_WIDTH, n), BF16), out],
        compiler_params=_cparams(("arbitrary", "arbitrary")),
        name="inproj",
    )(x, mod3, g, w_in_bf, cq, ck, sa, sb)


def _ctxkv_kernel(x_ref, mod_ref, g_ref, wk_ref, wv_ref, k_ref, vt_ref):
    h = _rms(x_ref[0], g_ref[...]) * (1.0 + mod_ref[0, 1:2, :]) + mod_ref[0, 0:1, :]
    hb = h.astype(BF16)
    k_ref[0] = _dot(hb, wk_ref[...]).astype(BF16)
    vt_ref[0] = jnp.transpose(_dot(hb, wv_ref[...])).astype(BF16)


def _ctxkv(ctx, mod3, g, w_in_bf, ctx_row):
    b, n, d = ctx.shape
    out = jax.ShapeDtypeStruct((b, n, ATTN_WIDTH), BF16)
    return pl.pallas_call(
        _ctxkv_kernel,
        grid=(b,),
        in_specs=[pl.BlockSpec((1, n, d), lambda bi: (bi, 0, 0)),
                  pl.BlockSpec((1, 6, d), lambda bi: (ctx_row, 0, 0)),
                  pl.BlockSpec((1, d), lambda bi: (0, 0)),
                  pl.BlockSpec((d, ATTN_WIDTH), lambda bi: (0, 1)),
                  pl.BlockSpec((d, ATTN_WIDTH), lambda bi: (0, 2))],
        out_specs=[pl.BlockSpec((1, n, ATTN_WIDTH), lambda bi: (bi, 0, 0)),
                   pl.BlockSpec((1, ATTN_WIDTH, n), lambda bi: (bi, 0, 0))],
        out_shape=[out, jax.ShapeDtypeStruct((b, ATTN_WIDTH, n), BF16)],
        compiler_params=_cparams(("arbitrary",)),
        name="ctxkv",
    )(ctx, mod3, g, w_in_bf, w_in_bf)


LOG2E = math.log2(math.e)
KEY_CHUNK = 256


def _attn_step(q_ref, kx_ref, kc_ref, vt_ref, vct_ref, lam_ref, g_ref, o_ref, s_new, m_new, s_old, m_old):
    nc = kc_ref.shape[1]
    tq = q_ref.shape[1]
    nk = s_old.shape[1]
    groups = KEY_CHUNK // 8
    q = q_ref[0]
    lane = lax.broadcasted_iota(jnp.int32, q.shape, 1)
    zero = jnp.zeros_like(q)
    qms = (jnp.where(lane < QK_DIM, q, zero), jnp.where(lane >= QK_DIM, q, zero))
    m_prev = (m_old[0], m_old[1])
    top = [jnp.full((8, tq), -jnp.inf, F32) for _ in range(2)]
    den = [jnp.zeros((8, tq), F32) for _ in range(2)]
    acc = [jnp.zeros((V_DIM, tq), F32) for _ in range(2)]
    for c in range(nk // KEY_CHUNK):
        lo = c * KEY_CHUNK
        if lo < nc:
            keys, vt = kc_ref[0, lo:lo + KEY_CHUNK, :], vct_ref[0, :, lo:lo + KEY_CHUNK]
        else:
            keys, vt = kx_ref[0, lo - nc:lo - nc + KEY_CHUNK, :], vt_ref[0, :, lo - nc:lo - nc + KEY_CHUNK]
        for mp in range(2):
            s = _dot_nt(keys, qms[mp])
            s_new[mp, lo:lo + KEY_CHUNK, :] = s
            top[mp] = jnp.maximum(top[mp], jnp.max(s.reshape(groups, 8, tq), axis=0))
            p = jnp.exp2(s_old[mp, lo:lo + KEY_CHUNK, :] - m_prev[mp])
            den[mp] = den[mp] + jnp.sum(p.reshape(groups, 8, tq), axis=0)
            acc[mp] = acc[mp] + _dot(vt, p.astype(BF16))
    for mp in range(2):
        m_new[mp] = jnp.max(top[mp], axis=0, keepdims=True)
    lv = lam_ref[...]
    lam = (jnp.exp(jnp.sum(lv[0:1] * lv[1:2], axis=-1, keepdims=True))
           - jnp.exp(jnp.sum(lv[2:3] * lv[3:4], axis=-1, keepdims=True)) + LAM_INIT)
    l1 = jnp.sum(den[0], axis=0, keepdims=True)
    l2 = jnp.sum(den[1], axis=0, keepdims=True)
    o = acc[0] * (1.0 / l1) - acc[1] * (lam / l2)
    o = o * lax.rsqrt(jnp.mean(o * o, axis=0, keepdims=True) + EPS) * (g_ref[...] * (1.0 - LAM_INIT))
    o_ref[0] = jnp.transpose(o).astype(BF16)


def _attn_kernel(q_ref, kx_ref, kc_ref, vt_ref, vct_ref, lam_ref, g_ref, o_ref, s0_ref, m0_ref, s1_ref, m1_ref):
    t = pl.program_id(0)
    io = (q_ref, kx_ref, kc_ref, vt_ref, vct_ref, lam_ref, g_ref, o_ref)

    @pl.when(t == 0)
    def _():
        s1_ref[...] = jnp.zeros_like(s1_ref)
        m1_ref[...] = jnp.zeros_like(m1_ref)

    @pl.when(lax.rem(t, 2) == 0)
    def _():
        _attn_step(*io, s0_ref, m0_ref, s1_ref, m1_ref)

    @pl.when(lax.rem(t, 2) == 1)
    def _():
        _attn_step(*io, s1_ref, m1_ref, s0_ref, m0_ref)


def _attn(q, k, kc, vt, vct, lamv, g_col, tq=256):
    b, n, _ = q.shape
    nc = kc.shape[1]
    nq = n // tq
    total = b * N_HEADS * nq

    def cur(t):
        ta = jnp.minimum(t, total - 1)
        bh = ta // nq
        return bh // N_HEADS, bh % N_HEADS, ta % nq

    def prv(t):
        tb = jnp.maximum(t - 1, 0)
        bh = tb // nq
        return bh // N_HEADS, bh % N_HEADS, tb % nq

    def at(fn, order):
        return lambda t: tuple((fn(t) + (0,))[j] for j in order)

    return pl.pallas_call(
        _attn_kernel,
        grid=(total + 1,),
        in_specs=[pl.BlockSpec((1, tq, LANES), at(cur, (0, 2, 1))),
                  pl.BlockSpec((1, n, LANES), at(cur, (0, 3, 1))),
                  pl.BlockSpec((1, nc, LANES), at(cur, (0, 3, 1))),
                  pl.BlockSpec((1, V_DIM, n), at(prv, (0, 1, 3))),
                  pl.BlockSpec((1, V_DIM, nc), at(prv, (0, 1, 3))),
                  pl.BlockSpec((4, QK_DIM), lambda t: (0, 0)),
                  pl.BlockSpec((V_DIM, 1), lambda t: (0, 0))],
        out_specs=pl.BlockSpec((1, tq, LANES), at(prv, (0, 2, 1))),
        out_shape=jax.ShapeDtypeStruct((b, n, ATTN_WIDTH), BF16),
        scratch_shapes=[pltpu.VMEM((2, nc + n, tq), F32), pltpu.VMEM((2, 1, tq), F32),
                        pltpu.VMEM((2, nc + n, tq), F32), pltpu.VMEM((2, 1, tq), F32)],
        compiler_params=_cparams(("arbitrary",)),
        name="attn",
    )(q, k, kc, vt, vct, lamv, g_col)


def _outproj_kernel(att_ref, p_ref, pprev_ref, pnext_ref, x_ref, mod_ref, wpool_ref, bpool_ref,
                    pscale_ref, wout_ref, gpost_ref, gpre_ref, wr_ref,
                    x1_ref, h2_ref, h2t_ref, lg_ref, *, tm, n):
    i = pl.program_id(1)
    p = p_ref[0].astype(F32)
    prev = jnp.where(i > 0, pprev_ref[0].astype(F32), 0.0)
    nxt = jnp.where(i < pl.num_programs(1) - 1, pnext_ref[0].astype(F32), 0.0)
    ext = jnp.concatenate([prev, p, nxt], axis=0)
    rows = tm + 2 * HALO
    trow = (i * tm + lax.broadcasted_iota(jnp.int32, (tm, 1), 0))

    pooled = []
    sums = {}
    acc = ext + pltpu.roll(ext, 1, 0)
    sums[2] = acc
    w = 2
    while w < POOL_WINDOWS[-1]:
        acc = pltpu.roll(acc, rows - w // 2, 0) + pltpu.roll(acc, w // 2, 0)
        w = 2 * w
        sums[w] = acc
    for gi, win in enumerate(POOL_WINDOWS):
        sl = slice(gi * POOL_GROUP, (gi + 1) * POOL_GROUP)
        wsum = sums[win][HALO:HALO + tm, sl]
        hi_c = jnp.minimum(trow + (win - win // 2), n)
        lo_c = jnp.maximum(trow - win // 2, 0)
        cnt = (hi_c - lo_c).astype(F32)
        d = wsum / cnt - p[:, sl]
        y = _dot(d.astype(BF16), wpool_ref[gi]) + bpool_ref[:, sl]
        pooled.append((y * pscale_ref[:, sl]).astype(BF16))
    pool = jnp.concatenate(pooled, axis=1)
    yx = _dot(att_ref[0], wout_ref[0:ATTN_WIDTH, :]) + _dot(pool, wout_ref[ATTN_WIDTH:, :])
    x1 = x_ref[0] + mod_ref[0, 2:3, :] * _rms(yx, gpost_ref[...])
    x1_ref[0] = x1
    h2 = _rms(x1, gpre_ref[...]) * (1.0 + mod_ref[0, 4:5, :]) + mod_ref[0, 3:4, :]
    h2_ref[0] = h2.astype(BF16)
    for k in range(CHUNKS):
        h2t_ref[0, pl.ds(k, tm, stride=CHUNKS), :] = h2[:, k * LANES:(k + 1) * LANES]
    lg_ref[...] = _dot3(wr_ref[...], h2, nt=True)


def _outproj(att, p, x, mod3, w_pool_bf, b_pool, pool_scale, w_out_bf, g_post, g_pre, w_router_t, tm=256):
    b, n, d = x.shape
    hb = tm // HALO
    nhb = n // HALO
    row = lambda bi, i: (bi, i, 0)
    vec = lambda bi, i: (0, 0)
    kern = functools.partial(_outproj_kernel, tm=tm, n=n)
    return pl.pallas_call(
        kern,
        grid=(b, n // tm),
        in_specs=[pl.BlockSpec((1, tm, ATTN_WIDTH), row),
                  pl.BlockSpec((1, tm, POOL_WIDTH), row),
                  pl.BlockSpec((1, HALO, POOL_WIDTH), lambda bi, i: (bi, jnp.maximum(i * hb - 1, 0), 0)),
                  pl.BlockSpec((1, HALO, POOL_WIDTH), lambda bi, i: (bi, jnp.minimum((i + 1) * hb, nhb - 1), 0)),
                  pl.BlockSpec((1, tm, d), row),
                  pl.BlockSpec((1, 6, d), lambda bi, i: (bi, 0, 0)),
                  pl.BlockSpec((len(POOL_WINDOWS), POOL_GROUP, POOL_GROUP), lambda bi, i: (0, 0, 0)),
                  pl.BlockSpec((1, POOL_WIDTH), vec),
                  pl.BlockSpec((1, POOL_WIDTH), vec),
                  pl.BlockSpec((d, d), vec),
                  pl.BlockSpec((1, d), vec),
                  pl.BlockSpec((1, d), vec),
                  pl.BlockSpec((N_EXPERTS, d), vec)],
        out_specs=[pl.BlockSpec((1, tm, d), row),
                   pl.BlockSpec((1, tm, d), row),
                   pl.BlockSpec((1, tm * CHUNKS, LANES), row),
                   pl.BlockSpec((N_EXPERTS, tm), lambda bi, i: (0, bi * (n // tm) + i))],
        out_shape=[jax.ShapeDtypeStruct((b, n, d), F32),
                   jax.ShapeDtypeStruct((b, n, d), BF16),
                   jax.ShapeDtypeStruct((b, n * CHUNKS, LANES), F32),
                   jax.ShapeDtypeStruct((N_EXPERTS, b * n), F32)],
        compiler_params=_cparams(("arbitrary", "arbitrary")),
        name="outproj",
    )(att, p, p, p, x, mod3, w_pool_bf, b_pool, pool_scale, w_out_bf, g_post, g_pre, w_router_t)


def _beats(a, b, a_first):
    return jnp.where((a >= b) if a_first else (a > b), 1.0, 0.0)


def _route_kernel(lg_ref, bias_ref, w_ref):
    scores = [1.0 / (1.0 + jnp.exp(-lg_ref[e])) for e in range(N_EXPERTS)]
    biased = [scores[e] + bias_ref[e] for e in range(N_EXPERTS)]
    gscore = []
    for g in range(N_GROUPS):
        vals = biased[g * GROUP_SIZE:(g + 1) * GROUP_SIZE]
        m1 = vals[0]
        m2 = jnp.full_like(m1, -jnp.inf)
        for v in vals[1:]:
            m2 = jnp.maximum(m2, jnp.minimum(m1, v))
            m1 = jnp.maximum(m1, v)
        gscore.append(m1 + m2)
    gsel = []
    for g in range(N_GROUPS):
        rank = jnp.zeros_like(gscore[g])
        for g2 in range(N_GROUPS):
            if g2 != g:
                rank = rank + _beats(gscore[g2], gscore[g], g2 < g)
        gsel.append(rank < TOPK_GROUPS)
    masked = [jnp.where(gsel[e // GROUP_SIZE], biased[e], -jnp.inf) for e in range(N_EXPERTS)]
    picked = []
    for e in range(N_EXPERTS):
        rank = jnp.zeros_like(masked[e])
        for e2 in range(N_EXPERTS):
            if e2 != e:
                rank = rank + _beats(masked[e2], masked[e], e2 < e)
        picked.append(jnp.where(rank < TOP_K, scores[e], 0.0))
    denom = picked[0]
    for e in range(1, N_EXPERTS):
        denom = denom + picked[e]
    for e in range(N_EXPERTS):
        w_ref[e] = picked[e] / denom * ROUTED_SCALE


def _route(logits3, bias3, rows=8):
    e, r, l = logits3.shape
    return pl.pallas_call(
        _route_kernel,
        grid=(r // rows,),
        in_specs=[pl.BlockSpec((e, rows, l), lambda i: (0, i, 0)),
                  pl.BlockSpec((e, 1, l), lambda i: (0, 0, 0))],
        out_specs=pl.BlockSpec((e, rows, l), lambda i: (0, i, 0)),
        out_shape=jax.ShapeDtypeStruct((e, r, l), F32),
        compiler_params=_cparams(("arbitrary",)),
        name="route",
    )(logits3, bias3)


CHUNKS = D_MODEL // LANES
SLAB = SLOT_TILE + 4
ROW_BATCH = 16


def _gather_tile(row_ref, tile, xs_ref, tx_ref):
    for s in range(SLOT_TILE):
        row = pl.multiple_of(row_ref[0, tile, s], CHUNKS)
        tx_ref[pl.ds(s, CHUNKS, stride=SLAB), :] = xs_ref[pl.ds(row, CHUNKS), :]


def _experts_step(j, row_ref, sw_ref, wg_ref, wu_ref, wd_ref, xs_ref, acc_ref,
                  tx_cur, tx_nxt, ty_cur, ty_prv):
    ntile = row_ref.shape[1]
    _gather_tile(row_ref, jnp.minimum(j + 1, ntile - 1), xs_ref, tx_nxt)
    xb = jnp.concatenate([tx_cur[pl.ds(k * SLAB, SLOT_TILE), :] for k in range(CHUNKS)], axis=1).astype(BF16)
    g = _dot(xb, wg_ref[0])
    u = _dot(xb, wu_ref[0])
    h = (_silu(g) * u).astype(BF16)
    y = _dot(h, wd_ref[0])
    w_rows = jnp.transpose(jnp.broadcast_to(sw_ref[0, pl.ds(j, 1), :], (LANES, SLOT_TILE)))
    for k in range(CHUNKS):
        ty_cur[pl.ds(k * SLAB, SLOT_TILE), :] = y[:, k * LANES:(k + 1) * LANES] * w_rows
    prv = jnp.maximum(j - 1, 0)
    for s0 in range(0, SLOT_TILE, ROW_BATCH):
        new = []
        for s in range(s0, s0 + ROW_BATCH):
            row = pl.multiple_of(row_ref[0, prv, s], CHUNKS)
            new.append((row, acc_ref[pl.ds(row, CHUNKS), :] + ty_prv[pl.ds(s, CHUNKS, stride=SLAB), :]))
        for row, v in new:
            acc_ref[pl.ds(row, CHUNKS), :] = v


def _experts_kernel(be_ref, nt_ref, row_ref, sw_ref, x_hbm, wg_ref, wu_ref, wd_ref, o_hbm,
                    xs_ref, acc_ref, tx0_ref, tx1_ref, ty0_ref, ty1_ref, sem, *, ntile, n):
    c = pl.program_id(0)
    j = pl.program_id(1)
    rows = n * CHUNKS

    @pl.when(j == 0)
    def _():
        cp = pltpu.make_async_copy(x_hbm.at[c], xs_ref.at[pl.ds(0, rows)], sem.at[0])
        cp.start()
        xs_ref[pl.ds(rows, CHUNKS), :] = jnp.zeros((CHUNKS, LANES), F32)
        acc_ref[...] = jnp.zeros_like(acc_ref)
        ty1_ref[...] = jnp.zeros_like(ty1_ref)
        cp.wait()
        _gather_tile(row_ref, 0, xs_ref, tx0_ref)

    live = j <= nt_ref[c]
    args = (j, row_ref, sw_ref, wg_ref, wu_ref, wd_ref, xs_ref, acc_ref)

    @pl.when(live & (lax.rem(j, 2) == 0))
    def _():
        _experts_step(*args, tx0_ref, tx1_ref, ty0_ref, ty1_ref)

    @pl.when(live & (lax.rem(j, 2) == 1))
    def _():
        _experts_step(*args, tx1_ref, tx0_ref, ty1_ref, ty0_ref)

    @pl.when(j == ntile - 1)
    def _():
        cp = pltpu.make_async_copy(acc_ref.at[pl.ds(0, rows)], o_hbm.at[c], sem.at[1])
        cp.start()
        cp.wait()


def _experts(block_e, ntiles, slot_row, slot_w, h2t, wg_bf, wu_bf, wd_bf, ntile, n):
    b = h2t.shape[0]
    d, f = wg_bf.shape[1], wg_bf.shape[2]
    widx = lambda c, j, be, nt: (be[c * ntile + j], 0, 0)
    batch = lambda c, j, be, nt: (c, 0, 0)
    kern = functools.partial(_experts_kernel, ntile=ntile, n=n)
    slab = pltpu.VMEM((CHUNKS * SLAB, LANES), F32)
    grid_spec = pltpu.PrefetchScalarGridSpec(
        num_scalar_prefetch=2,
        grid=(b, ntile),
        in_specs=[pl.BlockSpec((1, ntile, SLOT_TILE), batch, memory_space=pltpu.SMEM),
                  pl.BlockSpec((1, ntile, SLOT_TILE), batch),
                  pl.BlockSpec(memory_space=pl.ANY),
                  pl.BlockSpec((1, d, f), widx),
                  pl.BlockSpec((1, d, f), widx),
                  pl.BlockSpec((1, f, d), widx)],
        out_specs=pl.BlockSpec(memory_space=pl.ANY),
        scratch_shapes=[pltpu.VMEM(((n + 1) * CHUNKS, LANES), F32),
                        pltpu.VMEM(((n + 1) * CHUNKS, LANES), F32),
                        slab, slab, slab, slab,
                        pltpu.SemaphoreType.DMA((2,))],
    )
    return pl.pallas_call(
        kern,
        grid_spec=grid_spec,
        out_shape=jax.ShapeDtypeStruct((b, n * CHUNKS, LANES), F32),
        compiler_params=_cparams(("arbitrary", "arbitrary")),
        name="experts",
    )(block_e, ntiles, slot_row, slot_w, h2t, wg_bf, wu_bf, wd_bf)


EXPERTS_PER_LANE_GROUP = LANES // 32


def _plan_tables_kernel(w_ref, re_ref, ro_ref, int_ref, wt_ref, *, rows):
    ne = w_ref.shape[0]
    w2 = w_ref[...].reshape(ne * rows, LANES)
    selb = jnp.where(w2 > 0.0, 1.0, 0.0).astype(BF16)
    i0 = lax.broadcasted_iota(jnp.int32, (LANES, LANES), 0)
    i1 = lax.broadcasted_iota(jnp.int32, (LANES, LANES), 1)
    rowtot = _dot(selb, jnp.ones((LANES, LANES), BF16))
    ridx = lax.broadcasted_iota(jnp.int32, rowtot.shape, 0) & (rows - 1)
    acc = rowtot
    sh = 1
    while sh < rows:
        acc = acc + jnp.where(ridx >= sh, pltpu.roll(acc, sh, 0), 0.0)
        sh *= 2
    re_ref[0] = acc
    ro_ref[0] = acc - rowtot
    incl_t = _dot_nt(jnp.where(i1 <= i0, 1.0, 0.0).astype(BF16), selb)
    eye = jnp.where(i0 == i1, 1.0, 0.0).astype(BF16)
    w_hi = w2.astype(BF16)
    r1 = w2 - w_hi.astype(F32)
    w_mid = r1.astype(BF16)
    w_lo = (r1 - w_mid.astype(F32)).astype(BF16)
    parts = [_dot_nt(eye, p) for p in (w_hi, w_mid, w_lo)]
    for g in range(ne * rows // LANES):
        sl = slice(g * LANES, (g + 1) * LANES)
        int_ref[0, g] = incl_t[:, sl].astype(BF16)
        for k in range(3):
            wt_ref[0, k, g] = parts[k][:, sl].astype(BF16)


def _plan_tables(selw3, b):
    ne, r_all, _ = selw3.shape
    rows = r_all // b
    groups = ne * rows // LANES
    kern = functools.partial(_plan_tables_kernel, rows=rows)
    return pl.pallas_call(
        kern,
        grid=(b,),
        in_specs=[pl.BlockSpec((ne, rows, LANES), lambda c: (0, c, 0))],
        out_specs=[pl.BlockSpec((1, ne * rows, LANES), lambda c: (c, 0, 0)),
                   pl.BlockSpec((1, ne * rows, LANES), lambda c: (c, 0, 0)),
                   pl.BlockSpec((1, groups, LANES, LANES), lambda c: (c, 0, 0, 0)),
                   pl.BlockSpec((1, 3, groups, LANES, LANES), lambda c: (c, 0, 0, 0, 0))],
        out_shape=[jax.ShapeDtypeStruct((b, ne * rows, LANES), F32),
                   jax.ShapeDtypeStruct((b, ne * rows, LANES), F32),
                   jax.ShapeDtypeStruct((b, groups, LANES, LANES), BF16),
                   jax.ShapeDtypeStruct((b, 3, groups, LANES, LANES), BF16)],
        compiler_params=_cparams(("arbitrary",)),
        name="plan_tables",
    )(selw3)


def _plan_slots_kernel(be_ref, q0_ref, nt_ref, re_ref, ro_ref, int_ref, wt_ref, tok_ref, sw_ref,
                       *, ntile, rows, n):
    c = pl.program_id(0)
    tok_ref[...] = jnp.full(tok_ref.shape, n * CHUNKS, jnp.int32)
    sw_ref[...] = jnp.zeros(sw_ref.shape, F32)
    lane = lax.broadcasted_iota(jnp.int32, (1, SLOT_TILE), 1).astype(F32)
    sub_r = lax.broadcasted_iota(jnp.int32, (rows, SLOT_TILE), 0).astype(F32)
    sub_l = lax.broadcasted_iota(jnp.int32, (LANES, SLOT_TILE), 0).astype(F32)
    reps = SLOT_TILE // LANES

    def tile_body(j, carry):
        e = be_ref[c * ntile + j]
        q = q0_ref[c * ntile + j].astype(F32) + lane
        base = pl.multiple_of(e * rows, rows)
        row_end = jnp.concatenate([re_ref[0, pl.ds(base, rows), :]] * reps, axis=1)
        row_off = jnp.concatenate([ro_ref[0, pl.ds(base, rows), :]] * reps, axis=1)
        r = jnp.sum(jnp.where(row_end <= q, 1.0, 0.0), axis=0, keepdims=True)
        row_start = jnp.sum(jnp.where(sub_r == r, row_off, 0.0), axis=0, keepdims=True)
        g = lax.shift_right_logical(e, EXPERTS_PER_LANE_GROUP.bit_length() - 1)
        col = r + ((e & (EXPERTS_PER_LANE_GROUP - 1)) * rows).astype(F32)
        onehot = jnp.where(sub_l == col, 1.0, 0.0).astype(BF16)
        incl = _dot(int_ref[0, g], onehot)
        lane_idx = jnp.sum(jnp.where(incl <= q - row_start, 1.0, 0.0), axis=0, keepdims=True)
        wrow = (_dot(wt_ref[0, 0, g], onehot) + _dot(wt_ref[0, 1, g], onehot)) + _dot(wt_ref[0, 2, g], onehot)
        wsel = jnp.sum(jnp.where(sub_l == lane_idx, wrow, 0.0), axis=0, keepdims=True)
        valid = q < row_end[rows - 1:rows, :]
        tok = jnp.where(valid, r * LANES + lane_idx, float(n)).astype(jnp.int32)
        tok_ref[0, pl.ds(j, 1), :] = tok * CHUNKS
        sw_ref[0, pl.ds(j, 1), :] = jnp.where(valid, wsel, 0.0)
        return carry

    lax.fori_loop(0, nt_ref[c], tile_body, 0)


def _plan_slots(block_e, tile_q0, ntiles, row_end, row_off, incl_t, w_t, ntile, n):
    b, er, _ = row_end.shape
    rows = n // LANES
    groups = incl_t.shape[1]
    kern = functools.partial(_plan_slots_kernel, ntile=ntile, rows=rows, n=n)
    grid_spec = pltpu.PrefetchScalarGridSpec(
        num_scalar_prefetch=3,
        grid=(b,),
        in_specs=[pl.BlockSpec((1, er, LANES), lambda c, *_: (c, 0, 0)),
                  pl.BlockSpec((1, er, LANES), lambda c, *_: (c, 0, 0)),
                  pl.BlockSpec((1, groups, LANES, LANES), lambda c, *_: (c, 0, 0, 0)),
                  pl.BlockSpec((1, 3, groups, LANES, LANES), lambda c, *_: (c, 0, 0, 0, 0))],
        out_specs=[pl.BlockSpec((1, ntile, SLOT_TILE), lambda c, *_: (c, 0, 0)),
                   pl.BlockSpec((1, ntile, SLOT_TILE), lambda c, *_: (c, 0, 0))],
    )
    return pl.pallas_call(
        kern,
        grid_spec=grid_spec,
        out_shape=[jax.ShapeDtypeStruct((b, ntile, SLOT_TILE), jnp.int32),
                   jax.ShapeDtypeStruct((b, ntile, SLOT_TILE), F32)],
        compiler_params=_cparams(("arbitrary",)),
        name="plan_slots",
    )(block_e, tile_q0, ntiles, row_end, row_off, incl_t, w_t)


def _tile_meta(row_end, b, n, ntile):
    rows = n // LANES
    cnt = row_end.reshape(b, N_EXPERTS, rows, LANES)[:, :, rows - 1, 0].astype(jnp.int32)
    nt_e = (cnt + SLOT_TILE - 1) // SLOT_TILE
    tend = jnp.cumsum(nt_e, axis=1)
    tstart = tend - nt_e
    ntiles = tend[:, -1]
    tile_id = jnp.arange(ntile, dtype=jnp.int32)
    tid = jnp.minimum(tile_id[None, :], ntiles[:, None] - 1)
    be = jnp.minimum(jnp.sum(tend[:, None, :] <= tid[:, :, None], axis=2).astype(jnp.int32), N_EXPERTS - 1)
    q0 = (tid - jnp.take_along_axis(tstart, be, axis=1)) * SLOT_TILE
    return be.reshape(-1), q0.reshape(-1).astype(jnp.int32), ntiles.astype(jnp.int32)


def _dispatch_plan(selw3, b, n, ntile):
    row_end, row_off, incl_t, w_t = _plan_tables(selw3, b)
    block_e, tile_q0, ntiles = _tile_meta(row_end, b, n, ntile)
    slot_tok, slot_w = _plan_slots(block_e, tile_q0, ntiles, row_end, row_off, incl_t, w_t, ntile, n)
    return block_e, ntiles, slot_tok, slot_w


def _final_kernel(h2_ref, r_ref, x1_ref, mod_ref, wg_ref, wu_ref, wd_ref, g_ref, o_ref):
    hb = h2_ref[0]
    tm = hb.shape[0]
    sh = _dot((_silu(_dot(hb, wg_ref[...])) * _dot(hb, wu_ref[...])).astype(BF16), wd_ref[...])
    routed = jnp.concatenate([r_ref[0, pl.ds(k, tm, stride=CHUNKS), :] for k in range(CHUNKS)], axis=1)
    moe = routed + sh
    o_ref[0] = x1_ref[0] + mod_ref[0, 5:6, :] * _rms(moe, g_ref[...])


def _final(h2, routed, x1, mod3, wsg_bf, wsu_bf, wsd_bf, g_post, tm=512):
    b, n, d = x1.shape
    f = wsg_bf.shape[1]
    row = lambda bi, i: (bi, i, 0)
    vec = lambda bi, i: (0, 0)
    return pl.pallas_call(
        _final_kernel,
        grid=(b, n // tm),
        in_specs=[pl.BlockSpec((1, tm, d), row),
                  pl.BlockSpec((1, tm * CHUNKS, LANES), row),
                  pl.BlockSpec((1, tm, d), row),
                  pl.BlockSpec((1, 6, d), lambda bi, i: (bi, 0, 0)),
                  pl.BlockSpec((d, f), vec),
                  pl.BlockSpec((d, f), vec),
                  pl.BlockSpec((f, d), vec),
                  pl.BlockSpec((1, d), vec)],
        out_specs=pl.BlockSpec((1, tm, d), row),
        out_shape=jax.ShapeDtypeStruct((b, n, d), F32),
        compiler_params=_cparams(("arbitrary", "arbitrary")),
        name="final",
    )(h2, routed, x1, mod3, wsg_bf, wsu_bf, wsd_bf, g_post)


def _rope_tables(n):
    t = np.arange(n)
    row = (t // GRID_W).astype(np.float32)
    col = (t % GRID_W).astype(np.float32)
    freqs = np.float32(ROPE_THETA) ** (-np.arange(0, AXIS_DIM, 2, dtype=np.float32) / np.float32(AXIS_DIM))
    ar = row[:, None] * freqs
    ac = col[:, None] * freqs
    zeros = np.zeros_like(ar)
    cos64 = np.concatenate([np.cos(ar), np.cos(ar), np.cos(ac), np.cos(ac)], axis=1)
    sa64 = np.concatenate([-np.sin(ar), zeros, -np.sin(ac), zeros], axis=1)
    sb64 = np.concatenate([zeros, np.sin(ar), zeros, np.sin(ac)], axis=1)
    two = lambda a: jnp.asarray(np.concatenate([a, a], axis=1).astype(np.float32))
    return two(cos64), two(sa64), two(sb64)


def kernel(x, c, ctx, c_ctx, w_ada, b_ada, g_pre_mix, g_post_mix, g_pre_ffn, g_post_ffn, w_in, lambda_q1, lambda_k1, lambda_q2, lambda_k2, g_subln, w_pool, b_pool, pool_scale, w_out, w_router, router_bias, w_e_gate, w_e_up, w_e_down, w_sh_gate, w_sh_up, w_sh_down):
    b, n, d = x.shape
    l = 0
    mod_rows = 8
    cin = jnp.concatenate([c, c_ctx[None, :], jnp.zeros((mod_rows - b - 1, d), F32)], axis=0)
    mod3 = _ada(cin, w_ada[l], b_ada[l]).reshape(mod_rows, 6, d)

    w_in_bf = w_in[l].astype(BF16)
    cos, sa, sb = _rope_tables(n)
    q, k, vt, p = _inproj(x, mod3, g_pre_mix[l][None, :], w_in_bf, cos * (QK_DIM ** -0.5 * LOG2E), cos, sa, sb)
    kc, vct = _ctxkv(ctx, mod3, g_pre_mix[l][None, :], w_in_bf, b)

    lamv = jnp.stack([lambda_q1[l], lambda_k1[l], lambda_q2[l], lambda_k2[l]], axis=0)
    att = _attn(q, k, kc, vt, vct, lamv, g_subln[l][:, None])

    x1, h2, h2t, logits_t = _outproj(
        att, p, x, mod3, w_pool[l].astype(BF16), b_pool[l][None, :], pool_scale[l][None, :],
        w_out[l].astype(BF16), g_post_mix[l][None, :], g_pre_ffn[l][None, :], w_router[l].T)

    t = b * n
    selw3 = _route(logits_t.reshape(N_EXPERTS, t // LANES, LANES),
                   jnp.broadcast_to(router_bias[l][:, None, None], (N_EXPERTS, 1, LANES)))

    ntile = (n * TOP_K) // SLOT_TILE + N_EXPERTS
    block_e, ntiles, slot_row, slot_w = _dispatch_plan(selw3, b, n, ntile)
    routed = _experts(block_e, ntiles, slot_row, slot_w, h2t,
                      w_e_gate[l].astype(BF16), w_e_up[l].astype(BF16), w_e_down[l].astype(BF16), ntile, n)

    return _final(h2, routed, x1, mod3, w_sh_gate[l].astype(BF16), w_sh_up[l].astype(BF16),
                  w_sh_down[l].astype(BF16), g_post_ffn[l][None, :])
```

```python
import functools
import math

import jax
import jax.numpy as jnp
import numpy as np
from jax import lax
from jax.experimental import pallas as pl
from jax.experimental.pallas import tpu as pltpu

F32 = jnp.float32
BF16 = jnp.bfloat16

D_MODEL = 1024
GRID_W = 64
N_HEADS = 4
QK_DIM = 64
V_DIM = 128
ATTN_WIDTH = N_HEADS * V_DIM
POOL_WIDTH = D_MODEL - ATTN_WIDTH
POOL_WINDOWS = (2, 4, 8, 16)
POOL_GROUP = POOL_WIDTH // len(POOL_WINDOWS)
AXIS_DIM = QK_DIM // 2
ROPE_THETA = 10000.0
N_EXPERTS = 64
TOP_K = 8
N_GROUPS = 8
GROUP_SIZE = N_EXPERTS // N_GROUPS
TOPK_GROUPS = 4
EXPERT_DIM = 256
ROUTED_SCALE = 2.5
EPS = 1e-6
LAM_INIT = 0.8 - 0.6 * math.exp(-0.3 * 0)

LANES = 128
SLOT_TILE = 256
HALO = 16
VMEM_LIMIT = 56 * 1024 * 1024


def _cparams(sem, vmem=VMEM_LIMIT):
    return pltpu.CompilerParams(dimension_semantics=sem, vmem_limit_bytes=vmem)


def _split(a):
    hi = a.astype(BF16)
    lo = (a - hi.astype(F32)).astype(BF16)
    return hi, lo


def _dot(a, b):
    return jnp.dot(a, b, preferred_element_type=F32)


def _dot_nt(a, b):
    return lax.dot_general(a, b, (((1,), (1,)), ((), ())), preferred_element_type=F32)


def _dot3(a, b, nt=False):
    d = _dot_nt if nt else _dot
    ah, al = _split(a)
    bh, bl = _split(b)
    return d(ah, bh) + d(ah, bl) + d(al, bh)


def _rms(x, g):
    return x * lax.rsqrt(jnp.mean(x * x, axis=-1, keepdims=True) + EPS) * g


def _silu(x):
    return x * (1.0 / (1.0 + jnp.exp(-x)))


def _ada_kernel(c_ref, w_ref, b_ref, o_ref):
    o_ref[...] = _dot3(_silu(c_ref[...]), w_ref[...]) + b_ref[...]


def _ada(cin, w_ada, b_ada):
    rows, d = cin.shape
    n = w_ada.shape[1]
    tn = 1024
    return pl.pallas_call(
        _ada_kernel,
        grid=(n // tn,),
        in_specs=[pl.BlockSpec((rows, d), lambda j: (0, 0)),
                  pl.BlockSpec((d, tn), lambda j: (0, j)),
                  pl.BlockSpec((1, tn), lambda j: (0, j))],
        out_specs=pl.BlockSpec((rows, tn), lambda j: (0, j)),
        out_shape=jax.ShapeDtypeStruct((rows, n), F32),
        compiler_params=_cparams(("arbitrary",)),
        name="ada",
    )(cin, w_ada, b_ada.reshape(1, n))


def _rope(t, cos, sa, sb):
    return t * cos + pltpu.roll(t, LANES - AXIS_DIM // 2, 1) * sa + pltpu.roll(t, AXIS_DIM // 2, 1) * sb


def _inproj_kernel(x_ref, mod_ref, g_ref, w_ref, cq_ref, ck_ref, sa_ref, sb_ref,
                   q_ref, k_ref, vt_ref, p_ref):
    x = x_ref[0]
    h = _rms(x, g_ref[...]) * (1.0 + mod_ref[0, 1:2, :]) + mod_ref[0, 0:1, :]
    px = _dot(h.astype(BF16), w_ref[...])
    ck = ck_ref[...]
    sa = sa_ref[...]
    sb = sb_ref[...]
    cq = cq_ref[...]
    scale = QK_DIM ** -0.5 * LOG2E
    saq = sa * scale
    sbq = sb * scale
    for hh in range(N_HEADS):
        lo = hh * LANES
        q = px[:, lo:lo + LANES]
        k = px[:, ATTN_WIDTH + lo:ATTN_WIDTH + lo + LANES]
        q_ref[0, :, lo:lo + LANES] = _rope(q, cq, saq, sbq).astype(BF16)
        k_ref[0, :, lo:lo + LANES] = _rope(k, ck, sa, sb).astype(BF16)
    vt_ref[0] = jnp.transpose(px[:, 2 * ATTN_WIDTH:3 * ATTN_WIDTH]).astype(BF16)
    p_ref[0] = px[:, 3 * ATTN_WIDTH:].astype(BF16)


def _inproj(x, mod3, g, w_in_bf, cq, ck, sa, sb, tm=512):
    b, n, d = x.shape
    width = w_in_bf.shape[1]
    row = lambda bi, i: (bi, i, 0)
    tab = pl.BlockSpec((tm, LANES), lambda bi, i: (i, 0))
    out = jax.ShapeDtypeStruct((b, n, ATTN_WIDTH), BF16)
    return pl.pallas_call(
        _inproj_kernel,
        grid=(b, n // tm),
        in_specs=[pl.BlockSpec((1, tm, d), row),
                  pl.BlockSpec((1, 6, d), lambda bi, i: (bi, 0, 0)),
                  pl.BlockSpec((1, d), lambda bi, i: (0, 0)),
                  pl.BlockSpec((d, width), lambda bi, i: (0, 0)),
                  tab, tab, tab, tab],
        out_specs=[pl.BlockSpec((1, tm, ATTN_WIDTH), row),
                   pl.BlockSpec((1, tm, ATTN_WIDTH), row),
                   pl.BlockSpec((1, ATTN_WIDTH, tm), lambda bi, i: (bi, 0, i)),
                   pl.BlockSpec((1, tm, ATTN_WIDTH), row)],
        out_shape=[out, out, jax.ShapeDtypeStruct((b, ATTN_WIDTH, n), BF16), out],
        compiler_params=_cparams(("arbitrary", "arbitrary")),
        name="inproj",
    )(x, mod3, g, w_in_bf, cq, ck, sa, sb)


def _ctxkv_kernel(x_ref, mod_ref, g_ref, wk_ref, wv_ref, k_ref, vt_ref):
    h = _rms(x_ref[0], g_ref[...]) * (1.0 + mod_ref[0, 1:2, :]) + mod_ref[0, 0:1, :]
    hb = h.astype(BF16)
    k_ref[0] = _dot(hb, wk_ref[...]).astype(BF16)
    vt_ref[0] = jnp.transpose(_dot(hb, wv_ref[...])).astype(BF16)


def _ctxkv(ctx, mod3, g, w_in_bf, ctx_row):
    b, n, d = ctx.shape
    out = jax.ShapeDtypeStruct((b, n, ATTN_WIDTH), BF16)
    return pl.pallas_call(
        _ctxkv_kernel,
        grid=(b,),
        in_specs=[pl.BlockSpec((1, n, d), lambda bi: (bi, 0, 0)),
                  pl.BlockSpec((1, 6, d), lambda bi: (ctx_row, 0, 0)),
                  pl.BlockSpec((1, d), lambda bi: (0, 0)),
                  pl.BlockSpec((d, ATTN_WIDTH), lambda bi: (0, 1)),
                  pl.BlockSpec((d, ATTN_WIDTH), lambda bi: (0, 2))],
        out_specs=[pl.BlockSpec((1, n, ATTN_WIDTH), lambda bi: (bi, 0, 0)),
                   pl.BlockSpec((1, ATTN_WIDTH, n), lambda bi: (bi, 0, 0))],
        out_shape=[out, jax.ShapeDtypeStruct((b, ATTN_WIDTH, n), BF16)],
        compiler_params=_cparams(("arbitrary",)),
        name="ctxkv",
    )(ctx, mod3, g, w_in_bf, w_in_bf)


LOG2E = math.log2(math.e)
KEY_CHUNK = 256


def _attn_step(q_ref, kx_ref, kc_ref, vt_ref, vct_ref, lam_ref, g_ref, o_ref, s_new, m_new, s_old, m_old):
    nc = kc_ref.shape[1]
    tq = q_ref.shape[1]
    nk = s_old.shape[1]
    groups = KEY_CHUNK // 8
    q = q_ref[0]
    lane = lax.broadcasted_iota(jnp.int32, q.shape, 1)
    zero = jnp.zeros_like(q)
    qms = (jnp.where(lane < QK_DIM, q, zero), jnp.where(lane >= QK_DIM, q, zero))
    m_prev = (m_old[0], m_old[1])
    top = [jnp.full((8, tq), -jnp.inf, F32) for _ in range(2)]
    den = [jnp.zeros((8, tq), F32) for _ in range(2)]
    acc = [jnp.zeros((V_DIM, tq), F32) for _ in range(2)]
    for c in range(nk // KEY_CHUNK):
        lo = c * KEY_CHUNK
        if lo < nc:
            keys, vt = kc_ref[0, lo:lo + KEY_CHUNK, :], vct_ref[0, :, lo:lo + KEY_CHUNK]
        else:
            keys, vt = kx_ref[0, lo - nc:lo - nc + KEY_CHUNK, :], vt_ref[0, :, lo - nc:lo - nc + KEY_CHUNK]
        for mp in range(2):
            s = _dot_nt(keys, qms[mp])
            s_new[mp, lo:lo + KEY_CHUNK, :] = s
            top[mp] = jnp.maximum(top[mp], jnp.max(s.reshape(groups, 8, tq), axis=0))
            p = jnp.exp2(s_old[mp, lo:lo + KEY_CHUNK, :] - m_prev[mp])
            den[mp] = den[mp] + jnp.sum(p.reshape(groups, 8, tq), axis=0)
            acc[mp] = acc[mp] + _dot(vt, p.astype(BF16))
    for mp in range(2):
        m_new[mp] = jnp.max(top[mp], axis=0, keepdims=True)
    lv = lam_ref[...]
    lam = (jnp.exp(jnp.sum(lv[0:1] * lv[1:2], axis=-1, keepdims=True))
           - jnp.exp(jnp.sum(lv[2:3] * lv[3:4], axis=-1, keepdims=True)) + LAM_INIT)
    l1 = jnp.sum(den[0], axis=0, keepdims=True)
    l2 = jnp.sum(den[1], axis=0, keepdims=True)
    o = acc[0] * (1.0 / l1) - acc[1] * (lam / l2)
    o = o * lax.rsqrt(jnp.mean(o * o, axis=0, keepdims=True) + EPS) * (g_ref[...] * (1.0 - LAM_INIT))
    o_ref[0] = jnp.transpose(o).astype(BF16)


def _attn_kernel(q_ref, kx_ref, kc_ref, vt_ref, vct_ref, lam_ref, g_ref, o_ref, s0_ref, m0_ref, s1_ref, m1_ref):
    t = pl.program_id(0)
    io = (q_ref, kx_ref, kc_ref, vt_ref, vct_ref, lam_ref, g_ref, o_ref)

    @pl.when(t == 0)
    def _():
        s1_ref[...] = jnp.zeros_like(s1_ref)
        m1_ref[...] = jnp.zeros_like(m1_ref)

    @pl.when(lax.rem(t, 2) == 0)
    def _():
        _attn_step(*io, s0_ref, m0_ref, s1_ref, m1_ref)

    @pl.when(lax.rem(t, 2) == 1)
    def _():
        _attn_step(*io, s1_ref, m1_ref, s0_ref, m0_ref)


def _attn(q, k, kc, vt, vct, lamv, g_col, tq=256):
    b, n, _ = q.shape
    nc = kc.shape[1]
    nq = n // tq
    total = b * N_HEADS * nq

    def cur(t):
        ta = jnp.minimum(t, total - 1)
        bh = ta // nq
        return bh // N_HEADS, bh % N_HEADS, ta % nq

    def prv(t):
        tb = jnp.maximum(t - 1, 0)
        bh = tb // nq
        return bh // N_HEADS, bh % N_HEADS, tb % nq

    def at(fn, order):
        return lambda t: tuple((fn(t) + (0,))[j] for j in order)

    return pl.pallas_call(
        _attn_kernel,
        grid=(total + 1,),
        in_specs=[pl.BlockSpec((1, tq, LANES), at(cur, (0, 2, 1))),
                  pl.BlockSpec((1, n, LANES), at(cur, (0, 3, 1))),
                  pl.BlockSpec((1, nc, LANES), at(cur, (0, 3, 1))),
                  pl.BlockSpec((1, V_DIM, n), at(prv, (0, 1, 3))),
                  pl.BlockSpec((1, V_DIM, nc), at(prv, (0, 1, 3))),
                  pl.BlockSpec((4, QK_DIM), lambda t: (0, 0)),
                  pl.BlockSpec((V_DIM, 1), lambda t: (0, 0))],
        out_specs=pl.BlockSpec((1, tq, LANES), at(prv, (0, 2, 1))),
        out_shape=jax.ShapeDtypeStruct((b, n, ATTN_WIDTH), BF16),
        scratch_shapes=[pltpu.VMEM((2, nc + n, tq), F32), pltpu.VMEM((2, 1, tq), F32),
                        pltpu.VMEM((2, nc + n, tq), F32), pltpu.VMEM((2, 1, tq), F32)],
        compiler_params=_cparams(("arbitrary",)),
        name="attn",
    )(q, k, kc, vt, vct, lamv, g_col)


def _outproj_kernel(att_ref, p_ref, pprev_ref, pnext_ref, x_ref, mod_ref, wpool_ref, bpool_ref,
                    pscale_ref, wout_ref, gpost_ref, gpre_ref, wr_ref,
                    x1_ref, h2_ref, h2t_ref, lg_ref, *, tm, n):
    i = pl.program_id(1)
    p = p_ref[0].astype(F32)
    prev = jnp.where(i > 0, pprev_ref[0].astype(F32), 0.0)
    nxt = jnp.where(i < pl.num_programs(1) - 1, pnext_ref[0].astype(F32), 0.0)
    ext = jnp.concatenate([prev, p, nxt], axis=0)
    rows = tm + 2 * HALO
    trow = (i * tm + lax.broadcasted_iota(jnp.int32, (tm, 1), 0))

    pooled = []
    sums = {}
    acc = ext + pltpu.roll(ext, 1, 0)
    sums[2] = acc
    w = 2
    while w < POOL_WINDOWS[-1]:
        acc = pltpu.roll(acc, rows - w // 2, 0) + pltpu.roll(acc, w // 2, 0)
        w = 2 * w
        sums[w] = acc
    for gi, win in enumerate(POOL_WINDOWS):
        sl = slice(gi * POOL_GROUP, (gi + 1) * POOL_GROUP)
        wsum = sums[win][HALO:HALO + tm, sl]
        hi_c = jnp.minimum(trow + (win - win // 2), n)
        lo_c = jnp.maximum(trow - win // 2, 0)
        cnt = (hi_c - lo_c).astype(F32)
        d = wsum / cnt - p[:, sl]
        y = _dot(d.astype(BF16), wpool_ref[gi]) + bpool_ref[:, sl]
        pooled.append((y * pscale_ref[:, sl]).astype(BF16))
    pool = jnp.concatenate(pooled, axis=1)
    yx = _dot(att_ref[0], wout_ref[0:ATTN_WIDTH, :]) + _dot(pool, wout_ref[ATTN_WIDTH:, :])
    x1 = x_ref[0] + mod_ref[0, 2:3, :] * _rms(yx, gpost_ref[...])
    x1_ref[0] = x1
    h2 = _rms(x1, gpre_ref[...]) * (1.0 + mod_ref[0, 4:5, :]) + mod_ref[0, 3:4, :]
    h2_ref[0] = h2.astype(BF16)
    for k in range(CHUNKS):
        h2t_ref[0, pl.ds(k, tm, stride=CHUNKS), :] = h2[:, k * LANES:(k + 1) * LANES]
    lg_ref[...] = _dot3(wr_ref[...], h2, nt=True)


def _outproj(att, p, x, mod3, w_pool_bf, b_pool, pool_scale, w_out_bf, g_post, g_pre, w_router_t, tm=256):
    b, n, d = x.shape
    hb = tm // HALO
    nhb = n // HALO
    row = lambda bi, i: (bi, i, 0)
    vec = lambda bi, i: (0, 0)
    kern = functools.partial(_outproj_kernel, tm=tm, n=n)
    return pl.pallas_call(
        kern,
        grid=(b, n // tm),
        in_specs=[pl.BlockSpec((1, tm, ATTN_WIDTH), row),
                  pl.BlockSpec((1, tm, POOL_WIDTH), row),
                  pl.BlockSpec((1, HALO, POOL_WIDTH), lambda bi, i: (bi, jnp.maximum(i * hb - 1, 0), 0)),
                  pl.BlockSpec((1, HALO, POOL_WIDTH), lambda bi, i: (bi, jnp.minimum((i + 1) * hb, nhb - 1), 0)),
                  pl.BlockSpec((1, tm, d), row),
                  pl.BlockSpec((1, 6, d), lambda bi, i: (bi, 0, 0)),
                  pl.BlockSpec((len(POOL_WINDOWS), POOL_GROUP, POOL_GROUP), lambda bi, i: (0, 0, 0)),
                  pl.BlockSpec((1, POOL_WIDTH), vec),
                  pl.BlockSpec((1, POOL_WIDTH), vec),
                  pl.BlockSpec((d, d), vec),
                  pl.BlockSpec((1, d), vec),
                  pl.BlockSpec((1, d), vec),
                  pl.BlockSpec((N_EXPERTS, d), vec)],
        out_specs=[pl.BlockSpec((1, tm, d), row),
                   pl.BlockSpec((1, tm, d), row),
                   pl.BlockSpec((1, tm * CHUNKS, LANES), row),
                   pl.BlockSpec((N_EXPERTS, tm), lambda bi, i: (0, bi * (n // tm) + i))],
        out_shape=[jax.ShapeDtypeStruct((b, n, d), F32),
                   jax.ShapeDtypeStruct((b, n, d), BF16),
                   jax.ShapeDtypeStruct((b, n * CHUNKS, LANES), F32),
                   jax.ShapeDtypeStruct((N_EXPERTS, b * n), F32)],
        compiler_params=_cparams(("arbitrary", "arbitrary")),
        name="outproj",
    )(att, p, p, p, x, mod3, w_pool_bf, b_pool, pool_scale, w_out_bf, g_post, g_pre, w_router_t)


def _beats(a, b, a_first):
    return jnp.where((a >= b) if a_first else (a > b), 1.0, 0.0)


def _route_kernel(lg_ref, bias_ref, w_ref):
    scores = [1.0 / (1.0 + jnp.exp(-lg_ref[e])) for e in range(N_EXPERTS)]
    biased = [scores[e] + bias_ref[e] for e in range(N_EXPERTS)]
    gscore = []
    for g in range(N_GROUPS):
        vals = biased[g * GROUP_SIZE:(g + 1) * GROUP_SIZE]
        m1 = vals[0]
        m2 = jnp.full_like(m1, -jnp.inf)
        for v in vals[1:]:
            m2 = jnp.maximum(m2, jnp.minimum(m1, v))
            m1 = jnp.maximum(m1, v)
        gscore.append(m1 + m2)
    gsel = []
    for g in range(N_GROUPS):
        rank = jnp.zeros_like(gscore[g])
        for g2 in range(N_GROUPS):
            if g2 != g:
                rank = rank + _beats(gscore[g2], gscore[g], g2 < g)
        gsel.append(rank < TOPK_GROUPS)
    masked = [jnp.where(gsel[e // GROUP_SIZE], biased[e], -jnp.inf) for e in range(N_EXPERTS)]
    picked = []
    for e in range(N_EXPERTS):
        rank = jnp.zeros_like(masked[e])
        for e2 in range(N_EXPERTS):
            if e2 != e:
                rank = rank + _beats(masked[e2], masked[e], e2 < e)
        picked.append(jnp.where(rank < TOP_K, scores[e], 0.0))
    denom = picked[0]
    for e in range(1, N_EXPERTS):
        denom = denom + picked[e]
    for e in range(N_EXPERTS):
        w_ref[e] = picked[e] / denom * ROUTED_SCALE


def _route(logits3, bias3, rows=8):
    e, r, l = logits3.shape
    return pl.pallas_call(
        _route_kernel,
        grid=(r // rows,),
        in_specs=[pl.BlockSpec((e, rows, l), lambda i: (0, i, 0)),
                  pl.BlockSpec((e, 1, l), lambda i: (0, 0, 0))],
        out_specs=pl.BlockSpec((e, rows, l), lambda i: (0, i, 0)),
        out_shape=jax.ShapeDtypeStruct((e, r, l), F32),
        compiler_params=_cparams(("arbitrary",)),
        name="route",
    )(logits3, bias3)


CHUNKS = D_MODEL // LANES
SLAB = SLOT_TILE + 4
ROW_BATCH = 16


def _gather_tile(row_ref, tile, xs_ref, tx_ref):
    for s in range(SLOT_TILE):
        row = pl.multiple_of(row_ref[0, tile, s], CHUNKS)
        tx_ref[pl.ds(s, CHUNKS, stride=SLAB), :] = xs_ref[pl.ds(row, CHUNKS), :]


def _experts_step(j, wslot, row_ref, sw_ref, wg_buf, wu_buf, wd_buf, xs_ref, acc_ref,
                  tx_cur, tx_nxt, ty_cur, ty_prv):
    ntile = row_ref.shape[1]
    _gather_tile(row_ref, jnp.minimum(j + 1, ntile - 1), xs_ref, tx_nxt)
    xb = jnp.concatenate([tx_cur[pl.ds(k * SLAB, SLOT_TILE), :] for k in range(CHUNKS)], axis=1).astype(BF16)
    g = _dot(xb, wg_buf[wslot])
    u = _dot(xb, wu_buf[wslot])
    h = (_silu(g) * u).astype(BF16)
    y = _dot(h, wd_buf[wslot])
    w_rows = jnp.transpose(jnp.broadcast_to(sw_ref[0, pl.ds(j, 1), :], (LANES, SLOT_TILE)))
    for k in range(CHUNKS):
        ty_cur[pl.ds(k * SLAB, SLOT_TILE), :] = y[:, k * LANES:(k + 1) * LANES] * w_rows
    prv = jnp.maximum(j - 1, 0)
    for s0 in range(0, SLOT_TILE, ROW_BATCH):
        new = []
        for s in range(s0, s0 + ROW_BATCH):
            row = pl.multiple_of(row_ref[0, prv, s], CHUNKS)
            new.append((row, acc_ref[pl.ds(row, CHUNKS), :] + ty_prv[pl.ds(s, CHUNKS, stride=SLAB), :]))
        for row, v in new:
            acc_ref[pl.ds(row, CHUNKS), :] = v


def _experts_kernel(be_ref, nt_ref, first_ref, nxt_ref, ws_ref, row_ref, sw_ref, x_hbm,
                    wg_hbm, wu_hbm, wd_hbm, o_hbm,
                    xs_ref, acc_ref, tx0_ref, tx1_ref, ty0_ref, ty1_ref, wg_buf, wu_buf, wd_buf,
                    sem, wsem, *, ntile, n):
    c = pl.program_id(0)
    j = pl.program_id(1)
    t = c * ntile + j
    rows = n * CHUNKS
    wslot = ws_ref[t]

    def weight_copies(e, slot):
        return [pltpu.make_async_copy(src.at[e], dst.at[slot], wsem.at[slot])
                for src, dst in ((wg_hbm, wg_buf), (wu_hbm, wu_buf), (wd_hbm, wd_buf))]

    @pl.when(j == 0)
    def _():
        for wcp in weight_copies(be_ref[t], 0):
            wcp.start()
        cp = pltpu.make_async_copy(x_hbm.at[c], xs_ref.at[pl.ds(0, rows)], sem.at[0])
        cp.start()
        xs_ref[pl.ds(rows, CHUNKS), :] = jnp.zeros((CHUNKS, LANES), F32)
        acc_ref[...] = jnp.zeros_like(acc_ref)
        ty1_ref[...] = jnp.zeros_like(ty1_ref)
        cp.wait()
        _gather_tile(row_ref, 0, xs_ref, tx0_ref)

    @pl.when(first_ref[t] == 1)
    def _():
        for wcp in weight_copies(be_ref[t], wslot):
            wcp.wait()

    @pl.when((first_ref[t] == 1) & (nxt_ref[t] >= 0))
    def _():
        for wcp in weight_copies(nxt_ref[t], 1 - wslot):
            wcp.start()

    live = j <= nt_ref[c]
    args = (j, wslot, row_ref, sw_ref, wg_buf, wu_buf, wd_buf, xs_ref, acc_ref)

    @pl.when(live & (lax.rem(j, 2) == 0))
    def _():
        _experts_step(*args, tx0_ref, tx1_ref, ty0_ref, ty1_ref)

    @pl.when(live & (lax.rem(j, 2) == 1))
    def _():
        _experts_step(*args, tx1_ref, tx0_ref, ty1_ref, ty0_ref)

    @pl.when(j == ntile - 1)
    def _():
        cp = pltpu.make_async_copy(acc_ref.at[pl.ds(0, rows)], o_hbm.at[c], sem.at[1])
        cp.start()
        cp.wait()


def _experts(block_e, ntiles, first, nxt_e, wslot, slot_row, slot_w, h2t, wg_bf, wu_bf, wd_bf, ntile, n):
    b = h2t.shape[0]
    d, f = wg_bf.shape[1], wg_bf.shape[2]
    batch = lambda c, j, *_: (c, 0, 0)
    kern = functools.partial(_experts_kernel, ntile=ntile, n=n)
    slab = pltpu.VMEM((CHUNKS * SLAB, LANES), F32)
    hbm = pl.BlockSpec(memory_space=pl.ANY)
    grid_spec = pltpu.PrefetchScalarGridSpec(
        num_scalar_prefetch=5,
        grid=(b, ntile),
        in_specs=[pl.BlockSpec((1, ntile, SLOT_TILE), batch, memory_space=pltpu.SMEM),
                  pl.BlockSpec((1, ntile, SLOT_TILE), batch),
                  hbm, hbm, hbm, hbm],
        out_specs=hbm,
        scratch_shapes=[pltpu.VMEM(((n + 1) * CHUNKS, LANES), F32),
                        pltpu.VMEM(((n + 1) * CHUNKS, LANES), F32),
                        slab, slab, slab, slab,
                        pltpu.VMEM((2, d, f), BF16), pltpu.VMEM((2, d, f), BF16), pltpu.VMEM((2, f, d), BF16),
                        pltpu.SemaphoreType.DMA((2,)), pltpu.SemaphoreType.DMA((2,))],
    )
    return pl.pallas_call(
        kern,
        grid_spec=grid_spec,
        out_shape=jax.ShapeDtypeStruct((b, n * CHUNKS, LANES), F32),
        compiler_params=_cparams(("arbitrary", "arbitrary")),
        name="experts",
    )(block_e, ntiles, first, nxt_e, wslot, slot_row, slot_w, h2t, wg_bf, wu_bf, wd_bf)


EXPERTS_PER_LANE_GROUP = LANES // 32


def _plan_tables_kernel(w_ref, re_ref, ro_ref, int_ref, wt_ref, *, rows):
    ne = w_ref.shape[0]
    w2 = w_ref[...].reshape(ne * rows, LANES)
    selb = jnp.where(w2 > 0.0, 1.0, 0.0).astype(BF16)
    i0 = lax.broadcasted_iota(jnp.int32, (LANES, LANES), 0)
    i1 = lax.broadcasted_iota(jnp.int32, (LANES, LANES), 1)
    rowtot = _dot(selb, jnp.ones((LANES, LANES), BF16))
    ridx = lax.broadcasted_iota(jnp.int32, rowtot.shape, 0) & (rows - 1)
    acc = rowtot
    sh = 1
    while sh < rows:
        acc = acc + jnp.where(ridx >= sh, pltpu.roll(acc, sh, 0), 0.0)
        sh *= 2
    re_ref[0] = acc
    ro_ref[0] = acc - rowtot
    incl_t = _dot_nt(jnp.where(i1 <= i0, 1.0, 0.0).astype(BF16), selb)
    eye = jnp.where(i0 == i1, 1.0, 0.0).astype(BF16)
    w_hi = w2.astype(BF16)
    r1 = w2 - w_hi.astype(F32)
    w_mid = r1.astype(BF16)
    w_lo = (r1 - w_mid.astype(F32)).astype(BF16)
    parts = [_dot_nt(eye, p) for p in (w_hi, w_mid, w_lo)]
    for g in range(ne * rows // LANES):
        sl = slice(g * LANES, (g + 1) * LANES)
        int_ref[0, g] = incl_t[:, sl].astype(BF16)
        for k in range(3):
            wt_ref[0, k, g] = parts[k][:, sl].astype(BF16)


def _plan_tables(selw3, b):
    ne, r_all, _ = selw3.shape
    rows = r_all // b
    groups = ne * rows // LANES
    kern = functools.partial(_plan_tables_kernel, rows=rows)
    return pl.pallas_call(
        kern,
        grid=(b,),
        in_specs=[pl.BlockSpec((ne, rows, LANES), lambda c: (0, c, 0))],
        out_specs=[pl.BlockSpec((1, ne * rows, LANES), lambda c: (c, 0, 0)),
                   pl.BlockSpec((1, ne * rows, LANES), lambda c: (c, 0, 0)),
                   pl.BlockSpec((1, groups, LANES, LANES), lambda c: (c, 0, 0, 0)),
                   pl.BlockSpec((1, 3, groups, LANES, LANES), lambda c: (c, 0, 0, 0, 0))],
        out_shape=[jax.ShapeDtypeStruct((b, ne * rows, LANES), F32),
                   jax.ShapeDtypeStruct((b, ne * rows, LANES), F32),
                   jax.ShapeDtypeStruct((b, groups, LANES, LANES), BF16),
                   jax.ShapeDtypeStruct((b, 3, groups, LANES, LANES), BF16)],
        compiler_params=_cparams(("arbitrary",)),
        name="plan_tables",
    )(selw3)


def _plan_slots_kernel(be_ref, q0_ref, nt_ref, re_ref, ro_ref, int_ref, wt_ref, tok_ref, sw_ref,
                       *, ntile, rows, n):
    c = pl.program_id(0)
    tok_ref[...] = jnp.full(tok_ref.shape, n * CHUNKS, jnp.int32)
    sw_ref[...] = jnp.zeros(sw_ref.shape, F32)
    lane = lax.broadcasted_iota(jnp.int32, (1, SLOT_TILE), 1).astype(F32)
    sub_r = lax.broadcasted_iota(jnp.int32, (rows, SLOT_TILE), 0).astype(F32)
    sub_l = lax.broadcasted_iota(jnp.int32, (LANES, SLOT_TILE), 0).astype(F32)
    reps = SLOT_TILE // LANES

    def tile_body(j, carry):
        e = be_ref[c * ntile + j]
        q = q0_ref[c * ntile + j].astype(F32) + lane
        base = pl.multiple_of(e * rows, rows)
        row_end = jnp.concatenate([re_ref[0, pl.ds(base, rows), :]] * reps, axis=1)
        row_off = jnp.concatenate([ro_ref[0, pl.ds(base, rows), :]] * reps, axis=1)
        r = jnp.sum(jnp.where(row_end <= q, 1.0, 0.0), axis=0, keepdims=True)
        row_start = jnp.sum(jnp.where(sub_r == r, row_off, 0.0), axis=0, keepdims=True)
        g = lax.shift_right_logical(e, EXPERTS_PER_LANE_GROUP.bit_length() - 1)
        col = r + ((e & (EXPERTS_PER_LANE_GROUP - 1)) * rows).astype(F32)
        onehot = jnp.where(sub_l == col, 1.0, 0.0).astype(BF16)
        incl = _dot(int_ref[0, g], onehot)
        lane_idx = jnp.sum(jnp.where(incl <= q - row_start, 1.0, 0.0), axis=0, keepdims=True)
        wrow = (_dot(wt_ref[0, 0, g], onehot) + _dot(wt_ref[0, 1, g], onehot)) + _dot(wt_ref[0, 2, g], onehot)
        wsel = jnp.sum(jnp.where(sub_l == lane_idx, wrow, 0.0), axis=0, keepdims=True)
        valid = q < row_end[rows - 1:rows, :]
        tok = jnp.where(valid, r * LANES + lane_idx, float(n)).astype(jnp.int32)
        tok_ref[0, pl.ds(j, 1), :] = tok * CHUNKS
        sw_ref[0, pl.ds(j, 1), :] = jnp.where(valid, wsel, 0.0)
        return carry

    lax.fori_loop(0, nt_ref[c], tile_body, 0)


def _plan_slots(block_e, tile_q0, ntiles, row_end, row_off, incl_t, w_t, ntile, n):
    b, er, _ = row_end.shape
    rows = n // LANES
    groups = incl_t.shape[1]
    kern = functools.partial(_plan_slots_kernel, ntile=ntile, rows=rows, n=n)
    grid_spec = pltpu.PrefetchScalarGridSpec(
        num_scalar_prefetch=3,
        grid=(b,),
        in_specs=[pl.BlockSpec((1, er, LANES), lambda c, *_: (c, 0, 0)),
                  pl.BlockSpec((1, er, LANES), lambda c, *_: (c, 0, 0)),
                  pl.BlockSpec((1, groups, LANES, LANES), lambda c, *_: (c, 0, 0, 0)),
                  pl.BlockSpec((1, 3, groups, LANES, LANES), lambda c, *_: (c, 0, 0, 0, 0))],
        out_specs=[pl.BlockSpec((1, ntile, SLOT_TILE), lambda c, *_: (c, 0, 0)),
                   pl.BlockSpec((1, ntile, SLOT_TILE), lambda c, *_: (c, 0, 0))],
    )
    return pl.pallas_call(
        kern,
        grid_spec=grid_spec,
        out_shape=[jax.ShapeDtypeStruct((b, ntile, SLOT_TILE), jnp.int32),
                   jax.ShapeDtypeStruct((b, ntile, SLOT_TILE), F32)],
        compiler_params=_cparams(("arbitrary",)),
        name="plan_slots",
    )(block_e, tile_q0, ntiles, row_end, row_off, incl_t, w_t)


def _tile_meta(row_end, b, n, ntile):
    rows = n // LANES
    cnt = row_end.reshape(b, N_EXPERTS, rows, LANES)[:, :, rows - 1, 0].astype(jnp.int32)
    nt_e = (cnt + SLOT_TILE - 1) // SLOT_TILE
    tend = jnp.cumsum(nt_e, axis=1)
    tstart = tend - nt_e
    ntiles = tend[:, -1]
    tile_id = jnp.arange(ntile, dtype=jnp.int32)
    tid = jnp.minimum(tile_id[None, :], ntiles[:, None] - 1)
    be = jnp.minimum(jnp.sum(tend[:, None, :] <= tid[:, :, None], axis=2).astype(jnp.int32), N_EXPERTS - 1)
    q0 = (tid - jnp.take_along_axis(tstart, be, axis=1)) * SLOT_TILE
    valid = tile_id[None, :] < ntiles[:, None]
    prev_be = jnp.concatenate([jnp.full((b, 1), -1, jnp.int32), be[:, :-1]], axis=1)
    first = (valid & (be != prev_be)).astype(jnp.int32)
    wslot = (jnp.cumsum(first, axis=1) - 1) % 2
    eid = jnp.arange(N_EXPERTS, dtype=jnp.int32)
    later = (nt_e > 0)[:, None, :] & (eid[None, None, :] > eid[None, :, None])
    nxt = jnp.min(jnp.where(later, eid[None, None, :], N_EXPERTS), axis=2)
    nxt = jnp.where(nxt < N_EXPERTS, nxt, -1)
    nxt_e = jnp.take_along_axis(nxt, be, axis=1)
    flat = lambda a: a.reshape(-1).astype(jnp.int32)
    return flat(be), flat(q0), ntiles.astype(jnp.int32), flat(first), flat(nxt_e), flat(wslot)


def _dispatch_plan(selw3, b, n, ntile):
    row_end, row_off, incl_t, w_t = _plan_tables(selw3, b)
    block_e, tile_q0, ntiles, first, nxt_e, wslot = _tile_meta(row_end, b, n, ntile)
    slot_tok, slot_w = _plan_slots(block_e, tile_q0, ntiles, row_end, row_off, incl_t, w_t, ntile, n)
    return block_e, ntiles, first, nxt_e, wslot, slot_tok, slot_w


def _final_kernel(h2_ref, r_ref, x1_ref, mod_ref, wg_ref, wu_ref, wd_ref, g_ref, o_ref):
    hb = h2_ref[0]
    tm = hb.shape[0]
    sh = _dot((_silu(_dot(hb, wg_ref[...])) * _dot(hb, wu_ref[...])).astype(BF16), wd_ref[...])
    routed = jnp.concatenate([r_ref[0, pl.ds(k, tm, stride=CHUNKS), :] for k in range(CHUNKS)], axis=1)
    moe = routed + sh
    o_ref[0] = x1_ref[0] + mod_ref[0, 5:6, :] * _rms(moe, g_ref[...])


def _final(h2, routed, x1, mod3, wsg_bf, wsu_bf, wsd_bf, g_post, tm=512):
    b, n, d = x1.shape
    f = wsg_bf.shape[1]
    row = lambda bi, i: (bi, i, 0)
    vec = lambda bi, i: (0, 0)
    return pl.pallas_call(
        _final_kernel,
        grid=(b, n // tm),
        in_specs=[pl.BlockSpec((1, tm, d), row),
                  pl.BlockSpec((1, tm * CHUNKS, LANES), row),
                  pl.BlockSpec((1, tm, d), row),
                  pl.BlockSpec((1, 6, d), lambda bi, i: (bi, 0, 0)),
                  pl.BlockSpec((d, f), vec),
                  pl.BlockSpec((d, f), vec),
                  pl.BlockSpec((f, d), vec),
                  pl.BlockSpec((1, d), vec)],
        out_specs=pl.BlockSpec((1, tm, d), row),
        out_shape=jax.ShapeDtypeStruct((b, n, d), F32),
        compiler_params=_cparams(("arbitrary", "arbitrary")),
        name="final",
    )(h2, routed, x1, mod3, wsg_bf, wsu_bf, wsd_bf, g_post)


def _rope_tables(n):
    t = np.arange(n)
    row = (t // GRID_W).astype(np.float32)
    col = (t % GRID_W).astype(np.float32)
    freqs = np.float32(ROPE_THETA) ** (-np.arange(0, AXIS_DIM, 2, dtype=np.float32) / np.float32(AXIS_DIM))
    ar = row[:, None] * freqs
    ac = col[:, None] * freqs
    zeros = np.zeros_like(ar)
    cos64 = np.concatenate([np.cos(ar), np.cos(ar), np.cos(ac), np.cos(ac)], axis=1)
    sa64 = np.concatenate([-np.sin(ar), zeros, -np.sin(ac), zeros], axis=1)
    sb64 = np.concatenate([zeros, np.sin(ar), zeros, np.sin(ac)], axis=1)
    two = lambda a: jnp.asarray(np.concatenate([a, a], axis=1).astype(np.float32))
    return two(cos64), two(sa64), two(sb64)


def kernel(x, c, ctx, c_ctx, w_ada, b_ada, g_pre_mix, g_post_mix, g_pre_ffn, g_post_ffn, w_in, lambda_q1, lambda_k1, lambda_q2, lambda_k2, g_subln, w_pool, b_pool, pool_scale, w_out, w_router, router_bias, w_e_gate, w_e_up, w_e_down, w_sh_gate, w_sh_up, w_sh_down):
    b, n, d = x.shape
    l = 0
    mod_rows = 8
    cin = jnp.concatenate([c, c_ctx[None, :], jnp.zeros((mod_rows - b - 1, d), F32)], axis=0)
    mod3 = _ada(cin, w_ada[l], b_ada[l]).reshape(mod_rows, 6, d)

    w_in_bf = w_in[l].astype(BF16)
    cos, sa, sb = _rope_tables(n)
    q, k, vt, p = _inproj(x, mod3, g_pre_mix[l][None, :], w_in_bf, cos * (QK_DIM ** -0.5 * LOG2E), cos, sa, sb)
    kc, vct = _ctxkv(ctx, mod3, g_pre_mix[l][None, :], w_in_bf, b)

    lamv = jnp.stack([lambda_q1[l], lambda_k1[l], lambda_q2[l], lambda_k2[l]], axis=0)
    att = _attn(q, k, kc, vt, vct, lamv, g_subln[l][:, None])

    x1, h2, h2t, logits_t = _outproj(
        att, p, x, mod3, w_pool[l].astype(BF16), b_pool[l][None, :], pool_scale[l][None, :],
        w_out[l].astype(BF16), g_post_mix[l][None, :], g_pre_ffn[l][None, :], w_router[l].T)

    t = b * n
    selw3 = _route(logits_t.reshape(N_EXPERTS, t // LANES, LANES),
                   jnp.broadcast_to(router_bias[l][:, None, None], (N_EXPERTS, 1, LANES)))

    ntile = (n * TOP_K) // SLOT_TILE + N_EXPERTS
    block_e, ntiles, first, nxt_e, wslot, slot_row, slot_w = _dispatch_plan(selw3, b, n, ntile)
    routed = _experts(block_e, ntiles, first, nxt_e, wslot, slot_row, slot_w, h2t,
                      w_e_gate[l].astype(BF16), w_e_up[l].astype(BF16), w_e_down[l].astype(BF16), ntile, n)

    return _final(h2, routed, x1, mod3, w_sh_gate[l].astype(BF16), w_sh_up[l].astype(BF16),
                  w_sh_down[l].astype(BF16), g_post_ffn[l][None, :])
```

```python
import functools
import math

import jax
import jax.numpy as jnp
import numpy as np
from jax import lax
from jax.experimental import pallas as pl
from jax.experimental.pallas import tpu as pltpu

F32 = jnp.float32
BF16 = jnp.bfloat16

D_MODEL = 1024
GRID_W = 64
N_HEADS = 4
QK_DIM = 64
V_DIM = 128
ATTN_WIDTH = N_HEADS * V_DIM
POOL_WIDTH = D_MODEL - ATTN_WIDTH
POOL_WINDOWS = (2, 4, 8, 16)
POOL_GROUP = POOL_WIDTH // len(POOL_WINDOWS)
AXIS_DIM = QK_DIM // 2
ROPE_THETA = 10000.0
N_EXPERTS = 64
TOP_K = 8
N_GROUPS = 8
GROUP_SIZE = N_EXPERTS // N_GROUPS
TOPK_GROUPS = 4
EXPERT_DIM = 256
ROUTED_SCALE = 2.5
EPS = 1e-6
LAM_INIT = 0.8 - 0.6 * math.exp(-0.3 * 0)

LANES = 128
SLOT_TILE = 256
HALO = 16
VMEM_LIMIT = 56 * 1024 * 1024


def _cparams(sem, vmem=VMEM_LIMIT):
    return pltpu.CompilerParams(dimension_semantics=sem, vmem_limit_bytes=vmem)


def _split(a):
    hi = a.astype(BF16)
    lo = (a - hi.astype(F32)).astype(BF16)
    return hi, lo


def _dot(a, b):
    return jnp.dot(a, b, preferred_element_type=F32)


def _dot_nt(a, b):
    return lax.dot_general(a, b, (((1,), (1,)), ((), ())), preferred_element_type=F32)


def _dot3(a, b, nt=False):
    d = _dot_nt if nt else _dot
    ah, al = _split(a)
    bh, bl = _split(b)
    return d(ah, bh) + d(ah, bl) + d(al, bh)


def _rms(x, g):
    return x * lax.rsqrt(jnp.mean(x * x, axis=-1, keepdims=True) + EPS) * g


def _silu(x):
    return x * (1.0 / (1.0 + jnp.exp(-x)))


def _ada_kernel(c_ref, w_ref, b_ref, o_ref):
    o_ref[...] = _dot3(_silu(c_ref[...]), w_ref[...]) + b_ref[...]


def _ada(cin, w_ada, b_ada):
    rows, d = cin.shape
    n = w_ada.shape[1]
    tn = 1024
    return pl.pallas_call(
        _ada_kernel,
        grid=(n // tn,),
        in_specs=[pl.BlockSpec((rows, d), lambda j: (0, 0)),
                  pl.BlockSpec((d, tn), lambda j: (0, j)),
                  pl.BlockSpec((1, tn), lambda j: (0, j))],
        out_specs=pl.BlockSpec((rows, tn), lambda j: (0, j)),
        out_shape=jax.ShapeDtypeStruct((rows, n), F32),
        compiler_params=_cparams(("arbitrary",)),
        name="ada",
    )(cin, w_ada, b_ada.reshape(1, n))


def _rope(t, cos, sa, sb):
    return t * cos + pltpu.roll(t, LANES - AXIS_DIM // 2, 1) * sa + pltpu.roll(t, AXIS_DIM // 2, 1) * sb


def _inproj_kernel(x_ref, mod_ref, g_ref, w_ref, cq_ref, ck_ref, sa_ref, sb_ref,
                   q_ref, k_ref, vt_ref, p_ref):
    x = x_ref[0]
    h = _rms(x, g_ref[...]) * (1.0 + mod_ref[0, 1:2, :]) + mod_ref[0, 0:1, :]
    px = _dot(h.astype(BF16), w_ref[...])
    ck = ck_ref[...]
    sa = sa_ref[...]
    sb = sb_ref[...]
    cq = cq_ref[...]
    scale = QK_DIM ** -0.5 * LOG2E
    saq = sa * scale
    sbq = sb * scale
    for hh in range(N_HEADS):
        lo = hh * LANES
        q = px[:, lo:lo + LANES]
        k = px[:, ATTN_WIDTH + lo:ATTN_WIDTH + lo + LANES]
        q_ref[0, :, lo:lo + LANES] = _rope(q, cq, saq, sbq).astype(BF16)
        k_ref[0, :, lo:lo + LANES] = _rope(k, ck, sa, sb).astype(BF16)
    vt_ref[0] = jnp.transpose(px[:, 2 * ATTN_WIDTH:3 * ATTN_WIDTH]).astype(BF16)
    p_ref[0] = px[:, 3 * ATTN_WIDTH:].astype(BF16)


def _inproj(x, mod3, g, w_in_bf, cq, ck, sa, sb, tm=512):
    b, n, d = x.shape
    width = w_in_bf.shape[1]
    row = lambda bi, i: (bi, i, 0)
    tab = pl.BlockSpec((tm, LANES), lambda bi, i: (i, 0))
    out = jax.ShapeDtypeStruct((b, n, ATTN_WIDTH), BF16)
    return pl.pallas_call(
        _inproj_kernel,
        grid=(b, n // tm),
        in_specs=[pl.BlockSpec((1, tm, d), row),
                  pl.BlockSpec((1, 6, d), lambda bi, i: (bi, 0, 0)),
                  pl.BlockSpec((1, d), lambda bi, i: (0, 0)),
                  pl.BlockSpec((d, width), lambda bi, i: (0, 0)),
                  tab, tab, tab, tab],
        out_specs=[pl.BlockSpec((1, tm, ATTN_WIDTH), row),
                   pl.BlockSpec((1, tm, ATTN_WIDTH), row),
                   pl.BlockSpec((1, ATTN_WIDTH, tm), lambda bi, i: (bi, 0, i)),
                   pl.BlockSpec((1, tm, ATTN_WIDTH), row)],
        out_shape=[out, out, jax.ShapeDtypeStruct((b, ATTN_WIDTH, n), BF16), out],
        compiler_params=_cparams(("arbitrary", "arbitrary")),
        name="inproj",
    )(x, mod3, g, w_in_bf, cq, ck, sa, sb)


def _ctxkv_kernel(x_ref, mod_ref, g_ref, wk_ref, wv_ref, k_ref, vt_ref):
    h = _rms(x_ref[0], g_ref[...]) * (1.0 + mod_ref[0, 1:2, :]) + mod_ref[0, 0:1, :]
    hb = h.astype(BF16)
    k_ref[0] = _dot(hb, wk_ref[...]).astype(BF16)
    vt_ref[0] = jnp.transpose(_dot(hb, wv_ref[...])).astype(BF16)


def _ctxkv(ctx, mod3, g, w_in_bf, ctx_row):
    b, n, d = ctx.shape
    out = jax.ShapeDtypeStruct((b, n, ATTN_WIDTH), BF16)
    return pl.pallas_call(
        _ctxkv_kernel,
        grid=(b,),
        in_specs=[pl.BlockSpec((1, n, d), lambda bi: (bi, 0, 0)),
                  pl.BlockSpec((1, 6, d), lambda bi: (ctx_row, 0, 0)),
                  pl.BlockSpec((1, d), lambda bi: (0, 0)),
                  pl.BlockSpec((d, ATTN_WIDTH), lambda bi: (0, 1)),
                  pl.BlockSpec((d, ATTN_WIDTH), lambda bi: (0, 2))],
        out_specs=[pl.BlockSpec((1, n, ATTN_WIDTH), lambda bi: (bi, 0, 0)),
                   pl.BlockSpec((1, ATTN_WIDTH, n), lambda bi: (bi, 0, 0))],
        out_shape=[out, jax.ShapeDtypeStruct((b, ATTN_WIDTH, n), BF16)],
        compiler_params=_cparams(("arbitrary",)),
        name="ctxkv",
    )(ctx, mod3, g, w_in_bf, w_in_bf)


LOG2E = math.log2(math.e)
KEY_CHUNK = 256


def _attn_step(q_ref, kx_ref, kc_ref, vt_ref, vct_ref, lam_ref, g_ref, o_ref, s_new, m_new, s_old, m_old):
    nc = kc_ref.shape[1]
    tq = q_ref.shape[1]
    nk = s_old.shape[1]
    groups = KEY_CHUNK // 8
    q = q_ref[0]
    lane = lax.broadcasted_iota(jnp.int32, q.shape, 1)
    zero = jnp.zeros_like(q)
    qms = (jnp.where(lane < QK_DIM, q, zero), jnp.where(lane >= QK_DIM, q, zero))
    m_prev = (m_old[0], m_old[1])
    top = [jnp.full((8, tq), -jnp.inf, F32) for _ in range(2)]
    den = [jnp.zeros((8, tq), F32) for _ in range(2)]
    acc = [jnp.zeros((V_DIM, tq), F32) for _ in range(2)]
    for c in range(nk // KEY_CHUNK):
        lo = c * KEY_CHUNK
        if lo < nc:
            keys, vt = kc_ref[0, lo:lo + KEY_CHUNK, :], vct_ref[0, :, lo:lo + KEY_CHUNK]
        else:
            keys, vt = kx_ref[0, lo - nc:lo - nc + KEY_CHUNK, :], vt_ref[0, :, lo - nc:lo - nc + KEY_CHUNK]
        for mp in range(2):
            s = _dot_nt(keys, qms[mp])
            s_new[mp, lo:lo + KEY_CHUNK, :] = s
            top[mp] = jnp.maximum(top[mp], jnp.max(s.reshape(groups, 8, tq), axis=0))
            p = jnp.exp2(s_old[mp, lo:lo + KEY_CHUNK, :] - m_prev[mp])
            den[mp] = den[mp] + jnp.sum(p.reshape(groups, 8, tq), axis=0)
            acc[mp] = acc[mp] + _dot(vt, p.astype(BF16))
    for mp in range(2):
        m_new[mp] = jnp.max(top[mp], axis=0, keepdims=True)
    lv = lam_ref[...]
    lam = (jnp.exp(jnp.sum(lv[0:1] * lv[1:2], axis=-1, keepdims=True))
           - jnp.exp(jnp.sum(lv[2:3] * lv[3:4], axis=-1, keepdims=True)) + LAM_INIT)
    l1 = jnp.sum(den[0], axis=0, keepdims=True)
    l2 = jnp.sum(den[1], axis=0, keepdims=True)
    o = acc[0] * (1.0 / l1) - acc[1] * (lam / l2)
    o = o * lax.rsqrt(jnp.mean(o * o, axis=0, keepdims=True) + EPS) * (g_ref[...] * (1.0 - LAM_INIT))
    o_ref[0] = jnp.transpose(o).astype(BF16)


def _attn_kernel(q_ref, kx_ref, kc_ref, vt_ref, vct_ref, lam_ref, g_ref, o_ref, s0_ref, m0_ref, s1_ref, m1_ref):
    t = pl.program_id(0)
    io = (q_ref, kx_ref, kc_ref, vt_ref, vct_ref, lam_ref, g_ref, o_ref)

    @pl.when(t == 0)
    def _():
        s1_ref[...] = jnp.zeros_like(s1_ref)
        m1_ref[...] = jnp.zeros_like(m1_ref)

    @pl.when(lax.rem(t, 2) == 0)
    def _():
        _attn_step(*io, s0_ref, m0_ref, s1_ref, m1_ref)

    @pl.when(lax.rem(t, 2) == 1)
    def _():
        _attn_step(*io, s1_ref, m1_ref, s0_ref, m0_ref)


def _attn(q, k, kc, vt, vct, lamv, g_col, tq=256):
    b, n, _ = q.shape
    nc = kc.shape[1]
    nq = n // tq
    total = b * N_HEADS * nq

    def cur(t):
        ta = jnp.minimum(t, total - 1)
        bh = ta // nq
        return bh // N_HEADS, bh % N_HEADS, ta % nq

    def prv(t):
        tb = jnp.maximum(t - 1, 0)
        bh = tb // nq
        return bh // N_HEADS, bh % N_HEADS, tb % nq

    def at(fn, order):
        return lambda t: tuple((fn(t) + (0,))[j] for j in order)

    return pl.pallas_call(
        _attn_kernel,
        grid=(total + 1,),
        in_specs=[pl.BlockSpec((1, tq, LANES), at(cur, (0, 2, 1))),
                  pl.BlockSpec((1, n, LANES), at(cur, (0, 3, 1))),
                  pl.BlockSpec((1, nc, LANES), at(cur, (0, 3, 1))),
                  pl.BlockSpec((1, V_DIM, n), at(prv, (0, 1, 3))),
                  pl.BlockSpec((1, V_DIM, nc), at(prv, (0, 1, 3))),
                  pl.BlockSpec((4, QK_DIM), lambda t: (0, 0)),
                  pl.BlockSpec((V_DIM, 1), lambda t: (0, 0))],
        out_specs=pl.BlockSpec((1, tq, LANES), at(prv, (0, 2, 1))),
        out_shape=jax.ShapeDtypeStruct((b, n, ATTN_WIDTH), BF16),
        scratch_shapes=[pltpu.VMEM((2, nc + n, tq), F32), pltpu.VMEM((2, 1, tq), F32),
                        pltpu.VMEM((2, nc + n, tq), F32), pltpu.VMEM((2, 1, tq), F32)],
        compiler_params=_cparams(("arbitrary",)),
        name="attn",
    )(q, k, kc, vt, vct, lamv, g_col)


def _outproj_kernel(att_ref, p_ref, pprev_ref, pnext_ref, x_ref, mod_ref, wpool_ref, bpool_ref,
                    pscale_ref, wout_ref, gpost_ref, gpre_ref, wr_ref,
                    x1_ref, h2_ref, h2t_ref, lg_ref, *, tm, n):
    i = pl.program_id(1)
    p = p_ref[0].astype(F32)
    prev = jnp.where(i > 0, pprev_ref[0].astype(F32), 0.0)
    nxt = jnp.where(i < pl.num_programs(1) - 1, pnext_ref[0].astype(F32), 0.0)
    ext = jnp.concatenate([prev, p, nxt], axis=0)
    rows = tm + 2 * HALO
    trow = (i * tm + lax.broadcasted_iota(jnp.int32, (tm, 1), 0))

    pooled = []
    sums = {}
    acc = ext + pltpu.roll(ext, 1, 0)
    sums[2] = acc
    w = 2
    while w < POOL_WINDOWS[-1]:
        acc = pltpu.roll(acc, rows - w // 2, 0) + pltpu.roll(acc, w // 2, 0)
        w = 2 * w
        sums[w] = acc
    for gi, win in enumerate(POOL_WINDOWS):
        sl = slice(gi * POOL_GROUP, (gi + 1) * POOL_GROUP)
        wsum = sums[win][HALO:HALO + tm, sl]
        hi_c = jnp.minimum(trow + (win - win // 2), n)
        lo_c = jnp.maximum(trow - win // 2, 0)
        cnt = (hi_c - lo_c).astype(F32)
        d = wsum / cnt - p[:, sl]
        y = _dot(d.astype(BF16), wpool_ref[gi]) + bpool_ref[:, sl]
        pooled.append((y * pscale_ref[:, sl]).astype(BF16))
    pool = jnp.concatenate(pooled, axis=1)
    yx = _dot(att_ref[0], wout_ref[0:ATTN_WIDTH, :]) + _dot(pool, wout_ref[ATTN_WIDTH:, :])
    x1 = x_ref[0] + mod_ref[0, 2:3, :] * _rms(yx, gpost_ref[...])
    x1_ref[0] = x1
    h2 = _rms(x1, gpre_ref[...]) * (1.0 + mod_ref[0, 4:5, :]) + mod_ref[0, 3:4, :]
    h2_ref[0] = h2.astype(BF16)
    for k in range(CHUNKS):
        h2t_ref[0, pl.ds(k, tm, stride=CHUNKS), :] = h2[:, k * LANES:(k + 1) * LANES]
    lg_ref[...] = _dot3(wr_ref[...], h2, nt=True)


def _outproj(att, p, x, mod3, w_pool_bf, b_pool, pool_scale, w_out_bf, g_post, g_pre, w_router_t, tm=256):
    b, n, d = x.shape
    hb = tm // HALO
    nhb = n // HALO
    row = lambda bi, i: (bi, i, 0)
    vec = lambda bi, i: (0, 0)
    kern = functools.partial(_outproj_kernel, tm=tm, n=n)
    return pl.pallas_call(
        kern,
        grid=(b, n // tm),
        in_specs=[pl.BlockSpec((1, tm, ATTN_WIDTH), row),
                  pl.BlockSpec((1, tm, POOL_WIDTH), row),
                  pl.BlockSpec((1, HALO, POOL_WIDTH), lambda bi, i: (bi, jnp.maximum(i * hb - 1, 0), 0)),
                  pl.BlockSpec((1, HALO, POOL_WIDTH), lambda bi, i: (bi, jnp.minimum((i + 1) * hb, nhb - 1), 0)),
                  pl.BlockSpec((1, tm, d), row),
                  pl.BlockSpec((1, 6, d), lambda bi, i: (bi, 0, 0)),
                  pl.BlockSpec((len(POOL_WINDOWS), POOL_GROUP, POOL_GROUP), lambda bi, i: (0, 0, 0)),
                  pl.BlockSpec((1, POOL_WIDTH), vec),
                  pl.BlockSpec((1, POOL_WIDTH), vec),
                  pl.BlockSpec((d, d), vec),
                  pl.BlockSpec((1, d), vec),
                  pl.BlockSpec((1, d), vec),
                  pl.BlockSpec((N_EXPERTS, d), vec)],
        out_specs=[pl.BlockSpec((1, tm, d), row),
                   pl.BlockSpec((1, tm, d), row),
                   pl.BlockSpec((1, tm * CHUNKS, LANES), row),
                   pl.BlockSpec((N_EXPERTS, tm), lambda bi, i: (0, bi * (n // tm) + i))],
        out_shape=[jax.ShapeDtypeStruct((b, n, d), F32),
                   jax.ShapeDtypeStruct((b, n, d), BF16),
                   jax.ShapeDtypeStruct((b, n * CHUNKS, LANES), F32),
                   jax.ShapeDtypeStruct((N_EXPERTS, b * n), F32)],
        compiler_params=_cparams(("arbitrary", "arbitrary")),
        name="outproj",
    )(att, p, p, p, x, mod3, w_pool_bf, b_pool, pool_scale, w_out_bf, g_post, g_pre, w_router_t)


def _beats(a, b, a_first):
    return jnp.where((a >= b) if a_first else (a > b), 1.0, 0.0)


def _route_kernel(lg_ref, bias_ref, w_ref):
    scores = [1.0 / (1.0 + jnp.exp(-lg_ref[e])) for e in range(N_EXPERTS)]
    biased = [scores[e] + bias_ref[e] for e in range(N_EXPERTS)]
    gscore = []
    for g in range(N_GROUPS):
        vals = biased[g * GROUP_SIZE:(g + 1) * GROUP_SIZE]
        m1 = vals[0]
        m2 = jnp.full_like(m1, -jnp.inf)
        for v in vals[1:]:
            m2 = jnp.maximum(m2, jnp.minimum(m1, v))
            m1 = jnp.maximum(m1, v)
        gscore.append(m1 + m2)
    gsel = []
    for g in range(N_GROUPS):
        rank = jnp.zeros_like(gscore[g])
        for g2 in range(N_GROUPS):
            if g2 != g:
                rank = rank + _beats(gscore[g2], gscore[g], g2 < g)
        gsel.append(rank < TOPK_GROUPS)
    masked = [jnp.where(gsel[e // GROUP_SIZE], biased[e], -jnp.inf) for e in range(N_EXPERTS)]
    picked = []
    for e in range(N_EXPERTS):
        rank = jnp.zeros_like(masked[e])
        for e2 in range(N_EXPERTS):
            if e2 != e:
                rank = rank + _beats(masked[e2], masked[e], e2 < e)
        picked.append(jnp.where(rank < TOP_K, scores[e], 0.0))
    denom = picked[0]
    for e in range(1, N_EXPERTS):
        denom = denom + picked[e]
    for e in range(N_EXPERTS):
        w_ref[e] = picked[e] / denom * ROUTED_SCALE


def _route(logits3, bias3, rows=8):
    e, r, l = logits3.shape
    return pl.pallas_call(
        _route_kernel,
        grid=(r // rows,),
        in_specs=[pl.BlockSpec((e, rows, l), lambda i: (0, i, 0)),
                  pl.BlockSpec((e, 1, l), lambda i: (0, 0, 0))],
        out_specs=pl.BlockSpec((e, rows, l), lambda i: (0, i, 0)),
        out_shape=jax.ShapeDtypeStruct((e, r, l), F32),
        compiler_params=_cparams(("arbitrary",)),
        name="route",
    )(logits3, bias3)


CHUNKS = D_MODEL // LANES
SLAB = SLOT_TILE + 4
ROW_BATCH = 16


def _gather_tile(row_ref, tile, xs_ref, tx_ref):
    for s in range(SLOT_TILE):
        row = pl.multiple_of(row_ref[0, tile, s], CHUNKS)
        tx_ref[pl.ds(s, CHUNKS, stride=SLAB), :] = xs_ref[pl.ds(row, CHUNKS), :]


def _experts_step(j, wslot, row_ref, sw_ref, wg_buf, wu_buf, wd_buf, xs_ref, acc_ref,
                  tx_cur, tx_nxt, ty_cur, ty_prv):
    ntile = row_ref.shape[1]
    _gather_tile(row_ref, jnp.minimum(j + 1, ntile - 1), xs_ref, tx_nxt)
    xb = jnp.concatenate([tx_cur[pl.ds(k * SLAB, SLOT_TILE), :] for k in range(CHUNKS)], axis=1).astype(BF16)
    g = _dot(xb, wg_buf[wslot])
    u = _dot(xb, wu_buf[wslot])
    h = (_silu(g) * u).astype(BF16)
    y = _dot(h, wd_buf[wslot])
    w_rows = jnp.transpose(jnp.broadcast_to(sw_ref[0, pl.ds(j, 1), :], (LANES, SLOT_TILE)))
    for k in range(CHUNKS):
        ty_cur[pl.ds(k * SLAB, SLOT_TILE), :] = y[:, k * LANES:(k + 1) * LANES] * w_rows
    prv = jnp.maximum(j - 1, 0)
    for s0 in range(0, SLOT_TILE, ROW_BATCH):
        new = []
        for s in range(s0, s0 + ROW_BATCH):
            row = pl.multiple_of(row_ref[0, prv, s], CHUNKS)
            new.append((row, acc_ref[pl.ds(row, CHUNKS), :] + ty_prv[pl.ds(s, CHUNKS, stride=SLAB), :]))
        for row, v in new:
            acc_ref[pl.ds(row, CHUNKS), :] = v


def _experts_kernel(be_ref, nt_ref, first_ref, nxt_ref, ws_ref, row_ref, sw_ref, x_hbm,
                    wg_hbm, wu_hbm, wd_hbm, o_hbm,
                    xs_ref, acc_ref, tx0_ref, tx1_ref, ty0_ref, ty1_ref, wg_buf, wu_buf, wd_buf,
                    sem, wsem, *, ntile, n):
    c = pl.program_id(0)
    rows = n * CHUNKS

    def weight_copies(e, slot):
        return [pltpu.make_async_copy(src.at[e], dst.at[slot], wsem.at[slot])
                for src, dst in ((wg_hbm, wg_buf), (wu_hbm, wu_buf), (wd_hbm, wd_buf))]

    for wcp in weight_copies(be_ref[c * ntile], 0):
        wcp.start()
    xcp = pltpu.make_async_copy(x_hbm.at[c], xs_ref.at[pl.ds(0, rows)], sem.at[0])
    xcp.start()
    xs_ref[pl.ds(rows, CHUNKS), :] = jnp.zeros((CHUNKS, LANES), F32)
    acc_ref[...] = jnp.zeros_like(acc_ref)
    ty1_ref[...] = jnp.zeros_like(ty1_ref)
    xcp.wait()
    _gather_tile(row_ref, 0, xs_ref, tx0_ref)

    def tile_step(j, tx_cur, tx_nxt, ty_cur, ty_prv):
        t = c * ntile + j
        wslot = ws_ref[t]

        @pl.when(first_ref[t] == 1)
        def _():
            for wcp in weight_copies(be_ref[t], wslot):
                wcp.wait()

        @pl.when((first_ref[t] == 1) & (nxt_ref[t] >= 0))
        def _():
            for wcp in weight_copies(nxt_ref[t], 1 - wslot):
                wcp.start()

        _experts_step(j, wslot, row_ref, sw_ref, wg_buf, wu_buf, wd_buf, xs_ref, acc_ref,
                      tx_cur, tx_nxt, ty_cur, ty_prv)

    def step_pair(p, carry):
        tile_step(2 * p, tx0_ref, tx1_ref, ty0_ref, ty1_ref)
        tile_step(2 * p + 1, tx1_ref, tx0_ref, ty1_ref, ty0_ref)
        return carry

    lax.fori_loop(0, (nt_ref[c] + 2) // 2, step_pair, 0)
    ocp = pltpu.make_async_copy(acc_ref.at[pl.ds(0, rows)], o_hbm.at[c], sem.at[1])
    ocp.start()
    ocp.wait()


def _experts(block_e, ntiles, first, nxt_e, wslot, slot_row, slot_w, h2t, wg_bf, wu_bf, wd_bf, ntile, n):
    b = h2t.shape[0]
    d, f = wg_bf.shape[1], wg_bf.shape[2]
    batch = lambda c, *_: (c, 0, 0)
    kern = functools.partial(_experts_kernel, ntile=ntile, n=n)
    slab = pltpu.VMEM((CHUNKS * SLAB, LANES), F32)
    hbm = pl.BlockSpec(memory_space=pl.ANY)
    grid_spec = pltpu.PrefetchScalarGridSpec(
        num_scalar_prefetch=5,
        grid=(b,),
        in_specs=[pl.BlockSpec((1, ntile, SLOT_TILE), batch, memory_space=pltpu.SMEM),
                  pl.BlockSpec((1, ntile, SLOT_TILE), batch),
                  hbm, hbm, hbm, hbm],
        out_specs=hbm,
        scratch_shapes=[pltpu.VMEM(((n + 1) * CHUNKS, LANES), F32),
                        pltpu.VMEM(((n + 1) * CHUNKS, LANES), F32),
                        slab, slab, slab, slab,
                        pltpu.VMEM((2, d, f), BF16), pltpu.VMEM((2, d, f), BF16), pltpu.VMEM((2, f, d), BF16),
                        pltpu.SemaphoreType.DMA((2,)), pltpu.SemaphoreType.DMA((2,))],
    )
    return pl.pallas_call(
        kern,
        grid_spec=grid_spec,
        out_shape=jax.ShapeDtypeStruct((b, n * CHUNKS, LANES), F32),
        compiler_params=_cparams(("arbitrary",)),
        name="experts",
    )(block_e, ntiles, first, nxt_e, wslot, slot_row, slot_w, h2t, wg_bf, wu_bf, wd_bf)


EXPERTS_PER_LANE_GROUP = LANES // 32


def _plan_tables_kernel(w_ref, re_ref, ro_ref, int_ref, wt_ref, *, rows):
    ne = w_ref.shape[0]
    w2 = w_ref[...].reshape(ne * rows, LANES)
    selb = jnp.where(w2 > 0.0, 1.0, 0.0).astype(BF16)
    i0 = lax.broadcasted_iota(jnp.int32, (LANES, LANES), 0)
    i1 = lax.broadcasted_iota(jnp.int32, (LANES, LANES), 1)
    rowtot = _dot(selb, jnp.ones((LANES, LANES), BF16))
    ridx = lax.broadcasted_iota(jnp.int32, rowtot.shape, 0) & (rows - 1)
    acc = rowtot
    sh = 1
    while sh < rows:
        acc = acc + jnp.where(ridx >= sh, pltpu.roll(acc, sh, 0), 0.0)
        sh *= 2
    re_ref[0] = acc
    ro_ref[0] = acc - rowtot
    incl_t = _dot_nt(jnp.where(i1 <= i0, 1.0, 0.0).astype(BF16), selb)
    eye = jnp.where(i0 == i1, 1.0, 0.0).astype(BF16)
    w_hi = w2.astype(BF16)
    r1 = w2 - w_hi.astype(F32)
    w_mid = r1.astype(BF16)
    w_lo = (r1 - w_mid.astype(F32)).astype(BF16)
    parts = [_dot_nt(eye, p) for p in (w_hi, w_mid, w_lo)]
    for g in range(ne * rows // LANES):
        sl = slice(g * LANES, (g + 1) * LANES)
        int_ref[0, g] = incl_t[:, sl].astype(BF16)
        for k in range(3):
            wt_ref[0, k, g] = parts[k][:, sl].astype(BF16)


def _plan_tables(selw3, b):
    ne, r_all, _ = selw3.shape
    rows = r_all // b
    groups = ne * rows // LANES
    kern = functools.partial(_plan_tables_kernel, rows=rows)
    return pl.pallas_call(
        kern,
        grid=(b,),
        in_specs=[pl.BlockSpec((ne, rows, LANES), lambda c: (0, c, 0))],
        out_specs=[pl.BlockSpec((1, ne * rows, LANES), lambda c: (c, 0, 0)),
                   pl.BlockSpec((1, ne * rows, LANES), lambda c: (c, 0, 0)),
                   pl.BlockSpec((1, groups, LANES, LANES), lambda c: (c, 0, 0, 0)),
                   pl.BlockSpec((1, 3, groups, LANES, LANES), lambda c: (c, 0, 0, 0, 0))],
        out_shape=[jax.ShapeDtypeStruct((b, ne * rows, LANES), F32),
                   jax.ShapeDtypeStruct((b, ne * rows, LANES), F32),
                   jax.ShapeDtypeStruct((b, groups, LANES, LANES), BF16),
                   jax.ShapeDtypeStruct((b, 3, groups, LANES, LANES), BF16)],
        compiler_params=_cparams(("arbitrary",)),
        name="plan_tables",
    )(selw3)


def _plan_slots_kernel(be_ref, q0_ref, nt_ref, re_ref, ro_ref, int_ref, wt_ref, tok_ref, sw_ref,
                       *, ntile, rows, n):
    c = pl.program_id(0)
    tok_ref[...] = jnp.full(tok_ref.shape, n * CHUNKS, jnp.int32)
    sw_ref[...] = jnp.zeros(sw_ref.shape, F32)
    lane = lax.broadcasted_iota(jnp.int32, (1, SLOT_TILE), 1).astype(F32)
    sub_r = lax.broadcasted_iota(jnp.int32, (rows, SLOT_TILE), 0).astype(F32)
    sub_l = lax.broadcasted_iota(jnp.int32, (LANES, SLOT_TILE), 0).astype(F32)
    reps = SLOT_TILE // LANES

    def tile_body(j, carry):
        e = be_ref[c * ntile + j]
        q = q0_ref[c * ntile + j].astype(F32) + lane
        base = pl.multiple_of(e * rows, rows)
        row_end = jnp.concatenate([re_ref[0, pl.ds(base, rows), :]] * reps, axis=1)
        row_off = jnp.concatenate([ro_ref[0, pl.ds(base, rows), :]] * reps, axis=1)
        r = jnp.sum(jnp.where(row_end <= q, 1.0, 0.0), axis=0, keepdims=True)
        row_start = jnp.sum(jnp.where(sub_r == r, row_off, 0.0), axis=0, keepdims=True)
        g = lax.shift_right_logical(e, EXPERTS_PER_LANE_GROUP.bit_length() - 1)
        col = r + ((e & (EXPERTS_PER_LANE_GROUP - 1)) * rows).astype(F32)
        onehot = jnp.where(sub_l == col, 1.0, 0.0).astype(BF16)
        incl = _dot(int_ref[0, g], onehot)
        lane_idx = jnp.sum(jnp.where(incl <= q - row_start, 1.0, 0.0), axis=0, keepdims=True)
        wrow = (_dot(wt_ref[0, 0, g], onehot) + _dot(wt_ref[0, 1, g], onehot)) + _dot(wt_ref[0, 2, g], onehot)
        wsel = jnp.sum(jnp.where(sub_l == lane_idx, wrow, 0.0), axis=0, keepdims=True)
        valid = q < row_end[rows - 1:rows, :]
        tok = jnp.where(valid, r * LANES + lane_idx, float(n)).astype(jnp.int32)
        tok_ref[0, pl.ds(j, 1), :] = tok * CHUNKS
        sw_ref[0, pl.ds(j, 1), :] = jnp.where(valid, wsel, 0.0)
        return carry

    lax.fori_loop(0, nt_ref[c], tile_body, 0)


def _plan_slots(block_e, tile_q0, ntiles, row_end, row_off, incl_t, w_t, ntile, n):
    b, er, _ = row_end.shape
    rows = n // LANES
    groups = incl_t.shape[1]
    kern = functools.partial(_plan_slots_kernel, ntile=ntile, rows=rows, n=n)
    grid_spec = pltpu.PrefetchScalarGridSpec(
        num_scalar_prefetch=3,
        grid=(b,),
        in_specs=[pl.BlockSpec((1, er, LANES), lambda c, *_: (c, 0, 0)),
                  pl.BlockSpec((1, er, LANES), lambda c, *_: (c, 0, 0)),
                  pl.BlockSpec((1, groups, LANES, LANES), lambda c, *_: (c, 0, 0, 0)),
                  pl.BlockSpec((1, 3, groups, LANES, LANES), lambda c, *_: (c, 0, 0, 0, 0))],
        out_specs=[pl.BlockSpec((1, ntile, SLOT_TILE), lambda c, *_: (c, 0, 0)),
                   pl.BlockSpec((1, ntile, SLOT_TILE), lambda c, *_: (c, 0, 0))],
    )
    return pl.pallas_call(
        kern,
        grid_spec=grid_spec,
        out_shape=[jax.ShapeDtypeStruct((b, ntile, SLOT_TILE), jnp.int32),
                   jax.ShapeDtypeStruct((b, ntile, SLOT_TILE), F32)],
        compiler_params=_cparams(("arbitrary",)),
        name="plan_slots",
    )(block_e, tile_q0, ntiles, row_end, row_off, incl_t, w_t)


def _tile_meta(row_end, b, n, ntile):
    rows = n // LANES
    cnt = row_end.reshape(b, N_EXPERTS, rows, LANES)[:, :, rows - 1, 0].astype(jnp.int32)
    nt_e = (cnt + SLOT_TILE - 1) // SLOT_TILE
    tend = jnp.cumsum(nt_e, axis=1)
    tstart = tend - nt_e
    ntiles = tend[:, -1]
    tile_id = jnp.arange(ntile, dtype=jnp.int32)
    tid = jnp.minimum(tile_id[None, :], ntiles[:, None] - 1)
    be = jnp.minimum(jnp.sum(tend[:, None, :] <= tid[:, :, None], axis=2).astype(jnp.int32), N_EXPERTS - 1)
    q0 = (tid - jnp.take_along_axis(tstart, be, axis=1)) * SLOT_TILE
    valid = tile_id[None, :] < ntiles[:, None]
    prev_be = jnp.concatenate([jnp.full((b, 1), -1, jnp.int32), be[:, :-1]], axis=1)
    first = (valid & (be != prev_be)).astype(jnp.int32)
    wslot = (jnp.cumsum(first, axis=1) - 1) % 2
    eid = jnp.arange(N_EXPERTS, dtype=jnp.int32)
    later = (nt_e > 0)[:, None, :] & (eid[None, None, :] > eid[None, :, None])
    nxt = jnp.min(jnp.where(later, eid[None, None, :], N_EXPERTS), axis=2)
    nxt = jnp.where(nxt < N_EXPERTS, nxt, -1)
    nxt_e = jnp.take_along_axis(nxt, be, axis=1)
    flat = lambda a: a.reshape(-1).astype(jnp.int32)
    return flat(be), flat(q0), ntiles.astype(jnp.int32), flat(first), flat(nxt_e), flat(wslot)


def _dispatch_plan(selw3, b, n, ntile):
    row_end, row_off, incl_t, w_t = _plan_tables(selw3, b)
    block_e, tile_q0, ntiles, first, nxt_e, wslot = _tile_meta(row_end, b, n, ntile)
    slot_tok, slot_w = _plan_slots(block_e, tile_q0, ntiles, row_end, row_off, incl_t, w_t, ntile, n)
    return block_e, ntiles, first, nxt_e, wslot, slot_tok, slot_w


def _final_kernel(h2_ref, r_ref, x1_ref, mod_ref, wg_ref, wu_ref, wd_ref, g_ref, o_ref):
    hb = h2_ref[0]
    tm = hb.shape[0]
    sh = _dot((_silu(_dot(hb, wg_ref[...])) * _dot(hb, wu_ref[...])).astype(BF16), wd_ref[...])
    routed = jnp.concatenate([r_ref[0, pl.ds(k, tm, stride=CHUNKS), :] for k in range(CHUNKS)], axis=1)
    moe = routed + sh
    o_ref[0] = x1_ref[0] + mod_ref[0, 5:6, :] * _rms(moe, g_ref[...])


def _final(h2, routed, x1, mod3, wsg_bf, wsu_bf, wsd_bf, g_post, tm=512):
    b, n, d = x1.shape
    f = wsg_bf.shape[1]
    row = lambda bi, i: (bi, i, 0)
    vec = lambda bi, i: (0, 0)
    return pl.pallas_call(
        _final_kernel,
        grid=(b, n // tm),
        in_specs=[pl.BlockSpec((1, tm, d), row),
                  pl.BlockSpec((1, tm * CHUNKS, LANES), row),
                  pl.BlockSpec((1, tm, d), row),
                  pl.BlockSpec((1, 6, d), lambda bi, i: (bi, 0, 0)),
                  pl.BlockSpec((d, f), vec),
                  pl.BlockSpec((d, f), vec),
                  pl.BlockSpec((f, d), vec),
                  pl.BlockSpec((1, d), vec)],
        out_specs=pl.BlockSpec((1, tm, d), row),
        out_shape=jax.ShapeDtypeStruct((b, n, d), F32),
        compiler_params=_cparams(("arbitrary", "arbitrary")),
        name="final",
    )(h2, routed, x1, mod3, wsg_bf, wsu_bf, wsd_bf, g_post)


def _rope_tables(n):
    t = np.arange(n)
    row = (t // GRID_W).astype(np.float32)
    col = (t % GRID_W).astype(np.float32)
    freqs = np.float32(ROPE_THETA) ** (-np.arange(0, AXIS_DIM, 2, dtype=np.float32) / np.float32(AXIS_DIM))
    ar = row[:, None] * freqs
    ac = col[:, None] * freqs
    zeros = np.zeros_like(ar)
    cos64 = np.concatenate([np.cos(ar), np.cos(ar), np.cos(ac), np.cos(ac)], axis=1)
    sa64 = np.concatenate([-np.sin(ar), zeros, -np.sin(ac), zeros], axis=1)
    sb64 = np.concatenate([zeros, np.sin(ar), zeros, np.sin(ac)], axis=1)
    two = lambda a: jnp.asarray(np.concatenate([a, a], axis=1).astype(np.float32))
    return two(cos64), two(sa64), two(sb64)


def kernel(x, c, ctx, c_ctx, w_ada, b_ada, g_pre_mix, g_post_mix, g_pre_ffn, g_post_ffn, w_in, lambda_q1, lambda_k1, lambda_q2, lambda_k2, g_subln, w_pool, b_pool, pool_scale, w_out, w_router, router_bias, w_e_gate, w_e_up, w_e_down, w_sh_gate, w_sh_up, w_sh_down):
    b, n, d = x.shape
    l = 0
    mod_rows = 8
    cin = jnp.concatenate([c, c_ctx[None, :], jnp.zeros((mod_rows - b - 1, d), F32)], axis=0)
    mod3 = _ada(cin, w_ada[l], b_ada[l]).reshape(mod_rows, 6, d)

    w_in_bf = w_in[l].astype(BF16)
    cos, sa, sb = _rope_tables(n)
    q, k, vt, p = _inproj(x, mod3, g_pre_mix[l][None, :], w_in_bf, cos * (QK_DIM ** -0.5 * LOG2E), cos, sa, sb)
    kc, vct = _ctxkv(ctx, mod3, g_pre_mix[l][None, :], w_in_bf, b)

    lamv = jnp.stack([lambda_q1[l], lambda_k1[l], lambda_q2[l], lambda_k2[l]], axis=0)
    att = _attn(q, k, kc, vt, vct, lamv, g_subln[l][:, None])

    x1, h2, h2t, logits_t = _outproj(
        att, p, x, mod3, w_pool[l].astype(BF16), b_pool[l][None, :], pool_scale[l][None, :],
        w_out[l].astype(BF16), g_post_mix[l][None, :], g_pre_ffn[l][None, :], w_router[l].T)

    t = b * n
    selw3 = _route(logits_t.reshape(N_EXPERTS, t // LANES, LANES),
                   jnp.broadcast_to(router_bias[l][:, None, None], (N_EXPERTS, 1, LANES)))

    ntile = (n * TOP_K) // SLOT_TILE + N_EXPERTS
    block_e, ntiles, first, nxt_e, wslot, slot_row, slot_w = _dispatch_plan(selw3, b, n, ntile)
    routed = _experts(block_e, ntiles, first, nxt_e, wslot, slot_row, slot_w, h2t,
                      w_e_gate[l].astype(BF16), w_e_up[l].astype(BF16), w_e_down[l].astype(BF16), ntile, n)

    return _final(h2, routed, x1, mod3, w_sh_gate[l].astype(BF16), w_sh_up[l].astype(BF16),
                  w_sh_down[l].astype(BF16), g_post_ffn[l][None, :])
```

```python
import functools
import math

import jax
import jax.numpy as jnp
import numpy as np
from jax import lax
from jax.experimental import pallas as pl
from jax.experimental.pallas import tpu as pltpu

F32 = jnp.float32
BF16 = jnp.bfloat16

D_MODEL = 1024
GRID_W = 64
N_HEADS = 4
QK_DIM = 64
V_DIM = 128
ATTN_WIDTH = N_HEADS * V_DIM
POOL_WIDTH = D_MODEL - ATTN_WIDTH
POOL_WINDOWS = (2, 4, 8, 16)
POOL_GROUP = POOL_WIDTH // len(POOL_WINDOWS)
AXIS_DIM = QK_DIM // 2
ROPE_THETA = 10000.0
N_EXPERTS = 64
TOP_K = 8
N_GROUPS = 8
GROUP_SIZE = N_EXPERTS // N_GROUPS
TOPK_GROUPS = 4
EXPERT_DIM = 256
ROUTED_SCALE = 2.5
EPS = 1e-6
LAM_INIT = 0.8 - 0.6 * math.exp(-0.3 * 0)

LANES = 128
SLOT_TILE = 256
HALO = 16
VMEM_LIMIT = 56 * 1024 * 1024


def _cparams(sem, vmem=VMEM_LIMIT):
    return pltpu.CompilerParams(dimension_semantics=sem, vmem_limit_bytes=vmem)


def _split(a):
    hi = a.astype(BF16)
    lo = (a - hi.astype(F32)).astype(BF16)
    return hi, lo


def _dot(a, b):
    return jnp.dot(a, b, preferred_element_type=F32)


def _dot_nt(a, b):
    return lax.dot_general(a, b, (((1,), (1,)), ((), ())), preferred_element_type=F32)


def _dot3(a, b, nt=False):
    d = _dot_nt if nt else _dot
    ah, al = _split(a)
    bh, bl = _split(b)
    return d(ah, bh) + d(ah, bl) + d(al, bh)


def _rms(x, g):
    return x * lax.rsqrt(jnp.mean(x * x, axis=-1, keepdims=True) + EPS) * g


def _silu(x):
    return x * (1.0 / (1.0 + jnp.exp(-x)))


def _ada_kernel(c_ref, w_ref, b_ref, o_ref):
    o_ref[...] = _dot3(_silu(c_ref[...]), w_ref[...]) + b_ref[...]


def _ada(cin, w_ada, b_ada):
    rows, d = cin.shape
    n = w_ada.shape[1]
    tn = 1024
    return pl.pallas_call(
        _ada_kernel,
        grid=(n // tn,),
        in_specs=[pl.BlockSpec((rows, d), lambda j: (0, 0)),
                  pl.BlockSpec((d, tn), lambda j: (0, j)),
                  pl.BlockSpec((1, tn), lambda j: (0, j))],
        out_specs=pl.BlockSpec((rows, tn), lambda j: (0, j)),
        out_shape=jax.ShapeDtypeStruct((rows, n), F32),
        compiler_params=_cparams(("arbitrary",)),
        name="ada",
    )(cin, w_ada, b_ada.reshape(1, n))


def _rope(t, cos, sa, sb):
    return t * cos + pltpu.roll(t, LANES - AXIS_DIM // 2, 1) * sa + pltpu.roll(t, AXIS_DIM // 2, 1) * sb


def _inproj_kernel(x_ref, mod_ref, g_ref, w_ref, cq_ref, ck_ref, sa_ref, sb_ref,
                   q_ref, k_ref, vt_ref, p_ref):
    x = x_ref[0]
    h = _rms(x, g_ref[...]) * (1.0 + mod_ref[0, 1:2, :]) + mod_ref[0, 0:1, :]
    px = _dot(h.astype(BF16), w_ref[...])
    ck = ck_ref[...]
    sa = sa_ref[...]
    sb = sb_ref[...]
    cq = cq_ref[...]
    scale = QK_DIM ** -0.5 * LOG2E
    saq = sa * scale
    sbq = sb * scale
    for hh in range(N_HEADS):
        lo = hh * LANES
        q = px[:, lo:lo + LANES]
        k = px[:, ATTN_WIDTH + lo:ATTN_WIDTH + lo + LANES]
        q_ref[0, :, lo:lo + LANES] = _rope(q, cq, saq, sbq).astype(BF16)
        k_ref[0, :, lo:lo + LANES] = _rope(k, ck, sa, sb).astype(BF16)
    vt_ref[0] = jnp.transpose(px[:, 2 * ATTN_WIDTH:3 * ATTN_WIDTH]).astype(BF16)
    p_ref[0] = px[:, 3 * ATTN_WIDTH:].astype(BF16)


def _inproj(x, mod3, g, w_in_bf, cq, ck, sa, sb, tm=512):
    b, n, d = x.shape
    width = w_in_bf.shape[1]
    row = lambda bi, i: (bi, i, 0)
    tab = pl.BlockSpec((tm, LANES), lambda bi, i: (i, 0))
    out = jax.ShapeDtypeStruct((b, n, ATTN_WIDTH), BF16)
    return pl.pallas_call(
        _inproj_kernel,
        grid=(b, n // tm),
        in_specs=[pl.BlockSpec((1, tm, d), row),
                  pl.BlockSpec((1, 6, d), lambda bi, i: (bi, 0, 0)),
                  pl.BlockSpec((1, d), lambda bi, i: (0, 0)),
                  pl.BlockSpec((d, width), lambda bi, i: (0, 0)),
                  tab, tab, tab, tab],
        out_specs=[pl.BlockSpec((1, tm, ATTN_WIDTH), row),
                   pl.BlockSpec((1, tm, ATTN_WIDTH), row),
                   pl.BlockSpec((1, ATTN_WIDTH, tm), lambda bi, i: (bi, 0, i)),
                   pl.BlockSpec((1, tm, ATTN_WIDTH), row)],
        out_shape=[out, out, jax.ShapeDtypeStruct((b, ATTN_WIDTH, n), BF16), out],
        compiler_params=_cparams(("arbitrary", "arbitrary")),
        name="inproj",
    )(x, mod3, g, w_in_bf, cq, ck, sa, sb)


def _ctxkv_kernel(x_ref, mod_ref, g_ref, wk_ref, wv_ref, k_ref, vt_ref):
    h = _rms(x_ref[0], g_ref[...]) * (1.0 + mod_ref[0, 1:2, :]) + mod_ref[0, 0:1, :]
    hb = h.astype(BF16)
    k_ref[0] = _dot(hb, wk_ref[...]).astype(BF16)
    vt_ref[0] = jnp.transpose(_dot(hb, wv_ref[...])).astype(BF16)


def _ctxkv(ctx, mod3, g, w_in_bf, ctx_row):
    b, n, d = ctx.shape
    out = jax.ShapeDtypeStruct((b, n, ATTN_WIDTH), BF16)
    return pl.pallas_call(
        _ctxkv_kernel,
        grid=(b,),
        in_specs=[pl.BlockSpec((1, n, d), lambda bi: (bi, 0, 0)),
                  pl.BlockSpec((1, 6, d), lambda bi: (ctx_row, 0, 0)),
                  pl.BlockSpec((1, d), lambda bi: (0, 0)),
                  pl.BlockSpec((d, ATTN_WIDTH), lambda bi: (0, 1)),
                  pl.BlockSpec((d, ATTN_WIDTH), lambda bi: (0, 2))],
        out_specs=[pl.BlockSpec((1, n, ATTN_WIDTH), lambda bi: (bi, 0, 0)),
                   pl.BlockSpec((1, ATTN_WIDTH, n), lambda bi: (bi, 0, 0))],
        out_shape=[out, jax.ShapeDtypeStruct((b, ATTN_WIDTH, n), BF16)],
        compiler_params=_cparams(("arbitrary",)),
        name="ctxkv",
    )(ctx, mod3, g, w_in_bf, w_in_bf)


LOG2E = math.log2(math.e)
KEY_CHUNK = 256


def _attn_step(q_ref, kx_ref, kc_ref, vt_ref, vct_ref, lam_ref, g_ref, o_ref, s_new, m_new, s_old, m_old):
    nc = kc_ref.shape[1]
    tq = q_ref.shape[1]
    nk = s_old.shape[1]
    groups = KEY_CHUNK // 8
    q = q_ref[0]
    lane = lax.broadcasted_iota(jnp.int32, q.shape, 1)
    zero = jnp.zeros_like(q)
    qms = (jnp.where(lane < QK_DIM, q, zero), jnp.where(lane >= QK_DIM, q, zero))
    m_prev = (m_old[0], m_old[1])
    top = [jnp.full((8, tq), -jnp.inf, F32) for _ in range(2)]
    den = [jnp.zeros((8, tq), F32) for _ in range(2)]
    acc = [jnp.zeros((V_DIM, tq), F32) for _ in range(2)]
    for c in range(nk // KEY_CHUNK):
        lo = c * KEY_CHUNK
        if lo < nc:
            keys, vt = kc_ref[0, lo:lo + KEY_CHUNK, :], vct_ref[0, :, lo:lo + KEY_CHUNK]
        else:
            keys, vt = kx_ref[0, lo - nc:lo - nc + KEY_CHUNK, :], vt_ref[0, :, lo - nc:lo - nc + KEY_CHUNK]
        for mp in range(2):
            s = _dot_nt(keys, qms[mp])
            s_new[mp, lo:lo + KEY_CHUNK, :] = s
            top[mp] = jnp.maximum(top[mp], jnp.max(s.reshape(groups, 8, tq), axis=0))
            p = jnp.exp2(s_old[mp, lo:lo + KEY_CHUNK, :] - m_prev[mp])
            den[mp] = den[mp] + jnp.sum(p.reshape(groups, 8, tq), axis=0)
            acc[mp] = acc[mp] + _dot(vt, p.astype(BF16))
    for mp in range(2):
        m_new[mp] = jnp.max(top[mp], axis=0, keepdims=True)
    lv = lam_ref[...]
    lam = (jnp.exp(jnp.sum(lv[0:1] * lv[1:2], axis=-1, keepdims=True))
           - jnp.exp(jnp.sum(lv[2:3] * lv[3:4], axis=-1, keepdims=True)) + LAM_INIT)
    l1 = jnp.sum(den[0], axis=0, keepdims=True)
    l2 = jnp.sum(den[1], axis=0, keepdims=True)
    o = acc[0] * (1.0 / l1) - acc[1] * (lam / l2)
    o = o * lax.rsqrt(jnp.mean(o * o, axis=0, keepdims=True) + EPS) * (g_ref[...] * (1.0 - LAM_INIT))
    o_ref[0] = jnp.transpose(o).astype(BF16)


def _attn_kernel(q_ref, kx_ref, kc_ref, vt_ref, vct_ref, lam_ref, g_ref, o_ref, s0_ref, m0_ref, s1_ref, m1_ref):
    t = pl.program_id(0)
    io = (q_ref, kx_ref, kc_ref, vt_ref, vct_ref, lam_ref, g_ref, o_ref)

    @pl.when(t == 0)
    def _():
        s1_ref[...] = jnp.zeros_like(s1_ref)
        m1_ref[...] = jnp.zeros_like(m1_ref)

    @pl.when(lax.rem(t, 2) == 0)
    def _():
        _attn_step(*io, s0_ref, m0_ref, s1_ref, m1_ref)

    @pl.when(lax.rem(t, 2) == 1)
    def _():
        _attn_step(*io, s1_ref, m1_ref, s0_ref, m0_ref)


def _attn(q, k, kc, vt, vct, lamv, g_col, tq=256):
    b, n, _ = q.shape
    nc = kc.shape[1]
    nq = n // tq
    total = b * N_HEADS * nq

    def cur(t):
        ta = jnp.minimum(t, total - 1)
        bh = ta // nq
        return bh // N_HEADS, bh % N_HEADS, ta % nq

    def prv(t):
        tb = jnp.maximum(t - 1, 0)
        bh = tb // nq
        return bh // N_HEADS, bh % N_HEADS, tb % nq

    def at(fn, order):
        return lambda t: tuple((fn(t) + (0,))[j] for j in order)

    return pl.pallas_call(
        _attn_kernel,
        grid=(total + 1,),
        in_specs=[pl.BlockSpec((1, tq, LANES), at(cur, (0, 2, 1))),
                  pl.BlockSpec((1, n, LANES), at(cur, (0, 3, 1))),
                  pl.BlockSpec((1, nc, LANES), at(cur, (0, 3, 1))),
                  pl.BlockSpec((1, V_DIM, n), at(prv, (0, 1, 3))),
                  pl.BlockSpec((1, V_DIM, nc), at(prv, (0, 1, 3))),
                  pl.BlockSpec((4, QK_DIM), lambda t: (0, 0)),
                  pl.BlockSpec((V_DIM, 1), lambda t: (0, 0))],
        out_specs=pl.BlockSpec((1, tq, LANES), at(prv, (0, 2, 1))),
        out_shape=jax.ShapeDtypeStruct((b, n, ATTN_WIDTH), BF16),
        scratch_shapes=[pltpu.VMEM((2, nc + n, tq), F32), pltpu.VMEM((2, 1, tq), F32),
                        pltpu.VMEM((2, nc + n, tq), F32), pltpu.VMEM((2, 1, tq), F32)],
        compiler_params=_cparams(("arbitrary",)),
        name="attn",
    )(q, k, kc, vt, vct, lamv, g_col)


def _outproj_kernel(att_ref, p_ref, pprev_ref, pnext_ref, x_ref, mod_ref, wpool_ref, bpool_ref,
                    pscale_ref, wout_ref, gpost_ref, gpre_ref, wr_ref,
                    x1_ref, h2_ref, h2t_ref, lg_ref, *, tm, n):
    i = pl.program_id(1)
    p = p_ref[0].astype(F32)
    prev = jnp.where(i > 0, pprev_ref[0].astype(F32), 0.0)
    nxt = jnp.where(i < pl.num_programs(1) - 1, pnext_ref[0].astype(F32), 0.0)
    ext = jnp.concatenate([prev, p, nxt], axis=0)
    rows = tm + 2 * HALO
    trow = (i * tm + lax.broadcasted_iota(jnp.int32, (tm, 1), 0))

    pooled = []
    sums = {}
    acc = ext + pltpu.roll(ext, 1, 0)
    sums[2] = acc
    w = 2
    while w < POOL_WINDOWS[-1]:
        acc = pltpu.roll(acc, rows - w // 2, 0) + pltpu.roll(acc, w // 2, 0)
        w = 2 * w
        sums[w] = acc
    for gi, win in enumerate(POOL_WINDOWS):
        sl = slice(gi * POOL_GROUP, (gi + 1) * POOL_GROUP)
        wsum = sums[win][HALO:HALO + tm, sl]
        hi_c = jnp.minimum(trow + (win - win // 2), n)
        lo_c = jnp.maximum(trow - win // 2, 0)
        cnt = (hi_c - lo_c).astype(F32)
        d = wsum / cnt - p[:, sl]
        y = _dot(d.astype(BF16), wpool_ref[gi]) + bpool_ref[:, sl]
        pooled.append((y * pscale_ref[:, sl]).astype(BF16))
    pool = jnp.concatenate(pooled, axis=1)
    yx = _dot(att_ref[0], wout_ref[0:ATTN_WIDTH, :]) + _dot(pool, wout_ref[ATTN_WIDTH:, :])
    x1 = x_ref[0] + mod_ref[0, 2:3, :] * _rms(yx, gpost_ref[...])
    x1_ref[0] = x1
    h2 = _rms(x1, gpre_ref[...]) * (1.0 + mod_ref[0, 4:5, :]) + mod_ref[0, 3:4, :]
    h2_ref[0] = h2.astype(BF16)
    for k in range(CHUNKS):
        h2t_ref[0, pl.ds(k, tm, stride=CHUNKS), :] = h2[:, k * LANES:(k + 1) * LANES]
    lg_ref[...] = _dot3(wr_ref[...], h2, nt=True)


def _outproj(att, p, x, mod3, w_pool_bf, b_pool, pool_scale, w_out_bf, g_post, g_pre, w_router_t, tm=256):
    b, n, d = x.shape
    hb = tm // HALO
    nhb = n // HALO
    row = lambda bi, i: (bi, i, 0)
    vec = lambda bi, i: (0, 0)
    kern = functools.partial(_outproj_kernel, tm=tm, n=n)
    return pl.pallas_call(
        kern,
        grid=(b, n // tm),
        in_specs=[pl.BlockSpec((1, tm, ATTN_WIDTH), row),
                  pl.BlockSpec((1, tm, POOL_WIDTH), row),
                  pl.BlockSpec((1, HALO, POOL_WIDTH), lambda bi, i: (bi, jnp.maximum(i * hb - 1, 0), 0)),
                  pl.BlockSpec((1, HALO, POOL_WIDTH), lambda bi, i: (bi, jnp.minimum((i + 1) * hb, nhb - 1), 0)),
                  pl.BlockSpec((1, tm, d), row),
                  pl.BlockSpec((1, 6, d), lambda bi, i: (bi, 0, 0)),
                  pl.BlockSpec((len(POOL_WINDOWS), POOL_GROUP, POOL_GROUP), lambda bi, i: (0, 0, 0)),
                  pl.BlockSpec((1, POOL_WIDTH), vec),
                  pl.BlockSpec((1, POOL_WIDTH), vec),
                  pl.BlockSpec((d, d), vec),
                  pl.BlockSpec((1, d), vec),
                  pl.BlockSpec((1, d), vec),
                  pl.BlockSpec((N_EXPERTS, d), vec)],
        out_specs=[pl.BlockSpec((1, tm, d), row),
                   pl.BlockSpec((1, tm, d), row),
                   pl.BlockSpec((1, tm * CHUNKS, LANES), row),
                   pl.BlockSpec((N_EXPERTS, tm), lambda bi, i: (0, bi * (n // tm) + i))],
        out_shape=[jax.ShapeDtypeStruct((b, n, d), F32),
                   jax.ShapeDtypeStruct((b, n, d), BF16),
                   jax.ShapeDtypeStruct((b, n * CHUNKS, LANES), F32),
                   jax.ShapeDtypeStruct((N_EXPERTS, b * n), F32)],
        compiler_params=_cparams(("arbitrary", "arbitrary")),
        name="outproj",
    )(att, p, p, p, x, mod3, w_pool_bf, b_pool, pool_scale, w_out_bf, g_post, g_pre, w_router_t)


def _beats(a, b, a_first):
    return jnp.where((a >= b) if a_first else (a > b), 1.0, 0.0)


def _route_kernel(lg_ref, bias_ref, w_ref):
    scores = [1.0 / (1.0 + jnp.exp(-lg_ref[e])) for e in range(N_EXPERTS)]
    biased = [scores[e] + bias_ref[e] for e in range(N_EXPERTS)]
    gscore = []
    for g in range(N_GROUPS):
        vals = biased[g * GROUP_SIZE:(g + 1) * GROUP_SIZE]
        m1 = vals[0]
        m2 = jnp.full_like(m1, -jnp.inf)
        for v in vals[1:]:
            m2 = jnp.maximum(m2, jnp.minimum(m1, v))
            m1 = jnp.maximum(m1, v)
        gscore.append(m1 + m2)
    gsel = []
    for g in range(N_GROUPS):
        rank = jnp.zeros_like(gscore[g])
        for g2 in range(N_GROUPS):
            if g2 != g:
                rank = rank + _beats(gscore[g2], gscore[g], g2 < g)
        gsel.append(rank < TOPK_GROUPS)
    masked = [jnp.where(gsel[e // GROUP_SIZE], biased[e], -jnp.inf) for e in range(N_EXPERTS)]
    picked = []
    for e in range(N_EXPERTS):
        rank = jnp.zeros_like(masked[e])
        for e2 in range(N_EXPERTS):
            if e2 != e:
                rank = rank + _beats(masked[e2], masked[e], e2 < e)
        picked.append(jnp.where(rank < TOP_K, scores[e], 0.0))
    denom = picked[0]
    for e in range(1, N_EXPERTS):
        denom = denom + picked[e]
    for e in range(N_EXPERTS):
        w_ref[e] = picked[e] / denom * ROUTED_SCALE


def _route(logits3, bias3, rows=8):
    e, r, l = logits3.shape
    return pl.pallas_call(
        _route_kernel,
        grid=(r // rows,),
        in_specs=[pl.BlockSpec((e, rows, l), lambda i: (0, i, 0)),
                  pl.BlockSpec((e, 1, l), lambda i: (0, 0, 0))],
        out_specs=pl.BlockSpec((e, rows, l), lambda i: (0, i, 0)),
        out_shape=jax.ShapeDtypeStruct((e, r, l), F32),
        compiler_params=_cparams(("arbitrary",)),
        name="route",
    )(logits3, bias3)


CHUNKS = D_MODEL // LANES
SLAB = SLOT_TILE + 4
ROW_BATCH = 16


def _gather_tile(row_ref, tile, xs_ref, tx_ref):
    for s in range(SLOT_TILE):
        row = pl.multiple_of(row_ref[0, tile, s], CHUNKS)
        tx_ref[pl.ds(s, CHUNKS, stride=SLAB), :] = xs_ref[pl.ds(row, CHUNKS), :]


def _experts_step(j, wslot, row_ref, sw_ref, wg_buf, wu_buf, wd_buf, xs_ref, acc_ref,
                  tx_cur, tx_nxt, ty_cur, ty_prv):
    ntile = row_ref.shape[1]
    _gather_tile(row_ref, jnp.minimum(j + 1, ntile - 1), xs_ref, tx_nxt)
    xb = jnp.concatenate([tx_cur[pl.ds(k * SLAB, SLOT_TILE), :] for k in range(CHUNKS)], axis=1).astype(BF16)
    g = _dot(xb, wg_buf[wslot])
    u = _dot(xb, wu_buf[wslot])
    h = (_silu(g) * u).astype(BF16)
    y = _dot(h, wd_buf[wslot])
    w_rows = jnp.transpose(jnp.broadcast_to(sw_ref[0, pl.ds(j, 1), :], (LANES, SLOT_TILE)))
    for k in range(CHUNKS):
        ty_cur[pl.ds(k * SLAB, SLOT_TILE), :] = y[:, k * LANES:(k + 1) * LANES] * w_rows
    prv = jnp.maximum(j - 1, 0)
    for s0 in range(0, SLOT_TILE, ROW_BATCH):
        new = []
        for s in range(s0, s0 + ROW_BATCH):
            row = pl.multiple_of(row_ref[0, prv, s], CHUNKS)
            new.append((row, acc_ref[pl.ds(row, CHUNKS), :] + ty_prv[pl.ds(s, CHUNKS, stride=SLAB), :]))
        for row, v in new:
            acc_ref[pl.ds(row, CHUNKS), :] = v


def _experts_kernel(be_ref, nt_ref, first_ref, nxt_ref, ws_ref, row_ref, sw_ref, x_hbm,
                    wg_hbm, wu_hbm, wd_hbm, o_hbm,
                    xs_ref, acc_ref, tx0_ref, tx1_ref, ty0_ref, ty1_ref, wg_buf, wu_buf, wd_buf,
                    sem, wsem, *, ntile, n):
    c = pl.program_id(0)
    p = pl.program_id(1)
    rows = n * CHUNKS

    def weight_copies(e, slot):
        return [pltpu.make_async_copy(src.at[e], dst.at[slot], wsem.at[slot])
                for src, dst in ((wg_hbm, wg_buf), (wu_hbm, wu_buf), (wd_hbm, wd_buf))]

    @pl.when(p == 0)
    def _():
        for wcp in weight_copies(be_ref[c * ntile], 0):
            wcp.start()
        xcp = pltpu.make_async_copy(x_hbm.at[c], xs_ref.at[pl.ds(0, rows)], sem.at[0])
        xcp.start()
        xs_ref[pl.ds(rows, CHUNKS), :] = jnp.zeros((CHUNKS, LANES), F32)
        acc_ref[...] = jnp.zeros_like(acc_ref)
        ty1_ref[...] = jnp.zeros_like(ty1_ref)
        xcp.wait()
        _gather_tile(row_ref, 0, xs_ref, tx0_ref)

    def tile_step(j, tx_cur, tx_nxt, ty_cur, ty_prv):
        t = c * ntile + j
        wslot = ws_ref[t]

        @pl.when(first_ref[t] == 1)
        def _():
            for wcp in weight_copies(be_ref[t], wslot):
                wcp.wait()

        @pl.when((first_ref[t] == 1) & (nxt_ref[t] >= 0))
        def _():
            for wcp in weight_copies(nxt_ref[t], 1 - wslot):
                wcp.start()

        _experts_step(j, wslot, row_ref, sw_ref, wg_buf, wu_buf, wd_buf, xs_ref, acc_ref,
                      tx_cur, tx_nxt, ty_cur, ty_prv)

    @pl.when(2 * p <= nt_ref[c])
    def _():
        tile_step(2 * p, tx0_ref, tx1_ref, ty0_ref, ty1_ref)
        tile_step(2 * p + 1, tx1_ref, tx0_ref, ty1_ref, ty0_ref)

    @pl.when(p == ntile // 2 - 1)
    def _():
        ocp = pltpu.make_async_copy(acc_ref.at[pl.ds(0, rows)], o_hbm.at[c], sem.at[1])
        ocp.start()
        ocp.wait()


def _experts(block_e, ntiles, first, nxt_e, wslot, slot_row, slot_w, h2t, wg_bf, wu_bf, wd_bf, ntile, n):
    b = h2t.shape[0]
    d, f = wg_bf.shape[1], wg_bf.shape[2]
    batch = lambda c, *_: (c, 0, 0)
    kern = functools.partial(_experts_kernel, ntile=ntile, n=n)
    slab = pltpu.VMEM((CHUNKS * SLAB, LANES), F32)
    hbm = pl.BlockSpec(memory_space=pl.ANY)
    grid_spec = pltpu.PrefetchScalarGridSpec(
        num_scalar_prefetch=5,
        grid=(b, ntile // 2),
        in_specs=[pl.BlockSpec((1, ntile, SLOT_TILE), batch, memory_space=pltpu.SMEM),
                  pl.BlockSpec((1, ntile, SLOT_TILE), batch),
                  hbm, hbm, hbm, hbm],
        out_specs=hbm,
        scratch_shapes=[pltpu.VMEM(((n + 1) * CHUNKS, LANES), F32),
                        pltpu.VMEM(((n + 1) * CHUNKS, LANES), F32),
                        slab, slab, slab, slab,
                        pltpu.VMEM((2, d, f), BF16), pltpu.VMEM((2, d, f), BF16), pltpu.VMEM((2, f, d), BF16),
                        pltpu.SemaphoreType.DMA((2,)), pltpu.SemaphoreType.DMA((2,))],
    )
    return pl.pallas_call(
        kern,
        grid_spec=grid_spec,
        out_shape=jax.ShapeDtypeStruct((b, n * CHUNKS, LANES), F32),
        compiler_params=_cparams(("arbitrary", "arbitrary")),
        name="experts",
    )(block_e, ntiles, first, nxt_e, wslot, slot_row, slot_w, h2t, wg_bf, wu_bf, wd_bf)


EXPERTS_PER_LANE_GROUP = LANES // 32


def _plan_tables_kernel(w_ref, re_ref, ro_ref, int_ref, wt_ref, *, rows):
    ne = w_ref.shape[0]
    w2 = w_ref[...].reshape(ne * rows, LANES)
    selb = jnp.where(w2 > 0.0, 1.0, 0.0).astype(BF16)
    i0 = lax.broadcasted_iota(jnp.int32, (LANES, LANES), 0)
    i1 = lax.broadcasted_iota(jnp.int32, (LANES, LANES), 1)
    rowtot = _dot(selb, jnp.ones((LANES, LANES), BF16))
    ridx = lax.broadcasted_iota(jnp.int32, rowtot.shape, 0) & (rows - 1)
    acc = rowtot
    sh = 1
    while sh < rows:
        acc = acc + jnp.where(ridx >= sh, pltpu.roll(acc, sh, 0), 0.0)
        sh *= 2
    re_ref[0] = acc
    ro_ref[0] = acc - rowtot
    incl_t = _dot_nt(jnp.where(i1 <= i0, 1.0, 0.0).astype(BF16), selb)
    eye = jnp.where(i0 == i1, 1.0, 0.0).astype(BF16)
    w_hi = w2.astype(BF16)
    r1 = w2 - w_hi.astype(F32)
    w_mid = r1.astype(BF16)
    w_lo = (r1 - w_mid.astype(F32)).astype(BF16)
    parts = [_dot_nt(eye, p) for p in (w_hi, w_mid, w_lo)]
    for g in range(ne * rows // LANES):
        sl = slice(g * LANES, (g + 1) * LANES)
        int_ref[0, g] = incl_t[:, sl].astype(BF16)
        for k in range(3):
            wt_ref[0, k, g] = parts[k][:, sl].astype(BF16)


def _plan_tables(selw3, b):
    ne, r_all, _ = selw3.shape
    rows = r_all // b
    groups = ne * rows // LANES
    kern = functools.partial(_plan_tables_kernel, rows=rows)
    return pl.pallas_call(
        kern,
        grid=(b,),
        in_specs=[pl.BlockSpec((ne, rows, LANES), lambda c: (0, c, 0))],
        out_specs=[pl.BlockSpec((1, ne * rows, LANES), lambda c: (c, 0, 0)),
                   pl.BlockSpec((1, ne * rows, LANES), lambda c: (c, 0, 0)),
                   pl.BlockSpec((1, groups, LANES, LANES), lambda c: (c, 0, 0, 0)),
                   pl.BlockSpec((1, 3, groups, LANES, LANES), lambda c: (c, 0, 0, 0, 0))],
        out_shape=[jax.ShapeDtypeStruct((b, ne * rows, LANES), F32),
                   jax.ShapeDtypeStruct((b, ne * rows, LANES), F32),
                   jax.ShapeDtypeStruct((b, groups, LANES, LANES), BF16),
                   jax.ShapeDtypeStruct((b, 3, groups, LANES, LANES), BF16)],
        compiler_params=_cparams(("arbitrary",)),
        name="plan_tables",
    )(selw3)


def _plan_slots_kernel(be_ref, q0_ref, nt_ref, re_ref, ro_ref, int_ref, wt_ref, tok_ref, sw_ref,
                       *, ntile, rows, n):
    c = pl.program_id(0)
    tok_ref[...] = jnp.full(tok_ref.shape, n * CHUNKS, jnp.int32)
    sw_ref[...] = jnp.zeros(sw_ref.shape, F32)
    lane = lax.broadcasted_iota(jnp.int32, (1, SLOT_TILE), 1).astype(F32)
    sub_r = lax.broadcasted_iota(jnp.int32, (rows, SLOT_TILE), 0).astype(F32)
    sub_l = lax.broadcasted_iota(jnp.int32, (LANES, SLOT_TILE), 0).astype(F32)
    reps = SLOT_TILE // LANES

    def tile_body(j, carry):
        e = be_ref[c * ntile + j]
        q = q0_ref[c * ntile + j].astype(F32) + lane
        base = pl.multiple_of(e * rows, rows)
        row_end = jnp.concatenate([re_ref[0, pl.ds(base, rows), :]] * reps, axis=1)
        row_off = jnp.concatenate([ro_ref[0, pl.ds(base, rows), :]] * reps, axis=1)
        r = jnp.sum(jnp.where(row_end <= q, 1.0, 0.0), axis=0, keepdims=True)
        row_start = jnp.sum(jnp.where(sub_r == r, row_off, 0.0), axis=0, keepdims=True)
        g = lax.shift_right_logical(e, EXPERTS_PER_LANE_GROUP.bit_length() - 1)
        col = r + ((e & (EXPERTS_PER_LANE_GROUP - 1)) * rows).astype(F32)
        onehot = jnp.where(sub_l == col, 1.0, 0.0).astype(BF16)
        incl = _dot(int_ref[0, g], onehot)
        lane_idx = jnp.sum(jnp.where(incl <= q - row_start, 1.0, 0.0), axis=0, keepdims=True)
        wrow = (_dot(wt_ref[0, 0, g], onehot) + _dot(wt_ref[0, 1, g], onehot)) + _dot(wt_ref[0, 2, g], onehot)
        wsel = jnp.sum(jnp.where(sub_l == lane_idx, wrow, 0.0), axis=0, keepdims=True)
        valid = q < row_end[rows - 1:rows, :]
        tok = jnp.where(valid, r * LANES + lane_idx, float(n)).astype(jnp.int32)
        tok_ref[0, pl.ds(j, 1), :] = tok * CHUNKS
        sw_ref[0, pl.ds(j, 1), :] = jnp.where(valid, wsel, 0.0)
        return carry

    lax.fori_loop(0, nt_ref[c], tile_body, 0)


def _plan_slots(block_e, tile_q0, ntiles, row_end, row_off, incl_t, w_t, ntile, n):
    b, er, _ = row_end.shape
    rows = n // LANES
    groups = incl_t.shape[1]
    kern = functools.partial(_plan_slots_kernel, ntile=ntile, rows=rows, n=n)
    grid_spec = pltpu.PrefetchScalarGridSpec(
        num_scalar_prefetch=3,
        grid=(b,),
        in_specs=[pl.BlockSpec((1, er, LANES), lambda c, *_: (c, 0, 0)),
                  pl.BlockSpec((1, er, LANES), lambda c, *_: (c, 0, 0)),
                  pl.BlockSpec((1, groups, LANES, LANES), lambda c, *_: (c, 0, 0, 0)),
                  pl.BlockSpec((1, 3, groups, LANES, LANES), lambda c, *_: (c, 0, 0, 0, 0))],
        out_specs=[pl.BlockSpec((1, ntile, SLOT_TILE), lambda c, *_: (c, 0, 0)),
                   pl.BlockSpec((1, ntile, SLOT_TILE), lambda c, *_: (c, 0, 0))],
    )
    return pl.pallas_call(
        kern,
        grid_spec=grid_spec,
        out_shape=[jax.ShapeDtypeStruct((b, ntile, SLOT_TILE), jnp.int32),
                   jax.ShapeDtypeStruct((b, ntile, SLOT_TILE), F32)],
        compiler_params=_cparams(("arbitrary",)),
        name="plan_slots",
    )(block_e, tile_q0, ntiles, row_end, row_off, incl_t, w_t)


def _tile_meta(row_end, b, n, ntile):
    rows = n // LANES
    cnt = row_end.reshape(b, N_EXPERTS, rows, LANES)[:, :, rows - 1, 0].astype(jnp.int32)
    nt_e = (cnt + SLOT_TILE - 1) // SLOT_TILE
    tend = jnp.cumsum(nt_e, axis=1)
    tstart = tend - nt_e
    ntiles = tend[:, -1]
    tile_id = jnp.arange(ntile, dtype=jnp.int32)
    tid = jnp.minimum(tile_id[None, :], ntiles[:, None] - 1)
    be = jnp.minimum(jnp.sum(tend[:, None, :] <= tid[:, :, None], axis=2).astype(jnp.int32), N_EXPERTS - 1)
    q0 = (tid - jnp.take_along_axis(tstart, be, axis=1)) * SLOT_TILE
    valid = tile_id[None, :] < ntiles[:, None]
    prev_be = jnp.concatenate([jnp.full((b, 1), -1, jnp.int32), be[:, :-1]], axis=1)
    first = (valid & (be != prev_be)).astype(jnp.int32)
    wslot = (jnp.cumsum(first, axis=1) - 1) % 2
    eid = jnp.arange(N_EXPERTS, dtype=jnp.int32)
    later = (nt_e > 0)[:, None, :] & (eid[None, None, :] > eid[None, :, None])
    nxt = jnp.min(jnp.where(later, eid[None, None, :], N_EXPERTS), axis=2)
    nxt = jnp.where(nxt < N_EXPERTS, nxt, -1)
    nxt_e = jnp.take_along_axis(nxt, be, axis=1)
    flat = lambda a: a.reshape(-1).astype(jnp.int32)
    return flat(be), flat(q0), ntiles.astype(jnp.int32), flat(first), flat(nxt_e), flat(wslot)


def _dispatch_plan(selw3, b, n, ntile):
    row_end, row_off, incl_t, w_t = _plan_tables(selw3, b)
    block_e, tile_q0, ntiles, first, nxt_e, wslot = _tile_meta(row_end, b, n, ntile)
    slot_tok, slot_w = _plan_slots(block_e, tile_q0, ntiles, row_end, row_off, incl_t, w_t, ntile, n)
    return block_e, ntiles, first, nxt_e, wslot, slot_tok, slot_w


def _final_kernel(h2_ref, r_ref, x1_ref, mod_ref, wg_ref, wu_ref, wd_ref, g_ref, o_ref):
    hb = h2_ref[0]
    tm = hb.shape[0]
    sh = _dot((_silu(_dot(hb, wg_ref[...])) * _dot(hb, wu_ref[...])).astype(BF16), wd_ref[...])
    routed = jnp.concatenate([r_ref[0, pl.ds(k, tm, stride=CHUNKS), :] for k in range(CHUNKS)], axis=1)
    moe = routed + sh
    o_ref[0] = x1_ref[0] + mod_ref[0, 5:6, :] * _rms(moe, g_ref[...])


def _final(h2, routed, x1, mod3, wsg_bf, wsu_bf, wsd_bf, g_post, tm=512):
    b, n, d = x1.shape
    f = wsg_bf.shape[1]
    row = lambda bi, i: (bi, i, 0)
    vec = lambda bi, i: (0, 0)
    return pl.pallas_call(
        _final_kernel,
        grid=(b, n // tm),
        in_specs=[pl.BlockSpec((1, tm, d), row),
                  pl.BlockSpec((1, tm * CHUNKS, LANES), row),
                  pl.BlockSpec((1, tm, d), row),
                  pl.BlockSpec((1, 6, d), lambda bi, i: (bi, 0, 0)),
                  pl.BlockSpec((d, f), vec),
                  pl.BlockSpec((d, f), vec),
                  pl.BlockSpec((f, d), vec),
                  pl.BlockSpec((1, d), vec)],
        out_specs=pl.BlockSpec((1, tm, d), row),
        out_shape=jax.ShapeDtypeStruct((b, n, d), F32),
        compiler_params=_cparams(("arbitrary", "arbitrary")),
        name="final",
    )(h2, routed, x1, mod3, wsg_bf, wsu_bf, wsd_bf, g_post)


def _rope_tables(n):
    t = np.arange(n)
    row = (t // GRID_W).astype(np.float32)
    col = (t % GRID_W).astype(np.float32)
    freqs = np.float32(ROPE_THETA) ** (-np.arange(0, AXIS_DIM, 2, dtype=np.float32) / np.float32(AXIS_DIM))
    ar = row[:, None] * freqs
    ac = col[:, None] * freqs
    zeros = np.zeros_like(ar)
    cos64 = np.concatenate([np.cos(ar), np.cos(ar), np.cos(ac), np.cos(ac)], axis=1)
    sa64 = np.concatenate([-np.sin(ar), zeros, -np.sin(ac), zeros], axis=1)
    sb64 = np.concatenate([zeros, np.sin(ar), zeros, np.sin(ac)], axis=1)
    two = lambda a: jnp.asarray(np.concatenate([a, a], axis=1).astype(np.float32))
    return two(cos64), two(sa64), two(sb64)


def kernel(x, c, ctx, c_ctx, w_ada, b_ada, g_pre_mix, g_post_mix, g_pre_ffn, g_post_ffn, w_in, lambda_q1, lambda_k1, lambda_q2, lambda_k2, g_subln, w_pool, b_pool, pool_scale, w_out, w_router, router_bias, w_e_gate, w_e_up, w_e_down, w_sh_gate, w_sh_up, w_sh_down):
    b, n, d = x.shape
    l = 0
    mod_rows = 8
    cin = jnp.concatenate([c, c_ctx[None, :], jnp.zeros((mod_rows - b - 1, d), F32)], axis=0)
    mod3 = _ada(cin, w_ada[l], b_ada[l]).reshape(mod_rows, 6, d)

    w_in_bf = w_in[l].astype(BF16)
    cos, sa, sb = _rope_tables(n)
    q, k, vt, p = _inproj(x, mod3, g_pre_mix[l][None, :], w_in_bf, cos * (QK_DIM ** -0.5 * LOG2E), cos, sa, sb)
    kc, vct = _ctxkv(ctx, mod3, g_pre_mix[l][None, :], w_in_bf, b)

    lamv = jnp.stack([lambda_q1[l], lambda_k1[l], lambda_q2[l], lambda_k2[l]], axis=0)
    att = _attn(q, k, kc, vt, vct, lamv, g_subln[l][:, None])

    x1, h2, h2t, logits_t = _outproj(
        att, p, x, mod3, w_pool[l].astype(BF16), b_pool[l][None, :], pool_scale[l][None, :],
        w_out[l].astype(BF16), g_post_mix[l][None, :], g_pre_ffn[l][None, :], w_router[l].T)

    t = b * n
    selw3 = _route(logits_t.reshape(N_EXPERTS, t // LANES, LANES),
                   jnp.broadcast_to(router_bias[l][:, None, None], (N_EXPERTS, 1, LANES)))

    ntile = (n * TOP_K) // SLOT_TILE + N_EXPERTS
    block_e, ntiles, first, nxt_e, wslot, slot_row, slot_w = _dispatch_plan(selw3, b, n, ntile)
    routed = _experts(block_e, ntiles, first, nxt_e, wslot, slot_row, slot_w, h2t,
                      w_e_gate[l].astype(BF16), w_e_up[l].astype(BF16), w_e_down[l].astype(BF16), ntile, n)

    return _final(h2, routed, x1, mod3, w_sh_gate[l].astype(BF16), w_sh_up[l].astype(BF16),
                  w_sh_down[l].astype(BF16), g_post_ffn[l][None, :])
```

```python
import functools
import math

import jax
import jax.numpy as jnp
import numpy as np
from jax import lax
from jax.experimental import pallas as pl
from jax.experimental.pallas import tpu as pltpu

F32 = jnp.float32
BF16 = jnp.bfloat16

D_MODEL = 1024
GRID_W = 64
N_HEADS = 4
QK_DIM = 64
V_DIM = 128
ATTN_WIDTH = N_HEADS * V_DIM
POOL_WIDTH = D_MODEL - ATTN_WIDTH
POOL_WINDOWS = (2, 4, 8, 16)
POOL_GROUP = POOL_WIDTH // len(POOL_WINDOWS)
AXIS_DIM = QK_DIM // 2
ROPE_THETA = 10000.0
N_EXPERTS = 64
TOP_K = 8
N_GROUPS = 8
GROUP_SIZE = N_EXPERTS // N_GROUPS
TOPK_GROUPS = 4
EXPERT_DIM = 256
ROUTED_SCALE = 2.5
EPS = 1e-6
LAM_INIT = 0.8 - 0.6 * math.exp(-0.3 * 0)

LANES = 128
SLOT_TILE = 256
HALO = 16
VMEM_LIMIT = 56 * 1024 * 1024


def _cparams(sem, vmem=VMEM_LIMIT):
    return pltpu.CompilerParams(dimension_semantics=sem, vmem_limit_bytes=vmem)


def _split(a):
    hi = a.astype(BF16)
    lo = (a - hi.astype(F32)).astype(BF16)
    return hi, lo


def _dot(a, b):
    return jnp.dot(a, b, preferred_element_type=F32)


def _dot_nt(a, b):
    return lax.dot_general(a, b, (((1,), (1,)), ((), ())), preferred_element_type=F32)


def _dot3(a, b, nt=False):
    d = _dot_nt if nt else _dot
    ah, al = _split(a)
    bh, bl = _split(b)
    return d(ah, bh) + d(ah, bl) + d(al, bh)


def _rms(x, g):
    return x * lax.rsqrt(jnp.mean(x * x, axis=-1, keepdims=True) + EPS) * g


def _silu(x):
    return x * (1.0 / (1.0 + jnp.exp(-x)))


def _ada_kernel(c_ref, w_ref, b_ref, o_ref):
    o_ref[...] = _dot3(_silu(c_ref[...]), w_ref[...]) + b_ref[...]


def _ada(cin, w_ada, b_ada):
    rows, d = cin.shape
    n = w_ada.shape[1]
    tn = 1024
    return pl.pallas_call(
        _ada_kernel,
        grid=(n // tn,),
        in_specs=[pl.BlockSpec((rows, d), lambda j: (0, 0)),
                  pl.BlockSpec((d, tn), lambda j: (0, j)),
                  pl.BlockSpec((1, tn), lambda j: (0, j))],
        out_specs=pl.BlockSpec((rows, tn), lambda j: (0, j)),
        out_shape=jax.ShapeDtypeStruct((rows, n), F32),
        compiler_params=_cparams(("arbitrary",)),
        name="ada",
    )(cin, w_ada, b_ada.reshape(1, n))


def _rope(t, cos, sa, sb):
    return t * cos + pltpu.roll(t, LANES - AXIS_DIM // 2, 1) * sa + pltpu.roll(t, AXIS_DIM // 2, 1) * sb


def _inproj_kernel(x_ref, mod_ref, g_ref, w_ref, cq_ref, ck_ref, sa_ref, sb_ref,
                   q_ref, k_ref, vt_ref, p_ref):
    x = x_ref[0]
    h = _rms(x, g_ref[...]) * (1.0 + mod_ref[0, 1:2, :]) + mod_ref[0, 0:1, :]
    px = _dot(h.astype(BF16), w_ref[...])
    ck = ck_ref[...]
    sa = sa_ref[...]
    sb = sb_ref[...]
    cq = cq_ref[...]
    scale = QK_DIM ** -0.5 * LOG2E
    saq = sa * scale
    sbq = sb * scale
    for hh in range(N_HEADS):
        lo = hh * LANES
        q = px[:, lo:lo + LANES]
        k = px[:, ATTN_WIDTH + lo:ATTN_WIDTH + lo + LANES]
        q_ref[0, :, lo:lo + LANES] = _rope(q, cq, saq, sbq).astype(BF16)
        k_ref[0, :, lo:lo + LANES] = _rope(k, ck, sa, sb).astype(BF16)
    vt_ref[0] = jnp.transpose(px[:, 2 * ATTN_WIDTH:3 * ATTN_WIDTH]).astype(BF16)
    p_ref[0] = px[:, 3 * ATTN_WIDTH:].astype(BF16)


def _inproj(x, mod3, g, w_in_bf, cq, ck, sa, sb, tm=512):
    b, n, d = x.shape
    width = w_in_bf.shape[1]
    row = lambda bi, i: (bi, i, 0)
    tab = pl.BlockSpec((tm, LANES), lambda bi, i: (i, 0))
    out = jax.ShapeDtypeStruct((b, n, ATTN_WIDTH), BF16)
    return pl.pallas_call(
        _inproj_kernel,
        grid=(b, n // tm),
        in_specs=[pl.BlockSpec((1, tm, d), row),
                  pl.BlockSpec((1, 6, d), lambda bi, i: (bi, 0, 0)),
                  pl.BlockSpec((1, d), lambda bi, i: (0, 0)),
                  pl.BlockSpec((d, width), lambda bi, i: (0, 0)),
                  tab, tab, tab, tab],
        out_specs=[pl.BlockSpec((1, tm, ATTN_WIDTH), row),
                   pl.BlockSpec((1, tm, ATTN_WIDTH), row),
                   pl.BlockSpec((1, ATTN_WIDTH, tm), lambda bi, i: (bi, 0, i)),
                   pl.BlockSpec((1, tm, ATTN_WIDTH), row)],
        out_shape=[out, out, jax.ShapeDtypeStruct((b, ATTN_WIDTH, n), BF16), out],
        compiler_params=_cparams(("arbitrary", "arbitrary")),
        name="inproj",
    )(x, mod3, g, w_in_bf, cq, ck, sa, sb)


def _ctxkv_kernel(x_ref, mod_ref, g_ref, wk_ref, wv_ref, k_ref, vt_ref):
    h = _rms(x_ref[0], g_ref[...]) * (1.0 + mod_ref[0, 1:2, :]) + mod_ref[0, 0:1, :]
    hb = h.astype(BF16)
    k_ref[0] = _dot(hb, wk_ref[...]).astype(BF16)
    vt_ref[0] = jnp.transpose(_dot(hb, wv_ref[...])).astype(BF16)


def _ctxkv(ctx, mod3, g, w_in_bf, ctx_row):
    b, n, d = ctx.shape
    out = jax.ShapeDtypeStruct((b, n, ATTN_WIDTH), BF16)
    return pl.pallas_call(
        _ctxkv_kernel,
        grid=(b,),
        in_specs=[pl.BlockSpec((1, n, d), lambda bi: (bi, 0, 0)),
                  pl.BlockSpec((1, 6, d), lambda bi: (ctx_row, 0, 0)),
                  pl.BlockSpec((1, d), lambda bi: (0, 0)),
                  pl.BlockSpec((d, ATTN_WIDTH), lambda bi: (0, 1)),
                  pl.BlockSpec((d, ATTN_WIDTH), lambda bi: (0, 2))],
        out_specs=[pl.BlockSpec((1, n, ATTN_WIDTH), lambda bi: (bi, 0, 0)),
                   pl.BlockSpec((1, ATTN_WIDTH, n), lambda bi: (bi, 0, 0))],
        out_shape=[out, jax.ShapeDtypeStruct((b, ATTN_WIDTH, n), BF16)],
        compiler_params=_cparams(("arbitrary",)),
        name="ctxkv",
    )(ctx, mod3, g, w_in_bf, w_in_bf)


LOG2E = math.log2(math.e)
KEY_CHUNK = 256


def _attn_step(q_ref, kx_ref, kc_ref, vt_ref, vct_ref, lam_ref, g_ref, o_ref, s_new, m_new, s_old, m_old):
    nc = kc_ref.shape[1]
    tq = q_ref.shape[1]
    nk = s_old.shape[1]
    groups = KEY_CHUNK // 8
    q = q_ref[0]
    lane = lax.broadcasted_iota(jnp.int32, q.shape, 1)
    zero = jnp.zeros_like(q)
    qms = (jnp.where(lane < QK_DIM, q, zero), jnp.where(lane >= QK_DIM, q, zero))
    m_prev = (m_old[0], m_old[1])
    top = [jnp.full((8, tq), -jnp.inf, F32) for _ in range(2)]
    den = [jnp.zeros((8, tq), F32) for _ in range(2)]
    acc = [jnp.zeros((V_DIM, tq), F32) for _ in range(2)]
    for c in range(nk // KEY_CHUNK):
        lo = c * KEY_CHUNK
        if lo < nc:
            keys, vt = kc_ref[0, lo:lo + KEY_CHUNK, :], vct_ref[0, :, lo:lo + KEY_CHUNK]
        else:
            keys, vt = kx_ref[0, lo - nc:lo - nc + KEY_CHUNK, :], vt_ref[0, :, lo - nc:lo - nc + KEY_CHUNK]
        for mp in range(2):
            s = _dot_nt(keys, qms[mp])
            s_new[mp, lo:lo + KEY_CHUNK, :] = s
            top[mp] = jnp.maximum(top[mp], jnp.max(s.reshape(groups, 8, tq), axis=0))
            p = jnp.exp2(s_old[mp, lo:lo + KEY_CHUNK, :] - m_prev[mp])
            den[mp] = den[mp] + jnp.sum(p.reshape(groups, 8, tq), axis=0)
            acc[mp] = acc[mp] + _dot(vt, p.astype(BF16))
    for mp in range(2):
        m_new[mp] = jnp.max(top[mp], axis=0, keepdims=True)
    lv = lam_ref[...]
    lam = (jnp.exp(jnp.sum(lv[0:1] * lv[1:2], axis=-1, keepdims=True))
           - jnp.exp(jnp.sum(lv[2:3] * lv[3:4], axis=-1, keepdims=True)) + LAM_INIT)
    l1 = jnp.sum(den[0], axis=0, keepdims=True)
    l2 = jnp.sum(den[1], axis=0, keepdims=True)
    o = acc[0] * (1.0 / l1) - acc[1] * (lam / l2)
    o = o * lax.rsqrt(jnp.mean(o * o, axis=0, keepdims=True) + EPS) * (g_ref[...] * (1.0 - LAM_INIT))
    o_ref[0] = jnp.transpose(o).astype(BF16)


def _attn_kernel(q_ref, kx_ref, kc_ref, vt_ref, vct_ref, lam_ref, g_ref, o_ref, s0_ref, m0_ref, s1_ref, m1_ref):
    t = pl.program_id(0)
    io = (q_ref, kx_ref, kc_ref, vt_ref, vct_ref, lam_ref, g_ref, o_ref)

    @pl.when(t == 0)
    def _():
        s1_ref[...] = jnp.zeros_like(s1_ref)
        m1_ref[...] = jnp.zeros_like(m1_ref)

    @pl.when(lax.rem(t, 2) == 0)
    def _():
        _attn_step(*io, s0_ref, m0_ref, s1_ref, m1_ref)

    @pl.when(lax.rem(t, 2) == 1)
    def _():
        _attn_step(*io, s1_ref, m1_ref, s0_ref, m0_ref)


def _attn(q, k, kc, vt, vct, lamv, g_col, tq=256):
    b, n, _ = q.shape
    nc = kc.shape[1]
    nq = n // tq
    total = b * N_HEADS * nq

    def cur(t):
        ta = jnp.minimum(t, total - 1)
        bh = ta // nq
        return bh // N_HEADS, bh % N_HEADS, ta % nq

    def prv(t):
        tb = jnp.maximum(t - 1, 0)
        bh = tb // nq
        return bh // N_HEADS, bh % N_HEADS, tb % nq

    def at(fn, order):
        return lambda t: tuple((fn(t) + (0,))[j] for j in order)

    return pl.pallas_call(
        _attn_kernel,
        grid=(total + 1,),
        in_specs=[pl.BlockSpec((1, tq, LANES), at(cur, (0, 2, 1))),
                  pl.BlockSpec((1, n, LANES), at(cur, (0, 3, 1))),
                  pl.BlockSpec((1, nc, LANES), at(cur, (0, 3, 1))),
                  pl.BlockSpec((1, V_DIM, n), at(prv, (0, 1, 3))),
                  pl.BlockSpec((1, V_DIM, nc), at(prv, (0, 1, 3))),
                  pl.BlockSpec((4, QK_DIM), lambda t: (0, 0)),
                  pl.BlockSpec((V_DIM, 1), lambda t: (0, 0))],
        out_specs=pl.BlockSpec((1, tq, LANES), at(prv, (0, 2, 1))),
        out_shape=jax.ShapeDtypeStruct((b, n, ATTN_WIDTH), BF16),
        scratch_shapes=[pltpu.VMEM((2, nc + n, tq), F32), pltpu.VMEM((2, 1, tq), F32),
                        pltpu.VMEM((2, nc + n, tq), F32), pltpu.VMEM((2, 1, tq), F32)],
        compiler_params=_cparams(("arbitrary",)),
        name="attn",
    )(q, k, kc, vt, vct, lamv, g_col)


def _outproj_kernel(att_ref, p_ref, pprev_ref, pnext_ref, x_ref, mod_ref, wpool_ref, bpool_ref,
                    pscale_ref, wout_ref, gpost_ref, gpre_ref, wr_ref,
                    x1_ref, h2_ref, h2t_ref, lg_ref, *, tm, n):
    i = pl.program_id(1)
    p = p_ref[0].astype(F32)
    prev = jnp.where(i > 0, pprev_ref[0].astype(F32), 0.0)
    nxt = jnp.where(i < pl.num_programs(1) - 1, pnext_ref[0].astype(F32), 0.0)
    ext = jnp.concatenate([prev, p, nxt], axis=0)
    rows = tm + 2 * HALO
    trow = (i * tm + lax.broadcasted_iota(jnp.int32, (tm, 1), 0))

    pooled = []
    sums = {}
    acc = ext + pltpu.roll(ext, 1, 0)
    sums[2] = acc
    w = 2
    while w < POOL_WINDOWS[-1]:
        acc = pltpu.roll(acc, rows - w // 2, 0) + pltpu.roll(acc, w // 2, 0)
        w = 2 * w
        sums[w] = acc
    for gi, win in enumerate(POOL_WINDOWS):
        sl = slice(gi * POOL_GROUP, (gi + 1) * POOL_GROUP)
        wsum = sums[win][HALO:HALO + tm, sl]
        hi_c = jnp.minimum(trow + (win - win // 2), n)
        lo_c = jnp.maximum(trow - win // 2, 0)
        cnt = (hi_c - lo_c).astype(F32)
        d = wsum / cnt - p[:, sl]
        y = _dot(d.astype(BF16), wpool_ref[gi]) + bpool_ref[:, sl]
        pooled.append((y * pscale_ref[:, sl]).astype(BF16))
    pool = jnp.concatenate(pooled, axis=1)
    yx = _dot(att_ref[0], wout_ref[0:ATTN_WIDTH, :]) + _dot(pool, wout_ref[ATTN_WIDTH:, :])
    x1 = x_ref[0] + mod_ref[0, 2:3, :] * _rms(yx, gpost_ref[...])
    x1_ref[0] = x1
    h2 = _rms(x1, gpre_ref[...]) * (1.0 + mod_ref[0, 4:5, :]) + mod_ref[0, 3:4, :]
    h2_ref[0] = h2.astype(BF16)
    for k in range(CHUNKS):
        h2t_ref[0, pl.ds(k, tm, stride=CHUNKS), :] = h2[:, k * LANES:(k + 1) * LANES]
    lg_ref[...] = _dot3(wr_ref[...], h2, nt=True)


def _outproj(att, p, x, mod3, w_pool_bf, b_pool, pool_scale, w_out_bf, g_post, g_pre, w_router_t, tm=256):
    b, n, d = x.shape
    hb = tm // HALO
    nhb = n // HALO
    row = lambda bi, i: (bi, i, 0)
    vec = lambda bi, i: (0, 0)
    kern = functools.partial(_outproj_kernel, tm=tm, n=n)
    return pl.pallas_call(
        kern,
        grid=(b, n // tm),
        in_specs=[pl.BlockSpec((1, tm, ATTN_WIDTH), row),
                  pl.BlockSpec((1, tm, POOL_WIDTH), row),
                  pl.BlockSpec((1, HALO, POOL_WIDTH), lambda bi, i: (bi, jnp.maximum(i * hb - 1, 0), 0)),
                  pl.BlockSpec((1, HALO, POOL_WIDTH), lambda bi, i: (bi, jnp.minimum((i + 1) * hb, nhb - 1), 0)),
                  pl.BlockSpec((1, tm, d), row),
                  pl.BlockSpec((1, 6, d), lambda bi, i: (bi, 0, 0)),
                  pl.BlockSpec((len(POOL_WINDOWS), POOL_GROUP, POOL_GROUP), lambda bi, i: (0, 0, 0)),
                  pl.BlockSpec((1, POOL_WIDTH), vec),
                  pl.BlockSpec((1, POOL_WIDTH), vec),
                  pl.BlockSpec((d, d), vec),
                  pl.BlockSpec((1, d), vec),
                  pl.BlockSpec((1, d), vec),
                  pl.BlockSpec((N_EXPERTS, d), vec)],
        out_specs=[pl.BlockSpec((1, tm, d), row),
                   pl.BlockSpec((1, tm, d), row),
                   pl.BlockSpec((1, tm * CHUNKS, LANES), row),
                   pl.BlockSpec((N_EXPERTS, tm), lambda bi, i: (0, bi * (n // tm) + i))],
        out_shape=[jax.ShapeDtypeStruct((b, n, d), F32),
                   jax.ShapeDtypeStruct((b, n, d), BF16),
                   jax.ShapeDtypeStruct((b, n * CHUNKS, LANES), F32),
                   jax.ShapeDtypeStruct((N_EXPERTS, b * n), F32)],
        compiler_params=_cparams(("arbitrary", "arbitrary")),
        name="outproj",
    )(att, p, p, p, x, mod3, w_pool_bf, b_pool, pool_scale, w_out_bf, g_post, g_pre, w_router_t)


def _beats(a, b, a_first):
    return jnp.where((a >= b) if a_first else (a > b), 1.0, 0.0)


def _route_kernel(lg_ref, bias_ref, w_ref):
    scores = [1.0 / (1.0 + jnp.exp(-lg_ref[e])) for e in range(N_EXPERTS)]
    biased = [scores[e] + bias_ref[e] for e in range(N_EXPERTS)]
    gscore = []
    for g in range(N_GROUPS):
        vals = biased[g * GROUP_SIZE:(g + 1) * GROUP_SIZE]
        m1 = vals[0]
        m2 = jnp.full_like(m1, -jnp.inf)
        for v in vals[1:]:
            m2 = jnp.maximum(m2, jnp.minimum(m1, v))
            m1 = jnp.maximum(m1, v)
        gscore.append(m1 + m2)
    gsel = []
    for g in range(N_GROUPS):
        rank = jnp.zeros_like(gscore[g])
        for g2 in range(N_GROUPS):
            if g2 != g:
                rank = rank + _beats(gscore[g2], gscore[g], g2 < g)
        gsel.append(rank < TOPK_GROUPS)
    masked = [jnp.where(gsel[e // GROUP_SIZE], biased[e], -jnp.inf) for e in range(N_EXPERTS)]
    picked = []
    for e in range(N_EXPERTS):
        rank = jnp.zeros_like(masked[e])
        for e2 in range(N_EXPERTS):
            if e2 != e:
                rank = rank + _beats(masked[e2], masked[e], e2 < e)
        picked.append(jnp.where(rank < TOP_K, scores[e], 0.0))
    denom = picked[0]
    for e in range(1, N_EXPERTS):
        denom = denom + picked[e]
    for e in range(N_EXPERTS):
        w_ref[e] = picked[e] / denom * ROUTED_SCALE


def _route(logits3, bias3, rows=8):
    e, r, l = logits3.shape
    return pl.pallas_call(
        _route_kernel,
        grid=(r // rows,),
        in_specs=[pl.BlockSpec((e, rows, l), lambda i: (0, i, 0)),
                  pl.BlockSpec((e, 1, l), lambda i: (0, 0, 0))],
        out_specs=pl.BlockSpec((e, rows, l), lambda i: (0, i, 0)),
        out_shape=jax.ShapeDtypeStruct((e, r, l), F32),
        compiler_params=_cparams(("arbitrary",)),
        name="route",
    )(logits3, bias3)


CHUNKS = D_MODEL // LANES
SLAB = SLOT_TILE + 4
ROW_BATCH = 16


def _gather_tile(row_ref, tile, xs_ref, tx_ref):
    for s in range(SLOT_TILE):
        row = pl.multiple_of(row_ref[0, tile, s], CHUNKS)
        tx_ref[pl.ds(s, CHUNKS, stride=SLAB), :] = xs_ref[pl.ds(row, CHUNKS), :]


def _experts_step(j, wslot, row_ref, sw_ref, wg_buf, wu_buf, wd_buf, xs_ref, acc_ref,
                  tx_cur, tx_nxt, ty_cur, ty_prv):
    ntile = row_ref.shape[1]
    _gather_tile(row_ref, jnp.minimum(j + 1, ntile - 1), xs_ref, tx_nxt)
    xb = jnp.concatenate([tx_cur[pl.ds(k * SLAB, SLOT_TILE), :] for k in range(CHUNKS)], axis=1).astype(BF16)
    g = _dot(xb, wg_buf[wslot])
    u = _dot(xb, wu_buf[wslot])
    h = (_silu(g) * u).astype(BF16)
    y = _dot(h, wd_buf[wslot])
    w_rows = jnp.transpose(jnp.broadcast_to(sw_ref[0, pl.ds(j, 1), :], (LANES, SLOT_TILE)))
    for k in range(CHUNKS):
        ty_cur[pl.ds(k * SLAB, SLOT_TILE), :] = y[:, k * LANES:(k + 1) * LANES] * w_rows
    prv = jnp.maximum(j - 1, 0)
    for s0 in range(0, SLOT_TILE, ROW_BATCH):
        new = []
        for s in range(s0, s0 + ROW_BATCH):
            row = pl.multiple_of(row_ref[0, prv, s], CHUNKS)
            new.append((row, acc_ref[pl.ds(row, CHUNKS), :] + ty_prv[pl.ds(s, CHUNKS, stride=SLAB), :]))
        for row, v in new:
            acc_ref[pl.ds(row, CHUNKS), :] = v


def _experts_kernel(be_ref, nt_ref, first_ref, nxt_ref, ws_ref, row_ref, sw_ref, x_hbm,
                    wg_hbm, wu_hbm, wd_hbm, o_hbm,
                    xs_ref, acc_ref, tx0_ref, tx1_ref, ty0_ref, ty1_ref, wg_buf, wu_buf, wd_buf,
                    sem, wsem, *, ntile, n):
    c = pl.program_id(0)
    j = pl.program_id(1)
    t = c * ntile + j
    rows = n * CHUNKS
    wslot = ws_ref[t]

    def weight_copies(e, slot):
        return [pltpu.make_async_copy(src.at[e], dst.at[slot], wsem.at[slot])
                for src, dst in ((wg_hbm, wg_buf), (wu_hbm, wu_buf), (wd_hbm, wd_buf))]

    @pl.when(j == 0)
    def _():
        for wcp in weight_copies(be_ref[t], 0):
            wcp.start()
        cp = pltpu.make_async_copy(x_hbm.at[c], xs_ref.at[pl.ds(0, rows)], sem.at[0])
        cp.start()
        xs_ref[pl.ds(rows, CHUNKS), :] = jnp.zeros((CHUNKS, LANES), F32)
        acc_ref[...] = jnp.zeros_like(acc_ref)
        ty1_ref[...] = jnp.zeros_like(ty1_ref)
        cp.wait()
        _gather_tile(row_ref, 0, xs_ref, tx0_ref)

    @pl.when(first_ref[t] == 1)
    def _():
        for wcp in weight_copies(be_ref[t], wslot):
            wcp.wait()

    @pl.when((first_ref[t] == 1) & (nxt_ref[t] >= 0))
    def _():
        for wcp in weight_copies(nxt_ref[t], 1 - wslot):
            wcp.start()

    live = j <= nt_ref[c]
    args = (j, wslot, row_ref, sw_ref, wg_buf, wu_buf, wd_buf, xs_ref, acc_ref)

    @pl.when(live & (lax.rem(j, 2) == 0))
    def _():
        _experts_step(*args, tx0_ref, tx1_ref, ty0_ref, ty1_ref)

    @pl.when(live & (lax.rem(j, 2) == 1))
    def _():
        _experts_step(*args, tx1_ref, tx0_ref, ty1_ref, ty0_ref)

    @pl.when(j == ntile - 1)
    def _():
        cp = pltpu.make_async_copy(acc_ref.at[pl.ds(0, rows)], o_hbm.at[c], sem.at[1])
        cp.start()
        cp.wait()


def _experts(block_e, ntiles, first, nxt_e, wslot, slot_row, slot_w, h2t, wg_bf, wu_bf, wd_bf, ntile, n):
    b = h2t.shape[0]
    d, f = wg_bf.shape[1], wg_bf.shape[2]
    batch = lambda c, j, *_: (c, 0, 0)
    kern = functools.partial(_experts_kernel, ntile=ntile, n=n)
    slab = pltpu.VMEM((CHUNKS * SLAB, LANES), F32)
    hbm = pl.BlockSpec(memory_space=pl.ANY)
    grid_spec = pltpu.PrefetchScalarGridSpec(
        num_scalar_prefetch=5,
        grid=(b, ntile),
        in_specs=[pl.BlockSpec((1, ntile, SLOT_TILE), batch, memory_space=pltpu.SMEM),
                  pl.BlockSpec((1, ntile, SLOT_TILE), batch),
                  hbm, hbm, hbm, hbm],
        out_specs=hbm,
        scratch_shapes=[pltpu.VMEM(((n + 1) * CHUNKS, LANES), F32),
                        pltpu.VMEM(((n + 1) * CHUNKS, LANES), F32),
                        slab, slab, slab, slab,
                        pltpu.VMEM((2, d, f), BF16), pltpu.VMEM((2, d, f), BF16), pltpu.VMEM((2, f, d), BF16),
                        pltpu.SemaphoreType.DMA((2,)), pltpu.SemaphoreType.DMA((2,))],
    )
    return pl.pallas_call(
        kern,
        grid_spec=grid_spec,
        out_shape=jax.ShapeDtypeStruct((b, n * CHUNKS, LANES), F32),
        compiler_params=_cparams(("arbitrary", "arbitrary")),
        name="experts",
    )(block_e, ntiles, first, nxt_e, wslot, slot_row, slot_w, h2t, wg_bf, wu_bf, wd_bf)


EXPERTS_PER_LANE_GROUP = LANES // 32
PLAN_GROUP = 4


def _plan_tables_kernel(w_ref, re_ref, ro_ref, tab_ref, *, rows):
    ne = w_ref.shape[0]
    w2 = w_ref[...].reshape(ne * rows, LANES)
    selb = jnp.where(w2 > 0.0, 1.0, 0.0).astype(BF16)
    i0 = lax.broadcasted_iota(jnp.int32, (LANES, LANES), 0)
    i1 = lax.broadcasted_iota(jnp.int32, (LANES, LANES), 1)
    rowtot = _dot(selb, jnp.ones((LANES, LANES), BF16))
    ridx = lax.broadcasted_iota(jnp.int32, rowtot.shape, 0) & (rows - 1)
    acc = rowtot
    sh = 1
    while sh < rows:
        acc = acc + jnp.where(ridx >= sh, pltpu.roll(acc, sh, 0), 0.0)
        sh *= 2
    re_ref[0] = acc
    ro_ref[0] = acc - rowtot
    incl_t = _dot_nt(jnp.where(i1 <= i0, 1.0, 0.0).astype(BF16), selb)
    eye = jnp.where(i0 == i1, 1.0, 0.0).astype(BF16)
    w_hi = w2.astype(BF16)
    r1 = w2 - w_hi.astype(F32)
    w_mid = r1.astype(BF16)
    w_lo = (r1 - w_mid.astype(F32)).astype(BF16)
    parts = [_dot_nt(eye, p) for p in (w_hi, w_mid, w_lo)]
    for g in range(ne * rows // LANES):
        sl = slice(g * LANES, (g + 1) * LANES)
        tab_ref[0, g, 0:LANES, :] = incl_t[:, sl].astype(BF16)
        for k in range(3):
            tab_ref[0, g, (k + 1) * LANES:(k + 2) * LANES, :] = parts[k][:, sl].astype(BF16)


def _plan_tables(selw3, b):
    ne, r_all, _ = selw3.shape
    rows = r_all // b
    groups = ne * rows // LANES
    kern = functools.partial(_plan_tables_kernel, rows=rows)
    return pl.pallas_call(
        kern,
        grid=(b,),
        in_specs=[pl.BlockSpec((ne, rows, LANES), lambda c: (0, c, 0))],
        out_specs=[pl.BlockSpec((1, ne * rows, LANES), lambda c: (c, 0, 0)),
                   pl.BlockSpec((1, ne * rows, LANES), lambda c: (c, 0, 0)),
                   pl.BlockSpec((1, groups, 4 * LANES, LANES), lambda c: (c, 0, 0, 0))],
        out_shape=[jax.ShapeDtypeStruct((b, ne * rows, LANES), F32),
                   jax.ShapeDtypeStruct((b, ne * rows, LANES), F32),
                   jax.ShapeDtypeStruct((b, groups, 4 * LANES, LANES), BF16)],
        compiler_params=_cparams(("arbitrary",)),
        name="plan_tables",
    )(selw3)


def _plan_slots_kernel(be_ref, q0_ref, nt_ref, re_ref, ro_ref, tab_ref, tok_ref, sw_ref,
                       *, ntile, rows, n):
    c = pl.program_id(0)
    tok_ref[...] = jnp.full(tok_ref.shape, n * CHUNKS, jnp.int32)
    sw_ref[...] = jnp.zeros(sw_ref.shape, F32)
    lane = lax.broadcasted_iota(jnp.int32, (1, SLOT_TILE), 1).astype(F32)
    sub_r = lax.broadcasted_iota(jnp.int32, (rows, SLOT_TILE), 0).astype(F32)
    sub_l = lax.broadcasted_iota(jnp.int32, (LANES, SLOT_TILE), 0).astype(F32)
    reps = SLOT_TILE // LANES

    def tile_body(j):
        e = be_ref[c * ntile + j]
        q = q0_ref[c * ntile + j].astype(F32) + lane
        base = pl.multiple_of(e * rows, rows)
        row_end = jnp.concatenate([re_ref[0, pl.ds(base, rows), :]] * reps, axis=1)
        row_off = jnp.concatenate([ro_ref[0, pl.ds(base, rows), :]] * reps, axis=1)
        r = jnp.sum(jnp.where(row_end <= q, 1.0, 0.0), axis=0, keepdims=True)
        row_start = jnp.sum(jnp.where(sub_r == r, row_off, 0.0), axis=0, keepdims=True)
        g = lax.shift_right_logical(e, EXPERTS_PER_LANE_GROUP.bit_length() - 1)
        col = r + ((e & (EXPERTS_PER_LANE_GROUP - 1)) * rows).astype(F32)
        onehot = jnp.where(sub_l == col, 1.0, 0.0).astype(BF16)
        picked = _dot(tab_ref[0, g], onehot)
        incl = picked[0:LANES]
        lane_idx = jnp.sum(jnp.where(incl <= q - row_start, 1.0, 0.0), axis=0, keepdims=True)
        wrow = (picked[LANES:2 * LANES] + picked[2 * LANES:3 * LANES]) + picked[3 * LANES:]
        wsel = jnp.sum(jnp.where(sub_l == lane_idx, wrow, 0.0), axis=0, keepdims=True)
        valid = (q < row_end[rows - 1:rows, :]) & (j < nt_ref[c])
        tok = jnp.where(valid, r * LANES + lane_idx, float(n)).astype(jnp.int32)
        tok_ref[0, pl.ds(j, 1), :] = tok * CHUNKS
        sw_ref[0, pl.ds(j, 1), :] = jnp.where(valid, wsel, 0.0)

    def tile_group(i, carry):
        for k in range(PLAN_GROUP):
            tile_body(PLAN_GROUP * i + k)
        return carry

    lax.fori_loop(0, (nt_ref[c] + PLAN_GROUP - 1) // PLAN_GROUP, tile_group, 0)


def _plan_slots(block_e, tile_q0, ntiles, row_end, row_off, tables, ntile, n):
    b, er, _ = row_end.shape
    rows = n // LANES
    groups = tables.shape[1]
    kern = functools.partial(_plan_slots_kernel, ntile=ntile, rows=rows, n=n)
    grid_spec = pltpu.PrefetchScalarGridSpec(
        num_scalar_prefetch=3,
        grid=(b,),
        in_specs=[pl.BlockSpec((1, er, LANES), lambda c, *_: (c, 0, 0)),
                  pl.BlockSpec((1, er, LANES), lambda c, *_: (c, 0, 0)),
                  pl.BlockSpec((1, groups, 4 * LANES, LANES), lambda c, *_: (c, 0, 0, 0))],
        out_specs=[pl.BlockSpec((1, ntile, SLOT_TILE), lambda c, *_: (c, 0, 0)),
                   pl.BlockSpec((1, ntile, SLOT_TILE), lambda c, *_: (c, 0, 0))],
    )
    return pl.pallas_call(
        kern,
        grid_spec=grid_spec,
        out_shape=[jax.ShapeDtypeStruct((b, ntile, SLOT_TILE), jnp.int32),
                   jax.ShapeDtypeStruct((b, ntile, SLOT_TILE), F32)],
        compiler_params=_cparams(("arbitrary",)),
        name="plan_slots",
    )(block_e, tile_q0, ntiles, row_end, row_off, tables)


def _tile_meta(row_end, b, n, ntile):
    rows = n // LANES
    cnt = row_end.reshape(b, N_EXPERTS, rows, LANES)[:, :, rows - 1, 0].astype(jnp.int32)
    nt_e = (cnt + SLOT_TILE - 1) // SLOT_TILE
    tend = jnp.cumsum(nt_e, axis=1)
    tstart = tend - nt_e
    ntiles = tend[:, -1]
    tile_id = jnp.arange(ntile, dtype=jnp.int32)
    tid = jnp.minimum(tile_id[None, :], ntiles[:, None] - 1)
    be = jnp.minimum(jnp.sum(tend[:, None, :] <= tid[:, :, None], axis=2).astype(jnp.int32), N_EXPERTS - 1)
    q0 = (tid - jnp.take_along_axis(tstart, be, axis=1)) * SLOT_TILE
    valid = tile_id[None, :] < ntiles[:, None]
    prev_be = jnp.concatenate([jnp.full((b, 1), -1, jnp.int32), be[:, :-1]], axis=1)
    first = (valid & (be != prev_be)).astype(jnp.int32)
    wslot = (jnp.cumsum(first, axis=1) - 1) % 2
    eid = jnp.arange(N_EXPERTS, dtype=jnp.int32)
    later = (nt_e > 0)[:, None, :] & (eid[None, None, :] > eid[None, :, None])
    nxt = jnp.min(jnp.where(later, eid[None, None, :], N_EXPERTS), axis=2)
    nxt = jnp.where(nxt < N_EXPERTS, nxt, -1)
    nxt_e = jnp.take_along_axis(nxt, be, axis=1)
    flat = lambda a: a.reshape(-1).astype(jnp.int32)
    return flat(be), flat(q0), ntiles.astype(jnp.int32), flat(first), flat(nxt_e), flat(wslot)


def _dispatch_plan(selw3, b, n, ntile):
    row_end, row_off, tables = _plan_tables(selw3, b)
    block_e, tile_q0, ntiles, first, nxt_e, wslot = _tile_meta(row_end, b, n, ntile)
    slot_tok, slot_w = _plan_slots(block_e, tile_q0, ntiles, row_end, row_off, tables, ntile, n)
    return block_e, ntiles, first, nxt_e, wslot, slot_tok, slot_w


def _final_kernel(h2_ref, r_ref, x1_ref, mod_ref, wg_ref, wu_ref, wd_ref, g_ref, o_ref):
    hb = h2_ref[0]
    tm = hb.shape[0]
    sh = _dot((_silu(_dot(hb, wg_ref[...])) * _dot(hb, wu_ref[...])).astype(BF16), wd_ref[...])
    routed = jnp.concatenate([r_ref[0, pl.ds(k, tm, stride=CHUNKS), :] for k in range(CHUNKS)], axis=1)
    moe = routed + sh
    o_ref[0] = x1_ref[0] + mod_ref[0, 5:6, :] * _rms(moe, g_ref[...])


def _final(h2, routed, x1, mod3, wsg_bf, wsu_bf, wsd_bf, g_post, tm=512):
    b, n, d = x1.shape
    f = wsg_bf.shape[1]
    row = lambda bi, i: (bi, i, 0)
    vec = lambda bi, i: (0, 0)
    return pl.pallas_call(
        _final_kernel,
        grid=(b, n // tm),
        in_specs=[pl.BlockSpec((1, tm, d), row),
                  pl.BlockSpec((1, tm * CHUNKS, LANES), row),
                  pl.BlockSpec((1, tm, d), row),
                  pl.BlockSpec((1, 6, d), lambda bi, i: (bi, 0, 0)),
                  pl.BlockSpec((d, f), vec),
                  pl.BlockSpec((d, f), vec),
                  pl.BlockSpec((f, d), vec),
                  pl.BlockSpec((1, d), vec)],
        out_specs=pl.BlockSpec((1, tm, d), row),
        out_shape=jax.ShapeDtypeStruct((b, n, d), F32),
        compiler_params=_cparams(("arbitrary", "arbitrary")),
        name="final",
    )(h2, routed, x1, mod3, wsg_bf, wsu_bf, wsd_bf, g_post)


def _rope_tables(n):
    t = np.arange(n)
    row = (t // GRID_W).astype(np.float32)
    col = (t % GRID_W).astype(np.float32)
    freqs = np.float32(ROPE_THETA) ** (-np.arange(0, AXIS_DIM, 2, dtype=np.float32) / np.float32(AXIS_DIM))
    ar = row[:, None] * freqs
    ac = col[:, None] * freqs
    zeros = np.zeros_like(ar)
    cos64 = np.concatenate([np.cos(ar), np.cos(ar), np.cos(ac), np.cos(ac)], axis=1)
    sa64 = np.concatenate([-np.sin(ar), zeros, -np.sin(ac), zeros], axis=1)
    sb64 = np.concatenate([zeros, np.sin(ar), zeros, np.sin(ac)], axis=1)
    two = lambda a: jnp.asarray(np.concatenate([a, a], axis=1).astype(np.float32))
    return two(cos64), two(sa64), two(sb64)


def kernel(x, c, ctx, c_ctx, w_ada, b_ada, g_pre_mix, g_post_mix, g_pre_ffn, g_post_ffn, w_in, lambda_q1, lambda_k1, lambda_q2, lambda_k2, g_subln, w_pool, b_pool, pool_scale, w_out, w_router, router_bias, w_e_gate, w_e_up, w_e_down, w_sh_gate, w_sh_up, w_sh_down):
    b, n, d = x.shape
    l = 0
    mod_rows = 8
    cin = jnp.concatenate([c, c_ctx[None, :], jnp.zeros((mod_rows - b - 1, d), F32)], axis=0)
    mod3 = _ada(cin, w_ada[l], b_ada[l]).reshape(mod_rows, 6, d)

    w_in_bf = w_in[l].astype(BF16)
    cos, sa, sb = _rope_tables(n)
    q, k, vt, p = _inproj(x, mod3, g_pre_mix[l][None, :], w_in_bf, cos * (QK_DIM ** -0.5 * LOG2E), cos, sa, sb)
    kc, vct = _ctxkv(ctx, mod3, g_pre_mix[l][None, :], w_in_bf, b)

    lamv = jnp.stack([lambda_q1[l], lambda_k1[l], lambda_q2[l], lambda_k2[l]], axis=0)
    att = _attn(q, k, kc, vt, vct, lamv, g_subln[l][:, None])

    x1, h2, h2t, logits_t = _outproj(
        att, p, x, mod3, w_pool[l].astype(BF16), b_pool[l][None, :], pool_scale[l][None, :],
        w_out[l].astype(BF16), g_post_mix[l][None, :], g_pre_ffn[l][None, :], w_router[l].T)

    t = b * n
    selw3 = _route(logits_t.reshape(N_EXPERTS, t // LANES, LANES),
                   jnp.broadcast_to(router_bias[l][:, None, None], (N_EXPERTS, 1, LANES)))

    ntile = (n * TOP_K) // SLOT_TILE + N_EXPERTS
    block_e, ntiles, first, nxt_e, wslot, slot_row, slot_w = _dispatch_plan(selw3, b, n, ntile)
    routed = _experts(block_e, ntiles, first, nxt_e, wslot, slot_row, slot_w, h2t,
                      w_e_gate[l].astype(BF16), w_e_up[l].astype(BF16), w_e_down[l].astype(BF16), ntile, n)

    return _final(h2, routed, x1, mod3, w_sh_gate[l].astype(BF16), w_sh_up[l].astype(BF16),
                  w_sh_down[l].astype(BF16), g_post_ffn[l][None, :])
```

```python
import functools
import math

import jax
import jax.numpy as jnp
import numpy as np
from jax import lax
from jax.experimental import pallas as pl
from jax.experimental.pallas import tpu as pltpu

F32 = jnp.float32
BF16 = jnp.bfloat16

D_MODEL = 1024
GRID_W = 64
N_HEADS = 4
QK_DIM = 64
V_DIM = 128
ATTN_WIDTH = N_HEADS * V_DIM
POOL_WIDTH = D_MODEL - ATTN_WIDTH
POOL_WINDOWS = (2, 4, 8, 16)
POOL_GROUP = POOL_WIDTH // len(POOL_WINDOWS)
AXIS_DIM = QK_DIM // 2
ROPE_THETA = 10000.0
N_EXPERTS = 64
TOP_K = 8
N_GROUPS = 8
GROUP_SIZE = N_EXPERTS // N_GROUPS
TOPK_GROUPS = 4
EXPERT_DIM = 256
ROUTED_SCALE = 2.5
EPS = 1e-6
LAM_INIT = 0.8 - 0.6 * math.exp(-0.3 * 0)

LANES = 128
SLOT_TILE = 256
HALO = 16
VMEM_LIMIT = 56 * 1024 * 1024


def _cparams(sem, vmem=VMEM_LIMIT):
    return pltpu.CompilerParams(dimension_semantics=sem, vmem_limit_bytes=vmem)


def _split(a):
    hi = a.astype(BF16)
    lo = (a - hi.astype(F32)).astype(BF16)
    return hi, lo


def _dot(a, b):
    return jnp.dot(a, b, preferred_element_type=F32)


def _dot_nt(a, b):
    return lax.dot_general(a, b, (((1,), (1,)), ((), ())), preferred_element_type=F32)


def _dot3(a, b, nt=False):
    d = _dot_nt if nt else _dot
    ah, al = _split(a)
    bh, bl = _split(b)
    return d(ah, bh) + d(ah, bl) + d(al, bh)


def _rms(x, g):
    return x * lax.rsqrt(jnp.mean(x * x, axis=-1, keepdims=True) + EPS) * g


def _silu(x):
    return x * (1.0 / (1.0 + jnp.exp(-x)))


def _ada_kernel(c_ref, w_ref, b_ref, o_ref):
    o_ref[...] = _dot3(_silu(c_ref[...]), w_ref[...]) + b_ref[...]


def _ada(cin, w_ada, b_ada):
    rows, d = cin.shape
    n = w_ada.shape[1]
    tn = 1024
    return pl.pallas_call(
        _ada_kernel,
        grid=(n // tn,),
        in_specs=[pl.BlockSpec((rows, d), lambda j: (0, 0)),
                  pl.BlockSpec((d, tn), lambda j: (0, j)),
                  pl.BlockSpec((1, tn), lambda j: (0, j))],
        out_specs=pl.BlockSpec((rows, tn), lambda j: (0, j)),
        out_shape=jax.ShapeDtypeStruct((rows, n), F32),
        compiler_params=_cparams(("arbitrary",)),
        name="ada",
    )(cin, w_ada, b_ada.reshape(1, n))


def _rope(t, cos, sa, sb):
    return t * cos + pltpu.roll(t, LANES - AXIS_DIM // 2, 1) * sa + pltpu.roll(t, AXIS_DIM // 2, 1) * sb


def _inproj_kernel(x_ref, mod_ref, g_ref, w_ref, cq_ref, ck_ref, sa_ref, sb_ref,
                   q_ref, k_ref, vt_ref, p_ref):
    x = x_ref[0]
    h = _rms(x, g_ref[...]) * (1.0 + mod_ref[0, 1:2, :]) + mod_ref[0, 0:1, :]
    px = _dot(h.astype(BF16), w_ref[...])
    ck = ck_ref[...]
    sa = sa_ref[...]
    sb = sb_ref[...]
    cq = cq_ref[...]
    scale = QK_DIM ** -0.5 * LOG2E
    saq = sa * scale
    sbq = sb * scale
    for hh in range(N_HEADS):
        lo = hh * LANES
        q = px[:, lo:lo + LANES]
        k = px[:, ATTN_WIDTH + lo:ATTN_WIDTH + lo + LANES]
        q_ref[0, :, lo:lo + LANES] = _rope(q, cq, saq, sbq).astype(BF16)
        k_ref[0, :, lo:lo + LANES] = _rope(k, ck, sa, sb).astype(BF16)
    vt_ref[0] = jnp.transpose(px[:, 2 * ATTN_WIDTH:3 * ATTN_WIDTH]).astype(BF16)
    p_ref[0] = px[:, 3 * ATTN_WIDTH:].astype(BF16)


def _inproj(x, mod3, g, w_in_bf, cq, ck, sa, sb, tm=512):
    b, n, d = x.shape
    width = w_in_bf.shape[1]
    row = lambda bi, i: (bi, i, 0)
    tab = pl.BlockSpec((tm, LANES), lambda bi, i: (i, 0))
    out = jax.ShapeDtypeStruct((b, n, ATTN_WIDTH), BF16)
    return pl.pallas_call(
        _inproj_kernel,
        grid=(b, n // tm),
        in_specs=[pl.BlockSpec((1, tm, d), row),
                  pl.BlockSpec((1, 6, d), lambda bi, i: (bi, 0, 0)),
                  pl.BlockSpec((1, d), lambda bi, i: (0, 0)),
                  pl.BlockSpec((d, width), lambda bi, i: (0, 0)),
                  tab, tab, tab, tab],
        out_specs=[pl.BlockSpec((1, tm, ATTN_WIDTH), row),
                   pl.BlockSpec((1, tm, ATTN_WIDTH), row),
                   pl.BlockSpec((1, ATTN_WIDTH, tm), lambda bi, i: (bi, 0, i)),
                   pl.BlockSpec((1, tm, ATTN_WIDTH), row)],
        out_shape=[out, out, jax.ShapeDtypeStruct((b, ATTN_WIDTH, n), BF16), out],
        compiler_params=_cparams(("arbitrary", "arbitrary")),
        name="inproj",
    )(x, mod3, g, w_in_bf, cq, ck, sa, sb)


def _ctxkv_kernel(x_ref, mod_ref, g_ref, wk_ref, wv_ref, k_ref, vt_ref):
    h = _rms(x_ref[0], g_ref[...]) * (1.0 + mod_ref[0, 1:2, :]) + mod_ref[0, 0:1, :]
    hb = h.astype(BF16)
    k_ref[0] = _dot(hb, wk_ref[...]).astype(BF16)
    vt_ref[0] = jnp.transpose(_dot(hb, wv_ref[...])).astype(BF16)


def _ctxkv(ctx, mod3, g, w_in_bf, ctx_row):
    b, n, d = ctx.shape
    out = jax.ShapeDtypeStruct((b, n, ATTN_WIDTH), BF16)
    return pl.pallas_call(
        _ctxkv_kernel,
        grid=(b,),
        in_specs=[pl.BlockSpec((1, n, d), lambda bi: (bi, 0, 0)),
                  pl.BlockSpec((1, 6, d), lambda bi: (ctx_row, 0, 0)),
                  pl.BlockSpec((1, d), lambda bi: (0, 0)),
                  pl.BlockSpec((d, ATTN_WIDTH), lambda bi: (0, 1)),
                  pl.BlockSpec((d, ATTN_WIDTH), lambda bi: (0, 2))],
        out_specs=[pl.BlockSpec((1, n, ATTN_WIDTH), lambda bi: (bi, 0, 0)),
                   pl.BlockSpec((1, ATTN_WIDTH, n), lambda bi: (bi, 0, 0))],
        out_shape=[out, jax.ShapeDtypeStruct((b, ATTN_WIDTH, n), BF16)],
        compiler_params=_cparams(("arbitrary",)),
        name="ctxkv",
    )(ctx, mod3, g, w_in_bf, w_in_bf)


LOG2E = math.log2(math.e)
KEY_CHUNK = 256


def _attn_step(q_ref, kx_ref, kc_ref, vt_ref, vct_ref, lam_ref, g_ref, o_ref, s_new, m_new, s_old, m_old):
    nc = kc_ref.shape[1]
    tq = q_ref.shape[1]
    nk = s_old.shape[1]
    groups = KEY_CHUNK // 8
    q = q_ref[0]
    lane = lax.broadcasted_iota(jnp.int32, q.shape, 1)
    zero = jnp.zeros_like(q)
    qms = (jnp.where(lane < QK_DIM, q, zero), jnp.where(lane >= QK_DIM, q, zero))
    m_prev = (m_old[0], m_old[1])
    top = [jnp.full((8, tq), -jnp.inf, F32) for _ in range(2)]
    den = [jnp.zeros((8, tq), F32) for _ in range(2)]
    acc = [jnp.zeros((V_DIM, tq), F32) for _ in range(2)]
    for c in range(nk // KEY_CHUNK):
        lo = c * KEY_CHUNK
        if lo < nc:
            keys, vt = kc_ref[0, lo:lo + KEY_CHUNK, :], vct_ref[0, :, lo:lo + KEY_CHUNK]
        else:
            keys, vt = kx_ref[0, lo - nc:lo - nc + KEY_CHUNK, :], vt_ref[0, :, lo - nc:lo - nc + KEY_CHUNK]
        for mp in range(2):
            s = _dot_nt(keys, qms[mp])
            s_new[mp, lo:lo + KEY_CHUNK, :] = s
            top[mp] = jnp.maximum(top[mp], jnp.max(s.reshape(groups, 8, tq), axis=0))
            p = jnp.exp2(s_old[mp, lo:lo + KEY_CHUNK, :] - m_prev[mp])
            den[mp] = den[mp] + jnp.sum(p.reshape(groups, 8, tq), axis=0)
            acc[mp] = acc[mp] + _dot(vt, p.astype(BF16))
    for mp in range(2):
        m_new[mp] = jnp.max(top[mp], axis=0, keepdims=True)
    lv = lam_ref[...]
    lam = (jnp.exp(jnp.sum(lv[0:1] * lv[1:2], axis=-1, keepdims=True))
           - jnp.exp(jnp.sum(lv[2:3] * lv[3:4], axis=-1, keepdims=True)) + LAM_INIT)
    l1 = jnp.sum(den[0], axis=0, keepdims=True)
    l2 = jnp.sum(den[1], axis=0, keepdims=True)
    o = acc[0] * (1.0 / l1) - acc[1] * (lam / l2)
    o = o * lax.rsqrt(jnp.mean(o * o, axis=0, keepdims=True) + EPS) * (g_ref[...] * (1.0 - LAM_INIT))
    o_ref[0] = jnp.transpose(o).astype(BF16)


def _attn_kernel(q_ref, kx_ref, kc_ref, vt_ref, vct_ref, lam_ref, g_ref, o_ref, s0_ref, m0_ref, s1_ref, m1_ref):
    t = pl.program_id(0)
    io = (q_ref, kx_ref, kc_ref, vt_ref, vct_ref, lam_ref, g_ref, o_ref)

    @pl.when(t == 0)
    def _():
        s1_ref[...] = jnp.zeros_like(s1_ref)
        m1_ref[...] = jnp.zeros_like(m1_ref)

    @pl.when(lax.rem(t, 2) == 0)
    def _():
        _attn_step(*io, s0_ref, m0_ref, s1_ref, m1_ref)

    @pl.when(lax.rem(t, 2) == 1)
    def _():
        _attn_step(*io, s1_ref, m1_ref, s0_ref, m0_ref)


def _attn(q, k, kc, vt, vct, lamv, g_col, tq=512):
    b, n, _ = q.shape
    nc = kc.shape[1]
    nq = n // tq
    total = b * N_HEADS * nq

    def cur(t):
        ta = jnp.minimum(t, total - 1)
        bh = ta // nq
        return bh // N_HEADS, bh % N_HEADS, ta % nq

    def prv(t):
        tb = jnp.maximum(t - 1, 0)
        bh = tb // nq
        return bh // N_HEADS, bh % N_HEADS, tb % nq

    def at(fn, order):
        return lambda t: tuple((fn(t) + (0,))[j] for j in order)

    return pl.pallas_call(
        _attn_kernel,
        grid=(total + 1,),
        in_specs=[pl.BlockSpec((1, tq, LANES), at(cur, (0, 2, 1))),
                  pl.BlockSpec((1, n, LANES), at(cur, (0, 3, 1))),
                  pl.BlockSpec((1, nc, LANES), at(cur, (0, 3, 1))),
                  pl.BlockSpec((1, V_DIM, n), at(prv, (0, 1, 3))),
                  pl.BlockSpec((1, V_DIM, nc), at(prv, (0, 1, 3))),
                  pl.BlockSpec((4, QK_DIM), lambda t: (0, 0)),
                  pl.BlockSpec((V_DIM, 1), lambda t: (0, 0))],
        out_specs=pl.BlockSpec((1, tq, LANES), at(prv, (0, 2, 1))),
        out_shape=jax.ShapeDtypeStruct((b, n, ATTN_WIDTH), BF16),
        scratch_shapes=[pltpu.VMEM((2, nc + n, tq), F32), pltpu.VMEM((2, 1, tq), F32),
                        pltpu.VMEM((2, nc + n, tq), F32), pltpu.VMEM((2, 1, tq), F32)],
        compiler_params=_cparams(("arbitrary",)),
        name="attn",
    )(q, k, kc, vt, vct, lamv, g_col)


def _outproj_kernel(att_ref, p_ref, pprev_ref, pnext_ref, x_ref, mod_ref, wpool_ref, bpool_ref,
                    pscale_ref, wout_ref, gpost_ref, gpre_ref, wr_ref,
                    x1_ref, h2_ref, h2t_ref, lg_ref, *, tm, n):
    i = pl.program_id(1)
    p = p_ref[0].astype(F32)
    prev = jnp.where(i > 0, pprev_ref[0].astype(F32), 0.0)
    nxt = jnp.where(i < pl.num_programs(1) - 1, pnext_ref[0].astype(F32), 0.0)
    ext = jnp.concatenate([prev, p, nxt], axis=0)
    rows = tm + 2 * HALO
    trow = (i * tm + lax.broadcasted_iota(jnp.int32, (tm, 1), 0))

    pooled = []
    sums = {}
    acc = ext + pltpu.roll(ext, 1, 0)
    sums[2] = acc
    w = 2
    while w < POOL_WINDOWS[-1]:
        acc = pltpu.roll(acc, rows - w // 2, 0) + pltpu.roll(acc, w // 2, 0)
        w = 2 * w
        sums[w] = acc
    for gi, win in enumerate(POOL_WINDOWS):
        sl = slice(gi * POOL_GROUP, (gi + 1) * POOL_GROUP)
        wsum = sums[win][HALO:HALO + tm, sl]
        hi_c = jnp.minimum(trow + (win - win // 2), n)
        lo_c = jnp.maximum(trow - win // 2, 0)
        cnt = (hi_c - lo_c).astype(F32)
        d = wsum / cnt - p[:, sl]
        y = _dot(d.astype(BF16), wpool_ref[gi]) + bpool_ref[:, sl]
        pooled.append((y * pscale_ref[:, sl]).astype(BF16))
    pool = jnp.concatenate(pooled, axis=1)
    yx = _dot(att_ref[0], wout_ref[0:ATTN_WIDTH, :]) + _dot(pool, wout_ref[ATTN_WIDTH:, :])
    x1 = x_ref[0] + mod_ref[0, 2:3, :] * _rms(yx, gpost_ref[...])
    x1_ref[0] = x1
    h2 = _rms(x1, gpre_ref[...]) * (1.0 + mod_ref[0, 4:5, :]) + mod_ref[0, 3:4, :]
    h2_ref[0] = h2.astype(BF16)
    for k in range(CHUNKS):
        h2t_ref[0, pl.ds(k, tm, stride=CHUNKS), :] = h2[:, k * LANES:(k + 1) * LANES]
    lg_ref[...] = _dot3(wr_ref[...], h2, nt=True)


def _outproj(att, p, x, mod3, w_pool_bf, b_pool, pool_scale, w_out_bf, g_post, g_pre, w_router_t, tm=512):
    b, n, d = x.shape
    hb = tm // HALO
    nhb = n // HALO
    row = lambda bi, i: (bi, i, 0)
    vec = lambda bi, i: (0, 0)
    kern = functools.partial(_outproj_kernel, tm=tm, n=n)
    return pl.pallas_call(
        kern,
        grid=(b, n // tm),
        in_specs=[pl.BlockSpec((1, tm, ATTN_WIDTH), row),
                  pl.BlockSpec((1, tm, POOL_WIDTH), row),
                  pl.BlockSpec((1, HALO, POOL_WIDTH), lambda bi, i: (bi, jnp.maximum(i * hb - 1, 0), 0)),
                  pl.BlockSpec((1, HALO, POOL_WIDTH), lambda bi, i: (bi, jnp.minimum((i + 1) * hb, nhb - 1), 0)),
                  pl.BlockSpec((1, tm, d), row),
                  pl.BlockSpec((1, 6, d), lambda bi, i: (bi, 0, 0)),
                  pl.BlockSpec((len(POOL_WINDOWS), POOL_GROUP, POOL_GROUP), lambda bi, i: (0, 0, 0)),
                  pl.BlockSpec((1, POOL_WIDTH), vec),
                  pl.BlockSpec((1, POOL_WIDTH), vec),
                  pl.BlockSpec((d, d), vec),
                  pl.BlockSpec((1, d), vec),
                  pl.BlockSpec((1, d), vec),
                  pl.BlockSpec((N_EXPERTS, d), vec)],
        out_specs=[pl.BlockSpec((1, tm, d), row),
                   pl.BlockSpec((1, tm, d), row),
                   pl.BlockSpec((1, tm * CHUNKS, LANES), row),
                   pl.BlockSpec((N_EXPERTS, tm), lambda bi, i: (0, bi * (n // tm) + i))],
        out_shape=[jax.ShapeDtypeStruct((b, n, d), F32),
                   jax.ShapeDtypeStruct((b, n, d), BF16),
                   jax.ShapeDtypeStruct((b, n * CHUNKS, LANES), F32),
                   jax.ShapeDtypeStruct((N_EXPERTS, b * n), F32)],
        compiler_params=_cparams(("arbitrary", "arbitrary")),
        name="outproj",
    )(att, p, p, p, x, mod3, w_pool_bf, b_pool, pool_scale, w_out_bf, g_post, g_pre, w_router_t)


def _beats(a, b, a_first):
    return jnp.where((a >= b) if a_first else (a > b), 1.0, 0.0)


def _route_kernel(lg_ref, bias_ref, w_ref):
    scores = [1.0 / (1.0 + jnp.exp(-lg_ref[e])) for e in range(N_EXPERTS)]
    biased = [scores[e] + bias_ref[e] for e in range(N_EXPERTS)]
    gscore = []
    for g in range(N_GROUPS):
        vals = biased[g * GROUP_SIZE:(g + 1) * GROUP_SIZE]
        m1 = vals[0]
        m2 = jnp.full_like(m1, -jnp.inf)
        for v in vals[1:]:
            m2 = jnp.maximum(m2, jnp.minimum(m1, v))
            m1 = jnp.maximum(m1, v)
        gscore.append(m1 + m2)
    gsel = []
    for g in range(N_GROUPS):
        rank = jnp.zeros_like(gscore[g])
        for g2 in range(N_GROUPS):
            if g2 != g:
                rank = rank + _beats(gscore[g2], gscore[g], g2 < g)
        gsel.append(rank < TOPK_GROUPS)
    masked = [jnp.where(gsel[e // GROUP_SIZE], biased[e], -jnp.inf) for e in range(N_EXPERTS)]
    picked = []
    for e in range(N_EXPERTS):
        rank = jnp.zeros_like(masked[e])
        for e2 in range(N_EXPERTS):
            if e2 != e:
                rank = rank + _beats(masked[e2], masked[e], e2 < e)
        picked.append(jnp.where(rank < TOP_K, scores[e], 0.0))
    denom = picked[0]
    for e in range(1, N_EXPERTS):
        denom = denom + picked[e]
    for e in range(N_EXPERTS):
        w_ref[e] = picked[e] / denom * ROUTED_SCALE


def _route(logits3, bias3, rows=8):
    e, r, l = logits3.shape
    return pl.pallas_call(
        _route_kernel,
        grid=(r // rows,),
        in_specs=[pl.BlockSpec((e, rows, l), lambda i: (0, i, 0)),
                  pl.BlockSpec((e, 1, l), lambda i: (0, 0, 0))],
        out_specs=pl.BlockSpec((e, rows, l), lambda i: (0, i, 0)),
        out_shape=jax.ShapeDtypeStruct((e, r, l), F32),
        compiler_params=_cparams(("arbitrary",)),
        name="route",
    )(logits3, bias3)


CHUNKS = D_MODEL // LANES
SLAB = SLOT_TILE + 4
ROW_BATCH = 16


def _gather_tile(row_ref, tile, xs_ref, tx_ref):
    for s in range(SLOT_TILE):
        row = pl.multiple_of(row_ref[0, tile, s], CHUNKS)
        tx_ref[pl.ds(s, CHUNKS, stride=SLAB), :] = xs_ref[pl.ds(row, CHUNKS), :]


def _experts_step(j, wslot, row_ref, sw_ref, wg_buf, wu_buf, wd_buf, xs_ref, acc_ref,
                  tx_cur, tx_nxt, ty_cur, ty_prv):
    ntile = row_ref.shape[1]
    _gather_tile(row_ref, jnp.minimum(j + 1, ntile - 1), xs_ref, tx_nxt)
    xb = jnp.concatenate([tx_cur[pl.ds(k * SLAB, SLOT_TILE), :] for k in range(CHUNKS)], axis=1).astype(BF16)
    g = _dot(xb, wg_buf[wslot])
    u = _dot(xb, wu_buf[wslot])
    h = (_silu(g) * u).astype(BF16)
    y = _dot(h, wd_buf[wslot])
    w_rows = jnp.transpose(jnp.broadcast_to(sw_ref[0, pl.ds(j, 1), :], (LANES, SLOT_TILE)))
    for k in range(CHUNKS):
        ty_cur[pl.ds(k * SLAB, SLOT_TILE), :] = y[:, k * LANES:(k + 1) * LANES] * w_rows
    prv = jnp.maximum(j - 1, 0)
    for s0 in range(0, SLOT_TILE, ROW_BATCH):
        new = []
        for s in range(s0, s0 + ROW_BATCH):
            row = pl.multiple_of(row_ref[0, prv, s], CHUNKS)
            new.append((row, acc_ref[pl.ds(row, CHUNKS), :] + ty_prv[pl.ds(s, CHUNKS, stride=SLAB), :]))
        for row, v in new:
            acc_ref[pl.ds(row, CHUNKS), :] = v


def _experts_kernel(be_ref, nt_ref, first_ref, nxt_ref, ws_ref, row_ref, sw_ref, x_hbm,
                    wg_hbm, wu_hbm, wd_hbm, o_hbm,
                    xs_ref, acc_ref, tx0_ref, tx1_ref, ty0_ref, ty1_ref, wg_buf, wu_buf, wd_buf,
                    sem, wsem, *, ntile, n):
    c = pl.program_id(0)
    j = pl.program_id(1)
    t = c * ntile + j
    rows = n * CHUNKS
    wslot = ws_ref[t]

    def weight_copies(e, slot):
        return [pltpu.make_async_copy(src.at[e], dst.at[slot], wsem.at[slot])
                for src, dst in ((wg_hbm, wg_buf), (wu_hbm, wu_buf), (wd_hbm, wd_buf))]

    @pl.when(j == 0)
    def _():
        for wcp in weight_copies(be_ref[t], 0):
            wcp.start()
        cp = pltpu.make_async_copy(x_hbm.at[c], xs_ref.at[pl.ds(0, rows)], sem.at[0])
        cp.start()
        xs_ref[pl.ds(rows, CHUNKS), :] = jnp.zeros((CHUNKS, LANES), F32)
        acc_ref[...] = jnp.zeros_like(acc_ref)
        ty1_ref[...] = jnp.zeros_like(ty1_ref)
        cp.wait()
        _gather_tile(row_ref, 0, xs_ref, tx0_ref)

    @pl.when(first_ref[t] == 1)
    def _():
        for wcp in weight_copies(be_ref[t], wslot):
            wcp.wait()

    @pl.when((first_ref[t] == 1) & (nxt_ref[t] >= 0))
    def _():
        for wcp in weight_copies(nxt_ref[t], 1 - wslot):
            wcp.start()

    live = j <= nt_ref[c]
    args = (j, wslot, row_ref, sw_ref, wg_buf, wu_buf, wd_buf, xs_ref, acc_ref)

    @pl.when(live & (lax.rem(j, 2) == 0))
    def _():
        _experts_step(*args, tx0_ref, tx1_ref, ty0_ref, ty1_ref)

    @pl.when(live & (lax.rem(j, 2) == 1))
    def _():
        _experts_step(*args, tx1_ref, tx0_ref, ty1_ref, ty0_ref)

    @pl.when(j == ntile - 1)
    def _():
        cp = pltpu.make_async_copy(acc_ref.at[pl.ds(0, rows)], o_hbm.at[c], sem.at[1])
        cp.start()
        cp.wait()


def _experts(block_e, ntiles, first, nxt_e, wslot, slot_row, slot_w, h2t, wg_bf, wu_bf, wd_bf, ntile, n):
    b = h2t.shape[0]
    d, f = wg_bf.shape[1], wg_bf.shape[2]
    batch = lambda c, j, *_: (c, 0, 0)
    kern = functools.partial(_experts_kernel, ntile=ntile, n=n)
    slab = pltpu.VMEM((CHUNKS * SLAB, LANES), F32)
    hbm = pl.BlockSpec(memory_space=pl.ANY)
    grid_spec = pltpu.PrefetchScalarGridSpec(
        num_scalar_prefetch=5,
        grid=(b, ntile),
        in_specs=[pl.BlockSpec((1, ntile, SLOT_TILE), batch, memory_space=pltpu.SMEM),
                  pl.BlockSpec((1, ntile, SLOT_TILE), batch),
                  hbm, hbm, hbm, hbm],
        out_specs=hbm,
        scratch_shapes=[pltpu.VMEM(((n + 1) * CHUNKS, LANES), F32),
                        pltpu.VMEM(((n + 1) * CHUNKS, LANES), F32),
                        slab, slab, slab, slab,
                        pltpu.VMEM((2, d, f), BF16), pltpu.VMEM((2, d, f), BF16), pltpu.VMEM((2, f, d), BF16),
                        pltpu.SemaphoreType.DMA((2,)), pltpu.SemaphoreType.DMA((2,))],
    )
    return pl.pallas_call(
        kern,
        grid_spec=grid_spec,
        out_shape=jax.ShapeDtypeStruct((b, n * CHUNKS, LANES), F32),
        compiler_params=_cparams(("arbitrary", "arbitrary")),
        name="experts",
    )(block_e, ntiles, first, nxt_e, wslot, slot_row, slot_w, h2t, wg_bf, wu_bf, wd_bf)


EXPERTS_PER_LANE_GROUP = LANES // 32
PLAN_GROUP = 4


def _plan_tables_kernel(w_ref, re_ref, ro_ref, tab_ref, *, rows):
    ne = w_ref.shape[0]
    w2 = w_ref[...].reshape(ne * rows, LANES)
    selb = jnp.where(w2 > 0.0, 1.0, 0.0).astype(BF16)
    i0 = lax.broadcasted_iota(jnp.int32, (LANES, LANES), 0)
    i1 = lax.broadcasted_iota(jnp.int32, (LANES, LANES), 1)
    rowtot = _dot(selb, jnp.ones((LANES, LANES), BF16))
    ridx = lax.broadcasted_iota(jnp.int32, rowtot.shape, 0) & (rows - 1)
    acc = rowtot
    sh = 1
    while sh < rows:
        acc = acc + jnp.where(ridx >= sh, pltpu.roll(acc, sh, 0), 0.0)
        sh *= 2
    re_ref[0] = acc
    ro_ref[0] = acc - rowtot
    incl_t = _dot_nt(jnp.where(i1 <= i0, 1.0, 0.0).astype(BF16), selb)
    eye = jnp.where(i0 == i1, 1.0, 0.0).astype(BF16)
    w_hi = w2.astype(BF16)
    r1 = w2 - w_hi.astype(F32)
    w_mid = r1.astype(BF16)
    w_lo = (r1 - w_mid.astype(F32)).astype(BF16)
    parts = [_dot_nt(eye, p) for p in (w_hi, w_mid, w_lo)]
    for g in range(ne * rows // LANES):
        sl = slice(g * LANES, (g + 1) * LANES)
        tab_ref[0, g, 0:LANES, :] = incl_t[:, sl].astype(BF16)
        for k in range(3):
            tab_ref[0, g, (k + 1) * LANES:(k + 2) * LANES, :] = parts[k][:, sl].astype(BF16)


def _plan_tables(selw3, b):
    ne, r_all, _ = selw3.shape
    rows = r_all // b
    groups = ne * rows // LANES
    kern = functools.partial(_plan_tables_kernel, rows=rows)
    return pl.pallas_call(
        kern,
        grid=(b,),
        in_specs=[pl.BlockSpec((ne, rows, LANES), lambda c: (0, c, 0))],
        out_specs=[pl.BlockSpec((1, ne * rows, LANES), lambda c: (c, 0, 0)),
                   pl.BlockSpec((1, ne * rows, LANES), lambda c: (c, 0, 0)),
                   pl.BlockSpec((1, groups, 4 * LANES, LANES), lambda c: (c, 0, 0, 0))],
        out_shape=[jax.ShapeDtypeStruct((b, ne * rows, LANES), F32),
                   jax.ShapeDtypeStruct((b, ne * rows, LANES), F32),
                   jax.ShapeDtypeStruct((b, groups, 4 * LANES, LANES), BF16)],
        compiler_params=_cparams(("arbitrary",)),
        name="plan_tables",
    )(selw3)


def _plan_slots_kernel(be_ref, q0_ref, nt_ref, re_ref, ro_ref, tab_ref, tok_ref, sw_ref,
                       *, ntile, rows, n):
    c = pl.program_id(0)
    tok_ref[...] = jnp.full(tok_ref.shape, n * CHUNKS, jnp.int32)
    sw_ref[...] = jnp.zeros(sw_ref.shape, F32)
    lane = lax.broadcasted_iota(jnp.int32, (1, SLOT_TILE), 1).astype(F32)
    sub_r = lax.broadcasted_iota(jnp.int32, (rows, SLOT_TILE), 0).astype(F32)
    sub_l = lax.broadcasted_iota(jnp.int32, (LANES, SLOT_TILE), 0).astype(F32)
    reps = SLOT_TILE // LANES

    def tile_body(j):
        e = be_ref[c * ntile + j]
        q = q0_ref[c * ntile + j].astype(F32) + lane
        base = pl.multiple_of(e * rows, rows)
        row_end = jnp.concatenate([re_ref[0, pl.ds(base, rows), :]] * reps, axis=1)
        row_off = jnp.concatenate([ro_ref[0, pl.ds(base, rows), :]] * reps, axis=1)
        r = jnp.sum(jnp.where(row_end <= q, 1.0, 0.0), axis=0, keepdims=True)
        row_start = jnp.sum(jnp.where(sub_r == r, row_off, 0.0), axis=0, keepdims=True)
        g = lax.shift_right_logical(e, EXPERTS_PER_LANE_GROUP.bit_length() - 1)
        col = r + ((e & (EXPERTS_PER_LANE_GROUP - 1)) * rows).astype(F32)
        onehot = jnp.where(sub_l == col, 1.0, 0.0).astype(BF16)
        picked = _dot(tab_ref[0, g], onehot)
        incl = picked[0:LANES]
        lane_idx = jnp.sum(jnp.where(incl <= q - row_start, 1.0, 0.0), axis=0, keepdims=True)
        wrow = (picked[LANES:2 * LANES] + picked[2 * LANES:3 * LANES]) + picked[3 * LANES:]
        wsel = jnp.sum(jnp.where(sub_l == lane_idx, wrow, 0.0), axis=0, keepdims=True)
        valid = (q < row_end[rows - 1:rows, :]) & (j < nt_ref[c])
        tok = jnp.where(valid, r * LANES + lane_idx, float(n)).astype(jnp.int32)
        tok_ref[0, pl.ds(j, 1), :] = tok * CHUNKS
        sw_ref[0, pl.ds(j, 1), :] = jnp.where(valid, wsel, 0.0)

    def tile_group(i, carry):
        for k in range(PLAN_GROUP):
            tile_body(PLAN_GROUP * i + k)
        return carry

    lax.fori_loop(0, (nt_ref[c] + PLAN_GROUP - 1) // PLAN_GROUP, tile_group, 0)


def _plan_slots(block_e, tile_q0, ntiles, row_end, row_off, tables, ntile, n):
    b, er, _ = row_end.shape
    rows = n // LANES
    groups = tables.shape[1]
    kern = functools.partial(_plan_slots_kernel, ntile=ntile, rows=rows, n=n)
    grid_spec = pltpu.PrefetchScalarGridSpec(
        num_scalar_prefetch=3,
        grid=(b,),
        in_specs=[pl.BlockSpec((1, er, LANES), lambda c, *_: (c, 0, 0)),
                  pl.BlockSpec((1, er, LANES), lambda c, *_: (c, 0, 0)),
                  pl.BlockSpec((1, groups, 4 * LANES, LANES), lambda c, *_: (c, 0, 0, 0))],
        out_specs=[pl.BlockSpec((1, ntile, SLOT_TILE), lambda c, *_: (c, 0, 0)),
                   pl.BlockSpec((1, ntile, SLOT_TILE), lambda c, *_: (c, 0, 0))],
    )
    return pl.pallas_call(
        kern,
        grid_spec=grid_spec,
        out_shape=[jax.ShapeDtypeStruct((b, ntile, SLOT_TILE), jnp.int32),
                   jax.ShapeDtypeStruct((b, ntile, SLOT_TILE), F32)],
        compiler_params=_cparams(("arbitrary",)),
        name="plan_slots",
    )(block_e, tile_q0, ntiles, row_end, row_off, tables)


def _tile_meta(row_end, b, n, ntile):
    rows = n // LANES
    cnt = row_end.reshape(b, N_EXPERTS, rows, LANES)[:, :, rows - 1, 0].astype(jnp.int32)
    nt_e = (cnt + SLOT_TILE - 1) // SLOT_TILE
    tend = jnp.cumsum(nt_e, axis=1)
    tstart = tend - nt_e
    ntiles = tend[:, -1]
    tile_id = jnp.arange(ntile, dtype=jnp.int32)
    tid = jnp.minimum(tile_id[None, :], ntiles[:, None] - 1)
    be = jnp.minimum(jnp.sum(tend[:, None, :] <= tid[:, :, None], axis=2).astype(jnp.int32), N_EXPERTS - 1)
    q0 = (tid - jnp.take_along_axis(tstart, be, axis=1)) * SLOT_TILE
    valid = tile_id[None, :] < ntiles[:, None]
    prev_be = jnp.concatenate([jnp.full((b, 1), -1, jnp.int32), be[:, :-1]], axis=1)
    first = (valid & (be != prev_be)).astype(jnp.int32)
    wslot = (jnp.cumsum(first, axis=1) - 1) % 2
    eid = jnp.arange(N_EXPERTS, dtype=jnp.int32)
    later = (nt_e > 0)[:, None, :] & (eid[None, None, :] > eid[None, :, None])
    nxt = jnp.min(jnp.where(later, eid[None, None, :], N_EXPERTS), axis=2)
    nxt = jnp.where(nxt < N_EXPERTS, nxt, -1)
    nxt_e = jnp.take_along_axis(nxt, be, axis=1)
    flat = lambda a: a.reshape(-1).astype(jnp.int32)
    return flat(be), flat(q0), ntiles.astype(jnp.int32), flat(first), flat(nxt_e), flat(wslot)


def _dispatch_plan(selw3, b, n, ntile):
    row_end, row_off, tables = _plan_tables(selw3, b)
    block_e, tile_q0, ntiles, first, nxt_e, wslot = _tile_meta(row_end, b, n, ntile)
    slot_tok, slot_w = _plan_slots(block_e, tile_q0, ntiles, row_end, row_off, tables, ntile, n)
    return block_e, ntiles, first, nxt_e, wslot, slot_tok, slot_w


def _final_kernel(h2_ref, r_ref, x1_ref, mod_ref, wg_ref, wu_ref, wd_ref, g_ref, o_ref):
    hb = h2_ref[0]
    tm = hb.shape[0]
    sh = _dot((_silu(_dot(hb, wg_ref[...])) * _dot(hb, wu_ref[...])).astype(BF16), wd_ref[...])
    routed = jnp.concatenate([r_ref[0, pl.ds(k, tm, stride=CHUNKS), :] for k in range(CHUNKS)], axis=1)
    moe = routed + sh
    o_ref[0] = x1_ref[0] + mod_ref[0, 5:6, :] * _rms(moe, g_ref[...])


def _final(h2, routed, x1, mod3, wsg_bf, wsu_bf, wsd_bf, g_post, tm=512):
    b, n, d = x1.shape
    f = wsg_bf.shape[1]
    row = lambda bi, i: (bi, i, 0)
    vec = lambda bi, i: (0, 0)
    return pl.pallas_call(
        _final_kernel,
        grid=(b, n // tm),
        in_specs=[pl.BlockSpec((1, tm, d), row),
                  pl.BlockSpec((1, tm * CHUNKS, LANES), row),
                  pl.BlockSpec((1, tm, d), row),
                  pl.BlockSpec((1, 6, d), lambda bi, i: (bi, 0, 0)),
                  pl.BlockSpec((d, f), vec),
                  pl.BlockSpec((d, f), vec),
                  pl.BlockSpec((f, d), vec),
                  pl.BlockSpec((1, d), vec)],
        out_specs=pl.BlockSpec((1, tm, d), row),
        out_shape=jax.ShapeDtypeStruct((b, n, d), F32),
        compiler_params=_cparams(("arbitrary", "arbitrary")),
        name="final",
    )(h2, routed, x1, mod3, wsg_bf, wsu_bf, wsd_bf, g_post)


def _rope_tables(n):
    t = np.arange(n)
    row = (t // GRID_W).astype(np.float32)
    col = (t % GRID_W).astype(np.float32)
    freqs = np.float32(ROPE_THETA) ** (-np.arange(0, AXIS_DIM, 2, dtype=np.float32) / np.float32(AXIS_DIM))
    ar = row[:, None] * freqs
    ac = col[:, None] * freqs
    zeros = np.zeros_like(ar)
    cos64 = np.concatenate([np.cos(ar), np.cos(ar), np.cos(ac), np.cos(ac)], axis=1)
    sa64 = np.concatenate([-np.sin(ar), zeros, -np.sin(ac), zeros], axis=1)
    sb64 = np.concatenate([zeros, np.sin(ar), zeros, np.sin(ac)], axis=1)
    two = lambda a: jnp.asarray(np.concatenate([a, a], axis=1).astype(np.float32))
    return two(cos64), two(sa64), two(sb64)


def kernel(x, c, ctx, c_ctx, w_ada, b_ada, g_pre_mix, g_post_mix, g_pre_ffn, g_post_ffn, w_in, lambda_q1, lambda_k1, lambda_q2, lambda_k2, g_subln, w_pool, b_pool, pool_scale, w_out, w_router, router_bias, w_e_gate, w_e_up, w_e_down, w_sh_gate, w_sh_up, w_sh_down):
    b, n, d = x.shape
    l = 0
    mod_rows = 8
    cin = jnp.concatenate([c, c_ctx[None, :], jnp.zeros((mod_rows - b - 1, d), F32)], axis=0)
    mod3 = _ada(cin, w_ada[l], b_ada[l]).reshape(mod_rows, 6, d)

    w_in_bf = w_in[l].astype(BF16)
    cos, sa, sb = _rope_tables(n)
    q, k, vt, p = _inproj(x, mod3, g_pre_mix[l][None, :], w_in_bf, cos * (QK_DIM ** -0.5 * LOG2E), cos, sa, sb)
    kc, vct = _ctxkv(ctx, mod3, g_pre_mix[l][None, :], w_in_bf, b)

    lamv = jnp.stack([lambda_q1[l], lambda_k1[l], lambda_q2[l], lambda_k2[l]], axis=0)
    att = _attn(q, k, kc, vt, vct, lamv, g_subln[l][:, None])

    x1, h2, h2t, logits_t = _outproj(
        att, p, x, mod3, w_pool[l].astype(BF16), b_pool[l][None, :], pool_scale[l][None, :],
        w_out[l].astype(BF16), g_post_mix[l][None, :], g_pre_ffn[l][None, :], w_router[l].T)

    t = b * n
    selw3 = _route(logits_t.reshape(N_EXPERTS, t // LANES, LANES),
                   jnp.broadcast_to(router_bias[l][:, None, None], (N_EXPERTS, 1, LANES)))

    ntile = (n * TOP_K) // SLOT_TILE + N_EXPERTS
    block_e, ntiles, first, nxt_e, wslot, slot_row, slot_w = _dispatch_plan(selw3, b, n, ntile)
    routed = _experts(block_e, ntiles, first, nxt_e, wslot, slot_row, slot_w, h2t,
                      w_e_gate[l].astype(BF16), w_e_up[l].astype(BF16), w_e_down[l].astype(BF16), ntile, n)

    return _final(h2, routed, x1, mod3, w_sh_gate[l].astype(BF16), w_sh_up[l].astype(BF16),
                  w_sh_down[l].astype(BF16), g_post_ffn[l][None, :])
```

```python
import functools
import math

import jax
import jax.numpy as jnp
import numpy as np
from jax import lax
from jax.experimental import pallas as pl
from jax.experimental.pallas import tpu as pltpu

F32 = jnp.float32
BF16 = jnp.bfloat16

D_MODEL = 1024
GRID_W = 64
N_HEADS = 4
QK_DIM = 64
V_DIM = 128
ATTN_WIDTH = N_HEADS * V_DIM
POOL_WIDTH = D_MODEL - ATTN_WIDTH
POOL_WINDOWS = (2, 4, 8, 16)
POOL_GROUP = POOL_WIDTH // len(POOL_WINDOWS)
AXIS_DIM = QK_DIM // 2
ROPE_THETA = 10000.0
N_EXPERTS = 64
TOP_K = 8
N_GROUPS = 8
GROUP_SIZE = N_EXPERTS // N_GROUPS
TOPK_GROUPS = 4
EXPERT_DIM = 256
ROUTED_SCALE = 2.5
EPS = 1e-6
LAM_INIT = 0.8 - 0.6 * math.exp(-0.3 * 0)

LANES = 128
SLOT_TILE = 256
HALO = 16
VMEM_LIMIT = 56 * 1024 * 1024


def _cparams(sem, vmem=VMEM_LIMIT):
    return pltpu.CompilerParams(dimension_semantics=sem, vmem_limit_bytes=vmem)


def _split(a):
    hi = a.astype(BF16)
    lo = (a - hi.astype(F32)).astype(BF16)
    return hi, lo


def _dot(a, b):
    return jnp.dot(a, b, preferred_element_type=F32)


def _dot_nt(a, b):
    return lax.dot_general(a, b, (((1,), (1,)), ((), ())), preferred_element_type=F32)


def _dot3(a, b, nt=False):
    d = _dot_nt if nt else _dot
    ah, al = _split(a)
    bh, bl = _split(b)
    return d(ah, bh) + d(ah, bl) + d(al, bh)


def _rms(x, g):
    return x * lax.rsqrt(jnp.mean(x * x, axis=-1, keepdims=True) + EPS) * g


def _silu(x):
    return x * (1.0 / (1.0 + jnp.exp(-x)))


def _ada_kernel(c_ref, w_ref, b_ref, o_ref):
    o_ref[...] = _dot3(_silu(c_ref[...]), w_ref[...]) + b_ref[...]


def _ada(cin, w_ada, b_ada):
    rows, d = cin.shape
    n = w_ada.shape[1]
    tn = 1024
    return pl.pallas_call(
        _ada_kernel,
        grid=(n // tn,),
        in_specs=[pl.BlockSpec((rows, d), lambda j: (0, 0)),
                  pl.BlockSpec((d, tn), lambda j: (0, j)),
                  pl.BlockSpec((1, tn), lambda j: (0, j))],
        out_specs=pl.BlockSpec((rows, tn), lambda j: (0, j)),
        out_shape=jax.ShapeDtypeStruct((rows, n), F32),
        compiler_params=_cparams(("arbitrary",)),
        name="ada",
    )(cin, w_ada, b_ada.reshape(1, n))


def _rope(t, cos, sa, sb):
    return t * cos + pltpu.roll(t, LANES - AXIS_DIM // 2, 1) * sa + pltpu.roll(t, AXIS_DIM // 2, 1) * sb


def _inproj_kernel(x_ref, mod_ref, g_ref, w_ref, cq_ref, ck_ref, sa_ref, sb_ref,
                   q_ref, k_ref, vt_ref, p_ref):
    x = x_ref[0]
    h = _rms(x, g_ref[...]) * (1.0 + mod_ref[0, 1:2, :]) + mod_ref[0, 0:1, :]
    px = _dot(h.astype(BF16), w_ref[...])
    ck = ck_ref[...]
    sa = sa_ref[...]
    sb = sb_ref[...]
    cq = cq_ref[...]
    scale = QK_DIM ** -0.5 * LOG2E
    saq = sa * scale
    sbq = sb * scale
    for hh in range(N_HEADS):
        lo = hh * LANES
        q = px[:, lo:lo + LANES]
        k = px[:, ATTN_WIDTH + lo:ATTN_WIDTH + lo + LANES]
        q_ref[0, :, lo:lo + LANES] = _rope(q, cq, saq, sbq).astype(BF16)
        k_ref[0, :, lo:lo + LANES] = _rope(k, ck, sa, sb).astype(BF16)
    vt_ref[0] = jnp.transpose(px[:, 2 * ATTN_WIDTH:3 * ATTN_WIDTH]).astype(BF16)
    p_ref[0] = px[:, 3 * ATTN_WIDTH:].astype(BF16)


def _inproj(x, mod3, g, w_in_bf, cq, ck, sa, sb, tm=1024):
    b, n, d = x.shape
    width = w_in_bf.shape[1]
    row = lambda bi, i: (bi, i, 0)
    tab = pl.BlockSpec((tm, LANES), lambda bi, i: (i, 0))
    out = jax.ShapeDtypeStruct((b, n, ATTN_WIDTH), BF16)
    return pl.pallas_call(
        _inproj_kernel,
        grid=(b, n // tm),
        in_specs=[pl.BlockSpec((1, tm, d), row),
                  pl.BlockSpec((1, 6, d), lambda bi, i: (bi, 0, 0)),
                  pl.BlockSpec((1, d), lambda bi, i: (0, 0)),
                  pl.BlockSpec((d, width), lambda bi, i: (0, 0)),
                  tab, tab, tab, tab],
        out_specs=[pl.BlockSpec((1, tm, ATTN_WIDTH), row),
                   pl.BlockSpec((1, tm, ATTN_WIDTH), row),
                   pl.BlockSpec((1, ATTN_WIDTH, tm), lambda bi, i: (bi, 0, i)),
                   pl.BlockSpec((1, tm, ATTN_WIDTH), row)],
        out_shape=[out, out, jax.ShapeDtypeStruct((b, ATTN_WIDTH, n), BF16), out],
        compiler_params=_cparams(("arbitrary", "arbitrary")),
        name="inproj",
    )(x, mod3, g, w_in_bf, cq, ck, sa, sb)


def _ctxkv_kernel(x_ref, mod_ref, g_ref, wk_ref, wv_ref, k_ref, vt_ref):
    h = _rms(x_ref[0], g_ref[...]) * (1.0 + mod_ref[0, 1:2, :]) + mod_ref[0, 0:1, :]
    hb = h.astype(BF16)
    k_ref[0] = _dot(hb, wk_ref[...]).astype(BF16)
    vt_ref[0] = jnp.transpose(_dot(hb, wv_ref[...])).astype(BF16)


def _ctxkv(ctx, mod3, g, w_in_bf, ctx_row):
    b, n, d = ctx.shape
    out = jax.ShapeDtypeStruct((b, n, ATTN_WIDTH), BF16)
    return pl.pallas_call(
        _ctxkv_kernel,
        grid=(b,),
        in_specs=[pl.BlockSpec((1, n, d), lambda bi: (bi, 0, 0)),
                  pl.BlockSpec((1, 6, d), lambda bi: (ctx_row, 0, 0)),
                  pl.BlockSpec((1, d), lambda bi: (0, 0)),
                  pl.BlockSpec((d, ATTN_WIDTH), lambda bi: (0, 1)),
                  pl.BlockSpec((d, ATTN_WIDTH), lambda bi: (0, 2))],
        out_specs=[pl.BlockSpec((1, n, ATTN_WIDTH), lambda bi: (bi, 0, 0)),
                   pl.BlockSpec((1, ATTN_WIDTH, n), lambda bi: (bi, 0, 0))],
        out_shape=[out, jax.ShapeDtypeStruct((b, ATTN_WIDTH, n), BF16)],
        compiler_params=_cparams(("arbitrary",)),
        name="ctxkv",
    )(ctx, mod3, g, w_in_bf, w_in_bf)


LOG2E = math.log2(math.e)
KEY_CHUNK = 256


def _attn_step(q_ref, kx_ref, kc_ref, vt_ref, vct_ref, lam_ref, g_ref, o_ref, s_new, m_new, s_old, m_old):
    nc = kc_ref.shape[1]
    tq = q_ref.shape[1]
    nk = s_old.shape[1]
    groups = KEY_CHUNK // 8
    q = q_ref[0]
    lane = lax.broadcasted_iota(jnp.int32, q.shape, 1)
    zero = jnp.zeros_like(q)
    qms = (jnp.where(lane < QK_DIM, q, zero), jnp.where(lane >= QK_DIM, q, zero))
    m_prev = (m_old[0], m_old[1])
    top = [jnp.full((8, tq), -jnp.inf, F32) for _ in range(2)]
    den = [jnp.zeros((8, tq), F32) for _ in range(2)]
    acc = [jnp.zeros((V_DIM, tq), F32) for _ in range(2)]
    for c in range(nk // KEY_CHUNK):
        lo = c * KEY_CHUNK
        if lo < nc:
            keys, vt = kc_ref[0, lo:lo + KEY_CHUNK, :], vct_ref[0, :, lo:lo + KEY_CHUNK]
        else:
            keys, vt = kx_ref[0, lo - nc:lo - nc + KEY_CHUNK, :], vt_ref[0, :, lo - nc:lo - nc + KEY_CHUNK]
        for mp in range(2):
            s = _dot_nt(keys, qms[mp])
            s_new[mp, lo:lo + KEY_CHUNK, :] = s
            top[mp] = jnp.maximum(top[mp], jnp.max(s.reshape(groups, 8, tq), axis=0))
            p = jnp.exp2(s_old[mp, lo:lo + KEY_CHUNK, :] - m_prev[mp])
            den[mp] = den[mp] + jnp.sum(p.reshape(groups, 8, tq), axis=0)
            acc[mp] = acc[mp] + _dot(vt, p.astype(BF16))
    for mp in range(2):
        m_new[mp] = jnp.max(top[mp], axis=0, keepdims=True)
    lv = lam_ref[...]
    lam = (jnp.exp(jnp.sum(lv[0:1] * lv[1:2], axis=-1, keepdims=True))
           - jnp.exp(jnp.sum(lv[2:3] * lv[3:4], axis=-1, keepdims=True)) + LAM_INIT)
    l1 = jnp.sum(den[0], axis=0, keepdims=True)
    l2 = jnp.sum(den[1], axis=0, keepdims=True)
    o = acc[0] * (1.0 / l1) - acc[1] * (lam / l2)
    o = o * lax.rsqrt(jnp.mean(o * o, axis=0, keepdims=True) + EPS) * (g_ref[...] * (1.0 - LAM_INIT))
    o_ref[0] = jnp.transpose(o).astype(BF16)


def _attn_kernel(q_ref, kx_ref, kc_ref, vt_ref, vct_ref, lam_ref, g_ref, o_ref, s0_ref, m0_ref, s1_ref, m1_ref):
    t = pl.program_id(0)
    io = (q_ref, kx_ref, kc_ref, vt_ref, vct_ref, lam_ref, g_ref, o_ref)

    @pl.when(t == 0)
    def _():
        s1_ref[...] = jnp.zeros_like(s1_ref)
        m1_ref[...] = jnp.zeros_like(m1_ref)

    @pl.when(lax.rem(t, 2) == 0)
    def _():
        _attn_step(*io, s0_ref, m0_ref, s1_ref, m1_ref)

    @pl.when(lax.rem(t, 2) == 1)
    def _():
        _attn_step(*io, s1_ref, m1_ref, s0_ref, m0_ref)


def _attn(q, k, kc, vt, vct, lamv, g_col, tq=512):
    b, n, _ = q.shape
    nc = kc.shape[1]
    nq = n // tq
    total = b * N_HEADS * nq

    def cur(t):
        ta = jnp.minimum(t, total - 1)
        bh = ta // nq
        return bh // N_HEADS, bh % N_HEADS, ta % nq

    def prv(t):
        tb = jnp.maximum(t - 1, 0)
        bh = tb // nq
        return bh // N_HEADS, bh % N_HEADS, tb % nq

    def at(fn, order):
        return lambda t: tuple((fn(t) + (0,))[j] for j in order)

    return pl.pallas_call(
        _attn_kernel,
        grid=(total + 1,),
        in_specs=[pl.BlockSpec((1, tq, LANES), at(cur, (0, 2, 1))),
                  pl.BlockSpec((1, n, LANES), at(cur, (0, 3, 1))),
                  pl.BlockSpec((1, nc, LANES), at(cur, (0, 3, 1))),
                  pl.BlockSpec((1, V_DIM, n), at(prv, (0, 1, 3))),
                  pl.BlockSpec((1, V_DIM, nc), at(prv, (0, 1, 3))),
                  pl.BlockSpec((4, QK_DIM), lambda t: (0, 0)),
                  pl.BlockSpec((V_DIM, 1), lambda t: (0, 0))],
        out_specs=pl.BlockSpec((1, tq, LANES), at(prv, (0, 2, 1))),
        out_shape=jax.ShapeDtypeStruct((b, n, ATTN_WIDTH), BF16),
        scratch_shapes=[pltpu.VMEM((2, nc + n, tq), F32), pltpu.VMEM((2, 1, tq), F32),
                        pltpu.VMEM((2, nc + n, tq), F32), pltpu.VMEM((2, 1, tq), F32)],
        compiler_params=_cparams(("arbitrary",)),
        name="attn",
    )(q, k, kc, vt, vct, lamv, g_col)


def _outproj_kernel(att_ref, p_ref, pprev_ref, pnext_ref, x_ref, mod_ref, wpool_ref, bpool_ref,
                    pscale_ref, wout_ref, gpost_ref, gpre_ref, wr_ref,
                    x1_ref, h2_ref, h2t_ref, lg_ref, *, tm, n):
    i = pl.program_id(1)
    p = p_ref[0].astype(F32)
    prev = jnp.where(i > 0, pprev_ref[0].astype(F32), 0.0)
    nxt = jnp.where(i < pl.num_programs(1) - 1, pnext_ref[0].astype(F32), 0.0)
    ext = jnp.concatenate([prev, p, nxt], axis=0)
    rows = tm + 2 * HALO
    trow = (i * tm + lax.broadcasted_iota(jnp.int32, (tm, 1), 0))

    pooled = []
    sums = {}
    acc = ext + pltpu.roll(ext, 1, 0)
    sums[2] = acc
    w = 2
    while w < POOL_WINDOWS[-1]:
        acc = pltpu.roll(acc, rows - w // 2, 0) + pltpu.roll(acc, w // 2, 0)
        w = 2 * w
        sums[w] = acc
    for gi, win in enumerate(POOL_WINDOWS):
        sl = slice(gi * POOL_GROUP, (gi + 1) * POOL_GROUP)
        wsum = sums[win][HALO:HALO + tm, sl]
        hi_c = jnp.minimum(trow + (win - win // 2), n)
        lo_c = jnp.maximum(trow - win // 2, 0)
        cnt = (hi_c - lo_c).astype(F32)
        d = wsum / cnt - p[:, sl]
        y = _dot(d.astype(BF16), wpool_ref[gi]) + bpool_ref[:, sl]
        pooled.append((y * pscale_ref[:, sl]).astype(BF16))
    pool = jnp.concatenate(pooled, axis=1)
    yx = _dot(att_ref[0], wout_ref[0:ATTN_WIDTH, :]) + _dot(pool, wout_ref[ATTN_WIDTH:, :])
    x1 = x_ref[0] + mod_ref[0, 2:3, :] * _rms(yx, gpost_ref[...])
    x1_ref[0] = x1
    h2 = _rms(x1, gpre_ref[...]) * (1.0 + mod_ref[0, 4:5, :]) + mod_ref[0, 3:4, :]
    h2_ref[0] = h2.astype(BF16)
    for k in range(CHUNKS):
        h2t_ref[0, pl.ds(k, tm, stride=CHUNKS), :] = h2[:, k * LANES:(k + 1) * LANES]
    lg_ref[...] = _dot3(wr_ref[...], h2, nt=True)


def _outproj(att, p, x, mod3, w_pool_bf, b_pool, pool_scale, w_out_bf, g_post, g_pre, w_router_t, tm=512):
    b, n, d = x.shape
    hb = tm // HALO
    nhb = n // HALO
    row = lambda bi, i: (bi, i, 0)
    vec = lambda bi, i: (0, 0)
    kern = functools.partial(_outproj_kernel, tm=tm, n=n)
    return pl.pallas_call(
        kern,
        grid=(b, n // tm),
        in_specs=[pl.BlockSpec((1, tm, ATTN_WIDTH), row),
                  pl.BlockSpec((1, tm, POOL_WIDTH), row),
                  pl.BlockSpec((1, HALO, POOL_WIDTH), lambda bi, i: (bi, jnp.maximum(i * hb - 1, 0), 0)),
                  pl.BlockSpec((1, HALO, POOL_WIDTH), lambda bi, i: (bi, jnp.minimum((i + 1) * hb, nhb - 1), 0)),
                  pl.BlockSpec((1, tm, d), row),
                  pl.BlockSpec((1, 6, d), lambda bi, i: (bi, 0, 0)),
                  pl.BlockSpec((len(POOL_WINDOWS), POOL_GROUP, POOL_GROUP), lambda bi, i: (0, 0, 0)),
                  pl.BlockSpec((1, POOL_WIDTH), vec),
                  pl.BlockSpec((1, POOL_WIDTH), vec),
                  pl.BlockSpec((d, d), vec),
                  pl.BlockSpec((1, d), vec),
                  pl.BlockSpec((1, d), vec),
                  pl.BlockSpec((N_EXPERTS, d), vec)],
        out_specs=[pl.BlockSpec((1, tm, d), row),
                   pl.BlockSpec((1, tm, d), row),
                   pl.BlockSpec((1, tm * CHUNKS, LANES), row),
                   pl.BlockSpec((N_EXPERTS, tm), lambda bi, i: (0, bi * (n // tm) + i))],
        out_shape=[jax.ShapeDtypeStruct((b, n, d), F32),
                   jax.ShapeDtypeStruct((b, n, d), BF16),
                   jax.ShapeDtypeStruct((b, n * CHUNKS, LANES), F32),
                   jax.ShapeDtypeStruct((N_EXPERTS, b * n), F32)],
        compiler_params=_cparams(("arbitrary", "arbitrary")),
        name="outproj",
    )(att, p, p, p, x, mod3, w_pool_bf, b_pool, pool_scale, w_out_bf, g_post, g_pre, w_router_t)


def _beats(a, b, a_first):
    return jnp.where((a >= b) if a_first else (a > b), 1.0, 0.0)


def _route_kernel(lg_ref, bias_ref, w_ref):
    scores = [1.0 / (1.0 + jnp.exp(-lg_ref[e])) for e in range(N_EXPERTS)]
    biased = [scores[e] + bias_ref[e] for e in range(N_EXPERTS)]
    gscore = []
    for g in range(N_GROUPS):
        vals = biased[g * GROUP_SIZE:(g + 1) * GROUP_SIZE]
        m1 = vals[0]
        m2 = jnp.full_like(m1, -jnp.inf)
        for v in vals[1:]:
            m2 = jnp.maximum(m2, jnp.minimum(m1, v))
            m1 = jnp.maximum(m1, v)
        gscore.append(m1 + m2)
    gsel = []
    for g in range(N_GROUPS):
        rank = jnp.zeros_like(gscore[g])
        for g2 in range(N_GROUPS):
            if g2 != g:
                rank = rank + _beats(gscore[g2], gscore[g], g2 < g)
        gsel.append(rank < TOPK_GROUPS)
    vals = [jnp.where(gsel[e // GROUP_SIZE], biased[e], -jnp.inf) for e in range(N_EXPERTS)]
    chosen = [jnp.zeros(vals[0].shape, jnp.bool_) for _ in range(N_EXPERTS)]
    for _ in range(TOP_K):
        top = vals[0]
        for v in vals[1:]:
            top = jnp.maximum(top, v)
        found = jnp.zeros(top.shape, jnp.bool_)
        for e in range(N_EXPERTS):
            hit = (vals[e] == top) & jnp.logical_not(found)
            found = found | hit
            chosen[e] = chosen[e] | hit
            vals[e] = jnp.where(hit, -jnp.inf, vals[e])
    picked = [jnp.where(chosen[e], scores[e], 0.0) for e in range(N_EXPERTS)]
    denom = picked[0]
    for e in range(1, N_EXPERTS):
        denom = denom + picked[e]
    for e in range(N_EXPERTS):
        w_ref[e] = picked[e] / denom * ROUTED_SCALE


def _route(logits3, bias3, rows=8):
    e, r, l = logits3.shape
    return pl.pallas_call(
        _route_kernel,
        grid=(r // rows,),
        in_specs=[pl.BlockSpec((e, rows, l), lambda i: (0, i, 0)),
                  pl.BlockSpec((e, 1, l), lambda i: (0, 0, 0))],
        out_specs=pl.BlockSpec((e, rows, l), lambda i: (0, i, 0)),
        out_shape=jax.ShapeDtypeStruct((e, r, l), F32),
        compiler_params=_cparams(("arbitrary",)),
        name="route",
    )(logits3, bias3)


CHUNKS = D_MODEL // LANES
SLAB = SLOT_TILE + 4
ROW_BATCH = 16


def _gather_tile(row_ref, tile, xs_ref, tx_ref):
    for s in range(SLOT_TILE):
        row = pl.multiple_of(row_ref[0, tile, s], CHUNKS)
        tx_ref[pl.ds(s, CHUNKS, stride=SLAB), :] = xs_ref[pl.ds(row, CHUNKS), :]


def _experts_step(j, row_ref, sw_ref, wg_buf, wu_buf, wd_buf, xs_ref, acc_ref,
                  tx_cur, tx_nxt, ty_cur, ty_prv):
    ntile = row_ref.shape[1]
    _gather_tile(row_ref, jnp.minimum(j + 1, ntile - 1), xs_ref, tx_nxt)
    xb = jnp.concatenate([tx_cur[pl.ds(k * SLAB, SLOT_TILE), :] for k in range(CHUNKS)], axis=1).astype(BF16)
    g = _dot(xb, wg_buf[...])
    u = _dot(xb, wu_buf[...])
    h = (_silu(g) * u).astype(BF16)
    y = _dot(h, wd_buf[...])
    w_rows = jnp.transpose(jnp.broadcast_to(sw_ref[0, pl.ds(j, 1), :], (LANES, SLOT_TILE)))
    for k in range(CHUNKS):
        ty_cur[pl.ds(k * SLAB, SLOT_TILE), :] = y[:, k * LANES:(k + 1) * LANES] * w_rows
    prv = jnp.maximum(j - 1, 0)
    for s0 in range(0, SLOT_TILE, ROW_BATCH):
        new = []
        for s in range(s0, s0 + ROW_BATCH):
            row = pl.multiple_of(row_ref[0, prv, s], CHUNKS)
            new.append((row, acc_ref[pl.ds(row, CHUNKS), :] + ty_prv[pl.ds(s, CHUNKS, stride=SLAB), :]))
        for row, v in new:
            acc_ref[pl.ds(row, CHUNKS), :] = v


def _experts_kernel(be_ref, nt_ref, first_ref, nxt_ref, ws_ref, row_ref, sw_ref, x_hbm,
                    wg_hbm, wu_hbm, wd_hbm, o_hbm,
                    xs_ref, acc_ref, tx0_ref, tx1_ref, ty0_ref, ty1_ref, wg_st, wu_st, wd_st,
                    wg_buf, wu_buf, wd_buf, sem, wsem, *, ntile, n):
    c = pl.program_id(0)
    j = pl.program_id(1)
    t = c * ntile + j
    rows = n * CHUNKS
    wslot = ws_ref[t]

    def weight_copies(e, slot):
        return [pltpu.make_async_copy(src.at[e], dst.at[slot], wsem.at[slot])
                for src, dst in ((wg_hbm, wg_st), (wu_hbm, wu_st), (wd_hbm, wd_st))]

    @pl.when(j == 0)
    def _():
        for wcp in weight_copies(be_ref[t], 0):
            wcp.start()
        cp = pltpu.make_async_copy(x_hbm.at[c], xs_ref.at[pl.ds(0, rows)], sem.at[0])
        cp.start()
        xs_ref[pl.ds(rows, CHUNKS), :] = jnp.zeros((CHUNKS, LANES), F32)
        acc_ref[...] = jnp.zeros_like(acc_ref)
        ty1_ref[...] = jnp.zeros_like(ty1_ref)
        cp.wait()
        _gather_tile(row_ref, 0, xs_ref, tx0_ref)

    @pl.when(first_ref[t] == 1)
    def _():
        for wcp in weight_copies(be_ref[t], wslot):
            wcp.wait()

    @pl.when((first_ref[t] == 1) & (nxt_ref[t] >= 0))
    def _():
        for wcp in weight_copies(nxt_ref[t], 1 - wslot):
            wcp.start()

    @pl.when(first_ref[t] == 1)
    def _():
        wg_buf[...] = wg_st[wslot].astype(BF16)
        wu_buf[...] = wu_st[wslot].astype(BF16)
        wd_buf[...] = wd_st[wslot].astype(BF16)

    live = j <= nt_ref[c]
    args = (j, row_ref, sw_ref, wg_buf, wu_buf, wd_buf, xs_ref, acc_ref)

    @pl.when(live & (lax.rem(j, 2) == 0))
    def _():
        _experts_step(*args, tx0_ref, tx1_ref, ty0_ref, ty1_ref)

    @pl.when(live & (lax.rem(j, 2) == 1))
    def _():
        _experts_step(*args, tx1_ref, tx0_ref, ty1_ref, ty0_ref)

    @pl.when(j == ntile - 1)
    def _():
        cp = pltpu.make_async_copy(acc_ref.at[pl.ds(0, rows)], o_hbm.at[c], sem.at[1])
        cp.start()
        cp.wait()


def _experts(block_e, ntiles, first, nxt_e, wslot, slot_row, slot_w, h2t, w_gate, w_up, w_down, ntile, n):
    b = h2t.shape[0]
    d, f = w_gate.shape[1], w_gate.shape[2]
    batch = lambda c, j, *_: (c, 0, 0)
    kern = functools.partial(_experts_kernel, ntile=ntile, n=n)
    slab = pltpu.VMEM((CHUNKS * SLAB, LANES), F32)
    hbm = pl.BlockSpec(memory_space=pl.ANY)
    grid_spec = pltpu.PrefetchScalarGridSpec(
        num_scalar_prefetch=5,
        grid=(b, ntile),
        in_specs=[pl.BlockSpec((1, ntile, SLOT_TILE), batch, memory_space=pltpu.SMEM),
                  pl.BlockSpec((1, ntile, SLOT_TILE), batch),
                  hbm, hbm, hbm, hbm],
        out_specs=hbm,
        scratch_shapes=[pltpu.VMEM(((n + 1) * CHUNKS, LANES), F32),
                        pltpu.VMEM(((n + 1) * CHUNKS, LANES), F32),
                        slab, slab, slab, slab,
                        pltpu.VMEM((2, d, f), F32), pltpu.VMEM((2, d, f), F32), pltpu.VMEM((2, f, d), F32),
                        pltpu.VMEM((d, f), BF16), pltpu.VMEM((d, f), BF16), pltpu.VMEM((f, d), BF16),
                        pltpu.SemaphoreType.DMA((2,)), pltpu.SemaphoreType.DMA((2,))],
    )
    return pl.pallas_call(
        kern,
        grid_spec=grid_spec,
        out_shape=jax.ShapeDtypeStruct((b, n * CHUNKS, LANES), F32),
        compiler_params=_cparams(("arbitrary", "arbitrary")),
        name="experts",
    )(block_e, ntiles, first, nxt_e, wslot, slot_row, slot_w, h2t, w_gate, w_up, w_down)


EXPERTS_PER_LANE_GROUP = LANES // 32
PLAN_GROUP = 4


def _plan_tables_kernel(w_ref, re_ref, ro_ref, tab_ref, *, rows):
    ne = w_ref.shape[0]
    w2 = w_ref[...].reshape(ne * rows, LANES)
    selb = jnp.where(w2 > 0.0, 1.0, 0.0).astype(BF16)
    i0 = lax.broadcasted_iota(jnp.int32, (LANES, LANES), 0)
    i1 = lax.broadcasted_iota(jnp.int32, (LANES, LANES), 1)
    rowtot = _dot(selb, jnp.ones((LANES, LANES), BF16))
    ridx = lax.broadcasted_iota(jnp.int32, rowtot.shape, 0) & (rows - 1)
    acc = rowtot
    sh = 1
    while sh < rows:
        acc = acc + jnp.where(ridx >= sh, pltpu.roll(acc, sh, 0), 0.0)
        sh *= 2
    re_ref[0] = acc
    ro_ref[0] = acc - rowtot
    incl_t = _dot_nt(jnp.where(i1 <= i0, 1.0, 0.0).astype(BF16), selb)
    eye = jnp.where(i0 == i1, 1.0, 0.0).astype(BF16)
    w_hi = w2.astype(BF16)
    r1 = w2 - w_hi.astype(F32)
    w_mid = r1.astype(BF16)
    w_lo = (r1 - w_mid.astype(F32)).astype(BF16)
    parts = [_dot_nt(eye, p) for p in (w_hi, w_mid, w_lo)]
    for g in range(ne * rows // LANES):
        sl = slice(g * LANES, (g + 1) * LANES)
        tab_ref[0, g, 0:LANES, :] = incl_t[:, sl].astype(BF16)
        for k in range(3):
            tab_ref[0, g, (k + 1) * LANES:(k + 2) * LANES, :] = parts[k][:, sl].astype(BF16)


def _plan_tables(selw3, b):
    ne, r_all, _ = selw3.shape
    rows = r_all // b
    groups = ne * rows // LANES
    kern = functools.partial(_plan_tables_kernel, rows=rows)
    return pl.pallas_call(
        kern,
        grid=(b,),
        in_specs=[pl.BlockSpec((ne, rows, LANES), lambda c: (0, c, 0))],
        out_specs=[pl.BlockSpec((1, ne * rows, LANES), lambda c: (c, 0, 0)),
                   pl.BlockSpec((1, ne * rows, LANES), lambda c: (c, 0, 0)),
                   pl.BlockSpec((1, groups, 4 * LANES, LANES), lambda c: (c, 0, 0, 0))],
        out_shape=[jax.ShapeDtypeStruct((b, ne * rows, LANES), F32),
                   jax.ShapeDtypeStruct((b, ne * rows, LANES), F32),
                   jax.ShapeDtypeStruct((b, groups, 4 * LANES, LANES), BF16)],
        compiler_params=_cparams(("arbitrary",)),
        name="plan_tables",
    )(selw3)


def _plan_slots_kernel(be_ref, q0_ref, nt_ref, re_ref, ro_ref, tab_ref, tok_ref, sw_ref,
                       *, ntile, rows, n):
    c = pl.program_id(0)
    tok_ref[...] = jnp.full(tok_ref.shape, n * CHUNKS, jnp.int32)
    sw_ref[...] = jnp.zeros(sw_ref.shape, F32)
    lane = lax.broadcasted_iota(jnp.int32, (1, SLOT_TILE), 1).astype(F32)
    sub_r = lax.broadcasted_iota(jnp.int32, (rows, SLOT_TILE), 0).astype(F32)
    sub_l = lax.broadcasted_iota(jnp.int32, (LANES, SLOT_TILE), 0).astype(F32)
    reps = SLOT_TILE // LANES

    def tile_body(j):
        e = be_ref[c * ntile + j]
        q = q0_ref[c * ntile + j].astype(F32) + lane
        base = pl.multiple_of(e * rows, rows)
        row_end = jnp.concatenate([re_ref[0, pl.ds(base, rows), :]] * reps, axis=1)
        row_off = jnp.concatenate([ro_ref[0, pl.ds(base, rows), :]] * reps, axis=1)
        r = jnp.sum(jnp.where(row_end <= q, 1.0, 0.0), axis=0, keepdims=True)
        row_start = jnp.sum(jnp.where(sub_r == r, row_off, 0.0), axis=0, keepdims=True)
        g = lax.shift_right_logical(e, EXPERTS_PER_LANE_GROUP.bit_length() - 1)
        col = r + ((e & (EXPERTS_PER_LANE_GROUP - 1)) * rows).astype(F32)
        onehot = jnp.where(sub_l == col, 1.0, 0.0).astype(BF16)
        picked = _dot(tab_ref[0, g], onehot)
        incl = picked[0:LANES]
        lane_idx = jnp.sum(jnp.where(incl <= q - row_start, 1.0, 0.0), axis=0, keepdims=True)
        wrow = (picked[LANES:2 * LANES] + picked[2 * LANES:3 * LANES]) + picked[3 * LANES:]
        wsel = jnp.sum(jnp.where(sub_l == lane_idx, wrow, 0.0), axis=0, keepdims=True)
        valid = (q < row_end[rows - 1:rows, :]) & (j < nt_ref[c])
        tok = jnp.where(valid, r * LANES + lane_idx, float(n)).astype(jnp.int32)
        tok_ref[0, pl.ds(j, 1), :] = tok * CHUNKS
        sw_ref[0, pl.ds(j, 1), :] = jnp.where(valid, wsel, 0.0)

    def tile_group(i, carry):
        for k in range(PLAN_GROUP):
            tile_body(PLAN_GROUP * i + k)
        return carry

    lax.fori_loop(0, (nt_ref[c] + PLAN_GROUP - 1) // PLAN_GROUP, tile_group, 0)


def _plan_slots(block_e, tile_q0, ntiles, row_end, row_off, tables, ntile, n):
    b, er, _ = row_end.shape
    rows = n // LANES
    groups = tables.shape[1]
    kern = functools.partial(_plan_slots_kernel, ntile=ntile, rows=rows, n=n)
    grid_spec = pltpu.PrefetchScalarGridSpec(
        num_scalar_prefetch=3,
        grid=(b,),
        in_specs=[pl.BlockSpec((1, er, LANES), lambda c, *_: (c, 0, 0)),
                  pl.BlockSpec((1, er, LANES), lambda c, *_: (c, 0, 0)),
                  pl.BlockSpec((1, groups, 4 * LANES, LANES), lambda c, *_: (c, 0, 0, 0))],
        out_specs=[pl.BlockSpec((1, ntile, SLOT_TILE), lambda c, *_: (c, 0, 0)),
                   pl.BlockSpec((1, ntile, SLOT_TILE), lambda c, *_: (c, 0, 0))],
    )
    return pl.pallas_call(
        kern,
        grid_spec=grid_spec,
        out_shape=[jax.ShapeDtypeStruct((b, ntile, SLOT_TILE), jnp.int32),
                   jax.ShapeDtypeStruct((b, ntile, SLOT_TILE), F32)],
        compiler_params=_cparams(("arbitrary",)),
        name="plan_slots",
    )(block_e, tile_q0, ntiles, row_end, row_off, tables)


def _tile_meta(row_end, b, n, ntile):
    rows = n // LANES
    cnt = row_end.reshape(b, N_EXPERTS, rows, LANES)[:, :, rows - 1, 0].astype(jnp.int32)
    nt_e = (cnt + SLOT_TILE - 1) // SLOT_TILE
    tend = jnp.cumsum(nt_e, axis=1)
    tstart = tend - nt_e
    ntiles = tend[:, -1]
    tile_id = jnp.arange(ntile, dtype=jnp.int32)
    tid = jnp.minimum(tile_id[None, :], ntiles[:, None] - 1)
    be = jnp.minimum(jnp.sum(tend[:, None, :] <= tid[:, :, None], axis=2).astype(jnp.int32), N_EXPERTS - 1)
    q0 = (tid - jnp.take_along_axis(tstart, be, axis=1)) * SLOT_TILE
    valid = tile_id[None, :] < ntiles[:, None]
    prev_be = jnp.concatenate([jnp.full((b, 1), -1, jnp.int32), be[:, :-1]], axis=1)
    first = (valid & (be != prev_be)).astype(jnp.int32)
    wslot = (jnp.cumsum(first, axis=1) - 1) % 2
    eid = jnp.arange(N_EXPERTS, dtype=jnp.int32)
    later = (nt_e > 0)[:, None, :] & (eid[None, None, :] > eid[None, :, None])
    nxt = jnp.min(jnp.where(later, eid[None, None, :], N_EXPERTS), axis=2)
    nxt = jnp.where(nxt < N_EXPERTS, nxt, -1)
    nxt_e = jnp.take_along_axis(nxt, be, axis=1)
    flat = lambda a: a.reshape(-1).astype(jnp.int32)
    return flat(be), flat(q0), ntiles.astype(jnp.int32), flat(first), flat(nxt_e), flat(wslot)


def _dispatch_plan(selw3, b, n, ntile):
    row_end, row_off, tables = _plan_tables(selw3, b)
    block_e, tile_q0, ntiles, first, nxt_e, wslot = _tile_meta(row_end, b, n, ntile)
    slot_tok, slot_w = _plan_slots(block_e, tile_q0, ntiles, row_end, row_off, tables, ntile, n)
    return block_e, ntiles, first, nxt_e, wslot, slot_tok, slot_w


def _final_kernel(h2_ref, r_ref, x1_ref, mod_ref, wg_ref, wu_ref, wd_ref, g_ref, o_ref):
    hb = h2_ref[0]
    tm = hb.shape[0]
    sh = _dot((_silu(_dot(hb, wg_ref[...])) * _dot(hb, wu_ref[...])).astype(BF16), wd_ref[...])
    routed = jnp.concatenate([r_ref[0, pl.ds(k, tm, stride=CHUNKS), :] for k in range(CHUNKS)], axis=1)
    moe = routed + sh
    o_ref[0] = x1_ref[0] + mod_ref[0, 5:6, :] * _rms(moe, g_ref[...])


def _final(h2, routed, x1, mod3, wsg_bf, wsu_bf, wsd_bf, g_post, tm=512):
    b, n, d = x1.shape
    f = wsg_bf.shape[1]
    row = lambda bi, i: (bi, i, 0)
    vec = lambda bi, i: (0, 0)
    return pl.pallas_call(
        _final_kernel,
        grid=(b, n // tm),
        in_specs=[pl.BlockSpec((1, tm, d), row),
                  pl.BlockSpec((1, tm * CHUNKS, LANES), row),
                  pl.BlockSpec((1, tm, d), row),
                  pl.BlockSpec((1, 6, d), lambda bi, i: (bi, 0, 0)),
                  pl.BlockSpec((d, f), vec),
                  pl.BlockSpec((d, f), vec),
                  pl.BlockSpec((f, d), vec),
                  pl.BlockSpec((1, d), vec)],
        out_specs=pl.BlockSpec((1, tm, d), row),
        out_shape=jax.ShapeDtypeStruct((b, n, d), F32),
        compiler_params=_cparams(("arbitrary", "arbitrary")),
        name="final",
    )(h2, routed, x1, mod3, wsg_bf, wsu_bf, wsd_bf, g_post)


def _rope_tables(n):
    t = np.arange(n)
    row = (t // GRID_W).astype(np.float32)
    col = (t % GRID_W).astype(np.float32)
    freqs = np.float32(ROPE_THETA) ** (-np.arange(0, AXIS_DIM, 2, dtype=np.float32) / np.float32(AXIS_DIM))
    ar = row[:, None] * freqs
    ac = col[:, None] * freqs
    zeros = np.zeros_like(ar)
    cos64 = np.concatenate([np.cos(ar), np.cos(ar), np.cos(ac), np.cos(ac)], axis=1)
    sa64 = np.concatenate([-np.sin(ar), zeros, -np.sin(ac), zeros], axis=1)
    sb64 = np.concatenate([zeros, np.sin(ar), zeros, np.sin(ac)], axis=1)
    two = lambda a: jnp.asarray(np.concatenate([a, a], axis=1).astype(np.float32))
    return two(cos64), two(sa64), two(sb64)


def kernel(x, c, ctx, c_ctx, w_ada, b_ada, g_pre_mix, g_post_mix, g_pre_ffn, g_post_ffn, w_in, lambda_q1, lambda_k1, lambda_q2, lambda_k2, g_subln, w_pool, b_pool, pool_scale, w_out, w_router, router_bias, w_e_gate, w_e_up, w_e_down, w_sh_gate, w_sh_up, w_sh_down):
    b, n, d = x.shape
    l = 0
    mod_rows = 8
    cin = jnp.concatenate([c, c_ctx[None, :], jnp.zeros((mod_rows - b - 1, d), F32)], axis=0)
    mod3 = _ada(cin, w_ada[l], b_ada[l]).reshape(mod_rows, 6, d)

    w_in_bf = w_in[l].astype(BF16)
    cos, sa, sb = _rope_tables(n)
    q, k, vt, p = _inproj(x, mod3, g_pre_mix[l][None, :], w_in_bf, cos * (QK_DIM ** -0.5 * LOG2E), cos, sa, sb)
    kc, vct = _ctxkv(ctx, mod3, g_pre_mix[l][None, :], w_in_bf, b)

    lamv = jnp.stack([lambda_q1[l], lambda_k1[l], lambda_q2[l], lambda_k2[l]], axis=0)
    att = _attn(q, k, kc, vt, vct, lamv, g_subln[l][:, None])

    x1, h2, h2t, logits_t = _outproj(
        att, p, x, mod3, w_pool[l].astype(BF16), b_pool[l][None, :], pool_scale[l][None, :],
        w_out[l].astype(BF16), g_post_mix[l][None, :], g_pre_ffn[l][None, :], w_router[l].T)

    t = b * n
    selw3 = _route(logits_t.reshape(N_EXPERTS, t // LANES, LANES),
                   jnp.broadcast_to(router_bias[l][:, None, None], (N_EXPERTS, 1, LANES)))

    ntile = (n * TOP_K) // SLOT_TILE + N_EXPERTS
    block_e, ntiles, first, nxt_e, wslot, slot_row, slot_w = _dispatch_plan(selw3, b, n, ntile)
    routed = _experts(block_e, ntiles, first, nxt_e, wslot, slot_row, slot_w, h2t,
                      w_e_gate[l], w_e_up[l], w_e_down[l], ntile, n)

    return _final(h2, routed, x1, mod3, w_sh_gate[l].astype(BF16), w_sh_up[l].astype(BF16),
                  w_sh_down[l].astype(BF16), g_post_ffn[l][None, :])
```

```python
import functools
import math

import jax
import jax.numpy as jnp
import numpy as np
from jax import lax
from jax.experimental import pallas as pl
from jax.experimental.pallas import tpu as pltpu

F32 = jnp.float32
BF16 = jnp.bfloat16

D_MODEL = 1024
GRID_W = 64
N_HEADS = 4
QK_DIM = 64
V_DIM = 128
ATTN_WIDTH = N_HEADS * V_DIM
POOL_WIDTH = D_MODEL - ATTN_WIDTH
POOL_WINDOWS = (2, 4, 8, 16)
POOL_GROUP = POOL_WIDTH // len(POOL_WINDOWS)
AXIS_DIM = QK_DIM // 2
ROPE_THETA = 10000.0
N_EXPERTS = 64
TOP_K = 8
N_GROUPS = 8
GROUP_SIZE = N_EXPERTS // N_GROUPS
TOPK_GROUPS = 4
EXPERT_DIM = 256
ROUTED_SCALE = 2.5
EPS = 1e-6
LAM_INIT = 0.8 - 0.6 * math.exp(-0.3 * 0)

LANES = 128
SLOT_TILE = 256
HALO = 16
VMEM_LIMIT = 56 * 1024 * 1024


def _cparams(sem, vmem=VMEM_LIMIT):
    return pltpu.CompilerParams(dimension_semantics=sem, vmem_limit_bytes=vmem)


def _split(a):
    hi = a.astype(BF16)
    lo = (a - hi.astype(F32)).astype(BF16)
    return hi, lo


def _dot(a, b):
    return jnp.dot(a, b, preferred_element_type=F32)


def _dot_nt(a, b):
    return lax.dot_general(a, b, (((1,), (1,)), ((), ())), preferred_element_type=F32)


def _dot3(a, b, nt=False):
    d = _dot_nt if nt else _dot
    ah, al = _split(a)
    bh, bl = _split(b)
    return d(ah, bh) + d(ah, bl) + d(al, bh)


def _rms(x, g):
    return x * lax.rsqrt(jnp.mean(x * x, axis=-1, keepdims=True) + EPS) * g


def _silu(x):
    return x * (1.0 / (1.0 + jnp.exp(-x)))


def _ada_kernel(c_ref, w_ref, b_ref, o_ref):
    o_ref[...] = _dot3(_silu(c_ref[...]), w_ref[...]) + b_ref[...]


def _ada(cin, w_ada, b_ada):
    rows, d = cin.shape
    n = w_ada.shape[1]
    tn = 1024
    return pl.pallas_call(
        _ada_kernel,
        grid=(n // tn,),
        in_specs=[pl.BlockSpec((rows, d), lambda j: (0, 0)),
                  pl.BlockSpec((d, tn), lambda j: (0, j)),
                  pl.BlockSpec((1, tn), lambda j: (0, j))],
        out_specs=pl.BlockSpec((rows, tn), lambda j: (0, j)),
        out_shape=jax.ShapeDtypeStruct((rows, n), F32),
        compiler_params=_cparams(("arbitrary",)),
        name="ada",
    )(cin, w_ada, b_ada.reshape(1, n))


def _rope(t, cos, sa, sb):
    return t * cos + pltpu.roll(t, LANES - AXIS_DIM // 2, 1) * sa + pltpu.roll(t, AXIS_DIM // 2, 1) * sb


def _inproj_kernel(x_ref, mod_ref, g_ref, w_ref, cq_ref, ck_ref, sa_ref, sb_ref,
                   q_ref, k_ref, vt_ref, p_ref):
    x = x_ref[0]
    h = _rms(x, g_ref[...]) * (1.0 + mod_ref[0, 1:2, :]) + mod_ref[0, 0:1, :]
    px = _dot(h.astype(BF16), w_ref[...])
    ck = ck_ref[...]
    sa = sa_ref[...]
    sb = sb_ref[...]
    cq = cq_ref[...]
    scale = QK_DIM ** -0.5 * LOG2E
    saq = sa * scale
    sbq = sb * scale
    for hh in range(N_HEADS):
        lo = hh * LANES
        q = px[:, lo:lo + LANES]
        k = px[:, ATTN_WIDTH + lo:ATTN_WIDTH + lo + LANES]
        q_ref[0, :, lo:lo + LANES] = _rope(q, cq, saq, sbq).astype(BF16)
        k_ref[0, :, lo:lo + LANES] = _rope(k, ck, sa, sb).astype(BF16)
    vt_ref[0] = jnp.transpose(px[:, 2 * ATTN_WIDTH:3 * ATTN_WIDTH]).astype(BF16)
    p_ref[0] = px[:, 3 * ATTN_WIDTH:].astype(BF16)


def _inproj(x, mod3, g, w_in_bf, cq, ck, sa, sb, tm=1024):
    b, n, d = x.shape
    width = w_in_bf.shape[1]
    row = lambda bi, i: (bi, i, 0)
    tab = pl.BlockSpec((tm, LANES), lambda bi, i: (i, 0))
    out = jax.ShapeDtypeStruct((b, n, ATTN_WIDTH), BF16)
    return pl.pallas_call(
        _inproj_kernel,
        grid=(b, n // tm),
        in_specs=[pl.BlockSpec((1, tm, d), row),
                  pl.BlockSpec((1, 6, d), lambda bi, i: (bi, 0, 0)),
                  pl.BlockSpec((1, d), lambda bi, i: (0, 0)),
                  pl.BlockSpec((d, width), lambda bi, i: (0, 0)),
                  tab, tab, tab, tab],
        out_specs=[pl.BlockSpec((1, tm, ATTN_WIDTH), row),
                   pl.BlockSpec((1, tm, ATTN_WIDTH), row),
                   pl.BlockSpec((1, ATTN_WIDTH, tm), lambda bi, i: (bi, 0, i)),
                   pl.BlockSpec((1, tm, ATTN_WIDTH), row)],
        out_shape=[out, out, jax.ShapeDtypeStruct((b, ATTN_WIDTH, n), BF16), out],
        compiler_params=_cparams(("arbitrary", "arbitrary")),
        name="inproj",
    )(x, mod3, g, w_in_bf, cq, ck, sa, sb)


def _ctxkv_kernel(x_ref, mod_ref, g_ref, wk_ref, wv_ref, k_ref, vt_ref):
    h = _rms(x_ref[0], g_ref[...]) * (1.0 + mod_ref[0, 1:2, :]) + mod_ref[0, 0:1, :]
    hb = h.astype(BF16)
    k_ref[0] = _dot(hb, wk_ref[...]).astype(BF16)
    vt_ref[0] = jnp.transpose(_dot(hb, wv_ref[...])).astype(BF16)


def _ctxkv(ctx, mod3, g, w_in_bf, ctx_row):
    b, n, d = ctx.shape
    out = jax.ShapeDtypeStruct((b, n, ATTN_WIDTH), BF16)
    return pl.pallas_call(
        _ctxkv_kernel,
        grid=(b,),
        in_specs=[pl.BlockSpec((1, n, d), lambda bi: (bi, 0, 0)),
                  pl.BlockSpec((1, 6, d), lambda bi: (ctx_row, 0, 0)),
                  pl.BlockSpec((1, d), lambda bi: (0, 0)),
                  pl.BlockSpec((d, ATTN_WIDTH), lambda bi: (0, 1)),
                  pl.BlockSpec((d, ATTN_WIDTH), lambda bi: (0, 2))],
        out_specs=[pl.BlockSpec((1, n, ATTN_WIDTH), lambda bi: (bi, 0, 0)),
                   pl.BlockSpec((1, ATTN_WIDTH, n), lambda bi: (bi, 0, 0))],
        out_shape=[out, jax.ShapeDtypeStruct((b, ATTN_WIDTH, n), BF16)],
        compiler_params=_cparams(("arbitrary",)),
        name="ctxkv",
    )(ctx, mod3, g, w_in_bf, w_in_bf)


LOG2E = math.log2(math.e)
KEY_CHUNK = 256


def _attn_step(q_ref, kx_ref, kc_ref, vt_ref, vct_ref, lam_ref, g_ref, o_ref, s_new, m_new, s_old, m_old):
    nc = kc_ref.shape[1]
    tq = q_ref.shape[1]
    nk = s_old.shape[1]
    groups = KEY_CHUNK // 8
    q = q_ref[0]
    lane = lax.broadcasted_iota(jnp.int32, q.shape, 1)
    zero = jnp.zeros_like(q)
    qms = (jnp.where(lane < QK_DIM, q, zero), jnp.where(lane >= QK_DIM, q, zero))
    m_prev = (m_old[0], m_old[1])
    top = [jnp.full((8, tq), -jnp.inf, F32) for _ in range(2)]
    den = [jnp.zeros((8, tq), F32) for _ in range(2)]
    acc = [jnp.zeros((V_DIM, tq), F32) for _ in range(2)]
    for c in range(nk // KEY_CHUNK):
        lo = c * KEY_CHUNK
        if lo < nc:
            keys, vt = kc_ref[0, lo:lo + KEY_CHUNK, :], vct_ref[0, :, lo:lo + KEY_CHUNK]
        else:
            keys, vt = kx_ref[0, lo - nc:lo - nc + KEY_CHUNK, :], vt_ref[0, :, lo - nc:lo - nc + KEY_CHUNK]
        for mp in range(2):
            s = _dot_nt(keys, qms[mp])
            s_new[mp, lo:lo + KEY_CHUNK, :] = s
            top[mp] = jnp.maximum(top[mp], jnp.max(s.reshape(groups, 8, tq), axis=0))
            p = jnp.exp2(s_old[mp, lo:lo + KEY_CHUNK, :] - m_prev[mp])
            den[mp] = den[mp] + jnp.sum(p.reshape(groups, 8, tq), axis=0)
            acc[mp] = acc[mp] + _dot(vt, p.astype(BF16))
    for mp in range(2):
        m_new[mp] = jnp.max(top[mp], axis=0, keepdims=True)
    lv = lam_ref[...]
    lam = (jnp.exp(jnp.sum(lv[0:1] * lv[1:2], axis=-1, keepdims=True))
           - jnp.exp(jnp.sum(lv[2:3] * lv[3:4], axis=-1, keepdims=True)) + LAM_INIT)
    l1 = jnp.sum(den[0], axis=0, keepdims=True)
    l2 = jnp.sum(den[1], axis=0, keepdims=True)
    o = acc[0] * (1.0 / l1) - acc[1] * (lam / l2)
    o = o * lax.rsqrt(jnp.mean(o * o, axis=0, keepdims=True) + EPS) * (g_ref[...] * (1.0 - LAM_INIT))
    o_ref[0] = jnp.transpose(o).astype(BF16)


def _attn_kernel(q_ref, kx_ref, kc_ref, vt_ref, vct_ref, lam_ref, g_ref, o_ref, s0_ref, m0_ref, s1_ref, m1_ref):
    t = pl.program_id(0)
    io = (q_ref, kx_ref, kc_ref, vt_ref, vct_ref, lam_ref, g_ref, o_ref)

    @pl.when(t == 0)
    def _():
        s1_ref[...] = jnp.zeros_like(s1_ref)
        m1_ref[...] = jnp.zeros_like(m1_ref)

    @pl.when(lax.rem(t, 2) == 0)
    def _():
        _attn_step(*io, s0_ref, m0_ref, s1_ref, m1_ref)

    @pl.when(lax.rem(t, 2) == 1)
    def _():
        _attn_step(*io, s1_ref, m1_ref, s0_ref, m0_ref)


def _attn(q, k, kc, vt, vct, lamv, g_col, tq=512):
    b, n, _ = q.shape
    nc = kc.shape[1]
    nq = n // tq
    total = b * N_HEADS * nq

    def cur(t):
        ta = jnp.minimum(t, total - 1)
        bh = ta // nq
        return bh // N_HEADS, bh % N_HEADS, ta % nq

    def prv(t):
        tb = jnp.maximum(t - 1, 0)
        bh = tb // nq
        return bh // N_HEADS, bh % N_HEADS, tb % nq

    def at(fn, order):
        return lambda t: tuple((fn(t) + (0,))[j] for j in order)

    return pl.pallas_call(
        _attn_kernel,
        grid=(total + 1,),
        in_specs=[pl.BlockSpec((1, tq, LANES), at(cur, (0, 2, 1))),
                  pl.BlockSpec((1, n, LANES), at(cur, (0, 3, 1))),
                  pl.BlockSpec((1, nc, LANES), at(cur, (0, 3, 1))),
                  pl.BlockSpec((1, V_DIM, n), at(prv, (0, 1, 3))),
                  pl.BlockSpec((1, V_DIM, nc), at(prv, (0, 1, 3))),
                  pl.BlockSpec((4, QK_DIM), lambda t: (0, 0)),
                  pl.BlockSpec((V_DIM, 1), lambda t: (0, 0))],
        out_specs=pl.BlockSpec((1, tq, LANES), at(prv, (0, 2, 1))),
        out_shape=jax.ShapeDtypeStruct((b, n, ATTN_WIDTH), BF16),
        scratch_shapes=[pltpu.VMEM((2, nc + n, tq), F32), pltpu.VMEM((2, 1, tq), F32),
                        pltpu.VMEM((2, nc + n, tq), F32), pltpu.VMEM((2, 1, tq), F32)],
        compiler_params=_cparams(("arbitrary",)),
        name="attn",
    )(q, k, kc, vt, vct, lamv, g_col)


def _outproj_kernel(att_ref, p_ref, pprev_ref, pnext_ref, x_ref, mod_ref, wpool_ref, bpool_ref,
                    pscale_ref, wout_ref, gpost_ref, gpre_ref, wr_ref,
                    x1_ref, h2_ref, h2t_ref, lg_ref, *, tm, n):
    i = pl.program_id(1)
    p = p_ref[0].astype(F32)
    prev = jnp.where(i > 0, pprev_ref[0].astype(F32), 0.0)
    nxt = jnp.where(i < pl.num_programs(1) - 1, pnext_ref[0].astype(F32), 0.0)
    ext = jnp.concatenate([prev, p, nxt], axis=0)
    rows = tm + 2 * HALO
    trow = (i * tm + lax.broadcasted_iota(jnp.int32, (tm, 1), 0))

    pooled = []
    sums = {}
    acc = ext + pltpu.roll(ext, 1, 0)
    sums[2] = acc
    w = 2
    while w < POOL_WINDOWS[-1]:
        acc = pltpu.roll(acc, rows - w // 2, 0) + pltpu.roll(acc, w // 2, 0)
        w = 2 * w
        sums[w] = acc
    for gi, win in enumerate(POOL_WINDOWS):
        sl = slice(gi * POOL_GROUP, (gi + 1) * POOL_GROUP)
        wsum = sums[win][HALO:HALO + tm, sl]
        hi_c = jnp.minimum(trow + (win - win // 2), n)
        lo_c = jnp.maximum(trow - win // 2, 0)
        cnt = (hi_c - lo_c).astype(F32)
        d = wsum / cnt - p[:, sl]
        y = _dot(d.astype(BF16), wpool_ref[gi]) + bpool_ref[:, sl]
        pooled.append((y * pscale_ref[:, sl]).astype(BF16))
    pool = jnp.concatenate(pooled, axis=1)
    yx = _dot(att_ref[0], wout_ref[0:ATTN_WIDTH, :]) + _dot(pool, wout_ref[ATTN_WIDTH:, :])
    x1 = x_ref[0] + mod_ref[0, 2:3, :] * _rms(yx, gpost_ref[...])
    x1_ref[0] = x1
    h2 = _rms(x1, gpre_ref[...]) * (1.0 + mod_ref[0, 4:5, :]) + mod_ref[0, 3:4, :]
    h2_ref[0] = h2.astype(BF16)
    for k in range(CHUNKS):
        h2t_ref[0, pl.ds(k, tm, stride=CHUNKS), :] = h2[:, k * LANES:(k + 1) * LANES]
    lg_ref[...] = _dot3(wr_ref[...], h2, nt=True)


def _outproj(att, p, x, mod3, w_pool_bf, b_pool, pool_scale, w_out_bf, g_post, g_pre, w_router_t, tm=512):
    b, n, d = x.shape
    hb = tm // HALO
    nhb = n // HALO
    row = lambda bi, i: (bi, i, 0)
    vec = lambda bi, i: (0, 0)
    kern = functools.partial(_outproj_kernel, tm=tm, n=n)
    return pl.pallas_call(
        kern,
        grid=(b, n // tm),
        in_specs=[pl.BlockSpec((1, tm, ATTN_WIDTH), row),
                  pl.BlockSpec((1, tm, POOL_WIDTH), row),
                  pl.BlockSpec((1, HALO, POOL_WIDTH), lambda bi, i: (bi, jnp.maximum(i * hb - 1, 0), 0)),
                  pl.BlockSpec((1, HALO, POOL_WIDTH), lambda bi, i: (bi, jnp.minimum((i + 1) * hb, nhb - 1), 0)),
                  pl.BlockSpec((1, tm, d), row),
                  pl.BlockSpec((1, 6, d), lambda bi, i: (bi, 0, 0)),
                  pl.BlockSpec((len(POOL_WINDOWS), POOL_GROUP, POOL_GROUP), lambda bi, i: (0, 0, 0)),
                  pl.BlockSpec((1, POOL_WIDTH), vec),
                  pl.BlockSpec((1, POOL_WIDTH), vec),
                  pl.BlockSpec((d, d), vec),
                  pl.BlockSpec((1, d), vec),
                  pl.BlockSpec((1, d), vec),
                  pl.BlockSpec((N_EXPERTS, d), vec)],
        out_specs=[pl.BlockSpec((1, tm, d), row),
                   pl.BlockSpec((1, tm, d), row),
                   pl.BlockSpec((1, tm * CHUNKS, LANES), row),
                   pl.BlockSpec((N_EXPERTS, tm), lambda bi, i: (0, bi * (n // tm) + i))],
        out_shape=[jax.ShapeDtypeStruct((b, n, d), F32),
                   jax.ShapeDtypeStruct((b, n, d), BF16),
                   jax.ShapeDtypeStruct((b, n * CHUNKS, LANES), F32),
                   jax.ShapeDtypeStruct((N_EXPERTS, b * n), F32)],
        compiler_params=_cparams(("arbitrary", "arbitrary")),
        name="outproj",
    )(att, p, p, p, x, mod3, w_pool_bf, b_pool, pool_scale, w_out_bf, g_post, g_pre, w_router_t)


def _beats(a, b, a_first):
    return jnp.where((a >= b) if a_first else (a > b), 1.0, 0.0)


def _route_kernel(lg_ref, bias_ref, w_ref):
    scores = [1.0 / (1.0 + jnp.exp(-lg_ref[e])) for e in range(N_EXPERTS)]
    biased = [scores[e] + bias_ref[e] for e in range(N_EXPERTS)]
    gscore = []
    for g in range(N_GROUPS):
        vals = biased[g * GROUP_SIZE:(g + 1) * GROUP_SIZE]
        m1 = vals[0]
        m2 = jnp.full_like(m1, -jnp.inf)
        for v in vals[1:]:
            m2 = jnp.maximum(m2, jnp.minimum(m1, v))
            m1 = jnp.maximum(m1, v)
        gscore.append(m1 + m2)
    gsel = []
    for g in range(N_GROUPS):
        rank = jnp.zeros_like(gscore[g])
        for g2 in range(N_GROUPS):
            if g2 != g:
                rank = rank + _beats(gscore[g2], gscore[g], g2 < g)
        gsel.append(rank < TOPK_GROUPS)
    vals = [jnp.where(gsel[e // GROUP_SIZE], biased[e], -jnp.inf) for e in range(N_EXPERTS)]
    chosen = [jnp.zeros(vals[0].shape, jnp.bool_) for _ in range(N_EXPERTS)]
    for _ in range(TOP_K):
        top = vals[0]
        for v in vals[1:]:
            top = jnp.maximum(top, v)
        found = jnp.zeros(top.shape, jnp.bool_)
        for e in range(N_EXPERTS):
            hit = (vals[e] == top) & jnp.logical_not(found)
            found = found | hit
            chosen[e] = chosen[e] | hit
            vals[e] = jnp.where(hit, -jnp.inf, vals[e])
    picked = [jnp.where(chosen[e], scores[e], 0.0) for e in range(N_EXPERTS)]
    denom = picked[0]
    for e in range(1, N_EXPERTS):
        denom = denom + picked[e]
    for e in range(N_EXPERTS):
        w_ref[e] = picked[e] / denom * ROUTED_SCALE


def _route(logits3, bias3, rows=8):
    e, r, l = logits3.shape
    return pl.pallas_call(
        _route_kernel,
        grid=(r // rows,),
        in_specs=[pl.BlockSpec((e, rows, l), lambda i: (0, i, 0)),
                  pl.BlockSpec((e, 1, l), lambda i: (0, 0, 0))],
        out_specs=pl.BlockSpec((e, rows, l), lambda i: (0, i, 0)),
        out_shape=jax.ShapeDtypeStruct((e, r, l), F32),
        compiler_params=_cparams(("arbitrary",)),
        name="route",
    )(logits3, bias3)


CHUNKS = D_MODEL // LANES
SLAB = SLOT_TILE + 4
ROW_BATCH = 8


def _gather_tile(row_ref, tile, xs_ref, tx_ref):
    for s in range(SLOT_TILE):
        row = pl.multiple_of(row_ref[0, tile, s], CHUNKS)
        tx_ref[pl.ds(s, CHUNKS, stride=SLAB), :] = xs_ref[pl.ds(row, CHUNKS), :]


def _experts_step(j, row_ref, sw_ref, wgu_buf, wd_buf, xs_ref, acc_ref,
                  tx_cur, tx_nxt, ty_cur, ty_prv):
    ntile = row_ref.shape[1]
    _gather_tile(row_ref, jnp.minimum(j + 1, ntile - 1), xs_ref, tx_nxt)
    xb = jnp.concatenate([tx_cur[pl.ds(k * SLAB, SLOT_TILE), :] for k in range(CHUNKS)], axis=1).astype(BF16)
    gu = _dot(xb, wgu_buf[...])
    f = wd_buf.shape[0]
    h = (_silu(gu[:, 0:f]) * gu[:, f:]).astype(BF16)
    y = _dot(h, wd_buf[...])
    w_rows = jnp.transpose(jnp.broadcast_to(sw_ref[0, pl.ds(j, 1), :], (LANES, SLOT_TILE)))
    for k in range(CHUNKS):
        ty_cur[pl.ds(k * SLAB, SLOT_TILE), :] = y[:, k * LANES:(k + 1) * LANES] * w_rows
    prv = jnp.maximum(j - 1, 0)
    for s0 in range(0, SLOT_TILE, ROW_BATCH):
        new = []
        for s in range(s0, s0 + ROW_BATCH):
            row = pl.multiple_of(row_ref[0, prv, s], CHUNKS)
            new.append((row, acc_ref[pl.ds(row, CHUNKS), :] + ty_prv[pl.ds(s, CHUNKS, stride=SLAB), :]))
        for row, v in new:
            acc_ref[pl.ds(row, CHUNKS), :] = v


def _experts_kernel(be_ref, nt_ref, first_ref, nxt_ref, ws_ref, row_ref, sw_ref, x_hbm,
                    wg_hbm, wu_hbm, wd_hbm, o_hbm,
                    xs_ref, acc_ref, tx0_ref, tx1_ref, ty0_ref, ty1_ref, wg_st, wu_st, wd_st,
                    wgu_buf, wd_buf, sem, wsem, *, ntile, n):
    c = pl.program_id(0)
    j = pl.program_id(1)
    t = c * ntile + j
    rows = n * CHUNKS
    wslot = ws_ref[t]

    def weight_copies(e, slot):
        return [pltpu.make_async_copy(src.at[e], dst.at[slot], wsem.at[slot])
                for src, dst in ((wg_hbm, wg_st), (wu_hbm, wu_st), (wd_hbm, wd_st))]

    @pl.when(j == 0)
    def _():
        for wcp in weight_copies(be_ref[t], 0):
            wcp.start()
        cp = pltpu.make_async_copy(x_hbm.at[c], xs_ref.at[pl.ds(0, rows)], sem.at[0])
        cp.start()
        xs_ref[pl.ds(rows, CHUNKS), :] = jnp.zeros((CHUNKS, LANES), F32)
        acc_ref[...] = jnp.zeros_like(acc_ref)
        ty1_ref[...] = jnp.zeros_like(ty1_ref)
        cp.wait()
        _gather_tile(row_ref, 0, xs_ref, tx0_ref)

    @pl.when(first_ref[t] == 1)
    def _():
        for wcp in weight_copies(be_ref[t], wslot):
            wcp.wait()

    @pl.when((first_ref[t] == 1) & (nxt_ref[t] >= 0))
    def _():
        for wcp in weight_copies(nxt_ref[t], 1 - wslot):
            wcp.start()

    @pl.when(first_ref[t] == 1)
    def _():
        f = wd_buf.shape[0]
        wgu_buf[:, 0:f] = wg_st[wslot].astype(BF16)
        wgu_buf[:, f:] = wu_st[wslot].astype(BF16)
        wd_buf[...] = wd_st[wslot].astype(BF16)

    live = j <= nt_ref[c]
    args = (j, row_ref, sw_ref, wgu_buf, wd_buf, xs_ref, acc_ref)

    @pl.when(live & (lax.rem(j, 2) == 0))
    def _():
        _experts_step(*args, tx0_ref, tx1_ref, ty0_ref, ty1_ref)

    @pl.when(live & (lax.rem(j, 2) == 1))
    def _():
        _experts_step(*args, tx1_ref, tx0_ref, ty1_ref, ty0_ref)

    @pl.when(j == ntile - 1)
    def _():
        cp = pltpu.make_async_copy(acc_ref.at[pl.ds(0, rows)], o_hbm.at[c], sem.at[1])
        cp.start()
        cp.wait()


def _experts(block_e, ntiles, first, nxt_e, wslot, slot_row, slot_w, h2t, w_gate, w_up, w_down, ntile, n):
    b = h2t.shape[0]
    d, f = w_gate.shape[1], w_gate.shape[2]
    batch = lambda c, j, *_: (c, 0, 0)
    kern = functools.partial(_experts_kernel, ntile=ntile, n=n)
    slab = pltpu.VMEM((CHUNKS * SLAB, LANES), F32)
    hbm = pl.BlockSpec(memory_space=pl.ANY)
    grid_spec = pltpu.PrefetchScalarGridSpec(
        num_scalar_prefetch=5,
        grid=(b, ntile),
        in_specs=[pl.BlockSpec((1, ntile, SLOT_TILE), batch, memory_space=pltpu.SMEM),
                  pl.BlockSpec((1, ntile, SLOT_TILE), batch),
                  hbm, hbm, hbm, hbm],
        out_specs=hbm,
        scratch_shapes=[pltpu.VMEM(((n + 1) * CHUNKS, LANES), F32),
                        pltpu.VMEM(((n + 1) * CHUNKS, LANES), F32),
                        slab, slab, slab, slab,
                        pltpu.VMEM((2, d, f), F32), pltpu.VMEM((2, d, f), F32), pltpu.VMEM((2, f, d), F32),
                        pltpu.VMEM((d, 2 * f), BF16), pltpu.VMEM((f, d), BF16),
                        pltpu.SemaphoreType.DMA((2,)), pltpu.SemaphoreType.DMA((2,))],
    )
    return pl.pallas_call(
        kern,
        grid_spec=grid_spec,
        out_shape=jax.ShapeDtypeStruct((b, n * CHUNKS, LANES), F32),
        compiler_params=_cparams(("arbitrary", "arbitrary")),
        name="experts",
    )(block_e, ntiles, first, nxt_e, wslot, slot_row, slot_w, h2t, w_gate, w_up, w_down)


PLAN_GROUP = 4


def _plan_tables_kernel(w_ref, re_ref, ro_ref, tab_ref, *, rows):
    ne = w_ref.shape[0]
    w2 = w_ref[...].reshape(ne * rows, LANES)
    selb = jnp.where(w2 > 0.0, 1.0, 0.0).astype(BF16)
    i0 = lax.broadcasted_iota(jnp.int32, (LANES, LANES), 0)
    i1 = lax.broadcasted_iota(jnp.int32, (LANES, LANES), 1)
    rowtot = _dot(selb, jnp.ones((LANES, LANES), BF16))
    ridx = lax.broadcasted_iota(jnp.int32, rowtot.shape, 0) & (rows - 1)
    acc = rowtot
    sh = 1
    while sh < rows:
        acc = acc + jnp.where(ridx >= sh, pltpu.roll(acc, sh, 0), 0.0)
        sh *= 2
    re_ref[0] = acc
    ro_ref[0] = acc - rowtot
    incl_t = _dot_nt(jnp.where(i1 <= i0, 1.0, 0.0).astype(BF16), selb)
    eye = jnp.where(i0 == i1, 1.0, 0.0).astype(BF16)
    w_hi = w2.astype(BF16)
    r1 = w2 - w_hi.astype(F32)
    w_mid = r1.astype(BF16)
    w_lo = (r1 - w_mid.astype(F32)).astype(BF16)
    parts = [_dot_nt(eye, p) for p in (w_hi, w_mid, w_lo)]
    for g in range(ne * rows // LANES):
        sl = slice(g * LANES, (g + 1) * LANES)
        tab_ref[0, g, 0:LANES, :] = incl_t[:, sl].astype(BF16)
        for k in range(3):
            tab_ref[0, g, (k + 1) * LANES:(k + 2) * LANES, :] = parts[k][:, sl].astype(BF16)


def _plan_tables(selw3, b):
    ne, r_all, _ = selw3.shape
    rows = r_all // b
    groups = ne * rows // LANES
    kern = functools.partial(_plan_tables_kernel, rows=rows)
    return pl.pallas_call(
        kern,
        grid=(b,),
        in_specs=[pl.BlockSpec((ne, rows, LANES), lambda c: (0, c, 0))],
        out_specs=[pl.BlockSpec((1, ne * rows, LANES), lambda c: (c, 0, 0)),
                   pl.BlockSpec((1, ne * rows, LANES), lambda c: (c, 0, 0)),
                   pl.BlockSpec((1, groups, 4 * LANES, LANES), lambda c: (c, 0, 0, 0))],
        out_shape=[jax.ShapeDtypeStruct((b, ne * rows, LANES), F32),
                   jax.ShapeDtypeStruct((b, ne * rows, LANES), F32),
                   jax.ShapeDtypeStruct((b, groups, 4 * LANES, LANES), BF16)],
        compiler_params=_cparams(("arbitrary",)),
        name="plan_tables",
    )(selw3)


def _plan_slots_kernel(be_ref, q0_ref, nt_ref, re_ref, ro_ref, tab_ref, tok_ref, sw_ref,
                       *, ntile, rows, n):
    c = pl.program_id(0)
    tok_ref[...] = jnp.full(tok_ref.shape, n * CHUNKS, jnp.int32)
    sw_ref[...] = jnp.zeros(sw_ref.shape, F32)
    lane = lax.broadcasted_iota(jnp.int32, (1, SLOT_TILE), 1).astype(F32)
    sub_r = lax.broadcasted_iota(jnp.int32, (rows, SLOT_TILE), 0).astype(F32)
    sub_l = lax.broadcasted_iota(jnp.int32, (LANES, SLOT_TILE), 0).astype(F32)
    reps = SLOT_TILE // LANES

    def tile_body(j):
        e = be_ref[c * ntile + j]
        q = q0_ref[c * ntile + j].astype(F32) + lane
        base = pl.multiple_of(e * rows, rows)
        row_end = jnp.concatenate([re_ref[0, pl.ds(base, rows), :]] * reps, axis=1)
        row_off = jnp.concatenate([ro_ref[0, pl.ds(base, rows), :]] * reps, axis=1)
        r = jnp.sum(jnp.where(row_end <= q, 1.0, 0.0), axis=0, keepdims=True)
        row_start = jnp.sum(jnp.where(sub_r == r, row_off, 0.0), axis=0, keepdims=True)
        per_group = LANES // rows
        g = lax.shift_right_logical(e, per_group.bit_length() - 1)
        col = r + ((e & (per_group - 1)) * rows).astype(F32)
        onehot = jnp.where(sub_l == col, 1.0, 0.0).astype(BF16)
        picked = _dot(tab_ref[0, g], onehot)
        incl = picked[0:LANES]
        lane_idx = jnp.sum(jnp.where(incl <= q - row_start, 1.0, 0.0), axis=0, keepdims=True)
        wrow = (picked[LANES:2 * LANES] + picked[2 * LANES:3 * LANES]) + picked[3 * LANES:]
        wsel = jnp.sum(jnp.where(sub_l == lane_idx, wrow, 0.0), axis=0, keepdims=True)
        valid = (q < row_end[rows - 1:rows, :]) & (j < nt_ref[c])
        tok = jnp.where(valid, r * LANES + lane_idx, float(n)).astype(jnp.int32)
        tok_ref[0, pl.ds(j, 1), :] = tok * CHUNKS
        sw_ref[0, pl.ds(j, 1), :] = jnp.where(valid, wsel, 0.0)

    def tile_group(i, carry):
        for k in range(PLAN_GROUP):
            tile_body(PLAN_GROUP * i + k)
        return carry

    lax.fori_loop(0, (nt_ref[c] + PLAN_GROUP - 1) // PLAN_GROUP, tile_group, 0)


def _plan_slots(block_e, tile_q0, ntiles, row_end, row_off, tables, ntile, n):
    b, er, _ = row_end.shape
    rows = n // LANES
    per_group = LANES // rows
    assert rows * per_group == LANES and per_group & (per_group - 1) == 0 and ntile % PLAN_GROUP == 0
    groups = tables.shape[1]
    kern = functools.partial(_plan_slots_kernel, ntile=ntile, rows=rows, n=n)
    grid_spec = pltpu.PrefetchScalarGridSpec(
        num_scalar_prefetch=3,
        grid=(b,),
        in_specs=[pl.BlockSpec((1, er, LANES), lambda c, *_: (c, 0, 0)),
                  pl.BlockSpec((1, er, LANES), lambda c, *_: (c, 0, 0)),
                  pl.BlockSpec((1, groups, 4 * LANES, LANES), lambda c, *_: (c, 0, 0, 0))],
        out_specs=[pl.BlockSpec((1, ntile, SLOT_TILE), lambda c, *_: (c, 0, 0)),
                   pl.BlockSpec((1, ntile, SLOT_TILE), lambda c, *_: (c, 0, 0))],
    )
    return pl.pallas_call(
        kern,
        grid_spec=grid_spec,
        out_shape=[jax.ShapeDtypeStruct((b, ntile, SLOT_TILE), jnp.int32),
                   jax.ShapeDtypeStruct((b, ntile, SLOT_TILE), F32)],
        compiler_params=_cparams(("arbitrary",)),
        name="plan_slots",
    )(block_e, tile_q0, ntiles, row_end, row_off, tables)


def _tile_meta(row_end, b, n, ntile):
    rows = n // LANES
    cnt = row_end.reshape(b, N_EXPERTS, rows, LANES)[:, :, rows - 1, 0].astype(jnp.int32)
    nt_e = (cnt + SLOT_TILE - 1) // SLOT_TILE
    tend = jnp.cumsum(nt_e, axis=1)
    tstart = tend - nt_e
    ntiles = tend[:, -1]
    tile_id = jnp.arange(ntile, dtype=jnp.int32)
    tid = jnp.minimum(tile_id[None, :], ntiles[:, None] - 1)
    be = jnp.minimum(jnp.sum(tend[:, None, :] <= tid[:, :, None], axis=2).astype(jnp.int32), N_EXPERTS - 1)
    q0 = (tid - jnp.take_along_axis(tstart, be, axis=1)) * SLOT_TILE
    valid = tile_id[None, :] < ntiles[:, None]
    prev_be = jnp.concatenate([jnp.full((b, 1), -1, jnp.int32), be[:, :-1]], axis=1)
    first = (valid & (be != prev_be)).astype(jnp.int32)
    wslot = (jnp.cumsum(first, axis=1) - 1) % 2
    eid = jnp.arange(N_EXPERTS, dtype=jnp.int32)
    later = (nt_e > 0)[:, None, :] & (eid[None, None, :] > eid[None, :, None])
    nxt = jnp.min(jnp.where(later, eid[None, None, :], N_EXPERTS), axis=2)
    nxt = jnp.where(nxt < N_EXPERTS, nxt, -1)
    nxt_e = jnp.take_along_axis(nxt, be, axis=1)
    flat = lambda a: a.reshape(-1).astype(jnp.int32)
    return flat(be), flat(q0), ntiles.astype(jnp.int32), flat(first), flat(nxt_e), flat(wslot)


def _dispatch_plan(selw3, b, n, ntile):
    row_end, row_off, tables = _plan_tables(selw3, b)
    block_e, tile_q0, ntiles, first, nxt_e, wslot = _tile_meta(row_end, b, n, ntile)
    slot_tok, slot_w = _plan_slots(block_e, tile_q0, ntiles, row_end, row_off, tables, ntile, n)
    return block_e, ntiles, first, nxt_e, wslot, slot_tok, slot_w


def _final_kernel(h2_ref, r_ref, x1_ref, mod_ref, wg_ref, wu_ref, wd_ref, g_ref, o_ref):
    hb = h2_ref[0]
    tm = hb.shape[0]
    sh = _dot((_silu(_dot(hb, wg_ref[...])) * _dot(hb, wu_ref[...])).astype(BF16), wd_ref[...])
    routed = jnp.concatenate([r_ref[0, pl.ds(k, tm, stride=CHUNKS), :] for k in range(CHUNKS)], axis=1)
    moe = routed + sh
    o_ref[0] = x1_ref[0] + mod_ref[0, 5:6, :] * _rms(moe, g_ref[...])


def _final(h2, routed, x1, mod3, wsg_bf, wsu_bf, wsd_bf, g_post, tm=512):
    b, n, d = x1.shape
    f = wsg_bf.shape[1]
    row = lambda bi, i: (bi, i, 0)
    vec = lambda bi, i: (0, 0)
    return pl.pallas_call(
        _final_kernel,
        grid=(b, n // tm),
        in_specs=[pl.BlockSpec((1, tm, d), row),
                  pl.BlockSpec((1, tm * CHUNKS, LANES), row),
                  pl.BlockSpec((1, tm, d), row),
                  pl.BlockSpec((1, 6, d), lambda bi, i: (bi, 0, 0)),
                  pl.BlockSpec((d, f), vec),
                  pl.BlockSpec((d, f), vec),
                  pl.BlockSpec((f, d), vec),
                  pl.BlockSpec((1, d), vec)],
        out_specs=pl.BlockSpec((1, tm, d), row),
        out_shape=jax.ShapeDtypeStruct((b, n, d), F32),
        compiler_params=_cparams(("arbitrary", "arbitrary")),
        name="final",
    )(h2, routed, x1, mod3, wsg_bf, wsu_bf, wsd_bf, g_post)


def _rope_tables(n):
    t = np.arange(n)
    row = (t // GRID_W).astype(np.float32)
    col = (t % GRID_W).astype(np.float32)
    freqs = np.float32(ROPE_THETA) ** (-np.arange(0, AXIS_DIM, 2, dtype=np.float32) / np.float32(AXIS_DIM))
    ar = row[:, None] * freqs
    ac = col[:, None] * freqs
    zeros = np.zeros_like(ar)
    cos64 = np.concatenate([np.cos(ar), np.cos(ar), np.cos(ac), np.cos(ac)], axis=1)
    sa64 = np.concatenate([-np.sin(ar), zeros, -np.sin(ac), zeros], axis=1)
    sb64 = np.concatenate([zeros, np.sin(ar), zeros, np.sin(ac)], axis=1)
    two = lambda a: jnp.asarray(np.concatenate([a, a], axis=1).astype(np.float32))
    return two(cos64), two(sa64), two(sb64)


def kernel(x, c, ctx, c_ctx, w_ada, b_ada, g_pre_mix, g_post_mix, g_pre_ffn, g_post_ffn, w_in, lambda_q1, lambda_k1, lambda_q2, lambda_k2, g_subln, w_pool, b_pool, pool_scale, w_out, w_router, router_bias, w_e_gate, w_e_up, w_e_down, w_sh_gate, w_sh_up, w_sh_down):
    b, n, d = x.shape
    l = 0
    mod_rows = 8
    cin = jnp.concatenate([c, c_ctx[None, :], jnp.zeros((mod_rows - b - 1, d), F32)], axis=0)
    mod3 = _ada(cin, w_ada[l], b_ada[l]).reshape(mod_rows, 6, d)

    w_in_bf = w_in[l].astype(BF16)
    cos, sa, sb = _rope_tables(n)
    q, k, vt, p = _inproj(x, mod3, g_pre_mix[l][None, :], w_in_bf, cos * (QK_DIM ** -0.5 * LOG2E), cos, sa, sb)
    kc, vct = _ctxkv(ctx, mod3, g_pre_mix[l][None, :], w_in_bf, b)

    lamv = jnp.stack([lambda_q1[l], lambda_k1[l], lambda_q2[l], lambda_k2[l]], axis=0)
    att = _attn(q, k, kc, vt, vct, lamv, g_subln[l][:, None])

    x1, h2, h2t, logits_t = _outproj(
        att, p, x, mod3, w_pool[l].astype(BF16), b_pool[l][None, :], pool_scale[l][None, :],
        w_out[l].astype(BF16), g_post_mix[l][None, :], g_pre_ffn[l][None, :], w_router[l].T)

    t = b * n
    selw3 = _route(logits_t.reshape(N_EXPERTS, t // LANES, LANES),
                   jnp.broadcast_to(router_bias[l][:, None, None], (N_EXPERTS, 1, LANES)))

    ntile = (n * TOP_K) // SLOT_TILE + N_EXPERTS
    block_e, ntiles, first, nxt_e, wslot, slot_row, slot_w = _dispatch_plan(selw3, b, n, ntile)
    routed = _experts(block_e, ntiles, first, nxt_e, wslot, slot_row, slot_w, h2t,
                      w_e_gate[l], w_e_up[l], w_e_down[l], ntile, n)

    return _final(h2, routed, x1, mod3, w_sh_gate[l].astype(BF16), w_sh_up[l].astype(BF16),
                  w_sh_down[l].astype(BF16), g_post_ffn[l][None, :])
```

```python
import functools
import math

import jax
import jax.numpy as jnp
import numpy as np
from jax import lax
from jax.experimental import pallas as pl
from jax.experimental.pallas import tpu as pltpu

F32 = jnp.float32
BF16 = jnp.bfloat16

D_MODEL = 1024
GRID_W = 64
N_HEADS = 4
QK_DIM = 64
V_DIM = 128
ATTN_WIDTH = N_HEADS * V_DIM
POOL_WIDTH = D_MODEL - ATTN_WIDTH
POOL_WINDOWS = (2, 4, 8, 16)
POOL_GROUP = POOL_WIDTH // len(POOL_WINDOWS)
AXIS_DIM = QK_DIM // 2
ROPE_THETA = 10000.0
N_EXPERTS = 64
TOP_K = 8
N_GROUPS = 8
GROUP_SIZE = N_EXPERTS // N_GROUPS
TOPK_GROUPS = 4
EXPERT_DIM = 256
ROUTED_SCALE = 2.5
EPS = 1e-6
LAM_INIT = 0.8 - 0.6 * math.exp(-0.3 * 0)

LANES = 128
SUBLANES = 8
SLOT_TILE = 256
HALO = 16
VMEM_LIMIT = 56 * 1024 * 1024


def _cparams(sem, vmem=VMEM_LIMIT):
    return pltpu.CompilerParams(dimension_semantics=sem, vmem_limit_bytes=vmem)


def _split(a):
    hi = a.astype(BF16)
    lo = (a - hi.astype(F32)).astype(BF16)
    return hi, lo


def _dot(a, b):
    return jnp.dot(a, b, preferred_element_type=F32)


def _dot_nt(a, b):
    return lax.dot_general(a, b, (((1,), (1,)), ((), ())), preferred_element_type=F32)


def _dot3(a, b, nt=False):
    d = _dot_nt if nt else _dot
    ah, al = _split(a)
    bh, bl = _split(b)
    return d(ah, bh) + d(ah, bl) + d(al, bh)


def _rms(x, g):
    return x * lax.rsqrt(jnp.mean(x * x, axis=-1, keepdims=True) + EPS) * g


def _silu(x):
    return x * (1.0 / (1.0 + jnp.exp(-x)))


def _ada_kernel(c_ref, w_ref, b_ref, o_ref):
    o_ref[...] = _dot3(_silu(c_ref[...]), w_ref[...]) + b_ref[...]


def _ada(cin, w_ada, b_ada):
    rows, d = cin.shape
    n = w_ada.shape[1]
    tn = 1024
    return pl.pallas_call(
        _ada_kernel,
        grid=(n // tn,),
        in_specs=[pl.BlockSpec((rows, d), lambda j: (0, 0)),
                  pl.BlockSpec((d, tn), lambda j: (0, j)),
                  pl.BlockSpec((1, tn), lambda j: (0, j))],
        out_specs=pl.BlockSpec((rows, tn), lambda j: (0, j)),
        out_shape=jax.ShapeDtypeStruct((rows, n), F32),
        compiler_params=_cparams(("arbitrary",)),
        name="ada",
    )(cin, w_ada, b_ada.reshape(1, n))


def _rope(t, cos, sa, sb):
    return t * cos + pltpu.roll(t, LANES - AXIS_DIM // 2, 1) * sa + pltpu.roll(t, AXIS_DIM // 2, 1) * sb


def _inproj_kernel(x_ref, mod_ref, g_ref, w_ref, cq_ref, ck_ref, sa_ref, sb_ref,
                   q_ref, k_ref, vt_ref, p_ref):
    x = x_ref[0]
    h = _rms(x, g_ref[...]) * (1.0 + mod_ref[0, 1:2, :]) + mod_ref[0, 0:1, :]
    px = _dot(h.astype(BF16), w_ref[...])
    ck = ck_ref[...]
    sa = sa_ref[...]
    sb = sb_ref[...]
    cq = cq_ref[...]
    scale = QK_DIM ** -0.5 * LOG2E
    saq = sa * scale
    sbq = sb * scale
    for hh in range(N_HEADS):
        lo = hh * LANES
        q = px[:, lo:lo + LANES]
        k = px[:, ATTN_WIDTH + lo:ATTN_WIDTH + lo + LANES]
        q_ref[0, :, lo:lo + LANES] = _rope(q, cq, saq, sbq).astype(BF16)
        k_ref[0, :, lo:lo + LANES] = _rope(k, ck, sa, sb).astype(BF16)
    vt_ref[0] = jnp.transpose(px[:, 2 * ATTN_WIDTH:3 * ATTN_WIDTH]).astype(BF16)
    p_ref[0] = px[:, 3 * ATTN_WIDTH:].astype(BF16)


def _inproj(x, mod3, g, w_in_bf, cq, ck, sa, sb, tm=1024):
    b, n, d = x.shape
    width = w_in_bf.shape[1]
    row = lambda bi, i: (bi, i, 0)
    tab = pl.BlockSpec((tm, LANES), lambda bi, i: (i, 0))
    out = jax.ShapeDtypeStruct((b, n, ATTN_WIDTH), BF16)
    return pl.pallas_call(
        _inproj_kernel,
        grid=(b, n // tm),
        in_specs=[pl.BlockSpec((1, tm, d), row),
                  pl.BlockSpec((1, 6, d), lambda bi, i: (bi, 0, 0)),
                  pl.BlockSpec((1, d), lambda bi, i: (0, 0)),
                  pl.BlockSpec((d, width), lambda bi, i: (0, 0)),
                  tab, tab, tab, tab],
        out_specs=[pl.BlockSpec((1, tm, ATTN_WIDTH), row),
                   pl.BlockSpec((1, tm, ATTN_WIDTH), row),
                   pl.BlockSpec((1, ATTN_WIDTH, tm), lambda bi, i: (bi, 0, i)),
                   pl.BlockSpec((1, tm, ATTN_WIDTH), row)],
        out_shape=[out, out, jax.ShapeDtypeStruct((b, ATTN_WIDTH, n), BF16), out],
        compiler_params=_cparams(("arbitrary", "arbitrary")),
        name="inproj",
    )(x, mod3, g, w_in_bf, cq, ck, sa, sb)


def _ctxkv_kernel(x_ref, mod_ref, g_ref, wk_ref, wv_ref, k_ref, vt_ref):
    h = _rms(x_ref[0], g_ref[...]) * (1.0 + mod_ref[0, 1:2, :]) + mod_ref[0, 0:1, :]
    hb = h.astype(BF16)
    k_ref[0] = _dot(hb, wk_ref[...]).astype(BF16)
    vt_ref[0] = jnp.transpose(_dot(hb, wv_ref[...])).astype(BF16)


def _ctxkv(ctx, mod3, g, w_in_bf, ctx_row):
    b, n, d = ctx.shape
    out = jax.ShapeDtypeStruct((b, n, ATTN_WIDTH), BF16)
    return pl.pallas_call(
        _ctxkv_kernel,
        grid=(b,),
        in_specs=[pl.BlockSpec((1, n, d), lambda bi: (bi, 0, 0)),
                  pl.BlockSpec((1, 6, d), lambda bi: (ctx_row, 0, 0)),
                  pl.BlockSpec((1, d), lambda bi: (0, 0)),
                  pl.BlockSpec((d, ATTN_WIDTH), lambda bi: (0, 1)),
                  pl.BlockSpec((d, ATTN_WIDTH), lambda bi: (0, 2))],
        out_specs=[pl.BlockSpec((1, n, ATTN_WIDTH), lambda bi: (bi, 0, 0)),
                   pl.BlockSpec((1, ATTN_WIDTH, n), lambda bi: (bi, 0, 0))],
        out_shape=[out, jax.ShapeDtypeStruct((b, ATTN_WIDTH, n), BF16)],
        compiler_params=_cparams(("arbitrary",)),
        name="ctxkv",
    )(ctx, mod3, g, w_in_bf, w_in_bf)


LOG2E = math.log2(math.e)
KEY_CHUNK = 256


def _attn_step(q_ref, kx_ref, kc_ref, vt_ref, vct_ref, lam_ref, g_ref, o_ref, s_new, m_new, s_old, m_old):
    nc = kc_ref.shape[1]
    tq = q_ref.shape[1]
    nk = s_old.shape[1]
    groups = KEY_CHUNK // SUBLANES
    q = q_ref[0]
    lane = lax.broadcasted_iota(jnp.int32, q.shape, 1)
    zero = jnp.zeros_like(q)
    qms = (jnp.where(lane < QK_DIM, q, zero), jnp.where(lane >= QK_DIM, q, zero))
    m_prev = (m_old[0], m_old[1])
    top = [jnp.full((SUBLANES, tq), -jnp.inf, F32) for _ in range(2)]
    den = [jnp.zeros((SUBLANES, tq), F32) for _ in range(2)]
    acc = [jnp.zeros((V_DIM, tq), F32) for _ in range(2)]
    for c in range(nk // KEY_CHUNK):
        lo = c * KEY_CHUNK
        if lo < nc:
            keys, vt = kc_ref[0, lo:lo + KEY_CHUNK, :], vct_ref[0, :, lo:lo + KEY_CHUNK]
        else:
            keys, vt = kx_ref[0, lo - nc:lo - nc + KEY_CHUNK, :], vt_ref[0, :, lo - nc:lo - nc + KEY_CHUNK]
        for mp in range(2):
            s = _dot_nt(keys, qms[mp])
            s_new[mp, lo:lo + KEY_CHUNK, :] = s
            top[mp] = jnp.maximum(top[mp], jnp.max(s.reshape(groups, SUBLANES, tq), axis=0))
            p = jnp.exp2(s_old[mp, lo:lo + KEY_CHUNK, :] - m_prev[mp])
            den[mp] = den[mp] + jnp.sum(p.reshape(groups, SUBLANES, tq), axis=0)
            acc[mp] = acc[mp] + _dot(vt, p.astype(BF16))
    for mp in range(2):
        m_new[mp] = jnp.max(top[mp], axis=0, keepdims=True)
    lv = lam_ref[...]
    lam = (jnp.exp(jnp.sum(lv[0:1] * lv[1:2], axis=-1, keepdims=True))
           - jnp.exp(jnp.sum(lv[2:3] * lv[3:4], axis=-1, keepdims=True)) + LAM_INIT)
    l1 = jnp.sum(den[0], axis=0, keepdims=True)
    l2 = jnp.sum(den[1], axis=0, keepdims=True)
    o = acc[0] * (1.0 / l1) - acc[1] * (lam / l2)
    o = o * lax.rsqrt(jnp.mean(o * o, axis=0, keepdims=True) + EPS) * (g_ref[...] * (1.0 - LAM_INIT))
    o_ref[0] = jnp.transpose(o).astype(BF16)


def _attn_kernel(q_ref, kx_ref, kc_ref, vt_ref, vct_ref, lam_ref, g_ref, o_ref, s0_ref, m0_ref, s1_ref, m1_ref):
    t = pl.program_id(0)
    io = (q_ref, kx_ref, kc_ref, vt_ref, vct_ref, lam_ref, g_ref, o_ref)

    @pl.when(t == 0)
    def _():
        s1_ref[...] = jnp.zeros_like(s1_ref)
        m1_ref[...] = jnp.zeros_like(m1_ref)

    @pl.when(lax.rem(t, 2) == 0)
    def _():
        _attn_step(*io, s0_ref, m0_ref, s1_ref, m1_ref)

    @pl.when(lax.rem(t, 2) == 1)
    def _():
        _attn_step(*io, s1_ref, m1_ref, s0_ref, m0_ref)


def _attn(q, k, kc, vt, vct, lamv, g_col, tq=512):
    b, n, _ = q.shape
    nc = kc.shape[1]
    nq = n // tq
    total = b * N_HEADS * nq

    def cur(t):
        ta = jnp.minimum(t, total - 1)
        bh = ta // nq
        return bh // N_HEADS, bh % N_HEADS, ta % nq

    def prv(t):
        tb = jnp.maximum(t - 1, 0)
        bh = tb // nq
        return bh // N_HEADS, bh % N_HEADS, tb % nq

    def at(fn, order):
        return lambda t: tuple((fn(t) + (0,))[j] for j in order)

    return pl.pallas_call(
        _attn_kernel,
        grid=(total + 1,),
        in_specs=[pl.BlockSpec((1, tq, LANES), at(cur, (0, 2, 1))),
                  pl.BlockSpec((1, n, LANES), at(cur, (0, 3, 1))),
                  pl.BlockSpec((1, nc, LANES), at(cur, (0, 3, 1))),
                  pl.BlockSpec((1, V_DIM, n), at(prv, (0, 1, 3))),
                  pl.BlockSpec((1, V_DIM, nc), at(prv, (0, 1, 3))),
                  pl.BlockSpec((4, QK_DIM), lambda t: (0, 0)),
                  pl.BlockSpec((V_DIM, 1), lambda t: (0, 0))],
        out_specs=pl.BlockSpec((1, tq, LANES), at(prv, (0, 2, 1))),
        out_shape=jax.ShapeDtypeStruct((b, n, ATTN_WIDTH), BF16),
        scratch_shapes=[pltpu.VMEM((2, nc + n, tq), F32), pltpu.VMEM((2, 1, tq), F32),
                        pltpu.VMEM((2, nc + n, tq), F32), pltpu.VMEM((2, 1, tq), F32)],
        compiler_params=_cparams(("arbitrary",)),
        name="attn",
    )(q, k, kc, vt, vct, lamv, g_col)


def _outproj_kernel(att_ref, p_ref, pprev_ref, pnext_ref, x_ref, mod_ref, wpool_ref, bpool_ref,
                    pscale_ref, wout_ref, gpost_ref, gpre_ref, wr_ref,
                    x1_ref, h2_ref, h2t_ref, lg_ref, *, tm, n):
    i = pl.program_id(1)
    p = p_ref[0].astype(F32)
    prev = jnp.where(i > 0, pprev_ref[0].astype(F32), 0.0)
    nxt = jnp.where(i < pl.num_programs(1) - 1, pnext_ref[0].astype(F32), 0.0)
    ext = jnp.concatenate([prev, p, nxt], axis=0)
    rows = tm + 2 * HALO
    trow = (i * tm + lax.broadcasted_iota(jnp.int32, (tm, 1), 0))

    pooled = []
    sums = {}
    acc = ext + pltpu.roll(ext, 1, 0)
    sums[2] = acc
    w = 2
    while w < POOL_WINDOWS[-1]:
        acc = pltpu.roll(acc, rows - w // 2, 0) + pltpu.roll(acc, w // 2, 0)
        w = 2 * w
        sums[w] = acc
    for gi, win in enumerate(POOL_WINDOWS):
        sl = slice(gi * POOL_GROUP, (gi + 1) * POOL_GROUP)
        wsum = sums[win][HALO:HALO + tm, sl]
        hi_c = jnp.minimum(trow + (win - win // 2), n)
        lo_c = jnp.maximum(trow - win // 2, 0)
        cnt = (hi_c - lo_c).astype(F32)
        d = wsum / cnt - p[:, sl]
        y = _dot(d.astype(BF16), wpool_ref[gi]) + bpool_ref[:, sl]
        pooled.append((y * pscale_ref[:, sl]).astype(BF16))
    pool = jnp.concatenate(pooled, axis=1)
    yx = _dot(att_ref[0], wout_ref[0:ATTN_WIDTH, :]) + _dot(pool, wout_ref[ATTN_WIDTH:, :])
    x1 = x_ref[0] + mod_ref[0, 2:3, :] * _rms(yx, gpost_ref[...])
    x1_ref[0] = x1
    h2 = _rms(x1, gpre_ref[...]) * (1.0 + mod_ref[0, 4:5, :]) + mod_ref[0, 3:4, :]
    h2_ref[0] = h2.astype(BF16)
    for k in range(CHUNKS):
        h2t_ref[0, pl.ds(k, tm, stride=CHUNKS), :] = h2[:, k * LANES:(k + 1) * LANES]
    lg_ref[...] = _dot3(wr_ref[...], h2, nt=True)


def _outproj(att, p, x, mod3, w_pool_bf, b_pool, pool_scale, w_out_bf, g_post, g_pre, w_router_t, tm=512):
    b, n, d = x.shape
    hb = tm // HALO
    nhb = n // HALO
    row = lambda bi, i: (bi, i, 0)
    vec = lambda bi, i: (0, 0)
    kern = functools.partial(_outproj_kernel, tm=tm, n=n)
    return pl.pallas_call(
        kern,
        grid=(b, n // tm),
        in_specs=[pl.BlockSpec((1, tm, ATTN_WIDTH), row),
                  pl.BlockSpec((1, tm, POOL_WIDTH), row),
                  pl.BlockSpec((1, HALO, POOL_WIDTH), lambda bi, i: (bi, jnp.maximum(i * hb - 1, 0), 0)),
                  pl.BlockSpec((1, HALO, POOL_WIDTH), lambda bi, i: (bi, jnp.minimum((i + 1) * hb, nhb - 1), 0)),
                  pl.BlockSpec((1, tm, d), row),
                  pl.BlockSpec((1, 6, d), lambda bi, i: (bi, 0, 0)),
                  pl.BlockSpec((len(POOL_WINDOWS), POOL_GROUP, POOL_GROUP), lambda bi, i: (0, 0, 0)),
                  pl.BlockSpec((1, POOL_WIDTH), vec),
                  pl.BlockSpec((1, POOL_WIDTH), vec),
                  pl.BlockSpec((d, d), vec),
                  pl.BlockSpec((1, d), vec),
                  pl.BlockSpec((1, d), vec),
                  pl.BlockSpec((N_EXPERTS, d), vec)],
        out_specs=[pl.BlockSpec((1, tm, d), row),
                   pl.BlockSpec((1, tm, d), row),
                   pl.BlockSpec((1, tm * CHUNKS, LANES), row),
                   pl.BlockSpec((N_EXPERTS, tm), lambda bi, i: (0, bi * (n // tm) + i))],
        out_shape=[jax.ShapeDtypeStruct((b, n, d), F32),
                   jax.ShapeDtypeStruct((b, n, d), BF16),
                   jax.ShapeDtypeStruct((b, n * CHUNKS, LANES), F32),
                   jax.ShapeDtypeStruct((N_EXPERTS, b * n), F32)],
        compiler_params=_cparams(("arbitrary", "arbitrary")),
        name="outproj",
    )(att, p, p, p, x, mod3, w_pool_bf, b_pool, pool_scale, w_out_bf, g_post, g_pre, w_router_t)


def _beats(a, b, a_first):
    return jnp.where((a >= b) if a_first else (a > b), 1.0, 0.0)


def _route_kernel(lg_ref, bias_ref, w_ref):
    scores = [1.0 / (1.0 + jnp.exp(-lg_ref[e])) for e in range(N_EXPERTS)]
    biased = [scores[e] + bias_ref[e] for e in range(N_EXPERTS)]
    gscore = []
    for g in range(N_GROUPS):
        vals = biased[g * GROUP_SIZE:(g + 1) * GROUP_SIZE]
        m1 = vals[0]
        m2 = jnp.full_like(m1, -jnp.inf)
        for v in vals[1:]:
            m2 = jnp.maximum(m2, jnp.minimum(m1, v))
            m1 = jnp.maximum(m1, v)
        gscore.append(m1 + m2)
    gsel = []
    for g in range(N_GROUPS):
        rank = jnp.zeros_like(gscore[g])
        for g2 in range(N_GROUPS):
            if g2 != g:
                rank = rank + _beats(gscore[g2], gscore[g], g2 < g)
        gsel.append(rank < TOPK_GROUPS)
    vals = [jnp.where(gsel[e // GROUP_SIZE], biased[e], -jnp.inf) for e in range(N_EXPERTS)]
    chosen = [jnp.zeros(vals[0].shape, jnp.bool_) for _ in range(N_EXPERTS)]
    for _ in range(TOP_K):
        top = vals[0]
        for v in vals[1:]:
            top = jnp.maximum(top, v)
        found = jnp.zeros(top.shape, jnp.bool_)
        for e in range(N_EXPERTS):
            hit = (vals[e] == top) & jnp.logical_not(found)
            found = found | hit
            chosen[e] = chosen[e] | hit
            vals[e] = jnp.where(hit, -jnp.inf, vals[e])
    picked = [jnp.where(chosen[e], scores[e], 0.0) for e in range(N_EXPERTS)]
    denom = picked[0]
    for e in range(1, N_EXPERTS):
        denom = denom + picked[e]
    for e in range(N_EXPERTS):
        w_ref[e] = picked[e] / denom * ROUTED_SCALE


def _route(logits3, bias3, rows=SUBLANES):
    e, r, l = logits3.shape
    return pl.pallas_call(
        _route_kernel,
        grid=(r // rows,),
        in_specs=[pl.BlockSpec((e, rows, l), lambda i: (0, i, 0)),
                  pl.BlockSpec((e, 1, l), lambda i: (0, 0, 0))],
        out_specs=pl.BlockSpec((e, rows, l), lambda i: (0, i, 0)),
        out_shape=jax.ShapeDtypeStruct((e, r, l), F32),
        compiler_params=_cparams(("arbitrary",)),
        name="route",
    )(logits3, bias3)


CHUNKS = D_MODEL // LANES
SLAB = SLOT_TILE + 4
ROW_BATCH = 8


def _gather_tile(row_ref, tile, xs_ref, tx_ref):
    for s in range(SLOT_TILE):
        row = pl.multiple_of(row_ref[0, tile, s], CHUNKS)
        tx_ref[pl.ds(s, CHUNKS, stride=SLAB), :] = xs_ref[pl.ds(row, CHUNKS), :]


def _experts_step(j, row_ref, sw_ref, wgu_buf, wd_buf, xs_ref, acc_ref,
                  tx_cur, tx_nxt, ty_cur, ty_prv):
    ntile = row_ref.shape[1]
    _gather_tile(row_ref, jnp.minimum(j + 1, ntile - 1), xs_ref, tx_nxt)
    xb = jnp.concatenate([tx_cur[pl.ds(k * SLAB, SLOT_TILE), :] for k in range(CHUNKS)], axis=1).astype(BF16)
    gu = _dot(xb, wgu_buf[...])
    f = wd_buf.shape[0]
    h = (_silu(gu[:, 0:f]) * gu[:, f:]).astype(BF16)
    y = _dot(h, wd_buf[...])
    w_rows = jnp.transpose(jnp.broadcast_to(sw_ref[0, pl.ds(j, 1), :], (LANES, SLOT_TILE)))
    for k in range(CHUNKS):
        ty_cur[pl.ds(k * SLAB, SLOT_TILE), :] = y[:, k * LANES:(k + 1) * LANES] * w_rows
    prv = jnp.maximum(j - 1, 0)
    for s0 in range(0, SLOT_TILE, ROW_BATCH):
        new = []
        for s in range(s0, s0 + ROW_BATCH):
            row = pl.multiple_of(row_ref[0, prv, s], CHUNKS)
            new.append((row, acc_ref[pl.ds(row, CHUNKS), :] + ty_prv[pl.ds(s, CHUNKS, stride=SLAB), :]))
        for row, v in new:
            acc_ref[pl.ds(row, CHUNKS), :] = v


def _experts_kernel(be_ref, nt_ref, first_ref, nxt_ref, ws_ref, row_ref, sw_ref, x_hbm,
                    wg_hbm, wu_hbm, wd_hbm, o_hbm,
                    xs_ref, acc_ref, tx0_ref, tx1_ref, ty0_ref, ty1_ref, wg_st, wu_st, wd_st,
                    wgu_buf, wd_buf, sem, wsem, *, ntile, n):
    c = pl.program_id(0)
    j = pl.program_id(1)
    t = c * ntile + j
    rows = n * CHUNKS
    wslot = ws_ref[t]

    def weight_copies(e, slot):
        return [pltpu.make_async_copy(src.at[e], dst.at[slot], wsem.at[slot])
                for src, dst in ((wg_hbm, wg_st), (wu_hbm, wu_st), (wd_hbm, wd_st))]

    @pl.when(j == 0)
    def _():
        for wcp in weight_copies(be_ref[t], 0):
            wcp.start()
        cp = pltpu.make_async_copy(x_hbm.at[c], xs_ref.at[pl.ds(0, rows)], sem.at[0])
        cp.start()
        xs_ref[pl.ds(rows, CHUNKS), :] = jnp.zeros((CHUNKS, LANES), F32)
        acc_ref[...] = jnp.zeros_like(acc_ref)
        ty1_ref[...] = jnp.zeros_like(ty1_ref)
        cp.wait()
        _gather_tile(row_ref, 0, xs_ref, tx0_ref)

    @pl.when(first_ref[t] == 1)
    def _():
        for wcp in weight_copies(be_ref[t], wslot):
            wcp.wait()

    @pl.when((first_ref[t] == 1) & (nxt_ref[t] >= 0))
    def _():
        for wcp in weight_copies(nxt_ref[t], 1 - wslot):
            wcp.start()

    @pl.when(first_ref[t] == 1)
    def _():
        f = wd_buf.shape[0]
        wgu_buf[:, 0:f] = wg_st[wslot].astype(BF16)
        wgu_buf[:, f:] = wu_st[wslot].astype(BF16)
        wd_buf[...] = wd_st[wslot].astype(BF16)

    live = j <= nt_ref[c]
    args = (j, row_ref, sw_ref, wgu_buf, wd_buf, xs_ref, acc_ref)

    @pl.when(live & (lax.rem(j, 2) == 0))
    def _():
        _experts_step(*args, tx0_ref, tx1_ref, ty0_ref, ty1_ref)

    @pl.when(live & (lax.rem(j, 2) == 1))
    def _():
        _experts_step(*args, tx1_ref, tx0_ref, ty1_ref, ty0_ref)

    @pl.when(j == ntile - 1)
    def _():
        cp = pltpu.make_async_copy(acc_ref.at[pl.ds(0, rows)], o_hbm.at[c], sem.at[1])
        cp.start()
        cp.wait()


def _experts(block_e, ntiles, first, nxt_e, wslot, slot_row, slot_w, h2t, w_gate, w_up, w_down, ntile, n):
    b = h2t.shape[0]
    d, f = w_gate.shape[1], w_gate.shape[2]
    batch = lambda c, j, *_: (c, 0, 0)
    kern = functools.partial(_experts_kernel, ntile=ntile, n=n)
    slab = pltpu.VMEM((CHUNKS * SLAB, LANES), F32)
    hbm = pl.BlockSpec(memory_space=pl.ANY)
    grid_spec = pltpu.PrefetchScalarGridSpec(
        num_scalar_prefetch=5,
        grid=(b, ntile),
        in_specs=[pl.BlockSpec((1, ntile, SLOT_TILE), batch, memory_space=pltpu.SMEM),
                  pl.BlockSpec((1, ntile, SLOT_TILE), batch),
                  hbm, hbm, hbm, hbm],
        out_specs=hbm,
        scratch_shapes=[pltpu.VMEM(((n + 1) * CHUNKS, LANES), F32),
                        pltpu.VMEM(((n + 1) * CHUNKS, LANES), F32),
                        slab, slab, slab, slab,
                        pltpu.VMEM((2, d, f), F32), pltpu.VMEM((2, d, f), F32), pltpu.VMEM((2, f, d), F32),
                        pltpu.VMEM((d, 2 * f), BF16), pltpu.VMEM((f, d), BF16),
                        pltpu.SemaphoreType.DMA((2,)), pltpu.SemaphoreType.DMA((2,))],
    )
    return pl.pallas_call(
        kern,
        grid_spec=grid_spec,
        out_shape=jax.ShapeDtypeStruct((b, n * CHUNKS, LANES), F32),
        compiler_params=_cparams(("arbitrary", "arbitrary")),
        name="experts",
    )(block_e, ntiles, first, nxt_e, wslot, slot_row, slot_w, h2t, w_gate, w_up, w_down)


PLAN_GROUP = 4


def _plan_tables_kernel(w_ref, re_ref, ro_ref, tab_ref, *, rows):
    ne = w_ref.shape[0]
    w2 = w_ref[...].reshape(ne * rows, LANES)
    selb = jnp.where(w2 > 0.0, 1.0, 0.0).astype(BF16)
    i0 = lax.broadcasted_iota(jnp.int32, (LANES, LANES), 0)
    i1 = lax.broadcasted_iota(jnp.int32, (LANES, LANES), 1)
    rowtot = _dot(selb, jnp.ones((LANES, LANES), BF16))
    ridx = lax.broadcasted_iota(jnp.int32, rowtot.shape, 0) & (rows - 1)
    acc = rowtot
    sh = 1
    while sh < rows:
        acc = acc + jnp.where(ridx >= sh, pltpu.roll(acc, sh, 0), 0.0)
        sh *= 2
    re_ref[0] = acc
    ro_ref[0] = acc - rowtot
    incl_t = _dot_nt(jnp.where(i1 <= i0, 1.0, 0.0).astype(BF16), selb)
    eye = jnp.where(i0 == i1, 1.0, 0.0).astype(BF16)
    w_hi = w2.astype(BF16)
    r1 = w2 - w_hi.astype(F32)
    w_mid = r1.astype(BF16)
    w_lo = (r1 - w_mid.astype(F32)).astype(BF16)
    parts = [_dot_nt(eye, p) for p in (w_hi, w_mid, w_lo)]
    for g in range(ne * rows // LANES):
        sl = slice(g * LANES, (g + 1) * LANES)
        tab_ref[0, g, 0:LANES, :] = incl_t[:, sl].astype(BF16)
        for k in range(3):
            tab_ref[0, g, (k + 1) * LANES:(k + 2) * LANES, :] = parts[k][:, sl].astype(BF16)


def _plan_tables(selw3, b):
    ne, r_all, _ = selw3.shape
    rows = r_all // b
    groups = ne * rows // LANES
    kern = functools.partial(_plan_tables_kernel, rows=rows)
    return pl.pallas_call(
        kern,
        grid=(b,),
        in_specs=[pl.BlockSpec((ne, rows, LANES), lambda c: (0, c, 0))],
        out_specs=[pl.BlockSpec((1, ne * rows, LANES), lambda c: (c, 0, 0)),
                   pl.BlockSpec((1, ne * rows, LANES), lambda c: (c, 0, 0)),
                   pl.BlockSpec((1, groups, 4 * LANES, LANES), lambda c: (c, 0, 0, 0))],
        out_shape=[jax.ShapeDtypeStruct((b, ne * rows, LANES), F32),
                   jax.ShapeDtypeStruct((b, ne * rows, LANES), F32),
                   jax.ShapeDtypeStruct((b, groups, 4 * LANES, LANES), BF16)],
        compiler_params=_cparams(("arbitrary",)),
        name="plan_tables",
    )(selw3)


def _plan_slots_kernel(be_ref, q0_ref, nt_ref, re_ref, ro_ref, tab_ref, tok_ref, sw_ref,
                       *, ntile, rows, n):
    c = pl.program_id(0)
    tok_ref[...] = jnp.full(tok_ref.shape, n * CHUNKS, jnp.int32)
    sw_ref[...] = jnp.zeros(sw_ref.shape, F32)
    lane = lax.broadcasted_iota(jnp.int32, (1, SLOT_TILE), 1).astype(F32)
    sub_r = lax.broadcasted_iota(jnp.int32, (rows, SLOT_TILE), 0).astype(F32)
    sub_l = lax.broadcasted_iota(jnp.int32, (LANES, SLOT_TILE), 0).astype(F32)
    reps = SLOT_TILE // LANES

    def tile_body(j):
        e = be_ref[c * ntile + j]
        q = q0_ref[c * ntile + j].astype(F32) + lane
        base = pl.multiple_of(e * rows, rows)
        row_end = jnp.concatenate([re_ref[0, pl.ds(base, rows), :]] * reps, axis=1)
        row_off = jnp.concatenate([ro_ref[0, pl.ds(base, rows), :]] * reps, axis=1)
        r = jnp.sum(jnp.where(row_end <= q, 1.0, 0.0), axis=0, keepdims=True)
        row_start = jnp.sum(jnp.where(sub_r == r, row_off, 0.0), axis=0, keepdims=True)
        per_group = LANES // rows
        g = lax.shift_right_logical(e, per_group.bit_length() - 1)
        col = r + ((e & (per_group - 1)) * rows).astype(F32)
        onehot = jnp.where(sub_l == col, 1.0, 0.0).astype(BF16)
        picked = _dot(tab_ref[0, g], onehot)
        incl = picked[0:LANES]
        lane_idx = jnp.sum(jnp.where(incl <= q - row_start, 1.0, 0.0), axis=0, keepdims=True)
        wrow = (picked[LANES:2 * LANES] + picked[2 * LANES:3 * LANES]) + picked[3 * LANES:]
        wsel = jnp.sum(jnp.where(sub_l == lane_idx, wrow, 0.0), axis=0, keepdims=True)
        valid = (q < row_end[rows - 1:rows, :]) & (j < nt_ref[c])
        tok = jnp.where(valid, r * LANES + lane_idx, float(n)).astype(jnp.int32)
        tok_ref[0, pl.ds(j, 1), :] = tok * CHUNKS
        sw_ref[0, pl.ds(j, 1), :] = jnp.where(valid, wsel, 0.0)

    def tile_group(i, carry):
        for k in range(PLAN_GROUP):
            tile_body(PLAN_GROUP * i + k)
        return carry

    lax.fori_loop(0, (nt_ref[c] + PLAN_GROUP - 1) // PLAN_GROUP, tile_group, 0)


def _plan_slots(block_e, tile_q0, ntiles, row_end, row_off, tables, ntile, n):
    b, er, _ = row_end.shape
    rows = n // LANES
    per_group = LANES // rows
    assert rows * per_group == LANES and per_group & (per_group - 1) == 0 and ntile % PLAN_GROUP == 0
    groups = tables.shape[1]
    kern = functools.partial(_plan_slots_kernel, ntile=ntile, rows=rows, n=n)
    grid_spec = pltpu.PrefetchScalarGridSpec(
        num_scalar_prefetch=3,
        grid=(b,),
        in_specs=[pl.BlockSpec((1, er, LANES), lambda c, *_: (c, 0, 0)),
                  pl.BlockSpec((1, er, LANES), lambda c, *_: (c, 0, 0)),
                  pl.BlockSpec((1, groups, 4 * LANES, LANES), lambda c, *_: (c, 0, 0, 0))],
        out_specs=[pl.BlockSpec((1, ntile, SLOT_TILE), lambda c, *_: (c, 0, 0)),
                   pl.BlockSpec((1, ntile, SLOT_TILE), lambda c, *_: (c, 0, 0))],
    )
    return pl.pallas_call(
        kern,
        grid_spec=grid_spec,
        out_shape=[jax.ShapeDtypeStruct((b, ntile, SLOT_TILE), jnp.int32),
                   jax.ShapeDtypeStruct((b, ntile, SLOT_TILE), F32)],
        compiler_params=_cparams(("arbitrary",)),
        name="plan_slots",
    )(block_e, tile_q0, ntiles, row_end, row_off, tables)


def _tile_meta(row_end, b, n, ntile):
    rows = n // LANES
    cnt = row_end.reshape(b, N_EXPERTS, rows, LANES)[:, :, rows - 1, 0].astype(jnp.int32)
    nt_e = (cnt + SLOT_TILE - 1) // SLOT_TILE
    tend = jnp.cumsum(nt_e, axis=1)
    tstart = tend - nt_e
    ntiles = tend[:, -1]
    tile_id = jnp.arange(ntile, dtype=jnp.int32)
    tid = jnp.minimum(tile_id[None, :], ntiles[:, None] - 1)
    be = jnp.minimum(jnp.sum(tend[:, None, :] <= tid[:, :, None], axis=2).astype(jnp.int32), N_EXPERTS - 1)
    eid = jnp.arange(N_EXPERTS, dtype=jnp.int32)
    is_be = be[:, :, None] == eid[None, None, :]
    pick = lambda per_expert: jnp.sum(jnp.where(is_be, per_expert[:, None, :], 0), axis=2)
    q0 = (tid - pick(tstart)) * SLOT_TILE
    valid = tile_id[None, :] < ntiles[:, None]
    prev_be = jnp.concatenate([jnp.full((b, 1), -1, jnp.int32), be[:, :-1]], axis=1)
    first = (valid & (be != prev_be)).astype(jnp.int32)
    wslot = (jnp.cumsum(first, axis=1) - 1) % 2
    later = (nt_e > 0)[:, None, :] & (eid[None, None, :] > eid[None, :, None])
    nxt = jnp.min(jnp.where(later, eid[None, None, :], N_EXPERTS), axis=2)
    nxt_e = pick(jnp.where(nxt < N_EXPERTS, nxt, -1))
    flat = lambda a: a.reshape(-1).astype(jnp.int32)
    return flat(be), flat(q0), ntiles.astype(jnp.int32), flat(first), flat(nxt_e), flat(wslot)


def _dispatch_plan(selw3, b, n, ntile):
    row_end, row_off, tables = _plan_tables(selw3, b)
    block_e, tile_q0, ntiles, first, nxt_e, wslot = _tile_meta(row_end, b, n, ntile)
    slot_tok, slot_w = _plan_slots(block_e, tile_q0, ntiles, row_end, row_off, tables, ntile, n)
    return block_e, ntiles, first, nxt_e, wslot, slot_tok, slot_w


def _final_kernel(h2_ref, r_ref, x1_ref, mod_ref, wg_ref, wu_ref, wd_ref, g_ref, o_ref):
    hb = h2_ref[0]
    tm = hb.shape[0]
    sh = _dot((_silu(_dot(hb, wg_ref[...])) * _dot(hb, wu_ref[...])).astype(BF16), wd_ref[...])
    routed = jnp.concatenate([r_ref[0, pl.ds(k, tm, stride=CHUNKS), :] for k in range(CHUNKS)], axis=1)
    moe = routed + sh
    o_ref[0] = x1_ref[0] + mod_ref[0, 5:6, :] * _rms(moe, g_ref[...])


def _final(h2, routed, x1, mod3, wsg_bf, wsu_bf, wsd_bf, g_post, tm=512):
    b, n, d = x1.shape
    f = wsg_bf.shape[1]
    row = lambda bi, i: (bi, i, 0)
    vec = lambda bi, i: (0, 0)
    return pl.pallas_call(
        _final_kernel,
        grid=(b, n // tm),
        in_specs=[pl.BlockSpec((1, tm, d), row),
                  pl.BlockSpec((1, tm * CHUNKS, LANES), row),
                  pl.BlockSpec((1, tm, d), row),
                  pl.BlockSpec((1, 6, d), lambda bi, i: (bi, 0, 0)),
                  pl.BlockSpec((d, f), vec),
                  pl.BlockSpec((d, f), vec),
                  pl.BlockSpec((f, d), vec),
                  pl.BlockSpec((1, d), vec)],
        out_specs=pl.BlockSpec((1, tm, d), row),
        out_shape=jax.ShapeDtypeStruct((b, n, d), F32),
        compiler_params=_cparams(("arbitrary", "arbitrary")),
        name="final",
    )(h2, routed, x1, mod3, wsg_bf, wsu_bf, wsd_bf, g_post)


def _rope_tables(n):
    t = np.arange(n)
    row = (t // GRID_W).astype(np.float32)
    col = (t % GRID_W).astype(np.float32)
    freqs = np.float32(ROPE_THETA) ** (-np.arange(0, AXIS_DIM, 2, dtype=np.float32) / np.float32(AXIS_DIM))
    ar = row[:, None] * freqs
    ac = col[:, None] * freqs
    zeros = np.zeros_like(ar)
    cos64 = np.concatenate([np.cos(ar), np.cos(ar), np.cos(ac), np.cos(ac)], axis=1)
    sa64 = np.concatenate([-np.sin(ar), zeros, -np.sin(ac), zeros], axis=1)
    sb64 = np.concatenate([zeros, np.sin(ar), zeros, np.sin(ac)], axis=1)
    two = lambda a: jnp.asarray(np.concatenate([a, a], axis=1).astype(np.float32))
    return two(cos64), two(sa64), two(sb64)


def kernel(x, c, ctx, c_ctx, w_ada, b_ada, g_pre_mix, g_post_mix, g_pre_ffn, g_post_ffn, w_in, lambda_q1, lambda_k1, lambda_q2, lambda_k2, g_subln, w_pool, b_pool, pool_scale, w_out, w_router, router_bias, w_e_gate, w_e_up, w_e_down, w_sh_gate, w_sh_up, w_sh_down):
    b, n, d = x.shape
    l = 0
    mod_rows = SUBLANES
    cin = jnp.concatenate([c, c_ctx[None, :], jnp.zeros((mod_rows - b - 1, d), F32)], axis=0)
    mod3 = _ada(cin, w_ada[l], b_ada[l]).reshape(mod_rows, 6, d)

    w_in_bf = w_in[l].astype(BF16)
    cos, sa, sb = _rope_tables(n)
    q, k, vt, p = _inproj(x, mod3, g_pre_mix[l][None, :], w_in_bf, cos * (QK_DIM ** -0.5 * LOG2E), cos, sa, sb)
    kc, vct = _ctxkv(ctx, mod3, g_pre_mix[l][None, :], w_in_bf, b)

    lamv = jnp.stack([lambda_q1[l], lambda_k1[l], lambda_q2[l], lambda_k2[l]], axis=0)
    att = _attn(q, k, kc, vt, vct, lamv, g_subln[l][:, None])

    x1, h2, h2t, logits_t = _outproj(
        att, p, x, mod3, w_pool[l].astype(BF16), b_pool[l][None, :], pool_scale[l][None, :],
        w_out[l].astype(BF16), g_post_mix[l][None, :], g_pre_ffn[l][None, :], w_router[l].T)

    t = b * n
    selw3 = _route(logits_t.reshape(N_EXPERTS, t // LANES, LANES),
                   jnp.broadcast_to(router_bias[l][:, None, None], (N_EXPERTS, 1, LANES)))

    ntile = (n * TOP_K) // SLOT_TILE + N_EXPERTS
    block_e, ntiles, first, nxt_e, wslot, slot_row, slot_w = _dispatch_plan(selw3, b, n, ntile)
    routed = _experts(block_e, ntiles, first, nxt_e, wslot, slot_row, slot_w, h2t,
                      w_e_gate[l], w_e_up[l], w_e_down[l], ntile, n)

    return _final(h2, routed, x1, mod3, w_sh_gate[l].astype(BF16), w_sh_up[l].astype(BF16),
                  w_sh_down[l].astype(BF16), g_post_ffn[l][None, :])
```

```python
import functools
import math

import jax
import jax.numpy as jnp
import numpy as np
from jax import lax
from jax.experimental import pallas as pl
from jax.experimental.pallas import tpu as pltpu

F32 = jnp.float32
BF16 = jnp.bfloat16

D_MODEL = 1024
GRID_W = 64
N_HEADS = 4
QK_DIM = 64
V_DIM = 128
ATTN_WIDTH = N_HEADS * V_DIM
POOL_WIDTH = D_MODEL - ATTN_WIDTH
POOL_WINDOWS = (2, 4, 8, 16)
POOL_GROUP = POOL_WIDTH // len(POOL_WINDOWS)
AXIS_DIM = QK_DIM // 2
ROPE_THETA = 10000.0
N_EXPERTS = 64
TOP_K = 8
N_GROUPS = 8
GROUP_SIZE = N_EXPERTS // N_GROUPS
TOPK_GROUPS = 4
EXPERT_DIM = 256
ROUTED_SCALE = 2.5
EPS = 1e-6
LAM_INIT = 0.8 - 0.6 * math.exp(-0.3 * 0)

LANES = 128
SUBLANES = 8
SLOT_TILE = 256
HALO = 16
VMEM_LIMIT = 56 * 1024 * 1024


def _cparams(sem, vmem=VMEM_LIMIT):
    return pltpu.CompilerParams(dimension_semantics=sem, vmem_limit_bytes=vmem)


def _split(a):
    hi = a.astype(BF16)
    lo = (a - hi.astype(F32)).astype(BF16)
    return hi, lo


def _dot(a, b):
    return jnp.dot(a, b, preferred_element_type=F32)


def _dot_nt(a, b):
    return lax.dot_general(a, b, (((1,), (1,)), ((), ())), preferred_element_type=F32)


def _dot3(a, b, nt=False):
    d = _dot_nt if nt else _dot
    ah, al = _split(a)
    bh, bl = _split(b)
    return d(ah, bh) + d(ah, bl) + d(al, bh)


def _rms(x, g):
    return x * lax.rsqrt(jnp.mean(x * x, axis=-1, keepdims=True) + EPS) * g


def _silu(x):
    return x * (1.0 / (1.0 + jnp.exp(-x)))


def _ada_kernel(c_ref, w_ref, b_ref, o_ref):
    o_ref[...] = _dot3(_silu(c_ref[...]), w_ref[...]) + b_ref[...]


def _ada(cin, w_ada, b_ada):
    rows, d = cin.shape
    n = w_ada.shape[1]
    tn = 1024
    return pl.pallas_call(
        _ada_kernel,
        grid=(n // tn,),
        in_specs=[pl.BlockSpec((rows, d), lambda j: (0, 0)),
                  pl.BlockSpec((d, tn), lambda j: (0, j)),
                  pl.BlockSpec((1, tn), lambda j: (0, j))],
        out_specs=pl.BlockSpec((rows, tn), lambda j: (0, j)),
        out_shape=jax.ShapeDtypeStruct((rows, n), F32),
        compiler_params=_cparams(("arbitrary",)),
        name="ada",
    )(cin, w_ada, b_ada.reshape(1, n))


def _rope(t, cos, sa, sb):
    return t * cos + pltpu.roll(t, LANES - AXIS_DIM // 2, 1) * sa + pltpu.roll(t, AXIS_DIM // 2, 1) * sb


def _inproj_kernel(x_ref, mod_ref, g_ref, w_ref, cq_ref, ck_ref, sa_ref, sb_ref,
                   q_ref, k_ref, vt_ref, p_ref):
    x = x_ref[0]
    h = _rms(x, g_ref[...]) * (1.0 + mod_ref[0, 1:2, :]) + mod_ref[0, 0:1, :]
    px = _dot(h.astype(BF16), w_ref[...])
    ck = ck_ref[...]
    sa = sa_ref[...]
    sb = sb_ref[...]
    cq = cq_ref[...]
    scale = QK_DIM ** -0.5 * LOG2E
    saq = sa * scale
    sbq = sb * scale
    for hh in range(N_HEADS):
        lo = hh * LANES
        q = px[:, lo:lo + LANES]
        k = px[:, ATTN_WIDTH + lo:ATTN_WIDTH + lo + LANES]
        q_ref[0, :, lo:lo + LANES] = _rope(q, cq, saq, sbq).astype(BF16)
        k_ref[0, :, lo:lo + LANES] = _rope(k, ck, sa, sb).astype(BF16)
    vt_ref[0] = jnp.transpose(px[:, 2 * ATTN_WIDTH:3 * ATTN_WIDTH]).astype(BF16)
    p_ref[0] = px[:, 3 * ATTN_WIDTH:].astype(BF16)


def _inproj(x, mod3, g, w_in_bf, cq, ck, sa, sb, tm=1024):
    b, n, d = x.shape
    width = w_in_bf.shape[1]
    row = lambda bi, i: (bi, i, 0)
    tab = pl.BlockSpec((tm, LANES), lambda bi, i: (i, 0))
    out = jax.ShapeDtypeStruct((b, n, ATTN_WIDTH), BF16)
    return pl.pallas_call(
        _inproj_kernel,
        grid=(b, n // tm),
        in_specs=[pl.BlockSpec((1, tm, d), row),
                  pl.BlockSpec((1, 6, d), lambda bi, i: (bi, 0, 0)),
                  pl.BlockSpec((1, d), lambda bi, i: (0, 0)),
                  pl.BlockSpec((d, width), lambda bi, i: (0, 0)),
                  tab, tab, tab, tab],
        out_specs=[pl.BlockSpec((1, tm, ATTN_WIDTH), row),
                   pl.BlockSpec((1, tm, ATTN_WIDTH), row),
                   pl.BlockSpec((1, ATTN_WIDTH, tm), lambda bi, i: (bi, 0, i)),
                   pl.BlockSpec((1, tm, ATTN_WIDTH), row)],
        out_shape=[out, out, jax.ShapeDtypeStruct((b, ATTN_WIDTH, n), BF16), out],
        compiler_params=_cparams(("arbitrary", "arbitrary")),
        name="inproj",
    )(x, mod3, g, w_in_bf, cq, ck, sa, sb)


def _ctxkv_kernel(x_ref, mod_ref, g_ref, wk_ref, wv_ref, k_ref, vt_ref):
    h = _rms(x_ref[0], g_ref[...]) * (1.0 + mod_ref[0, 1:2, :]) + mod_ref[0, 0:1, :]
    hb = h.astype(BF16)
    k_ref[0] = _dot(hb, wk_ref[...]).astype(BF16)
    vt_ref[0] = jnp.transpose(_dot(hb, wv_ref[...])).astype(BF16)


def _ctxkv(ctx, mod3, g, w_in_bf, ctx_row):
    b, n, d = ctx.shape
    out = jax.ShapeDtypeStruct((b, n, ATTN_WIDTH), BF16)
    return pl.pallas_call(
        _ctxkv_kernel,
        grid=(b,),
        in_specs=[pl.BlockSpec((1, n, d), lambda bi: (bi, 0, 0)),
                  pl.BlockSpec((1, 6, d), lambda bi: (ctx_row, 0, 0)),
                  pl.BlockSpec((1, d), lambda bi: (0, 0)),
                  pl.BlockSpec((d, ATTN_WIDTH), lambda bi: (0, 1)),
                  pl.BlockSpec((d, ATTN_WIDTH), lambda bi: (0, 2))],
        out_specs=[pl.BlockSpec((1, n, ATTN_WIDTH), lambda bi: (bi, 0, 0)),
                   pl.BlockSpec((1, ATTN_WIDTH, n), lambda bi: (bi, 0, 0))],
        out_shape=[out, jax.ShapeDtypeStruct((b, ATTN_WIDTH, n), BF16)],
        compiler_params=_cparams(("arbitrary",)),
        name="ctxkv",
    )(ctx, mod3, g, w_in_bf, w_in_bf)


LOG2E = math.log2(math.e)
KEY_CHUNK = 256


def _attn_step(q_ref, kx_ref, kc_ref, vt_ref, vct_ref, lam_ref, g_ref, o_ref, s_new, m_new, s_old, m_old):
    nc = kc_ref.shape[1]
    tq = q_ref.shape[1]
    nk = s_old.shape[1]
    groups = KEY_CHUNK // SUBLANES
    q = q_ref[0]
    lane = lax.broadcasted_iota(jnp.int32, q.shape, 1)
    zero = jnp.zeros_like(q)
    qms = (jnp.where(lane < QK_DIM, q, zero), jnp.where(lane >= QK_DIM, q, zero))
    m_prev = (m_old[0], m_old[1])
    top = [jnp.full((SUBLANES, tq), -jnp.inf, F32) for _ in range(2)]
    den = [jnp.zeros((SUBLANES, tq), F32) for _ in range(2)]
    acc = [jnp.zeros((V_DIM, tq), F32) for _ in range(2)]
    for c in range(nk // KEY_CHUNK):
        lo = c * KEY_CHUNK
        if lo < nc:
            keys, vt = kc_ref[0, lo:lo + KEY_CHUNK, :], vct_ref[0, :, lo:lo + KEY_CHUNK]
        else:
            keys, vt = kx_ref[0, lo - nc:lo - nc + KEY_CHUNK, :], vt_ref[0, :, lo - nc:lo - nc + KEY_CHUNK]
        for mp in range(2):
            s = _dot_nt(keys, qms[mp])
            s_new[mp, lo:lo + KEY_CHUNK, :] = s
            top[mp] = jnp.maximum(top[mp], jnp.max(s.reshape(groups, SUBLANES, tq), axis=0))
            p = jnp.exp2(s_old[mp, lo:lo + KEY_CHUNK, :] - m_prev[mp])
            den[mp] = den[mp] + jnp.sum(p.reshape(groups, SUBLANES, tq), axis=0)
            acc[mp] = acc[mp] + _dot(vt, p.astype(BF16))
    for mp in range(2):
        m_new[mp] = jnp.max(top[mp], axis=0, keepdims=True)
    lv = lam_ref[...]
    lam = (jnp.exp(jnp.sum(lv[0:1] * lv[1:2], axis=-1, keepdims=True))
           - jnp.exp(jnp.sum(lv[2:3] * lv[3:4], axis=-1, keepdims=True)) + LAM_INIT)
    l1 = jnp.sum(den[0], axis=0, keepdims=True)
    l2 = jnp.sum(den[1], axis=0, keepdims=True)
    o = acc[0] * (1.0 / l1) - acc[1] * (lam / l2)
    o = o * lax.rsqrt(jnp.mean(o * o, axis=0, keepdims=True) + EPS) * (g_ref[...] * (1.0 - LAM_INIT))
    o_ref[0] = jnp.transpose(o).astype(BF16)


def _attn_kernel(q_ref, kx_ref, kc_ref, vt_ref, vct_ref, lam_ref, g_ref, o_ref, s0_ref, m0_ref, s1_ref, m1_ref):
    t = pl.program_id(0)
    io = (q_ref, kx_ref, kc_ref, vt_ref, vct_ref, lam_ref, g_ref, o_ref)

    @pl.when(t == 0)
    def _():
        s1_ref[...] = jnp.zeros_like(s1_ref)
        m1_ref[...] = jnp.zeros_like(m1_ref)

    @pl.when(lax.rem(t, 2) == 0)
    def _():
        _attn_step(*io, s0_ref, m0_ref, s1_ref, m1_ref)

    @pl.when(lax.rem(t, 2) == 1)
    def _():
        _attn_step(*io, s1_ref, m1_ref, s0_ref, m0_ref)


def _attn(q, k, kc, vt, vct, lamv, g_col, tq=512):
    b, n, _ = q.shape
    nc = kc.shape[1]
    nq = n // tq
    total = b * N_HEADS * nq

    def cur(t):
        ta = jnp.minimum(t, total - 1)
        bh = ta // nq
        return bh // N_HEADS, bh % N_HEADS, ta % nq

    def prv(t):
        tb = jnp.maximum(t - 1, 0)
        bh = tb // nq
        return bh // N_HEADS, bh % N_HEADS, tb % nq

    def at(fn, order):
        return lambda t: tuple((fn(t) + (0,))[j] for j in order)

    return pl.pallas_call(
        _attn_kernel,
        grid=(total + 1,),
        in_specs=[pl.BlockSpec((1, tq, LANES), at(cur, (0, 2, 1))),
                  pl.BlockSpec((1, n, LANES), at(cur, (0, 3, 1))),
                  pl.BlockSpec((1, nc, LANES), at(cur, (0, 3, 1))),
                  pl.BlockSpec((1, V_DIM, n), at(prv, (0, 1, 3))),
                  pl.BlockSpec((1, V_DIM, nc), at(prv, (0, 1, 3))),
                  pl.BlockSpec((4, QK_DIM), lambda t: (0, 0)),
                  pl.BlockSpec((V_DIM, 1), lambda t: (0, 0))],
        out_specs=pl.BlockSpec((1, tq, LANES), at(prv, (0, 2, 1))),
        out_shape=jax.ShapeDtypeStruct((b, n, ATTN_WIDTH), BF16),
        scratch_shapes=[pltpu.VMEM((2, nc + n, tq), F32), pltpu.VMEM((2, 1, tq), F32),
                        pltpu.VMEM((2, nc + n, tq), F32), pltpu.VMEM((2, 1, tq), F32)],
        compiler_params=_cparams(("arbitrary",)),
        name="attn",
    )(q, k, kc, vt, vct, lamv, g_col)


def _outproj_kernel(att_ref, p_ref, pprev_ref, pnext_ref, x_ref, mod_ref, wpool_ref, bpool_ref,
                    pscale_ref, wout_ref, gpost_ref, gpre_ref, wr_ref,
                    x1_ref, h2_ref, h2t_ref, lg_ref, *, tm, n):
    i = pl.program_id(1)
    p = p_ref[0].astype(F32)
    prev = jnp.where(i > 0, pprev_ref[0].astype(F32), 0.0)
    nxt = jnp.where(i < pl.num_programs(1) - 1, pnext_ref[0].astype(F32), 0.0)
    ext = jnp.concatenate([prev, p, nxt], axis=0)
    rows = tm + 2 * HALO
    trow = (i * tm + lax.broadcasted_iota(jnp.int32, (tm, 1), 0))

    pooled = []
    sums = {}
    acc = ext + pltpu.roll(ext, 1, 0)
    sums[2] = acc
    w = 2
    while w < POOL_WINDOWS[-1]:
        acc = pltpu.roll(acc, rows - w // 2, 0) + pltpu.roll(acc, w // 2, 0)
        w = 2 * w
        sums[w] = acc
    for gi, win in enumerate(POOL_WINDOWS):
        sl = slice(gi * POOL_GROUP, (gi + 1) * POOL_GROUP)
        wsum = sums[win][HALO:HALO + tm, sl]
        hi_c = jnp.minimum(trow + (win - win // 2), n)
        lo_c = jnp.maximum(trow - win // 2, 0)
        cnt = (hi_c - lo_c).astype(F32)
        d = wsum / cnt - p[:, sl]
        y = _dot(d.astype(BF16), wpool_ref[gi]) + bpool_ref[:, sl]
        pooled.append((y * pscale_ref[:, sl]).astype(BF16))
    pool = jnp.concatenate(pooled, axis=1)
    yx = _dot(att_ref[0], wout_ref[0:ATTN_WIDTH, :]) + _dot(pool, wout_ref[ATTN_WIDTH:, :])
    x1 = x_ref[0] + mod_ref[0, 2:3, :] * _rms(yx, gpost_ref[...])
    x1_ref[0] = x1
    h2 = _rms(x1, gpre_ref[...]) * (1.0 + mod_ref[0, 4:5, :]) + mod_ref[0, 3:4, :]
    h2_ref[0] = h2.astype(BF16)
    for k in range(CHUNKS):
        h2t_ref[0, pl.ds(k, tm, stride=CHUNKS), :] = h2[:, k * LANES:(k + 1) * LANES]
    lg_ref[...] = _dot3(wr_ref[...], h2, nt=True)


def _outproj(att, p, x, mod3, w_pool_bf, b_pool, pool_scale, w_out_bf, g_post, g_pre, w_router_t, tm=512):
    b, n, d = x.shape
    hb = tm // HALO
    nhb = n // HALO
    row = lambda bi, i: (bi, i, 0)
    vec = lambda bi, i: (0, 0)
    kern = functools.partial(_outproj_kernel, tm=tm, n=n)
    return pl.pallas_call(
        kern,
        grid=(b, n // tm),
        in_specs=[pl.BlockSpec((1, tm, ATTN_WIDTH), row),
                  pl.BlockSpec((1, tm, POOL_WIDTH), row),
                  pl.BlockSpec((1, HALO, POOL_WIDTH), lambda bi, i: (bi, jnp.maximum(i * hb - 1, 0), 0)),
                  pl.BlockSpec((1, HALO, POOL_WIDTH), lambda bi, i: (bi, jnp.minimum((i + 1) * hb, nhb - 1), 0)),
                  pl.BlockSpec((1, tm, d), row),
                  pl.BlockSpec((1, 6, d), lambda bi, i: (bi, 0, 0)),
                  pl.BlockSpec((len(POOL_WINDOWS), POOL_GROUP, POOL_GROUP), lambda bi, i: (0, 0, 0)),
                  pl.BlockSpec((1, POOL_WIDTH), vec),
                  pl.BlockSpec((1, POOL_WIDTH), vec),
                  pl.BlockSpec((d, d), vec),
                  pl.BlockSpec((1, d), vec),
                  pl.BlockSpec((1, d), vec),
                  pl.BlockSpec((N_EXPERTS, d), vec)],
        out_specs=[pl.BlockSpec((1, tm, d), row),
                   pl.BlockSpec((1, tm, d), row),
                   pl.BlockSpec((1, tm * CHUNKS, LANES), row),
                   pl.BlockSpec((N_EXPERTS, tm), lambda bi, i: (0, bi * (n // tm) + i))],
        out_shape=[jax.ShapeDtypeStruct((b, n, d), F32),
                   jax.ShapeDtypeStruct((b, n, d), BF16),
                   jax.ShapeDtypeStruct((b, n * CHUNKS, LANES), F32),
                   jax.ShapeDtypeStruct((N_EXPERTS, b * n), F32)],
        compiler_params=_cparams(("arbitrary", "arbitrary")),
        name="outproj",
    )(att, p, p, p, x, mod3, w_pool_bf, b_pool, pool_scale, w_out_bf, g_post, g_pre, w_router_t)


def _beats(a, b, a_first):
    return jnp.where((a >= b) if a_first else (a > b), 1.0, 0.0)


def _route_kernel(lg_ref, bias_ref, w_ref):
    scores = [1.0 / (1.0 + jnp.exp(-lg_ref[e])) for e in range(N_EXPERTS)]
    biased = [scores[e] + bias_ref[e] for e in range(N_EXPERTS)]
    gscore = []
    for g in range(N_GROUPS):
        vals = biased[g * GROUP_SIZE:(g + 1) * GROUP_SIZE]
        m1 = vals[0]
        m2 = jnp.full_like(m1, -jnp.inf)
        for v in vals[1:]:
            m2 = jnp.maximum(m2, jnp.minimum(m1, v))
            m1 = jnp.maximum(m1, v)
        gscore.append(m1 + m2)
    gsel = []
    for g in range(N_GROUPS):
        rank = jnp.zeros_like(gscore[g])
        for g2 in range(N_GROUPS):
            if g2 != g:
                rank = rank + _beats(gscore[g2], gscore[g], g2 < g)
        gsel.append(rank < TOPK_GROUPS)
    vals = [jnp.where(gsel[e // GROUP_SIZE], biased[e], -jnp.inf) for e in range(N_EXPERTS)]
    chosen = [jnp.zeros(vals[0].shape, jnp.bool_) for _ in range(N_EXPERTS)]
    for _ in range(TOP_K):
        top = vals[0]
        for v in vals[1:]:
            top = jnp.maximum(top, v)
        found = jnp.zeros(top.shape, jnp.bool_)
        for e in range(N_EXPERTS):
            hit = (vals[e] == top) & jnp.logical_not(found)
            found = found | hit
            chosen[e] = chosen[e] | hit
            vals[e] = jnp.where(hit, -jnp.inf, vals[e])
    picked = [jnp.where(chosen[e], scores[e], 0.0) for e in range(N_EXPERTS)]
    denom = picked[0]
    for e in range(1, N_EXPERTS):
        denom = denom + picked[e]
    for e in range(N_EXPERTS):
        w_ref[e] = picked[e] / denom * ROUTED_SCALE


def _route(logits3, bias3, rows=SUBLANES):
    e, r, l = logits3.shape
    return pl.pallas_call(
        _route_kernel,
        grid=(r // rows,),
        in_specs=[pl.BlockSpec((e, rows, l), lambda i: (0, i, 0)),
                  pl.BlockSpec((e, 1, l), lambda i: (0, 0, 0))],
        out_specs=pl.BlockSpec((e, rows, l), lambda i: (0, i, 0)),
        out_shape=jax.ShapeDtypeStruct((e, r, l), F32),
        compiler_params=_cparams(("arbitrary",)),
        name="route",
    )(logits3, bias3)


CHUNKS = D_MODEL // LANES
SLAB = SLOT_TILE + 4
ROW_BATCH = 8


def _gather_tile(row_ref, tile, xs_ref, tx_ref):
    for s in range(SLOT_TILE):
        row = pl.multiple_of(row_ref[0, tile, s], CHUNKS)
        tx_ref[pl.ds(s, CHUNKS, stride=SLAB), :] = xs_ref[pl.ds(row, CHUNKS), :]


def _experts_step(j, wslot, row_ref, sw_ref, wg_st, wu_st, wd_st, xs_ref, acc_ref,
                  tx_cur, tx_nxt, ty_cur, ty_prv):
    ntile = row_ref.shape[1]
    _gather_tile(row_ref, jnp.minimum(j + 1, ntile - 1), xs_ref, tx_nxt)
    xb = jnp.concatenate([tx_cur[pl.ds(k * SLAB, SLOT_TILE), :] for k in range(CHUNKS)], axis=1).astype(BF16)
    g = _dot(xb, wg_st[wslot].astype(BF16))
    u = _dot(xb, wu_st[wslot].astype(BF16))
    h = (_silu(g) * u).astype(BF16)
    y = _dot(h, wd_st[wslot].astype(BF16))
    w_rows = jnp.transpose(jnp.broadcast_to(sw_ref[0, pl.ds(j, 1), :], (LANES, SLOT_TILE)))
    for k in range(CHUNKS):
        ty_cur[pl.ds(k * SLAB, SLOT_TILE), :] = y[:, k * LANES:(k + 1) * LANES] * w_rows
    prv = jnp.maximum(j - 1, 0)
    for s0 in range(0, SLOT_TILE, ROW_BATCH):
        new = []
        for s in range(s0, s0 + ROW_BATCH):
            row = pl.multiple_of(row_ref[0, prv, s], CHUNKS)
            new.append((row, acc_ref[pl.ds(row, CHUNKS), :] + ty_prv[pl.ds(s, CHUNKS, stride=SLAB), :]))
        for row, v in new:
            acc_ref[pl.ds(row, CHUNKS), :] = v


def _experts_kernel(be_ref, nt_ref, first_ref, nxt_ref, ws_ref, row_ref, sw_ref, x_hbm,
                    wg_hbm, wu_hbm, wd_hbm, o_hbm,
                    xs_ref, acc_ref, tx0_ref, tx1_ref, ty0_ref, ty1_ref, wg_st, wu_st, wd_st,
                    sem, wsem, *, ntile, n):
    c = pl.program_id(0)
    j = pl.program_id(1)
    t = c * ntile + j
    rows = n * CHUNKS
    wslot = ws_ref[t]

    def weight_copies(e, slot):
        return [pltpu.make_async_copy(src.at[e], dst.at[slot], wsem.at[slot])
                for src, dst in ((wg_hbm, wg_st), (wu_hbm, wu_st), (wd_hbm, wd_st))]

    @pl.when(j == 0)
    def _():
        for wcp in weight_copies(be_ref[t], 0):
            wcp.start()
        cp = pltpu.make_async_copy(x_hbm.at[c], xs_ref.at[pl.ds(0, rows)], sem.at[0])
        cp.start()
        xs_ref[pl.ds(rows, CHUNKS), :] = jnp.zeros((CHUNKS, LANES), F32)
        acc_ref[...] = jnp.zeros_like(acc_ref)
        ty1_ref[...] = jnp.zeros_like(ty1_ref)
        cp.wait()
        _gather_tile(row_ref, 0, xs_ref, tx0_ref)

    @pl.when(first_ref[t] == 1)
    def _():
        for wcp in weight_copies(be_ref[t], wslot):
            wcp.wait()

    @pl.when((first_ref[t] == 1) & (nxt_ref[t] >= 0))
    def _():
        for wcp in weight_copies(nxt_ref[t], 1 - wslot):
            wcp.start()

    live = j <= nt_ref[c]
    args = (j, wslot, row_ref, sw_ref, wg_st, wu_st, wd_st, xs_ref, acc_ref)

    @pl.when(live & (lax.rem(j, 2) == 0))
    def _():
        _experts_step(*args, tx0_ref, tx1_ref, ty0_ref, ty1_ref)

    @pl.when(live & (lax.rem(j, 2) == 1))
    def _():
        _experts_step(*args, tx1_ref, tx0_ref, ty1_ref, ty0_ref)

    @pl.when(j == ntile - 1)
    def _():
        cp = pltpu.make_async_copy(acc_ref.at[pl.ds(0, rows)], o_hbm.at[c], sem.at[1])
        cp.start()
        cp.wait()


def _experts(block_e, ntiles, first, nxt_e, wslot, slot_row, slot_w, h2t, w_gate, w_up, w_down, ntile, n):
    b = h2t.shape[0]
    d, f = w_gate.shape[1], w_gate.shape[2]
    batch = lambda c, j, *_: (c, 0, 0)
    kern = functools.partial(_experts_kernel, ntile=ntile, n=n)
    slab = pltpu.VMEM((CHUNKS * SLAB, LANES), F32)
    hbm = pl.BlockSpec(memory_space=pl.ANY)
    grid_spec = pltpu.PrefetchScalarGridSpec(
        num_scalar_prefetch=5,
        grid=(b, ntile),
        in_specs=[pl.BlockSpec((1, ntile, SLOT_TILE), batch, memory_space=pltpu.SMEM),
                  pl.BlockSpec((1, ntile, SLOT_TILE), batch),
                  hbm, hbm, hbm, hbm],
        out_specs=hbm,
        scratch_shapes=[pltpu.VMEM(((n + 1) * CHUNKS, LANES), F32),
                        pltpu.VMEM(((n + 1) * CHUNKS, LANES), F32),
                        slab, slab, slab, slab,
                        pltpu.VMEM((2, d, f), F32), pltpu.VMEM((2, d, f), F32), pltpu.VMEM((2, f, d), F32),
                        pltpu.SemaphoreType.DMA((2,)), pltpu.SemaphoreType.DMA((2,))],
    )
    return pl.pallas_call(
        kern,
        grid_spec=grid_spec,
        out_shape=jax.ShapeDtypeStruct((b, n * CHUNKS, LANES), F32),
        compiler_params=_cparams(("arbitrary", "arbitrary")),
        name="experts",
    )(block_e, ntiles, first, nxt_e, wslot, slot_row, slot_w, h2t, w_gate, w_up, w_down)


PLAN_GROUP = 4


def _plan_tables_kernel(w_ref, re_ref, ro_ref, tab_ref, *, rows):
    ne = w_ref.shape[0]
    w2 = w_ref[...].reshape(ne * rows, LANES)
    selb = jnp.where(w2 > 0.0, 1.0, 0.0).astype(BF16)
    i0 = lax.broadcasted_iota(jnp.int32, (LANES, LANES), 0)
    i1 = lax.broadcasted_iota(jnp.int32, (LANES, LANES), 1)
    rowtot = _dot(selb, jnp.ones((LANES, LANES), BF16))
    ridx = lax.broadcasted_iota(jnp.int32, rowtot.shape, 0) & (rows - 1)
    acc = rowtot
    sh = 1
    while sh < rows:
        acc = acc + jnp.where(ridx >= sh, pltpu.roll(acc, sh, 0), 0.0)
        sh *= 2
    re_ref[0] = acc
    ro_ref[0] = acc - rowtot
    incl_t = _dot_nt(jnp.where(i1 <= i0, 1.0, 0.0).astype(BF16), selb)
    eye = jnp.where(i0 == i1, 1.0, 0.0).astype(BF16)
    w_hi = w2.astype(BF16)
    r1 = w2 - w_hi.astype(F32)
    w_mid = r1.astype(BF16)
    w_lo = (r1 - w_mid.astype(F32)).astype(BF16)
    parts = [_dot_nt(eye, p) for p in (w_hi, w_mid, w_lo)]
    for g in range(ne * rows // LANES):
        sl = slice(g * LANES, (g + 1) * LANES)
        tab_ref[0, g, 0:LANES, :] = incl_t[:, sl].astype(BF16)
        for k in range(3):
            tab_ref[0, g, (k + 1) * LANES:(k + 2) * LANES, :] = parts[k][:, sl].astype(BF16)


def _plan_tables(selw3, b):
    ne, r_all, _ = selw3.shape
    rows = r_all // b
    groups = ne * rows // LANES
    kern = functools.partial(_plan_tables_kernel, rows=rows)
    return pl.pallas_call(
        kern,
        grid=(b,),
        in_specs=[pl.BlockSpec((ne, rows, LANES), lambda c: (0, c, 0))],
        out_specs=[pl.BlockSpec((1, ne * rows, LANES), lambda c: (c, 0, 0)),
                   pl.BlockSpec((1, ne * rows, LANES), lambda c: (c, 0, 0)),
                   pl.BlockSpec((1, groups, 4 * LANES, LANES), lambda c: (c, 0, 0, 0))],
        out_shape=[jax.ShapeDtypeStruct((b, ne * rows, LANES), F32),
                   jax.ShapeDtypeStruct((b, ne * rows, LANES), F32),
                   jax.ShapeDtypeStruct((b, groups, 4 * LANES, LANES), BF16)],
        compiler_params=_cparams(("arbitrary",)),
        name="plan_tables",
    )(selw3)


def _plan_slots_kernel(be_ref, q0_ref, nt_ref, re_ref, ro_ref, tab_ref, tok_ref, sw_ref,
                       *, ntile, rows, n):
    c = pl.program_id(0)
    tok_ref[...] = jnp.full(tok_ref.shape, n * CHUNKS, jnp.int32)
    sw_ref[...] = jnp.zeros(sw_ref.shape, F32)
    lane = lax.broadcasted_iota(jnp.int32, (1, SLOT_TILE), 1).astype(F32)
    sub_r = lax.broadcasted_iota(jnp.int32, (rows, SLOT_TILE), 0).astype(F32)
    sub_l = lax.broadcasted_iota(jnp.int32, (LANES, SLOT_TILE), 0).astype(F32)
    reps = SLOT_TILE // LANES

    def tile_body(j):
        e = be_ref[c * ntile + j]
        q = q0_ref[c * ntile + j].astype(F32) + lane
        base = pl.multiple_of(e * rows, rows)
        row_end = jnp.concatenate([re_ref[0, pl.ds(base, rows), :]] * reps, axis=1)
        row_off = jnp.concatenate([ro_ref[0, pl.ds(base, rows), :]] * reps, axis=1)
        r = jnp.sum(jnp.where(row_end <= q, 1.0, 0.0), axis=0, keepdims=True)
        row_start = jnp.sum(jnp.where(sub_r == r, row_off, 0.0), axis=0, keepdims=True)
        per_group = LANES // rows
        g = lax.shift_right_logical(e, per_group.bit_length() - 1)
        col = r + ((e & (per_group - 1)) * rows).astype(F32)
        onehot = jnp.where(sub_l == col, 1.0, 0.0).astype(BF16)
        picked = _dot(tab_ref[0, g], onehot)
        incl = picked[0:LANES]
        lane_idx = jnp.sum(jnp.where(incl <= q - row_start, 1.0, 0.0), axis=0, keepdims=True)
        wrow = (picked[LANES:2 * LANES] + picked[2 * LANES:3 * LANES]) + picked[3 * LANES:]
        wsel = jnp.sum(jnp.where(sub_l == lane_idx, wrow, 0.0), axis=0, keepdims=True)
        valid = (q < row_end[rows - 1:rows, :]) & (j < nt_ref[c])
        tok = jnp.where(valid, r * LANES + lane_idx, float(n)).astype(jnp.int32)
        tok_ref[0, pl.ds(j, 1), :] = tok * CHUNKS
        sw_ref[0, pl.ds(j, 1), :] = jnp.where(valid, wsel, 0.0)

    def tile_group(i, carry):
        for k in range(PLAN_GROUP):
            tile_body(PLAN_GROUP * i + k)
        return carry

    lax.fori_loop(0, (nt_ref[c] + PLAN_GROUP - 1) // PLAN_GROUP, tile_group, 0)


def _plan_slots(block_e, tile_q0, ntiles, row_end, row_off, tables, ntile, n):
    b, er, _ = row_end.shape
    rows = n // LANES
    per_group = LANES // rows
    assert rows * per_group == LANES and per_group & (per_group - 1) == 0 and ntile % PLAN_GROUP == 0
    groups = tables.shape[1]
    kern = functools.partial(_plan_slots_kernel, ntile=ntile, rows=rows, n=n)
    grid_spec = pltpu.PrefetchScalarGridSpec(
        num_scalar_prefetch=3,
        grid=(b,),
        in_specs=[pl.BlockSpec((1, er, LANES), lambda c, *_: (c, 0, 0)),
                  pl.BlockSpec((1, er, LANES), lambda c, *_: (c, 0, 0)),
                  pl.BlockSpec((1, groups, 4 * LANES, LANES), lambda c, *_: (c, 0, 0, 0))],
        out_specs=[pl.BlockSpec((1, ntile, SLOT_TILE), lambda c, *_: (c, 0, 0)),
                   pl.BlockSpec((1, ntile, SLOT_TILE), lambda c, *_: (c, 0, 0))],
    )
    return pl.pallas_call(
        kern,
        grid_spec=grid_spec,
        out_shape=[jax.ShapeDtypeStruct((b, ntile, SLOT_TILE), jnp.int32),
                   jax.ShapeDtypeStruct((b, ntile, SLOT_TILE), F32)],
        compiler_params=_cparams(("arbitrary",)),
        name="plan_slots",
    )(block_e, tile_q0, ntiles, row_end, row_off, tables)


def _tile_meta(row_end, b, n, ntile):
    rows = n // LANES
    cnt = row_end.reshape(b, N_EXPERTS, rows, LANES)[:, :, rows - 1, 0].astype(jnp.int32)
    nt_e = (cnt + SLOT_TILE - 1) // SLOT_TILE
    tend = jnp.cumsum(nt_e, axis=1)
    tstart = tend - nt_e
    ntiles = tend[:, -1]
    tile_id = jnp.arange(ntile, dtype=jnp.int32)
    tid = jnp.minimum(tile_id[None, :], ntiles[:, None] - 1)
    be = jnp.minimum(jnp.sum(tend[:, None, :] <= tid[:, :, None], axis=2).astype(jnp.int32), N_EXPERTS - 1)
    eid = jnp.arange(N_EXPERTS, dtype=jnp.int32)
    is_be = be[:, :, None] == eid[None, None, :]
    pick = lambda per_expert: jnp.sum(jnp.where(is_be, per_expert[:, None, :], 0), axis=2)
    q0 = (tid - pick(tstart)) * SLOT_TILE
    valid = tile_id[None, :] < ntiles[:, None]
    prev_be = jnp.concatenate([jnp.full((b, 1), -1, jnp.int32), be[:, :-1]], axis=1)
    first = (valid & (be != prev_be)).astype(jnp.int32)
    wslot = (jnp.cumsum(first, axis=1) - 1) % 2
    later = (nt_e > 0)[:, None, :] & (eid[None, None, :] > eid[None, :, None])
    nxt = jnp.min(jnp.where(later, eid[None, None, :], N_EXPERTS), axis=2)
    nxt_e = pick(jnp.where(nxt < N_EXPERTS, nxt, -1))
    flat = lambda a: a.reshape(-1).astype(jnp.int32)
    return flat(be), flat(q0), ntiles.astype(jnp.int32), flat(first), flat(nxt_e), flat(wslot)


def _dispatch_plan(selw3, b, n, ntile):
    row_end, row_off, tables = _plan_tables(selw3, b)
    block_e, tile_q0, ntiles, first, nxt_e, wslot = _tile_meta(row_end, b, n, ntile)
    slot_tok, slot_w = _plan_slots(block_e, tile_q0, ntiles, row_end, row_off, tables, ntile, n)
    return block_e, ntiles, first, nxt_e, wslot, slot_tok, slot_w


def _final_kernel(h2_ref, r_ref, x1_ref, mod_ref, wg_ref, wu_ref, wd_ref, g_ref, o_ref):
    hb = h2_ref[0]
    tm = hb.shape[0]
    sh = _dot((_silu(_dot(hb, wg_ref[...])) * _dot(hb, wu_ref[...])).astype(BF16), wd_ref[...])
    routed = jnp.concatenate([r_ref[0, pl.ds(k, tm, stride=CHUNKS), :] for k in range(CHUNKS)], axis=1)
    moe = routed + sh
    o_ref[0] = x1_ref[0] + mod_ref[0, 5:6, :] * _rms(moe, g_ref[...])


def _final(h2, routed, x1, mod3, wsg_bf, wsu_bf, wsd_bf, g_post, tm=512):
    b, n, d = x1.shape
    f = wsg_bf.shape[1]
    row = lambda bi, i: (bi, i, 0)
    vec = lambda bi, i: (0, 0)
    return pl.pallas_call(
        _final_kernel,
        grid=(b, n // tm),
        in_specs=[pl.BlockSpec((1, tm, d), row),
                  pl.BlockSpec((1, tm * CHUNKS, LANES), row),
                  pl.BlockSpec((1, tm, d), row),
                  pl.BlockSpec((1, 6, d), lambda bi, i: (bi, 0, 0)),
                  pl.BlockSpec((d, f), vec),
                  pl.BlockSpec((d, f), vec),
                  pl.BlockSpec((f, d), vec),
                  pl.BlockSpec((1, d), vec)],
        out_specs=pl.BlockSpec((1, tm, d), row),
        out_shape=jax.ShapeDtypeStruct((b, n, d), F32),
        compiler_params=_cparams(("arbitrary", "arbitrary")),
        name="final",
    )(h2, routed, x1, mod3, wsg_bf, wsu_bf, wsd_bf, g_post)


def _rope_tables(n):
    t = np.arange(n)
    row = (t // GRID_W).astype(np.float32)
    col = (t % GRID_W).astype(np.float32)
    freqs = np.float32(ROPE_THETA) ** (-np.arange(0, AXIS_DIM, 2, dtype=np.float32) / np.float32(AXIS_DIM))
    ar = row[:, None] * freqs
    ac = col[:, None] * freqs
    zeros = np.zeros_like(ar)
    cos64 = np.concatenate([np.cos(ar), np.cos(ar), np.cos(ac), np.cos(ac)], axis=1)
    sa64 = np.concatenate([-np.sin(ar), zeros, -np.sin(ac), zeros], axis=1)
    sb64 = np.concatenate([zeros, np.sin(ar), zeros, np.sin(ac)], axis=1)
    two = lambda a: jnp.asarray(np.concatenate([a, a], axis=1).astype(np.float32))
    return two(cos64), two(sa64), two(sb64)


def kernel(x, c, ctx, c_ctx, w_ada, b_ada, g_pre_mix, g_post_mix, g_pre_ffn, g_post_ffn, w_in, lambda_q1, lambda_k1, lambda_q2, lambda_k2, g_subln, w_pool, b_pool, pool_scale, w_out, w_router, router_bias, w_e_gate, w_e_up, w_e_down, w_sh_gate, w_sh_up, w_sh_down):
    b, n, d = x.shape
    l = 0
    mod_rows = SUBLANES
    cin = jnp.concatenate([c, c_ctx[None, :], jnp.zeros((mod_rows - b - 1, d), F32)], axis=0)
    mod3 = _ada(cin, w_ada[l], b_ada[l]).reshape(mod_rows, 6, d)

    w_in_bf = w_in[l].astype(BF16)
    cos, sa, sb = _rope_tables(n)
    q, k, vt, p = _inproj(x, mod3, g_pre_mix[l][None, :], w_in_bf, cos * (QK_DIM ** -0.5 * LOG2E), cos, sa, sb)
    kc, vct = _ctxkv(ctx, mod3, g_pre_mix[l][None, :], w_in_bf, b)

    lamv = jnp.stack([lambda_q1[l], lambda_k1[l], lambda_q2[l], lambda_k2[l]], axis=0)
    att = _attn(q, k, kc, vt, vct, lamv, g_subln[l][:, None])

    x1, h2, h2t, logits_t = _outproj(
        att, p, x, mod3, w_pool[l].astype(BF16), b_pool[l][None, :], pool_scale[l][None, :],
        w_out[l].astype(BF16), g_post_mix[l][None, :], g_pre_ffn[l][None, :], w_router[l].T)

    t = b * n
    selw3 = _route(logits_t.reshape(N_EXPERTS, t // LANES, LANES),
                   jnp.broadcast_to(router_bias[l][:, None, None], (N_EXPERTS, 1, LANES)))

    ntile = (n * TOP_K) // SLOT_TILE + N_EXPERTS
    block_e, ntiles, first, nxt_e, wslot, slot_row, slot_w = _dispatch_plan(selw3, b, n, ntile)
    routed = _experts(block_e, ntiles, first, nxt_e, wslot, slot_row, slot_w, h2t,
                      w_e_gate[l], w_e_up[l], w_e_down[l], ntile, n)

    return _final(h2, routed, x1, mod3, w_sh_gate[l].astype(BF16), w_sh_up[l].astype(BF16),
                  w_sh_down[l].astype(BF16), g_post_ffn[l][None, :])
```

```python
import functools
import math

import jax
import jax.numpy as jnp
import numpy as np
from jax import lax
from jax.experimental import pallas as pl
from jax.experimental.pallas import tpu as pltpu

F32 = jnp.float32
BF16 = jnp.bfloat16

D_MODEL = 1024
GRID_W = 64
N_HEADS = 4
QK_DIM = 64
V_DIM = 128
ATTN_WIDTH = N_HEADS * V_DIM
POOL_WIDTH = D_MODEL - ATTN_WIDTH
POOL_WINDOWS = (2, 4, 8, 16)
POOL_GROUP = POOL_WIDTH // len(POOL_WINDOWS)
AXIS_DIM = QK_DIM // 2
ROPE_THETA = 10000.0
N_EXPERTS = 64
TOP_K = 8
N_GROUPS = 8
GROUP_SIZE = N_EXPERTS // N_GROUPS
TOPK_GROUPS = 4
EXPERT_DIM = 256
ROUTED_SCALE = 2.5
EPS = 1e-6
LAM_INIT = 0.8 - 0.6 * math.exp(-0.3 * 0)

LANES = 128
SUBLANES = 8
SLOT_TILE = 256
HALO = 16
VMEM_LIMIT = 56 * 1024 * 1024


def _cparams(sem, vmem=VMEM_LIMIT):
    return pltpu.CompilerParams(dimension_semantics=sem, vmem_limit_bytes=vmem)


def _split(a):
    hi = a.astype(BF16)
    lo = (a - hi.astype(F32)).astype(BF16)
    return hi, lo


def _dot(a, b):
    return jnp.dot(a, b, preferred_element_type=F32)


def _dot_nt(a, b):
    return lax.dot_general(a, b, (((1,), (1,)), ((), ())), preferred_element_type=F32)


def _dot3(a, b, nt=False):
    d = _dot_nt if nt else _dot
    ah, al = _split(a)
    bh, bl = _split(b)
    return d(ah, bh) + d(ah, bl) + d(al, bh)


def _rms(x, g):
    return x * lax.rsqrt(jnp.mean(x * x, axis=-1, keepdims=True) + EPS) * g


def _silu(x):
    return x * (1.0 / (1.0 + jnp.exp(-x)))


def _ada_kernel(c_ref, w_ref, b_ref, o_ref):
    o_ref[...] = _dot3(_silu(c_ref[...]), w_ref[...]) + b_ref[...]


def _ada(cin, w_ada, b_ada):
    rows, d = cin.shape
    n = w_ada.shape[1]
    tn = 1024
    return pl.pallas_call(
        _ada_kernel,
        grid=(n // tn,),
        in_specs=[pl.BlockSpec((rows, d), lambda j: (0, 0)),
                  pl.BlockSpec((d, tn), lambda j: (0, j)),
                  pl.BlockSpec((1, tn), lambda j: (0, j))],
        out_specs=pl.BlockSpec((rows, tn), lambda j: (0, j)),
        out_shape=jax.ShapeDtypeStruct((rows, n), F32),
        compiler_params=_cparams(("arbitrary",)),
        name="ada",
    )(cin, w_ada, b_ada.reshape(1, n))


def _rope(t, cos, sa, sb):
    return t * cos + pltpu.roll(t, LANES - AXIS_DIM // 2, 1) * sa + pltpu.roll(t, AXIS_DIM // 2, 1) * sb


def _inproj_kernel(x_ref, mod_ref, g_ref, w_ref, cq_ref, ck_ref, sa_ref, sb_ref,
                   q_ref, k_ref, vt_ref, p_ref):
    x = x_ref[0]
    h = _rms(x, g_ref[...]) * (1.0 + mod_ref[0, 1:2, :]) + mod_ref[0, 0:1, :]
    px = _dot(h.astype(BF16), w_ref[...])
    ck = ck_ref[...]
    sa = sa_ref[...]
    sb = sb_ref[...]
    cq = cq_ref[...]
    scale = QK_DIM ** -0.5 * LOG2E
    saq = sa * scale
    sbq = sb * scale
    for hh in range(N_HEADS):
        lo = hh * LANES
        q = px[:, lo:lo + LANES]
        k = px[:, ATTN_WIDTH + lo:ATTN_WIDTH + lo + LANES]
        q_ref[0, :, lo:lo + LANES] = _rope(q, cq, saq, sbq).astype(BF16)
        k_ref[0, :, lo:lo + LANES] = _rope(k, ck, sa, sb).astype(BF16)
    vt_ref[0] = jnp.transpose(px[:, 2 * ATTN_WIDTH:3 * ATTN_WIDTH]).astype(BF16)
    p_ref[0] = px[:, 3 * ATTN_WIDTH:].astype(BF16)


def _inproj(x, mod3, g, w_in_bf, cq, ck, sa, sb, tm=1024):
    b, n, d = x.shape
    width = w_in_bf.shape[1]
    row = lambda bi, i: (bi, i, 0)
    tab = pl.BlockSpec((tm, LANES), lambda bi, i: (i, 0))
    out = jax.ShapeDtypeStruct((b, n, ATTN_WIDTH), BF16)
    return pl.pallas_call(
        _inproj_kernel,
        grid=(b, n // tm),
        in_specs=[pl.BlockSpec((1, tm, d), row),
                  pl.BlockSpec((1, 6, d), lambda bi, i: (bi, 0, 0)),
                  pl.BlockSpec((1, d), lambda bi, i: (0, 0)),
                  pl.BlockSpec((d, width), lambda bi, i: (0, 0)),
                  tab, tab, tab, tab],
        out_specs=[pl.BlockSpec((1, tm, ATTN_WIDTH), row),
                   pl.BlockSpec((1, tm, ATTN_WIDTH), row),
                   pl.BlockSpec((1, ATTN_WIDTH, tm), lambda bi, i: (bi, 0, i)),
                   pl.BlockSpec((1, tm, ATTN_WIDTH), row)],
        out_shape=[out, out, jax.ShapeDtypeStruct((b, ATTN_WIDTH, n), BF16), out],
        compiler_params=_cparams(("arbitrary", "arbitrary")),
        name="inproj",
    )(x, mod3, g, w_in_bf, cq, ck, sa, sb)


def _ctxkv_kernel(x_ref, mod_ref, g_ref, wk_ref, wv_ref, k_ref, vt_ref):
    h = _rms(x_ref[0], g_ref[...]) * (1.0 + mod_ref[0, 1:2, :]) + mod_ref[0, 0:1, :]
    hb = h.astype(BF16)
    k_ref[0] = _dot(hb, wk_ref[...]).astype(BF16)
    vt_ref[0] = jnp.transpose(_dot(hb, wv_ref[...])).astype(BF16)


def _ctxkv(ctx, mod3, g, w_in_bf, ctx_row):
    b, n, d = ctx.shape
    out = jax.ShapeDtypeStruct((b, n, ATTN_WIDTH), BF16)
    return pl.pallas_call(
        _ctxkv_kernel,
        grid=(b,),
        in_specs=[pl.BlockSpec((1, n, d), lambda bi: (bi, 0, 0)),
                  pl.BlockSpec((1, 6, d), lambda bi: (ctx_row, 0, 0)),
                  pl.BlockSpec((1, d), lambda bi: (0, 0)),
                  pl.BlockSpec((d, ATTN_WIDTH), lambda bi: (0, 1)),
                  pl.BlockSpec((d, ATTN_WIDTH), lambda bi: (0, 2))],
        out_specs=[pl.BlockSpec((1, n, ATTN_WIDTH), lambda bi: (bi, 0, 0)),
                   pl.BlockSpec((1, ATTN_WIDTH, n), lambda bi: (bi, 0, 0))],
        out_shape=[out, jax.ShapeDtypeStruct((b, ATTN_WIDTH, n), BF16)],
        compiler_params=_cparams(("arbitrary",)),
        name="ctxkv",
    )(ctx, mod3, g, w_in_bf, w_in_bf)


LOG2E = math.log2(math.e)
KEY_CHUNK = 256


def _attn_step(q_ref, kx_ref, kc_ref, vt_ref, vct_ref, lam_ref, g_ref, o_ref, s_new, m_new, s_old, m_old):
    nc = kc_ref.shape[1]
    tq = q_ref.shape[1]
    nk = s_old.shape[1]
    groups = KEY_CHUNK // SUBLANES
    q = q_ref[0]
    lane = lax.broadcasted_iota(jnp.int32, q.shape, 1)
    zero = jnp.zeros_like(q)
    qms = (jnp.where(lane < QK_DIM, q, zero), jnp.where(lane >= QK_DIM, q, zero))
    m_prev = (m_old[0], m_old[1])
    top = [jnp.full((SUBLANES, tq), -jnp.inf, F32) for _ in range(2)]
    den = [jnp.zeros((SUBLANES, tq), F32) for _ in range(2)]
    acc = [jnp.zeros((V_DIM, tq), F32) for _ in range(2)]
    for c in range(nk // KEY_CHUNK):
        lo = c * KEY_CHUNK
        if lo < nc:
            keys, vt = kc_ref[0, lo:lo + KEY_CHUNK, :], vct_ref[0, :, lo:lo + KEY_CHUNK]
        else:
            keys, vt = kx_ref[0, lo - nc:lo - nc + KEY_CHUNK, :], vt_ref[0, :, lo - nc:lo - nc + KEY_CHUNK]
        for mp in range(2):
            s = _dot_nt(keys, qms[mp])
            s_new[mp, lo:lo + KEY_CHUNK, :] = s
            top[mp] = jnp.maximum(top[mp], jnp.max(s.reshape(groups, SUBLANES, tq), axis=0))
            p = jnp.exp2(s_old[mp, lo:lo + KEY_CHUNK, :] - m_prev[mp])
            den[mp] = den[mp] + jnp.sum(p.reshape(groups, SUBLANES, tq), axis=0)
            acc[mp] = acc[mp] + _dot(vt, p.astype(BF16))
    for mp in range(2):
        m_new[mp] = jnp.max(top[mp], axis=0, keepdims=True)
    lv = lam_ref[...]
    lam = (jnp.exp(jnp.sum(lv[0:1] * lv[1:2], axis=-1, keepdims=True))
           - jnp.exp(jnp.sum(lv[2:3] * lv[3:4], axis=-1, keepdims=True)) + LAM_INIT)
    l1 = jnp.sum(den[0], axis=0, keepdims=True)
    l2 = jnp.sum(den[1], axis=0, keepdims=True)
    o = acc[0] * (1.0 / l1) - acc[1] * (lam / l2)
    o = o * lax.rsqrt(jnp.mean(o * o, axis=0, keepdims=True) + EPS) * (g_ref[...] * (1.0 - LAM_INIT))
    o_ref[0] = jnp.transpose(o).astype(BF16)


def _attn_kernel(q_ref, kx_ref, kc_ref, vt_ref, vct_ref, lam_ref, g_ref, o_ref, s0_ref, m0_ref, s1_ref, m1_ref):
    t = pl.program_id(0)
    io = (q_ref, kx_ref, kc_ref, vt_ref, vct_ref, lam_ref, g_ref, o_ref)

    @pl.when(t == 0)
    def _():
        s1_ref[...] = jnp.zeros_like(s1_ref)
        m1_ref[...] = jnp.zeros_like(m1_ref)

    @pl.when(lax.rem(t, 2) == 0)
    def _():
        _attn_step(*io, s0_ref, m0_ref, s1_ref, m1_ref)

    @pl.when(lax.rem(t, 2) == 1)
    def _():
        _attn_step(*io, s1_ref, m1_ref, s0_ref, m0_ref)


def _attn(q, k, kc, vt, vct, lamv, g_col, tq=512):
    b, n, _ = q.shape
    nc = kc.shape[1]
    nq = n // tq
    total = b * N_HEADS * nq

    def cur(t):
        ta = jnp.minimum(t, total - 1)
        bh = ta // nq
        return bh // N_HEADS, bh % N_HEADS, ta % nq

    def prv(t):
        tb = jnp.maximum(t - 1, 0)
        bh = tb // nq
        return bh // N_HEADS, bh % N_HEADS, tb % nq

    def at(fn, order):
        return lambda t: tuple((fn(t) + (0,))[j] for j in order)

    return pl.pallas_call(
        _attn_kernel,
        grid=(total + 1,),
        in_specs=[pl.BlockSpec((1, tq, LANES), at(cur, (0, 2, 1))),
                  pl.BlockSpec((1, n, LANES), at(cur, (0, 3, 1))),
                  pl.BlockSpec((1, nc, LANES), at(cur, (0, 3, 1))),
                  pl.BlockSpec((1, V_DIM, n), at(prv, (0, 1, 3))),
                  pl.BlockSpec((1, V_DIM, nc), at(prv, (0, 1, 3))),
                  pl.BlockSpec((4, QK_DIM), lambda t: (0, 0)),
                  pl.BlockSpec((V_DIM, 1), lambda t: (0, 0))],
        out_specs=pl.BlockSpec((1, tq, LANES), at(prv, (0, 2, 1))),
        out_shape=jax.ShapeDtypeStruct((b, n, ATTN_WIDTH), BF16),
        scratch_shapes=[pltpu.VMEM((2, nc + n, tq), F32), pltpu.VMEM((2, 1, tq), F32),
                        pltpu.VMEM((2, nc + n, tq), F32), pltpu.VMEM((2, 1, tq), F32)],
        compiler_params=_cparams(("arbitrary",)),
        name="attn",
    )(q, k, kc, vt, vct, lamv, g_col)


def _outproj_kernel(att_ref, p_ref, pprev_ref, pnext_ref, x_ref, mod_ref, wpool_ref, bpool_ref,
                    pscale_ref, wout_ref, gpost_ref, gpre_ref, wr_ref,
                    x1_ref, h2_ref, h2t_ref, lg_ref, *, tm, n):
    i = pl.program_id(1)
    p = p_ref[0].astype(F32)
    prev = jnp.where(i > 0, pprev_ref[0].astype(F32), 0.0)
    nxt = jnp.where(i < pl.num_programs(1) - 1, pnext_ref[0].astype(F32), 0.0)
    ext = jnp.concatenate([prev, p, nxt], axis=0)
    rows = tm + 2 * HALO
    trow = (i * tm + lax.broadcasted_iota(jnp.int32, (tm, 1), 0))

    pooled = []
    sums = {}
    acc = ext + pltpu.roll(ext, 1, 0)
    sums[2] = acc
    w = 2
    while w < POOL_WINDOWS[-1]:
        acc = pltpu.roll(acc, rows - w // 2, 0) + pltpu.roll(acc, w // 2, 0)
        w = 2 * w
        sums[w] = acc
    for gi, win in enumerate(POOL_WINDOWS):
        sl = slice(gi * POOL_GROUP, (gi + 1) * POOL_GROUP)
        wsum = sums[win][HALO:HALO + tm, sl]
        hi_c = jnp.minimum(trow + (win - win // 2), n)
        lo_c = jnp.maximum(trow - win // 2, 0)
        cnt = (hi_c - lo_c).astype(F32)
        d = wsum / cnt - p[:, sl]
        y = _dot(d.astype(BF16), wpool_ref[gi]) + bpool_ref[:, sl]
        pooled.append((y * pscale_ref[:, sl]).astype(BF16))
    pool = jnp.concatenate(pooled, axis=1)
    yx = _dot(att_ref[0], wout_ref[0:ATTN_WIDTH, :]) + _dot(pool, wout_ref[ATTN_WIDTH:, :])
    x1 = x_ref[0] + mod_ref[0, 2:3, :] * _rms(yx, gpost_ref[...])
    x1_ref[0] = x1
    h2 = _rms(x1, gpre_ref[...]) * (1.0 + mod_ref[0, 4:5, :]) + mod_ref[0, 3:4, :]
    h2_ref[0] = h2.astype(BF16)
    for k in range(CHUNKS):
        h2t_ref[0, pl.ds(k, tm, stride=CHUNKS), :] = h2[:, k * LANES:(k + 1) * LANES]
    lg_ref[...] = _dot3(wr_ref[...], h2, nt=True)


def _outproj(att, p, x, mod3, w_pool_bf, b_pool, pool_scale, w_out_bf, g_post, g_pre, w_router_t, tm=512):
    b, n, d = x.shape
    hb = tm // HALO
    nhb = n // HALO
    row = lambda bi, i: (bi, i, 0)
    vec = lambda bi, i: (0, 0)
    kern = functools.partial(_outproj_kernel, tm=tm, n=n)
    return pl.pallas_call(
        kern,
        grid=(b, n // tm),
        in_specs=[pl.BlockSpec((1, tm, ATTN_WIDTH), row),
                  pl.BlockSpec((1, tm, POOL_WIDTH), row),
                  pl.BlockSpec((1, HALO, POOL_WIDTH), lambda bi, i: (bi, jnp.maximum(i * hb - 1, 0), 0)),
                  pl.BlockSpec((1, HALO, POOL_WIDTH), lambda bi, i: (bi, jnp.minimum((i + 1) * hb, nhb - 1), 0)),
                  pl.BlockSpec((1, tm, d), row),
                  pl.BlockSpec((1, 6, d), lambda bi, i: (bi, 0, 0)),
                  pl.BlockSpec((len(POOL_WINDOWS), POOL_GROUP, POOL_GROUP), lambda bi, i: (0, 0, 0)),
                  pl.BlockSpec((1, POOL_WIDTH), vec),
                  pl.BlockSpec((1, POOL_WIDTH), vec),
                  pl.BlockSpec((d, d), vec),
                  pl.BlockSpec((1, d), vec),
                  pl.BlockSpec((1, d), vec),
                  pl.BlockSpec((N_EXPERTS, d), vec)],
        out_specs=[pl.BlockSpec((1, tm, d), row),
                   pl.BlockSpec((1, tm, d), row),
                   pl.BlockSpec((1, tm * CHUNKS, LANES), row),
                   pl.BlockSpec((N_EXPERTS, tm), lambda bi, i: (0, bi * (n // tm) + i))],
        out_shape=[jax.ShapeDtypeStruct((b, n, d), F32),
                   jax.ShapeDtypeStruct((b, n, d), BF16),
                   jax.ShapeDtypeStruct((b, n * CHUNKS, LANES), F32),
                   jax.ShapeDtypeStruct((N_EXPERTS, b * n), F32)],
        compiler_params=_cparams(("arbitrary", "arbitrary")),
        name="outproj",
    )(att, p, p, p, x, mod3, w_pool_bf, b_pool, pool_scale, w_out_bf, g_post, g_pre, w_router_t)


def _beats(a, b, a_first):
    return jnp.where((a >= b) if a_first else (a > b), 1.0, 0.0)


def _route_kernel(lg_ref, bias_ref, w_ref):
    scores = [1.0 / (1.0 + jnp.exp(-lg_ref[e])) for e in range(N_EXPERTS)]
    biased = [scores[e] + bias_ref[e] for e in range(N_EXPERTS)]
    gscore = []
    for g in range(N_GROUPS):
        vals = biased[g * GROUP_SIZE:(g + 1) * GROUP_SIZE]
        m1 = vals[0]
        m2 = jnp.full_like(m1, -jnp.inf)
        for v in vals[1:]:
            m2 = jnp.maximum(m2, jnp.minimum(m1, v))
            m1 = jnp.maximum(m1, v)
        gscore.append(m1 + m2)
    gsel = []
    for g in range(N_GROUPS):
        rank = jnp.zeros_like(gscore[g])
        for g2 in range(N_GROUPS):
            if g2 != g:
                rank = rank + _beats(gscore[g2], gscore[g], g2 < g)
        gsel.append(rank < TOPK_GROUPS)
    vals = [jnp.where(gsel[e // GROUP_SIZE], biased[e], -jnp.inf) for e in range(N_EXPERTS)]
    chosen = [jnp.zeros(vals[0].shape, jnp.bool_) for _ in range(N_EXPERTS)]
    for _ in range(TOP_K):
        top = vals[0]
        for v in vals[1:]:
            top = jnp.maximum(top, v)
        found = jnp.zeros(top.shape, jnp.bool_)
        for e in range(N_EXPERTS):
            hit = (vals[e] == top) & jnp.logical_not(found)
            found = found | hit
            chosen[e] = chosen[e] | hit
            vals[e] = jnp.where(hit, -jnp.inf, vals[e])
    picked = [jnp.where(chosen[e], scores[e], 0.0) for e in range(N_EXPERTS)]
    denom = picked[0]
    for e in range(1, N_EXPERTS):
        denom = denom + picked[e]
    for e in range(N_EXPERTS):
        w_ref[e] = picked[e] / denom * ROUTED_SCALE


def _route(logits3, bias3, rows=SUBLANES):
    e, r, l = logits3.shape
    return pl.pallas_call(
        _route_kernel,
        grid=(r // rows,),
        in_specs=[pl.BlockSpec((e, rows, l), lambda i: (0, i, 0)),
                  pl.BlockSpec((e, 1, l), lambda i: (0, 0, 0))],
        out_specs=pl.BlockSpec((e, rows, l), lambda i: (0, i, 0)),
        out_shape=jax.ShapeDtypeStruct((e, r, l), F32),
        compiler_params=_cparams(("arbitrary",)),
        name="route",
    )(logits3, bias3)


CHUNKS = D_MODEL // LANES
SLAB = SLOT_TILE + 4
ROW_BATCH = 8


def _gather_tile(row_ref, tile, xs_ref, tx_ref):
    for s in range(SLOT_TILE):
        row = pl.multiple_of(row_ref[0, tile, s], CHUNKS)
        tx_ref[pl.ds(s, CHUNKS, stride=SLAB), :] = xs_ref[pl.ds(row, CHUNKS), :]


def _experts_step(j, wslot, row_ref, sw_ref, wg_st, wu_st, wd_st, xs_ref, acc_ref,
                  tx_cur, tx_nxt, ty_cur, ty_prv):
    ntile = row_ref.shape[1]
    _gather_tile(row_ref, jnp.minimum(j + 1, ntile - 1), xs_ref, tx_nxt)
    xb = jnp.concatenate([tx_cur[pl.ds(k * SLAB, SLOT_TILE), :] for k in range(CHUNKS)], axis=1).astype(BF16)
    g = _dot(xb, wg_st[wslot].astype(BF16))
    u = _dot(xb, wu_st[wslot].astype(BF16))
    h = (_silu(g) * u).astype(BF16)
    y = _dot(h, wd_st[wslot].astype(BF16))
    w_rows = jnp.transpose(jnp.broadcast_to(sw_ref[0, pl.ds(j, 1), :], (LANES, SLOT_TILE)))
    for k in range(CHUNKS):
        ty_cur[pl.ds(k * SLAB, SLOT_TILE), :] = y[:, k * LANES:(k + 1) * LANES] * w_rows
    prv = jnp.maximum(j - 1, 0)
    for s0 in range(0, SLOT_TILE, ROW_BATCH):
        new = []
        for s in range(s0, s0 + ROW_BATCH):
            row = pl.multiple_of(row_ref[0, prv, s], CHUNKS)
            new.append((row, acc_ref[pl.ds(row, CHUNKS), :] + ty_prv[pl.ds(s, CHUNKS, stride=SLAB), :]))
        for row, v in new:
            acc_ref[pl.ds(row, CHUNKS), :] = v


def _experts_kernel(be_ref, nt_ref, first_ref, nxt_ref, ws_ref, row_ref, sw_ref, x_hbm,
                    wg_hbm, wu_hbm, wd_hbm, o_hbm,
                    xs_ref, acc_ref, tx0_ref, tx1_ref, ty0_ref, ty1_ref, wg_st, wu_st, wd_st,
                    sem, wsem, *, ntile, n):
    c = pl.program_id(0)
    j = pl.program_id(1)
    t = c * ntile + j
    rows = n * CHUNKS
    wslot = ws_ref[t]

    def weight_copies(e, slot):
        return [pltpu.make_async_copy(src.at[e], dst.at[slot], wsem.at[slot])
                for src, dst in ((wg_hbm, wg_st), (wu_hbm, wu_st), (wd_hbm, wd_st))]

    @pl.when(j == 0)
    def _():
        for wcp in weight_copies(be_ref[t], 0):
            wcp.start()
        cp = pltpu.make_async_copy(x_hbm.at[c], xs_ref.at[pl.ds(0, rows)], sem.at[0])
        cp.start()
        xs_ref[pl.ds(rows, CHUNKS), :] = jnp.zeros((CHUNKS, LANES), F32)
        acc_ref[...] = jnp.zeros_like(acc_ref)
        ty1_ref[...] = jnp.zeros_like(ty1_ref)
        cp.wait()
        _gather_tile(row_ref, 0, xs_ref, tx0_ref)

    @pl.when(first_ref[t] == 1)
    def _():
        for wcp in weight_copies(be_ref[t], wslot):
            wcp.wait()

    @pl.when((first_ref[t] == 1) & (nxt_ref[t] >= 0))
    def _():
        for wcp in weight_copies(nxt_ref[t], 1 - wslot):
            wcp.start()

    live = j <= nt_ref[c]
    args = (j, wslot, row_ref, sw_ref, wg_st, wu_st, wd_st, xs_ref, acc_ref)

    @pl.when(live & (lax.rem(j, 2) == 0))
    def _():
        _experts_step(*args, tx0_ref, tx1_ref, ty0_ref, ty1_ref)

    @pl.when(live & (lax.rem(j, 2) == 1))
    def _():
        _experts_step(*args, tx1_ref, tx0_ref, ty1_ref, ty0_ref)

    @pl.when(j == ntile - 1)
    def _():
        cp = pltpu.make_async_copy(acc_ref.at[pl.ds(0, rows)], o_hbm.at[c], sem.at[1])
        cp.start()
        cp.wait()


def _experts(block_e, ntiles, first, nxt_e, wslot, slot_row, slot_w, h2t, w_gate, w_up, w_down, ntile, n):
    b = h2t.shape[0]
    d, f = w_gate.shape[1], w_gate.shape[2]
    batch = lambda c, j, *_: (c, 0, 0)
    kern = functools.partial(_experts_kernel, ntile=ntile, n=n)
    slab = pltpu.VMEM((CHUNKS * SLAB, LANES), F32)
    hbm = pl.BlockSpec(memory_space=pl.ANY)
    grid_spec = pltpu.PrefetchScalarGridSpec(
        num_scalar_prefetch=5,
        grid=(b, ntile),
        in_specs=[pl.BlockSpec((1, ntile, SLOT_TILE), batch, memory_space=pltpu.SMEM),
                  pl.BlockSpec((1, ntile, SLOT_TILE), batch),
                  hbm, hbm, hbm, hbm],
        out_specs=hbm,
        scratch_shapes=[pltpu.VMEM(((n + 1) * CHUNKS, LANES), F32),
                        pltpu.VMEM(((n + 1) * CHUNKS, LANES), F32),
                        slab, slab, slab, slab,
                        pltpu.VMEM((2, d, f), F32), pltpu.VMEM((2, d, f), F32), pltpu.VMEM((2, f, d), F32),
                        pltpu.SemaphoreType.DMA((2,)), pltpu.SemaphoreType.DMA((2,))],
    )
    return pl.pallas_call(
        kern,
        grid_spec=grid_spec,
        out_shape=jax.ShapeDtypeStruct((b, n * CHUNKS, LANES), F32),
        compiler_params=_cparams(("arbitrary", "arbitrary")),
        name="experts",
    )(block_e, ntiles, first, nxt_e, wslot, slot_row, slot_w, h2t, w_gate, w_up, w_down)


PLAN_GROUP = 8


def _plan_tables_kernel(w_ref, re_ref, ro_ref, tab_ref, *, rows):
    ne = w_ref.shape[0]
    w2 = w_ref[...].reshape(ne * rows, LANES)
    selb = jnp.where(w2 > 0.0, 1.0, 0.0).astype(BF16)
    i0 = lax.broadcasted_iota(jnp.int32, (LANES, LANES), 0)
    i1 = lax.broadcasted_iota(jnp.int32, (LANES, LANES), 1)
    rowtot = _dot(selb, jnp.ones((LANES, LANES), BF16))
    ridx = lax.broadcasted_iota(jnp.int32, rowtot.shape, 0) & (rows - 1)
    acc = rowtot
    sh = 1
    while sh < rows:
        acc = acc + jnp.where(ridx >= sh, pltpu.roll(acc, sh, 0), 0.0)
        sh *= 2
    re_ref[0] = acc
    ro_ref[0] = acc - rowtot
    incl_t = _dot_nt(jnp.where(i1 <= i0, 1.0, 0.0).astype(BF16), selb)
    eye = jnp.where(i0 == i1, 1.0, 0.0).astype(BF16)
    w_hi = w2.astype(BF16)
    r1 = w2 - w_hi.astype(F32)
    w_mid = r1.astype(BF16)
    w_lo = (r1 - w_mid.astype(F32)).astype(BF16)
    parts = [_dot_nt(eye, p) for p in (w_hi, w_mid, w_lo)]
    for g in range(ne * rows // LANES):
        sl = slice(g * LANES, (g + 1) * LANES)
        tab_ref[0, g, 0:LANES, :] = incl_t[:, sl].astype(BF16)
        for k in range(3):
            tab_ref[0, g, (k + 1) * LANES:(k + 2) * LANES, :] = parts[k][:, sl].astype(BF16)


def _plan_tables(selw3, b):
    ne, r_all, _ = selw3.shape
    rows = r_all // b
    groups = ne * rows // LANES
    kern = functools.partial(_plan_tables_kernel, rows=rows)
    return pl.pallas_call(
        kern,
        grid=(b,),
        in_specs=[pl.BlockSpec((ne, rows, LANES), lambda c: (0, c, 0))],
        out_specs=[pl.BlockSpec((1, ne * rows, LANES), lambda c: (c, 0, 0)),
                   pl.BlockSpec((1, ne * rows, LANES), lambda c: (c, 0, 0)),
                   pl.BlockSpec((1, groups, 4 * LANES, LANES), lambda c: (c, 0, 0, 0))],
        out_shape=[jax.ShapeDtypeStruct((b, ne * rows, LANES), F32),
                   jax.ShapeDtypeStruct((b, ne * rows, LANES), F32),
                   jax.ShapeDtypeStruct((b, groups, 4 * LANES, LANES), BF16)],
        compiler_params=_cparams(("arbitrary",)),
        name="plan_tables",
    )(selw3)


def _plan_slots_kernel(be_ref, q0_ref, nt_ref, re_ref, ro_ref, tab_ref, tok_ref, sw_ref,
                       *, ntile, rows, n):
    c = pl.program_id(0)
    tok_ref[...] = jnp.full(tok_ref.shape, n * CHUNKS, jnp.int32)
    sw_ref[...] = jnp.zeros(sw_ref.shape, F32)
    lane = lax.broadcasted_iota(jnp.int32, (1, SLOT_TILE), 1).astype(F32)
    sub_r = lax.broadcasted_iota(jnp.int32, (rows, SLOT_TILE), 0).astype(F32)
    sub_l = lax.broadcasted_iota(jnp.int32, (LANES, SLOT_TILE), 0).astype(F32)
    reps = SLOT_TILE // LANES

    def tile_body(j):
        e = be_ref[c * ntile + j]
        q = q0_ref[c * ntile + j].astype(F32) + lane
        base = pl.multiple_of(e * rows, rows)
        row_end = jnp.concatenate([re_ref[0, pl.ds(base, rows), :]] * reps, axis=1)
        row_off = jnp.concatenate([ro_ref[0, pl.ds(base, rows), :]] * reps, axis=1)
        r = jnp.sum(jnp.where(row_end <= q, 1.0, 0.0), axis=0, keepdims=True)
        row_start = jnp.sum(jnp.where(sub_r == r, row_off, 0.0), axis=0, keepdims=True)
        per_group = LANES // rows
        g = lax.shift_right_logical(e, per_group.bit_length() - 1)
        col = r + ((e & (per_group - 1)) * rows).astype(F32)
        onehot = jnp.where(sub_l == col, 1.0, 0.0).astype(BF16)
        picked = _dot(tab_ref[0, g], onehot)
        incl = picked[0:LANES]
        lane_idx = jnp.sum(jnp.where(incl <= q - row_start, 1.0, 0.0), axis=0, keepdims=True)
        wrow = (picked[LANES:2 * LANES] + picked[2 * LANES:3 * LANES]) + picked[3 * LANES:]
        wsel = jnp.sum(jnp.where(sub_l == lane_idx, wrow, 0.0), axis=0, keepdims=True)
        valid = (q < row_end[rows - 1:rows, :]) & (j < nt_ref[c])
        tok = jnp.where(valid, r * LANES + lane_idx, float(n)).astype(jnp.int32)
        tok_ref[0, pl.ds(j, 1), :] = tok * CHUNKS
        sw_ref[0, pl.ds(j, 1), :] = jnp.where(valid, wsel, 0.0)

    def tile_group(i, carry):
        for k in range(PLAN_GROUP):
            tile_body(PLAN_GROUP * i + k)
        return carry

    lax.fori_loop(0, (nt_ref[c] + PLAN_GROUP - 1) // PLAN_GROUP, tile_group, 0)


def _plan_slots(block_e, tile_q0, ntiles, row_end, row_off, tables, ntile, n):
    b, er, _ = row_end.shape
    rows = n // LANES
    per_group = LANES // rows
    assert rows * per_group == LANES and per_group & (per_group - 1) == 0 and ntile % PLAN_GROUP == 0
    groups = tables.shape[1]
    kern = functools.partial(_plan_slots_kernel, ntile=ntile, rows=rows, n=n)
    grid_spec = pltpu.PrefetchScalarGridSpec(
        num_scalar_prefetch=3,
        grid=(b,),
        in_specs=[pl.BlockSpec((1, er, LANES), lambda c, *_: (c, 0, 0)),
                  pl.BlockSpec((1, er, LANES), lambda c, *_: (c, 0, 0)),
                  pl.BlockSpec((1, groups, 4 * LANES, LANES), lambda c, *_: (c, 0, 0, 0))],
        out_specs=[pl.BlockSpec((1, ntile, SLOT_TILE), lambda c, *_: (c, 0, 0)),
                   pl.BlockSpec((1, ntile, SLOT_TILE), lambda c, *_: (c, 0, 0))],
    )
    return pl.pallas_call(
        kern,
        grid_spec=grid_spec,
        out_shape=[jax.ShapeDtypeStruct((b, ntile, SLOT_TILE), jnp.int32),
                   jax.ShapeDtypeStruct((b, ntile, SLOT_TILE), F32)],
        compiler_params=_cparams(("arbitrary",)),
        name="plan_slots",
    )(block_e, tile_q0, ntiles, row_end, row_off, tables)


def _tile_meta(row_end, b, n, ntile):
    rows = n // LANES
    cnt = row_end.reshape(b, N_EXPERTS, rows, LANES)[:, :, rows - 1, 0].astype(jnp.int32)
    nt_e = (cnt + SLOT_TILE - 1) // SLOT_TILE
    tend = jnp.cumsum(nt_e, axis=1)
    tstart = tend - nt_e
    ntiles = tend[:, -1]
    tile_id = jnp.arange(ntile, dtype=jnp.int32)
    tid = jnp.minimum(tile_id[None, :], ntiles[:, None] - 1)
    be = jnp.minimum(jnp.sum(tend[:, None, :] <= tid[:, :, None], axis=2).astype(jnp.int32), N_EXPERTS - 1)
    eid = jnp.arange(N_EXPERTS, dtype=jnp.int32)
    is_be = be[:, :, None] == eid[None, None, :]
    pick = lambda per_expert: jnp.sum(jnp.where(is_be, per_expert[:, None, :], 0), axis=2)
    q0 = (tid - pick(tstart)) * SLOT_TILE
    valid = tile_id[None, :] < ntiles[:, None]
    prev_be = jnp.concatenate([jnp.full((b, 1), -1, jnp.int32), be[:, :-1]], axis=1)
    first = (valid & (be != prev_be)).astype(jnp.int32)
    wslot = (jnp.cumsum(first, axis=1) - 1) % 2
    later = (nt_e > 0)[:, None, :] & (eid[None, None, :] > eid[None, :, None])
    nxt = jnp.min(jnp.where(later, eid[None, None, :], N_EXPERTS), axis=2)
    nxt_e = pick(jnp.where(nxt < N_EXPERTS, nxt, -1))
    flat = lambda a: a.reshape(-1).astype(jnp.int32)
    return flat(be), flat(q0), ntiles.astype(jnp.int32), flat(first), flat(nxt_e), flat(wslot)


def _dispatch_plan(selw3, b, n, ntile):
    row_end, row_off, tables = _plan_tables(selw3, b)
    block_e, tile_q0, ntiles, first, nxt_e, wslot = _tile_meta(row_end, b, n, ntile)
    slot_tok, slot_w = _plan_slots(block_e, tile_q0, ntiles, row_end, row_off, tables, ntile, n)
    return block_e, ntiles, first, nxt_e, wslot, slot_tok, slot_w


def _final_kernel(h2_ref, r_ref, x1_ref, mod_ref, wg_ref, wu_ref, wd_ref, g_ref, o_ref):
    hb = h2_ref[0]
    tm = hb.shape[0]
    sh = _dot((_silu(_dot(hb, wg_ref[...])) * _dot(hb, wu_ref[...])).astype(BF16), wd_ref[...])
    routed = jnp.concatenate([r_ref[0, pl.ds(k, tm, stride=CHUNKS), :] for k in range(CHUNKS)], axis=1)
    moe = routed + sh
    o_ref[0] = x1_ref[0] + mod_ref[0, 5:6, :] * _rms(moe, g_ref[...])


def _final(h2, routed, x1, mod3, wsg_bf, wsu_bf, wsd_bf, g_post, tm=512):
    b, n, d = x1.shape
    f = wsg_bf.shape[1]
    row = lambda bi, i: (bi, i, 0)
    vec = lambda bi, i: (0, 0)
    return pl.pallas_call(
        _final_kernel,
        grid=(b, n // tm),
        in_specs=[pl.BlockSpec((1, tm, d), row),
                  pl.BlockSpec((1, tm * CHUNKS, LANES), row),
                  pl.BlockSpec((1, tm, d), row),
                  pl.BlockSpec((1, 6, d), lambda bi, i: (bi, 0, 0)),
                  pl.BlockSpec((d, f), vec),
                  pl.BlockSpec((d, f), vec),
                  pl.BlockSpec((f, d), vec),
                  pl.BlockSpec((1, d), vec)],
        out_specs=pl.BlockSpec((1, tm, d), row),
        out_shape=jax.ShapeDtypeStruct((b, n, d), F32),
        compiler_params=_cparams(("arbitrary", "arbitrary")),
        name="final",
    )(h2, routed, x1, mod3, wsg_bf, wsu_bf, wsd_bf, g_post)


def _rope_tables(n):
    t = np.arange(n)
    row = (t // GRID_W).astype(np.float32)
    col = (t % GRID_W).astype(np.float32)
    freqs = np.float32(ROPE_THETA) ** (-np.arange(0, AXIS_DIM, 2, dtype=np.float32) / np.float32(AXIS_DIM))
    ar = row[:, None] * freqs
    ac = col[:, None] * freqs
    zeros = np.zeros_like(ar)
    cos64 = np.concatenate([np.cos(ar), np.cos(ar), np.cos(ac), np.cos(ac)], axis=1)
    sa64 = np.concatenate([-np.sin(ar), zeros, -np.sin(ac), zeros], axis=1)
    sb64 = np.concatenate([zeros, np.sin(ar), zeros, np.sin(ac)], axis=1)
    two = lambda a: jnp.asarray(np.concatenate([a, a], axis=1).astype(np.float32))
    return two(cos64), two(sa64), two(sb64)


def kernel(x, c, ctx, c_ctx, w_ada, b_ada, g_pre_mix, g_post_mix, g_pre_ffn, g_post_ffn, w_in, lambda_q1, lambda_k1, lambda_q2, lambda_k2, g_subln, w_pool, b_pool, pool_scale, w_out, w_router, router_bias, w_e_gate, w_e_up, w_e_down, w_sh_gate, w_sh_up, w_sh_down):
    b, n, d = x.shape
    l = 0
    mod_rows = SUBLANES
    cin = jnp.concatenate([c, c_ctx[None, :], jnp.zeros((mod_rows - b - 1, d), F32)], axis=0)
    mod3 = _ada(cin, w_ada[l], b_ada[l]).reshape(mod_rows, 6, d)

    w_in_bf = w_in[l].astype(BF16)
    cos, sa, sb = _rope_tables(n)
    q, k, vt, p = _inproj(x, mod3, g_pre_mix[l][None, :], w_in_bf, cos * (QK_DIM ** -0.5 * LOG2E), cos, sa, sb)
    kc, vct = _ctxkv(ctx, mod3, g_pre_mix[l][None, :], w_in_bf, b)

    lamv = jnp.stack([lambda_q1[l], lambda_k1[l], lambda_q2[l], lambda_k2[l]], axis=0)
    att = _attn(q, k, kc, vt, vct, lamv, g_subln[l][:, None])

    x1, h2, h2t, logits_t = _outproj(
        att, p, x, mod3, w_pool[l].astype(BF16), b_pool[l][None, :], pool_scale[l][None, :],
        w_out[l].astype(BF16), g_post_mix[l][None, :], g_pre_ffn[l][None, :], w_router[l].T)

    t = b * n
    selw3 = _route(logits_t.reshape(N_EXPERTS, t // LANES, LANES),
                   jnp.broadcast_to(router_bias[l][:, None, None], (N_EXPERTS, 1, LANES)))

    ntile = (n * TOP_K) // SLOT_TILE + N_EXPERTS
    block_e, ntiles, first, nxt_e, wslot, slot_row, slot_w = _dispatch_plan(selw3, b, n, ntile)
    routed = _experts(block_e, ntiles, first, nxt_e, wslot, slot_row, slot_w, h2t,
                      w_e_gate[l], w_e_up[l], w_e_down[l], ntile, n)

    return _final(h2, routed, x1, mod3, w_sh_gate[l].astype(BF16), w_sh_up[l].astype(BF16),
                  w_sh_down[l].astype(BF16), g_post_ffn[l][None, :])
```

```python
import functools
import math

import jax
import jax.numpy as jnp
import numpy as np
from jax import lax
from jax.experimental import pallas as pl
from jax.experimental.pallas import tpu as pltpu

F32 = jnp.float32
BF16 = jnp.bfloat16

D_MODEL = 1024
GRID_W = 64
N_HEADS = 4
QK_DIM = 64
V_DIM = 128
ATTN_WIDTH = N_HEADS * V_DIM
POOL_WIDTH = D_MODEL - ATTN_WIDTH
POOL_WINDOWS = (2, 4, 8, 16)
POOL_GROUP = POOL_WIDTH // len(POOL_WINDOWS)
AXIS_DIM = QK_DIM // 2
ROPE_THETA = 10000.0
N_EXPERTS = 64
TOP_K = 8
N_GROUPS = 8
GROUP_SIZE = N_EXPERTS // N_GROUPS
TOPK_GROUPS = 4
EXPERT_DIM = 256
ROUTED_SCALE = 2.5
EPS = 1e-6
LAM_INIT = 0.8 - 0.6 * math.exp(-0.3 * 0)

LANES = 128
SUBLANES = 8
SLOT_TILE = 256
HALO = 16
VMEM_LIMIT = 56 * 1024 * 1024


def _cparams(sem, vmem=VMEM_LIMIT):
    return pltpu.CompilerParams(dimension_semantics=sem, vmem_limit_bytes=vmem)


def _split(a):
    hi = a.astype(BF16)
    lo = (a - hi.astype(F32)).astype(BF16)
    return hi, lo


def _dot(a, b):
    return jnp.dot(a, b, preferred_element_type=F32)


def _dot_nt(a, b):
    return lax.dot_general(a, b, (((1,), (1,)), ((), ())), preferred_element_type=F32)


def _dot3(a, b, nt=False):
    d = _dot_nt if nt else _dot
    ah, al = _split(a)
    bh, bl = _split(b)
    return d(ah, bh) + d(ah, bl) + d(al, bh)


def _rms(x, g):
    return x * lax.rsqrt(jnp.mean(x * x, axis=-1, keepdims=True) + EPS) * g


def _silu(x):
    return x * (1.0 / (1.0 + jnp.exp(-x)))


def _ada_kernel(c_ref, w_ref, b_ref, o_ref):
    o_ref[...] = _dot3(_silu(c_ref[...]), w_ref[...]) + b_ref[...]


def _ada(cin, w_ada, b_ada):
    rows, d = cin.shape
    n = w_ada.shape[1]
    tn = 1024
    return pl.pallas_call(
        _ada_kernel,
        grid=(n // tn,),
        in_specs=[pl.BlockSpec((rows, d), lambda j: (0, 0)),
                  pl.BlockSpec((d, tn), lambda j: (0, j)),
                  pl.BlockSpec((1, tn), lambda j: (0, j))],
        out_specs=pl.BlockSpec((rows, tn), lambda j: (0, j)),
        out_shape=jax.ShapeDtypeStruct((rows, n), F32),
        compiler_params=_cparams(("arbitrary",)),
        name="ada",
    )(cin, w_ada, b_ada.reshape(1, n))


def _rope(t, cos, sa, sb):
    return t * cos + pltpu.roll(t, LANES - AXIS_DIM // 2, 1) * sa + pltpu.roll(t, AXIS_DIM // 2, 1) * sb


def _inproj_kernel(x_ref, mod_ref, g_ref, w_ref, cq_ref, ck_ref, sa_ref, sb_ref,
                   q_ref, k_ref, vt_ref, p_ref):
    x = x_ref[0]
    h = _rms(x, g_ref[...]) * (1.0 + mod_ref[0, 1:2, :]) + mod_ref[0, 0:1, :]
    px = _dot(h.astype(BF16), w_ref[...])
    ck = ck_ref[...]
    sa = sa_ref[...]
    sb = sb_ref[...]
    cq = cq_ref[...]
    scale = QK_DIM ** -0.5 * LOG2E
    saq = sa * scale
    sbq = sb * scale
    for hh in range(N_HEADS):
        lo = hh * LANES
        q = px[:, lo:lo + LANES]
        k = px[:, ATTN_WIDTH + lo:ATTN_WIDTH + lo + LANES]
        q_ref[0, :, lo:lo + LANES] = _rope(q, cq, saq, sbq).astype(BF16)
        k_ref[0, :, lo:lo + LANES] = _rope(k, ck, sa, sb).astype(BF16)
    vt_ref[0] = jnp.transpose(px[:, 2 * ATTN_WIDTH:3 * ATTN_WIDTH]).astype(BF16)
    p_ref[0] = px[:, 3 * ATTN_WIDTH:].astype(BF16)


def _inproj(x, mod3, g, w_in_bf, cq, ck, sa, sb, tm=1024):
    b, n, d = x.shape
    width = w_in_bf.shape[1]
    row = lambda bi, i: (bi, i, 0)
    tab = pl.BlockSpec((tm, LANES), lambda bi, i: (i, 0))
    out = jax.ShapeDtypeStruct((b, n, ATTN_WIDTH), BF16)
    return pl.pallas_call(
        _inproj_kernel,
        grid=(b, n // tm),
        in_specs=[pl.BlockSpec((1, tm, d), row),
                  pl.BlockSpec((1, 6, d), lambda bi, i: (bi, 0, 0)),
                  pl.BlockSpec((1, d), lambda bi, i: (0, 0)),
                  pl.BlockSpec((d, width), lambda bi, i: (0, 0)),
                  tab, tab, tab, tab],
        out_specs=[pl.BlockSpec((1, tm, ATTN_WIDTH), row),
                   pl.BlockSpec((1, tm, ATTN_WIDTH), row),
                   pl.BlockSpec((1, ATTN_WIDTH, tm), lambda bi, i: (bi, 0, i)),
                   pl.BlockSpec((1, tm, ATTN_WIDTH), row)],
        out_shape=[out, out, jax.ShapeDtypeStruct((b, ATTN_WIDTH, n), BF16), out],
        compiler_params=_cparams(("arbitrary", "arbitrary")),
        name="inproj",
    )(x, mod3, g, w_in_bf, cq, ck, sa, sb)


def _ctxkv_kernel(x_ref, mod_ref, g_ref, wk_ref, wv_ref, k_ref, vt_ref):
    h = _rms(x_ref[0], g_ref[...]) * (1.0 + mod_ref[0, 1:2, :]) + mod_ref[0, 0:1, :]
    hb = h.astype(BF16)
    k_ref[0] = _dot(hb, wk_ref[...]).astype(BF16)
    vt_ref[0] = jnp.transpose(_dot(hb, wv_ref[...])).astype(BF16)


def _ctxkv(ctx, mod3, g, w_in_bf, ctx_row):
    b, n, d = ctx.shape
    out = jax.ShapeDtypeStruct((b, n, ATTN_WIDTH), BF16)
    return pl.pallas_call(
        _ctxkv_kernel,
        grid=(b,),
        in_specs=[pl.BlockSpec((1, n, d), lambda bi: (bi, 0, 0)),
                  pl.BlockSpec((1, 6, d), lambda bi: (ctx_row, 0, 0)),
                  pl.BlockSpec((1, d), lambda bi: (0, 0)),
                  pl.BlockSpec((d, ATTN_WIDTH), lambda bi: (0, 1)),
                  pl.BlockSpec((d, ATTN_WIDTH), lambda bi: (0, 2))],
        out_specs=[pl.BlockSpec((1, n, ATTN_WIDTH), lambda bi: (bi, 0, 0)),
                   pl.BlockSpec((1, ATTN_WIDTH, n), lambda bi: (bi, 0, 0))],
        out_shape=[out, jax.ShapeDtypeStruct((b, ATTN_WIDTH, n), BF16)],
        compiler_params=_cparams(("arbitrary",)),
        name="ctxkv",
    )(ctx, mod3, g, w_in_bf, w_in_bf)


LOG2E = math.log2(math.e)
KEY_CHUNK = 256


def _attn_step(q_ref, kx_ref, kc_ref, vt_ref, vct_ref, lam_ref, g_ref, o_ref, s_new, m_new, s_old, m_old):
    nc = kc_ref.shape[1]
    tq = q_ref.shape[1]
    nk = s_old.shape[1]
    groups = KEY_CHUNK // SUBLANES
    q = q_ref[0]
    lane = lax.broadcasted_iota(jnp.int32, q.shape, 1)
    zero = jnp.zeros_like(q)
    qms = (jnp.where(lane < QK_DIM, q, zero), jnp.where(lane >= QK_DIM, q, zero))
    m_prev = (m_old[0], m_old[1])
    top = [jnp.full((SUBLANES, tq), -jnp.inf, F32) for _ in range(2)]
    den = [jnp.zeros((SUBLANES, tq), F32) for _ in range(2)]
    acc = [jnp.zeros((V_DIM, tq), F32) for _ in range(2)]
    for c in range(nk // KEY_CHUNK):
        lo = c * KEY_CHUNK
        if lo < nc:
            keys, vt = kc_ref[0, lo:lo + KEY_CHUNK, :], vct_ref[0, :, lo:lo + KEY_CHUNK]
        else:
            keys, vt = kx_ref[0, lo - nc:lo - nc + KEY_CHUNK, :], vt_ref[0, :, lo - nc:lo - nc + KEY_CHUNK]
        for mp in range(2):
            s = _dot_nt(keys, qms[mp])
            s_new[mp, lo:lo + KEY_CHUNK, :] = s
            top[mp] = jnp.maximum(top[mp], jnp.max(s.reshape(groups, SUBLANES, tq), axis=0))
            p = jnp.exp2(s_old[mp, lo:lo + KEY_CHUNK, :] - m_prev[mp])
            den[mp] = den[mp] + jnp.sum(p.reshape(groups, SUBLANES, tq), axis=0)
            acc[mp] = acc[mp] + _dot(vt, p.astype(BF16))
    for mp in range(2):
        m_new[mp] = jnp.max(top[mp], axis=0, keepdims=True)
    lv = lam_ref[...]
    lam = (jnp.exp(jnp.sum(lv[0:1] * lv[1:2], axis=-1, keepdims=True))
           - jnp.exp(jnp.sum(lv[2:3] * lv[3:4], axis=-1, keepdims=True)) + LAM_INIT)
    l1 = jnp.sum(den[0], axis=0, keepdims=True)
    l2 = jnp.sum(den[1], axis=0, keepdims=True)
    o = acc[0] * (1.0 / l1) - acc[1] * (lam / l2)
    o = o * lax.rsqrt(jnp.mean(o * o, axis=0, keepdims=True) + EPS) * (g_ref[...] * (1.0 - LAM_INIT))
    o_ref[0] = jnp.transpose(o).astype(BF16)


def _attn_kernel(q_ref, kx_ref, kc_ref, vt_ref, vct_ref, lam_ref, g_ref, o_ref, s0_ref, m0_ref, s1_ref, m1_ref):
    t = pl.program_id(0)
    io = (q_ref, kx_ref, kc_ref, vt_ref, vct_ref, lam_ref, g_ref, o_ref)

    @pl.when(t == 0)
    def _():
        s1_ref[...] = jnp.zeros_like(s1_ref)
        m1_ref[...] = jnp.zeros_like(m1_ref)

    @pl.when(lax.rem(t, 2) == 0)
    def _():
        _attn_step(*io, s0_ref, m0_ref, s1_ref, m1_ref)

    @pl.when(lax.rem(t, 2) == 1)
    def _():
        _attn_step(*io, s1_ref, m1_ref, s0_ref, m0_ref)


def _attn(q, k, kc, vt, vct, lamv, g_col, tq=512):
    b, n, _ = q.shape
    nc = kc.shape[1]
    nq = n // tq
    total = b * N_HEADS * nq

    def cur(t):
        ta = jnp.minimum(t, total - 1)
        bh = ta // nq
        return bh // N_HEADS, bh % N_HEADS, ta % nq

    def prv(t):
        tb = jnp.maximum(t - 1, 0)
        bh = tb // nq
        return bh // N_HEADS, bh % N_HEADS, tb % nq

    def at(fn, order):
        return lambda t: tuple((fn(t) + (0,))[j] for j in order)

    return pl.pallas_call(
        _attn_kernel,
        grid=(total + 1,),
        in_specs=[pl.BlockSpec((1, tq, LANES), at(cur, (0, 2, 1))),
                  pl.BlockSpec((1, n, LANES), at(cur, (0, 3, 1))),
                  pl.BlockSpec((1, nc, LANES), at(cur, (0, 3, 1))),
                  pl.BlockSpec((1, V_DIM, n), at(prv, (0, 1, 3))),
                  pl.BlockSpec((1, V_DIM, nc), at(prv, (0, 1, 3))),
                  pl.BlockSpec((4, QK_DIM), lambda t: (0, 0)),
                  pl.BlockSpec((V_DIM, 1), lambda t: (0, 0))],
        out_specs=pl.BlockSpec((1, tq, LANES), at(prv, (0, 2, 1))),
        out_shape=jax.ShapeDtypeStruct((b, n, ATTN_WIDTH), BF16),
        scratch_shapes=[pltpu.VMEM((2, nc + n, tq), F32), pltpu.VMEM((2, 1, tq), F32),
                        pltpu.VMEM((2, nc + n, tq), F32), pltpu.VMEM((2, 1, tq), F32)],
        compiler_params=_cparams(("arbitrary",)),
        name="attn",
    )(q, k, kc, vt, vct, lamv, g_col)


def _outproj_kernel(att_ref, p_ref, pprev_ref, pnext_ref, x_ref, mod_ref, wpool_ref, bpool_ref,
                    pscale_ref, wout_ref, gpost_ref, gpre_ref, wr_ref,
                    x1_ref, h2_ref, h2t_ref, lg_ref, *, tm, n):
    i = pl.program_id(1)
    p = p_ref[0].astype(F32)
    prev = jnp.where(i > 0, pprev_ref[0].astype(F32), 0.0)
    nxt = jnp.where(i < pl.num_programs(1) - 1, pnext_ref[0].astype(F32), 0.0)
    ext = jnp.concatenate([prev, p, nxt], axis=0)
    rows = tm + 2 * HALO
    trow = (i * tm + lax.broadcasted_iota(jnp.int32, (tm, 1), 0))

    pooled = []
    sums = {}
    acc = ext + pltpu.roll(ext, 1, 0)
    sums[2] = acc
    w = 2
    while w < POOL_WINDOWS[-1]:
        acc = pltpu.roll(acc, rows - w // 2, 0) + pltpu.roll(acc, w // 2, 0)
        w = 2 * w
        sums[w] = acc
    for gi, win in enumerate(POOL_WINDOWS):
        sl = slice(gi * POOL_GROUP, (gi + 1) * POOL_GROUP)
        wsum = sums[win][HALO:HALO + tm, sl]
        hi_c = jnp.minimum(trow + (win - win // 2), n)
        lo_c = jnp.maximum(trow - win // 2, 0)
        cnt = (hi_c - lo_c).astype(F32)
        d = wsum / cnt - p[:, sl]
        y = _dot(d.astype(BF16), wpool_ref[gi]) + bpool_ref[:, sl]
        pooled.append((y * pscale_ref[:, sl]).astype(BF16))
    pool = jnp.concatenate(pooled, axis=1)
    yx = _dot(att_ref[0], wout_ref[0:ATTN_WIDTH, :]) + _dot(pool, wout_ref[ATTN_WIDTH:, :])
    x1 = x_ref[0] + mod_ref[0, 2:3, :] * _rms(yx, gpost_ref[...])
    x1_ref[0] = x1
    h2 = _rms(x1, gpre_ref[...]) * (1.0 + mod_ref[0, 4:5, :]) + mod_ref[0, 3:4, :]
    h2_ref[0] = h2.astype(BF16)
    for k in range(CHUNKS):
        h2t_ref[0, pl.ds(k, tm, stride=CHUNKS), :] = h2[:, k * LANES:(k + 1) * LANES]
    lg_ref[...] = _dot3(wr_ref[...], h2, nt=True)


def _outproj(att, p, x, mod3, w_pool_bf, b_pool, pool_scale, w_out_bf, g_post, g_pre, w_router_t, tm=1024):
    b, n, d = x.shape
    hb = tm // HALO
    nhb = n // HALO
    row = lambda bi, i: (bi, i, 0)
    vec = lambda bi, i: (0, 0)
    kern = functools.partial(_outproj_kernel, tm=tm, n=n)
    return pl.pallas_call(
        kern,
        grid=(b, n // tm),
        in_specs=[pl.BlockSpec((1, tm, ATTN_WIDTH), row),
                  pl.BlockSpec((1, tm, POOL_WIDTH), row),
                  pl.BlockSpec((1, HALO, POOL_WIDTH), lambda bi, i: (bi, jnp.maximum(i * hb - 1, 0), 0)),
                  pl.BlockSpec((1, HALO, POOL_WIDTH), lambda bi, i: (bi, jnp.minimum((i + 1) * hb, nhb - 1), 0)),
                  pl.BlockSpec((1, tm, d), row),
                  pl.BlockSpec((1, 6, d), lambda bi, i: (bi, 0, 0)),
                  pl.BlockSpec((len(POOL_WINDOWS), POOL_GROUP, POOL_GROUP), lambda bi, i: (0, 0, 0)),
                  pl.BlockSpec((1, POOL_WIDTH), vec),
                  pl.BlockSpec((1, POOL_WIDTH), vec),
                  pl.BlockSpec((d, d), vec),
                  pl.BlockSpec((1, d), vec),
                  pl.BlockSpec((1, d), vec),
                  pl.BlockSpec((N_EXPERTS, d), vec)],
        out_specs=[pl.BlockSpec((1, tm, d), row),
                   pl.BlockSpec((1, tm, d), row),
                   pl.BlockSpec((1, tm * CHUNKS, LANES), row),
                   pl.BlockSpec((N_EXPERTS, tm), lambda bi, i: (0, bi * (n // tm) + i))],
        out_shape=[jax.ShapeDtypeStruct((b, n, d), F32),
                   jax.ShapeDtypeStruct((b, n, d), BF16),
                   jax.ShapeDtypeStruct((b, n * CHUNKS, LANES), F32),
                   jax.ShapeDtypeStruct((N_EXPERTS, b * n), F32)],
        compiler_params=_cparams(("arbitrary", "arbitrary")),
        name="outproj",
    )(att, p, p, p, x, mod3, w_pool_bf, b_pool, pool_scale, w_out_bf, g_post, g_pre, w_router_t)


def _beats(a, b, a_first):
    return jnp.where((a >= b) if a_first else (a > b), 1.0, 0.0)


def _route_kernel(lg_ref, bias_ref, w_ref):
    scores = [1.0 / (1.0 + jnp.exp(-lg_ref[e])) for e in range(N_EXPERTS)]
    biased = [scores[e] + bias_ref[e] for e in range(N_EXPERTS)]
    gscore = []
    for g in range(N_GROUPS):
        vals = biased[g * GROUP_SIZE:(g + 1) * GROUP_SIZE]
        m1 = vals[0]
        m2 = jnp.full_like(m1, -jnp.inf)
        for v in vals[1:]:
            m2 = jnp.maximum(m2, jnp.minimum(m1, v))
            m1 = jnp.maximum(m1, v)
        gscore.append(m1 + m2)
    gsel = []
    for g in range(N_GROUPS):
        rank = jnp.zeros_like(gscore[g])
        for g2 in range(N_GROUPS):
            if g2 != g:
                rank = rank + _beats(gscore[g2], gscore[g], g2 < g)
        gsel.append(rank < TOPK_GROUPS)
    vals = [jnp.where(gsel[e // GROUP_SIZE], biased[e], -jnp.inf) for e in range(N_EXPERTS)]
    chosen = [jnp.zeros(vals[0].shape, jnp.bool_) for _ in range(N_EXPERTS)]
    for _ in range(TOP_K):
        top = vals[0]
        for v in vals[1:]:
            top = jnp.maximum(top, v)
        found = jnp.zeros(top.shape, jnp.bool_)
        for e in range(N_EXPERTS):
            hit = (vals[e] == top) & jnp.logical_not(found)
            found = found | hit
            chosen[e] = chosen[e] | hit
            vals[e] = jnp.where(hit, -jnp.inf, vals[e])
    picked = [jnp.where(chosen[e], scores[e], 0.0) for e in range(N_EXPERTS)]
    denom = picked[0]
    for e in range(1, N_EXPERTS):
        denom = denom + picked[e]
    for e in range(N_EXPERTS):
        w_ref[e] = picked[e] / denom * ROUTED_SCALE


def _route(logits3, bias3, rows=SUBLANES):
    e, r, l = logits3.shape
    return pl.pallas_call(
        _route_kernel,
        grid=(r // rows,),
        in_specs=[pl.BlockSpec((e, rows, l), lambda i: (0, i, 0)),
                  pl.BlockSpec((e, 1, l), lambda i: (0, 0, 0))],
        out_specs=pl.BlockSpec((e, rows, l), lambda i: (0, i, 0)),
        out_shape=jax.ShapeDtypeStruct((e, r, l), F32),
        compiler_params=_cparams(("arbitrary",)),
        name="route",
    )(logits3, bias3)


CHUNKS = D_MODEL // LANES
SLAB = SLOT_TILE + 4
ROW_BATCH = 8


def _gather_tile(row_ref, tile, xs_ref, tx_ref):
    for s in range(SLOT_TILE):
        row = pl.multiple_of(row_ref[0, tile, s], CHUNKS)
        tx_ref[pl.ds(s, CHUNKS, stride=SLAB), :] = xs_ref[pl.ds(row, CHUNKS), :]


def _experts_step(j, wslot, row_ref, sw_ref, wg_st, wu_st, wd_st, xs_ref, acc_ref,
                  tx_cur, tx_nxt, ty_cur, ty_prv):
    ntile = row_ref.shape[1]
    _gather_tile(row_ref, jnp.minimum(j + 1, ntile - 1), xs_ref, tx_nxt)
    xb = jnp.concatenate([tx_cur[pl.ds(k * SLAB, SLOT_TILE), :] for k in range(CHUNKS)], axis=1).astype(BF16)
    g = _dot(xb, wg_st[wslot].astype(BF16))
    u = _dot(xb, wu_st[wslot].astype(BF16))
    h = (_silu(g) * u).astype(BF16)
    y = _dot(h, wd_st[wslot].astype(BF16))
    w_rows = jnp.transpose(jnp.broadcast_to(sw_ref[0, pl.ds(j, 1), :], (LANES, SLOT_TILE)))
    for k in range(CHUNKS):
        ty_cur[pl.ds(k * SLAB, SLOT_TILE), :] = y[:, k * LANES:(k + 1) * LANES] * w_rows
    prv = jnp.maximum(j - 1, 0)
    for s0 in range(0, SLOT_TILE, ROW_BATCH):
        new = []
        for s in range(s0, s0 + ROW_BATCH):
            row = pl.multiple_of(row_ref[0, prv, s], CHUNKS)
            new.append((row, acc_ref[pl.ds(row, CHUNKS), :] + ty_prv[pl.ds(s, CHUNKS, stride=SLAB), :]))
        for row, v in new:
            acc_ref[pl.ds(row, CHUNKS), :] = v


def _experts_kernel(be_ref, nt_ref, first_ref, nxt_ref, ws_ref, row_ref, sw_ref, x_hbm,
                    wg_hbm, wu_hbm, wd_hbm, o_hbm,
                    xs_ref, acc_ref, tx0_ref, tx1_ref, ty0_ref, ty1_ref, wg_st, wu_st, wd_st,
                    sem, wsem, *, ntile, n):
    c = pl.program_id(0)
    j = pl.program_id(1)
    t = c * ntile + j
    rows = n * CHUNKS
    wslot = ws_ref[t]

    def weight_copies(e, slot):
        return [pltpu.make_async_copy(src.at[e], dst.at[slot], wsem.at[slot])
                for src, dst in ((wg_hbm, wg_st), (wu_hbm, wu_st), (wd_hbm, wd_st))]

    @pl.when(j == 0)
    def _():
        for wcp in weight_copies(be_ref[t], 0):
            wcp.start()
        cp = pltpu.make_async_copy(x_hbm.at[c], xs_ref.at[pl.ds(0, rows)], sem.at[0])
        cp.start()
        xs_ref[pl.ds(rows, CHUNKS), :] = jnp.zeros((CHUNKS, LANES), F32)
        acc_ref[...] = jnp.zeros_like(acc_ref)
        ty1_ref[...] = jnp.zeros_like(ty1_ref)
        cp.wait()
        _gather_tile(row_ref, 0, xs_ref, tx0_ref)

    @pl.when(first_ref[t] == 1)
    def _():
        for wcp in weight_copies(be_ref[t], wslot):
            wcp.wait()

    @pl.when((first_ref[t] == 1) & (nxt_ref[t] >= 0))
    def _():
        for wcp in weight_copies(nxt_ref[t], 1 - wslot):
            wcp.start()

    live = j <= nt_ref[c]
    args = (j, wslot, row_ref, sw_ref, wg_st, wu_st, wd_st, xs_ref, acc_ref)

    @pl.when(live & (lax.rem(j, 2) == 0))
    def _():
        _experts_step(*args, tx0_ref, tx1_ref, ty0_ref, ty1_ref)

    @pl.when(live & (lax.rem(j, 2) == 1))
    def _():
        _experts_step(*args, tx1_ref, tx0_ref, ty1_ref, ty0_ref)

    @pl.when(j == ntile - 1)
    def _():
        cp = pltpu.make_async_copy(acc_ref.at[pl.ds(0, rows)], o_hbm.at[c], sem.at[1])
        cp.start()
        cp.wait()


def _experts(block_e, ntiles, first, nxt_e, wslot, slot_row, slot_w, h2t, w_gate, w_up, w_down, ntile, n):
    b = h2t.shape[0]
    d, f = w_gate.shape[1], w_gate.shape[2]
    batch = lambda c, j, *_: (c, 0, 0)
    kern = functools.partial(_experts_kernel, ntile=ntile, n=n)
    slab = pltpu.VMEM((CHUNKS * SLAB, LANES), F32)
    hbm = pl.BlockSpec(memory_space=pl.ANY)
    grid_spec = pltpu.PrefetchScalarGridSpec(
        num_scalar_prefetch=5,
        grid=(b, ntile),
        in_specs=[pl.BlockSpec((1, ntile, SLOT_TILE), batch, memory_space=pltpu.SMEM),
                  pl.BlockSpec((1, ntile, SLOT_TILE), batch),
                  hbm, hbm, hbm, hbm],
        out_specs=hbm,
        scratch_shapes=[pltpu.VMEM(((n + 1) * CHUNKS, LANES), F32),
                        pltpu.VMEM(((n + 1) * CHUNKS, LANES), F32),
                        slab, slab, slab, slab,
                        pltpu.VMEM((2, d, f), F32), pltpu.VMEM((2, d, f), F32), pltpu.VMEM((2, f, d), F32),
                        pltpu.SemaphoreType.DMA((2,)), pltpu.SemaphoreType.DMA((2,))],
    )
    return pl.pallas_call(
        kern,
        grid_spec=grid_spec,
        out_shape=jax.ShapeDtypeStruct((b, n * CHUNKS, LANES), F32),
        compiler_params=_cparams(("arbitrary", "arbitrary")),
        name="experts",
    )(block_e, ntiles, first, nxt_e, wslot, slot_row, slot_w, h2t, w_gate, w_up, w_down)


PLAN_GROUP = 16


def _plan_tables_kernel(w_ref, re_ref, ro_ref, tab_ref, *, rows):
    ne = w_ref.shape[0]
    w2 = w_ref[...].reshape(ne * rows, LANES)
    selb = jnp.where(w2 > 0.0, 1.0, 0.0).astype(BF16)
    i0 = lax.broadcasted_iota(jnp.int32, (LANES, LANES), 0)
    i1 = lax.broadcasted_iota(jnp.int32, (LANES, LANES), 1)
    rowtot = _dot(selb, jnp.ones((LANES, LANES), BF16))
    ridx = lax.broadcasted_iota(jnp.int32, rowtot.shape, 0) & (rows - 1)
    acc = rowtot
    sh = 1
    while sh < rows:
        acc = acc + jnp.where(ridx >= sh, pltpu.roll(acc, sh, 0), 0.0)
        sh *= 2
    re_ref[0] = acc
    ro_ref[0] = acc - rowtot
    incl_t = _dot_nt(jnp.where(i1 <= i0, 1.0, 0.0).astype(BF16), selb)
    eye = jnp.where(i0 == i1, 1.0, 0.0).astype(BF16)
    w_hi = w2.astype(BF16)
    r1 = w2 - w_hi.astype(F32)
    w_mid = r1.astype(BF16)
    w_lo = (r1 - w_mid.astype(F32)).astype(BF16)
    parts = [_dot_nt(eye, p) for p in (w_hi, w_mid, w_lo)]
    for g in range(ne * rows // LANES):
        sl = slice(g * LANES, (g + 1) * LANES)
        tab_ref[0, g, 0:LANES, :] = incl_t[:, sl].astype(BF16)
        for k in range(3):
            tab_ref[0, g, (k + 1) * LANES:(k + 2) * LANES, :] = parts[k][:, sl].astype(BF16)


def _plan_tables(selw3, b):
    ne, r_all, _ = selw3.shape
    rows = r_all // b
    groups = ne * rows // LANES
    kern = functools.partial(_plan_tables_kernel, rows=rows)
    return pl.pallas_call(
        kern,
        grid=(b,),
        in_specs=[pl.BlockSpec((ne, rows, LANES), lambda c: (0, c, 0))],
        out_specs=[pl.BlockSpec((1, ne * rows, LANES), lambda c: (c, 0, 0)),
                   pl.BlockSpec((1, ne * rows, LANES), lambda c: (c, 0, 0)),
                   pl.BlockSpec((1, groups, 4 * LANES, LANES), lambda c: (c, 0, 0, 0))],
        out_shape=[jax.ShapeDtypeStruct((b, ne * rows, LANES), F32),
                   jax.ShapeDtypeStruct((b, ne * rows, LANES), F32),
                   jax.ShapeDtypeStruct((b, groups, 4 * LANES, LANES), BF16)],
        compiler_params=_cparams(("arbitrary",)),
        name="plan_tables",
    )(selw3)


def _plan_slots_kernel(be_ref, q0_ref, nt_ref, re_ref, ro_ref, tab_ref, tok_ref, sw_ref,
                       *, ntile, rows, n):
    c = pl.program_id(0)
    tok_ref[...] = jnp.full(tok_ref.shape, n * CHUNKS, jnp.int32)
    sw_ref[...] = jnp.zeros(sw_ref.shape, F32)
    lane = lax.broadcasted_iota(jnp.int32, (1, SLOT_TILE), 1).astype(F32)
    sub_r = lax.broadcasted_iota(jnp.int32, (rows, SLOT_TILE), 0).astype(F32)
    sub_l = lax.broadcasted_iota(jnp.int32, (LANES, SLOT_TILE), 0).astype(F32)
    reps = SLOT_TILE // LANES

    def tile_body(j):
        e = be_ref[c * ntile + j]
        q = q0_ref[c * ntile + j].astype(F32) + lane
        base = pl.multiple_of(e * rows, rows)
        row_end = jnp.concatenate([re_ref[0, pl.ds(base, rows), :]] * reps, axis=1)
        row_off = jnp.concatenate([ro_ref[0, pl.ds(base, rows), :]] * reps, axis=1)
        r = jnp.sum(jnp.where(row_end <= q, 1.0, 0.0), axis=0, keepdims=True)
        row_start = jnp.sum(jnp.where(sub_r == r, row_off, 0.0), axis=0, keepdims=True)
        per_group = LANES // rows
        g = lax.shift_right_logical(e, per_group.bit_length() - 1)
        col = r + ((e & (per_group - 1)) * rows).astype(F32)
        onehot = jnp.where(sub_l == col, 1.0, 0.0).astype(BF16)
        picked = _dot(tab_ref[0, g], onehot)
        incl = picked[0:LANES]
        lane_idx = jnp.sum(jnp.where(incl <= q - row_start, 1.0, 0.0), axis=0, keepdims=True)
        wrow = (picked[LANES:2 * LANES] + picked[2 * LANES:3 * LANES]) + picked[3 * LANES:]
        wsel = jnp.sum(jnp.where(sub_l == lane_idx, wrow, 0.0), axis=0, keepdims=True)
        valid = (q < row_end[rows - 1:rows, :]) & (j < nt_ref[c])
        tok = jnp.where(valid, r * LANES + lane_idx, float(n)).astype(jnp.int32)
        tok_ref[0, pl.ds(j, 1), :] = tok * CHUNKS
        sw_ref[0, pl.ds(j, 1), :] = jnp.where(valid, wsel, 0.0)

    def tile_group(i, carry):
        for k in range(PLAN_GROUP):
            tile_body(PLAN_GROUP * i + k)
        return carry

    lax.fori_loop(0, (nt_ref[c] + PLAN_GROUP - 1) // PLAN_GROUP, tile_group, 0)


def _plan_slots(block_e, tile_q0, ntiles, row_end, row_off, tables, ntile, n):
    b, er, _ = row_end.shape
    rows = n // LANES
    per_group = LANES // rows
    assert rows * per_group == LANES and per_group & (per_group - 1) == 0 and ntile % PLAN_GROUP == 0
    groups = tables.shape[1]
    kern = functools.partial(_plan_slots_kernel, ntile=ntile, rows=rows, n=n)
    grid_spec = pltpu.PrefetchScalarGridSpec(
        num_scalar_prefetch=3,
        grid=(b,),
        in_specs=[pl.BlockSpec((1, er, LANES), lambda c, *_: (c, 0, 0)),
                  pl.BlockSpec((1, er, LANES), lambda c, *_: (c, 0, 0)),
                  pl.BlockSpec((1, groups, 4 * LANES, LANES), lambda c, *_: (c, 0, 0, 0))],
        out_specs=[pl.BlockSpec((1, ntile, SLOT_TILE), lambda c, *_: (c, 0, 0)),
                   pl.BlockSpec((1, ntile, SLOT_TILE), lambda c, *_: (c, 0, 0))],
    )
    return pl.pallas_call(
        kern,
        grid_spec=grid_spec,
        out_shape=[jax.ShapeDtypeStruct((b, ntile, SLOT_TILE), jnp.int32),
                   jax.ShapeDtypeStruct((b, ntile, SLOT_TILE), F32)],
        compiler_params=_cparams(("arbitrary",)),
        name="plan_slots",
    )(block_e, tile_q0, ntiles, row_end, row_off, tables)


def _tile_meta(row_end, b, n, ntile):
    rows = n // LANES
    cnt = row_end.reshape(b, N_EXPERTS, rows, LANES)[:, :, rows - 1, 0].astype(jnp.int32)
    nt_e = (cnt + SLOT_TILE - 1) // SLOT_TILE
    tend = jnp.cumsum(nt_e, axis=1)
    tstart = tend - nt_e
    ntiles = tend[:, -1]
    tile_id = jnp.arange(ntile, dtype=jnp.int32)
    tid = jnp.minimum(tile_id[None, :], ntiles[:, None] - 1)
    be = jnp.minimum(jnp.sum(tend[:, None, :] <= tid[:, :, None], axis=2).astype(jnp.int32), N_EXPERTS - 1)
    eid = jnp.arange(N_EXPERTS, dtype=jnp.int32)
    is_be = be[:, :, None] == eid[None, None, :]
    pick = lambda per_expert: jnp.sum(jnp.where(is_be, per_expert[:, None, :], 0), axis=2)
    q0 = (tid - pick(tstart)) * SLOT_TILE
    valid = tile_id[None, :] < ntiles[:, None]
    prev_be = jnp.concatenate([jnp.full((b, 1), -1, jnp.int32), be[:, :-1]], axis=1)
    first = (valid & (be != prev_be)).astype(jnp.int32)
    wslot = (jnp.cumsum(first, axis=1) - 1) % 2
    later = (nt_e > 0)[:, None, :] & (eid[None, None, :] > eid[None, :, None])
    nxt = jnp.min(jnp.where(later, eid[None, None, :], N_EXPERTS), axis=2)
    nxt_e = pick(jnp.where(nxt < N_EXPERTS, nxt, -1))
    flat = lambda a: a.reshape(-1).astype(jnp.int32)
    return flat(be), flat(q0), ntiles.astype(jnp.int32), flat(first), flat(nxt_e), flat(wslot)


def _dispatch_plan(selw3, b, n, ntile):
    row_end, row_off, tables = _plan_tables(selw3, b)
    block_e, tile_q0, ntiles, first, nxt_e, wslot = _tile_meta(row_end, b, n, ntile)
    slot_tok, slot_w = _plan_slots(block_e, tile_q0, ntiles, row_end, row_off, tables, ntile, n)
    return block_e, ntiles, first, nxt_e, wslot, slot_tok, slot_w


def _final_kernel(h2_ref, r_ref, x1_ref, mod_ref, wg_ref, wu_ref, wd_ref, g_ref, o_ref):
    hb = h2_ref[0]
    tm = hb.shape[0]
    sh = _dot((_silu(_dot(hb, wg_ref[...])) * _dot(hb, wu_ref[...])).astype(BF16), wd_ref[...])
    routed = jnp.concatenate([r_ref[0, pl.ds(k, tm, stride=CHUNKS), :] for k in range(CHUNKS)], axis=1)
    moe = routed + sh
    o_ref[0] = x1_ref[0] + mod_ref[0, 5:6, :] * _rms(moe, g_ref[...])


def _final(h2, routed, x1, mod3, wsg_bf, wsu_bf, wsd_bf, g_post, tm=512):
    b, n, d = x1.shape
    f = wsg_bf.shape[1]
    row = lambda bi, i: (bi, i, 0)
    vec = lambda bi, i: (0, 0)
    return pl.pallas_call(
        _final_kernel,
        grid=(b, n // tm),
        in_specs=[pl.BlockSpec((1, tm, d), row),
                  pl.BlockSpec((1, tm * CHUNKS, LANES), row),
                  pl.BlockSpec((1, tm, d), row),
                  pl.BlockSpec((1, 6, d), lambda bi, i: (bi, 0, 0)),
                  pl.BlockSpec((d, f), vec),
                  pl.BlockSpec((d, f), vec),
                  pl.BlockSpec((f, d), vec),
                  pl.BlockSpec((1, d), vec)],
        out_specs=pl.BlockSpec((1, tm, d), row),
        out_shape=jax.ShapeDtypeStruct((b, n, d), F32),
        compiler_params=_cparams(("arbitrary", "arbitrary")),
        name="final",
    )(h2, routed, x1, mod3, wsg_bf, wsu_bf, wsd_bf, g_post)


def _rope_tables(n):
    t = np.arange(n)
    row = (t // GRID_W).astype(np.float32)
    col = (t % GRID_W).astype(np.float32)
    freqs = np.float32(ROPE_THETA) ** (-np.arange(0, AXIS_DIM, 2, dtype=np.float32) / np.float32(AXIS_DIM))
    ar = row[:, None] * freqs
    ac = col[:, None] * freqs
    zeros = np.zeros_like(ar)
    cos64 = np.concatenate([np.cos(ar), np.cos(ar), np.cos(ac), np.cos(ac)], axis=1)
    sa64 = np.concatenate([-np.sin(ar), zeros, -np.sin(ac), zeros], axis=1)
    sb64 = np.concatenate([zeros, np.sin(ar), zeros, np.sin(ac)], axis=1)
    two = lambda a: jnp.asarray(np.concatenate([a, a], axis=1).astype(np.float32))
    return two(cos64), two(sa64), two(sb64)


def kernel(x, c, ctx, c_ctx, w_ada, b_ada, g_pre_mix, g_post_mix, g_pre_ffn, g_post_ffn, w_in, lambda_q1, lambda_k1, lambda_q2, lambda_k2, g_subln, w_pool, b_pool, pool_scale, w_out, w_router, router_bias, w_e_gate, w_e_up, w_e_down, w_sh_gate, w_sh_up, w_sh_down):
    b, n, d = x.shape
    l = 0
    mod_rows = SUBLANES
    cin = jnp.concatenate([c, c_ctx[None, :], jnp.zeros((mod_rows - b - 1, d), F32)], axis=0)
    mod3 = _ada(cin, w_ada[l], b_ada[l]).reshape(mod_rows, 6, d)

    w_in_bf = w_in[l].astype(BF16)
    cos, sa, sb = _rope_tables(n)
    q, k, vt, p = _inproj(x, mod3, g_pre_mix[l][None, :], w_in_bf, cos * (QK_DIM ** -0.5 * LOG2E), cos, sa, sb)
    kc, vct = _ctxkv(ctx, mod3, g_pre_mix[l][None, :], w_in_bf, b)

    lamv = jnp.stack([lambda_q1[l], lambda_k1[l], lambda_q2[l], lambda_k2[l]], axis=0)
    att = _attn(q, k, kc, vt, vct, lamv, g_subln[l][:, None])

    x1, h2, h2t, logits_t = _outproj(
        att, p, x, mod3, w_pool[l].astype(BF16), b_pool[l][None, :], pool_scale[l][None, :],
        w_out[l].astype(BF16), g_post_mix[l][None, :], g_pre_ffn[l][None, :], w_router[l].T)

    t = b * n
    selw3 = _route(logits_t.reshape(N_EXPERTS, t // LANES, LANES),
                   jnp.broadcast_to(router_bias[l][:, None, None], (N_EXPERTS, 1, LANES)))

    ntile = (n * TOP_K) // SLOT_TILE + N_EXPERTS
    block_e, ntiles, first, nxt_e, wslot, slot_row, slot_w = _dispatch_plan(selw3, b, n, ntile)
    routed = _experts(block_e, ntiles, first, nxt_e, wslot, slot_row, slot_w, h2t,
                      w_e_gate[l], w_e_up[l], w_e_down[l], ntile, n)

    return _final(h2, routed, x1, mod3, w_sh_gate[l].astype(BF16), w_sh_up[l].astype(BF16),
                  w_sh_down[l].astype(BF16), g_post_ffn[l][None, :])
```

```python
import functools
import math

import jax
import jax.numpy as jnp
import numpy as np
from jax import lax
from jax.experimental import pallas as pl
from jax.experimental.pallas import tpu as pltpu

F32 = jnp.float32
BF16 = jnp.bfloat16

D_MODEL = 1024
GRID_W = 64
N_HEADS = 4
QK_DIM = 64
V_DIM = 128
ATTN_WIDTH = N_HEADS * V_DIM
POOL_WIDTH = D_MODEL - ATTN_WIDTH
POOL_WINDOWS = (2, 4, 8, 16)
POOL_GROUP = POOL_WIDTH // len(POOL_WINDOWS)
AXIS_DIM = QK_DIM // 2
ROPE_THETA = 10000.0
N_EXPERTS = 64
TOP_K = 8
N_GROUPS = 8
GROUP_SIZE = N_EXPERTS // N_GROUPS
TOPK_GROUPS = 4
EXPERT_DIM = 256
ROUTED_SCALE = 2.5
EPS = 1e-6
LAM_INIT = 0.8 - 0.6 * math.exp(-0.3 * 0)

LANES = 128
SUBLANES = 8
SLOT_TILE = 256
HALO = 16
VMEM_LIMIT = 56 * 1024 * 1024


def _cparams(sem, vmem=VMEM_LIMIT):
    return pltpu.CompilerParams(dimension_semantics=sem, vmem_limit_bytes=vmem)


def _split(a):
    hi = a.astype(BF16)
    lo = (a - hi.astype(F32)).astype(BF16)
    return hi, lo


def _dot(a, b):
    return jnp.dot(a, b, preferred_element_type=F32)


def _dot_nt(a, b):
    return lax.dot_general(a, b, (((1,), (1,)), ((), ())), preferred_element_type=F32)


def _dot3(a, b, nt=False):
    d = _dot_nt if nt else _dot
    ah, al = _split(a)
    bh, bl = _split(b)
    return d(ah, bh) + d(ah, bl) + d(al, bh)


def _rms(x, g):
    return x * lax.rsqrt(jnp.mean(x * x, axis=-1, keepdims=True) + EPS) * g


def _silu(x):
    return x * (1.0 / (1.0 + jnp.exp(-x)))


def _ada_kernel(c_ref, w_ref, b_ref, o_ref):
    o_ref[...] = _dot3(_silu(c_ref[...]), w_ref[...]) + b_ref[...]


def _ada(cin, w_ada, b_ada):
    rows, d = cin.shape
    n = w_ada.shape[1]
    tn = 1024
    return pl.pallas_call(
        _ada_kernel,
        grid=(n // tn,),
        in_specs=[pl.BlockSpec((rows, d), lambda j: (0, 0)),
                  pl.BlockSpec((d, tn), lambda j: (0, j)),
                  pl.BlockSpec((1, tn), lambda j: (0, j))],
        out_specs=pl.BlockSpec((rows, tn), lambda j: (0, j)),
        out_shape=jax.ShapeDtypeStruct((rows, n), F32),
        compiler_params=_cparams(("arbitrary",)),
        name="ada",
    )(cin, w_ada, b_ada.reshape(1, n))


def _rope(t, cos, sa, sb):
    return t * cos + pltpu.roll(t, LANES - AXIS_DIM // 2, 1) * sa + pltpu.roll(t, AXIS_DIM // 2, 1) * sb


def _inproj_kernel(x_ref, mod_ref, g_ref, w_ref, cq_ref, ck_ref, sa_ref, sb_ref,
                   q_ref, k_ref, vt_ref, p_ref):
    x = x_ref[0]
    h = _rms(x, g_ref[...]) * (1.0 + mod_ref[0, 1:2, :]) + mod_ref[0, 0:1, :]
    px = _dot(h.astype(BF16), w_ref[...])
    ck = ck_ref[...]
    sa = sa_ref[...]
    sb = sb_ref[...]
    cq = cq_ref[...]
    scale = QK_DIM ** -0.5 * LOG2E
    saq = sa * scale
    sbq = sb * scale
    for hh in range(N_HEADS):
        lo = hh * LANES
        q = px[:, lo:lo + LANES]
        k = px[:, ATTN_WIDTH + lo:ATTN_WIDTH + lo + LANES]
        q_ref[0, :, lo:lo + LANES] = _rope(q, cq, saq, sbq).astype(BF16)
        k_ref[0, :, lo:lo + LANES] = _rope(k, ck, sa, sb).astype(BF16)
    vt_ref[0] = jnp.transpose(px[:, 2 * ATTN_WIDTH:3 * ATTN_WIDTH]).astype(BF16)
    p_ref[0] = px[:, 3 * ATTN_WIDTH:].astype(BF16)


def _inproj(x, mod3, g, w_in_bf, cq, ck, sa, sb, tm=1024):
    b, n, d = x.shape
    width = w_in_bf.shape[1]
    row = lambda bi, i: (bi, i, 0)
    tab = pl.BlockSpec((tm, LANES), lambda bi, i: (i, 0))
    out = jax.ShapeDtypeStruct((b, n, ATTN_WIDTH), BF16)
    return pl.pallas_call(
        _inproj_kernel,
        grid=(b, n // tm),
        in_specs=[pl.BlockSpec((1, tm, d), row),
                  pl.BlockSpec((1, 6, d), lambda bi, i: (bi, 0, 0)),
                  pl.BlockSpec((1, d), lambda bi, i: (0, 0)),
                  pl.BlockSpec((d, width), lambda bi, i: (0, 0)),
                  tab, tab, tab, tab],
        out_specs=[pl.BlockSpec((1, tm, ATTN_WIDTH), row),
                   pl.BlockSpec((1, tm, ATTN_WIDTH), row),
                   pl.BlockSpec((1, ATTN_WIDTH, tm), lambda bi, i: (bi, 0, i)),
                   pl.BlockSpec((1, tm, ATTN_WIDTH), row)],
        out_shape=[out, out, jax.ShapeDtypeStruct((b, ATTN_WIDTH, n), BF16), out],
        compiler_params=_cparams(("arbitrary", "arbitrary")),
        name="inproj",
    )(x, mod3, g, w_in_bf, cq, ck, sa, sb)


def _ctxkv_kernel(x_ref, mod_ref, g_ref, wk_ref, wv_ref, k_ref, vt_ref):
    h = _rms(x_ref[0], g_ref[...]) * (1.0 + mod_ref[0, 1:2, :]) + mod_ref[0, 0:1, :]
    hb = h.astype(BF16)
    k_ref[0] = _dot(hb, wk_ref[...]).astype(BF16)
    vt_ref[0] = jnp.transpose(_dot(hb, wv_ref[...])).astype(BF16)


def _ctxkv(ctx, mod3, g, w_in_bf, ctx_row):
    b, n, d = ctx.shape
    out = jax.ShapeDtypeStruct((b, n, ATTN_WIDTH), BF16)
    return pl.pallas_call(
        _ctxkv_kernel,
        grid=(b,),
        in_specs=[pl.BlockSpec((1, n, d), lambda bi: (bi, 0, 0)),
                  pl.BlockSpec((1, 6, d), lambda bi: (ctx_row, 0, 0)),
                  pl.BlockSpec((1, d), lambda bi: (0, 0)),
                  pl.BlockSpec((d, ATTN_WIDTH), lambda bi: (0, 1)),
                  pl.BlockSpec((d, ATTN_WIDTH), lambda bi: (0, 2))],
        out_specs=[pl.BlockSpec((1, n, ATTN_WIDTH), lambda bi: (bi, 0, 0)),
                   pl.BlockSpec((1, ATTN_WIDTH, n), lambda bi: (bi, 0, 0))],
        out_shape=[out, jax.ShapeDtypeStruct((b, ATTN_WIDTH, n), BF16)],
        compiler_params=_cparams(("arbitrary",)),
        name="ctxkv",
    )(ctx, mod3, g, w_in_bf, w_in_bf)


LOG2E = math.log2(math.e)
KEY_CHUNK = 256


def _attn_step(q_ref, kx_ref, kc_ref, vt_ref, vct_ref, lam_ref, g_ref, o_ref, s_new, m_new, s_old, m_old):
    nc = kc_ref.shape[1]
    tq = q_ref.shape[1]
    nk = s_old.shape[1]
    groups = KEY_CHUNK // SUBLANES
    q = q_ref[0]
    lane = lax.broadcasted_iota(jnp.int32, q.shape, 1)
    zero = jnp.zeros_like(q)
    qms = (jnp.where(lane < QK_DIM, q, zero), jnp.where(lane >= QK_DIM, q, zero))
    m_prev = (m_old[0], m_old[1])
    top = [jnp.full((SUBLANES, tq), -jnp.inf, F32) for _ in range(2)]
    den = [jnp.zeros((SUBLANES, tq), F32) for _ in range(2)]
    acc = [jnp.zeros((V_DIM, tq), F32) for _ in range(2)]
    for c in range(nk // KEY_CHUNK):
        lo = c * KEY_CHUNK
        if lo < nc:
            keys, vt = kc_ref[0, lo:lo + KEY_CHUNK, :], vct_ref[0, :, lo:lo + KEY_CHUNK]
        else:
            keys, vt = kx_ref[0, lo - nc:lo - nc + KEY_CHUNK, :], vt_ref[0, :, lo - nc:lo - nc + KEY_CHUNK]
        for mp in range(2):
            s = _dot_nt(keys, qms[mp])
            s_new[mp, lo:lo + KEY_CHUNK, :] = s
            top[mp] = jnp.maximum(top[mp], jnp.max(s.reshape(groups, SUBLANES, tq), axis=0))
            p = jnp.exp2(s_old[mp, lo:lo + KEY_CHUNK, :] - m_prev[mp])
            den[mp] = den[mp] + jnp.sum(p.reshape(groups, SUBLANES, tq), axis=0)
            acc[mp] = acc[mp] + _dot(vt, p.astype(BF16))
    for mp in range(2):
        m_new[mp] = jnp.max(top[mp], axis=0, keepdims=True)
    lv = lam_ref[...]
    lam = (jnp.exp(jnp.sum(lv[0:1] * lv[1:2], axis=-1, keepdims=True))
           - jnp.exp(jnp.sum(lv[2:3] * lv[3:4], axis=-1, keepdims=True)) + LAM_INIT)
    l1 = jnp.sum(den[0], axis=0, keepdims=True)
    l2 = jnp.sum(den[1], axis=0, keepdims=True)
    o = acc[0] * (1.0 / l1) - acc[1] * (lam / l2)
    o = o * lax.rsqrt(jnp.mean(o * o, axis=0, keepdims=True) + EPS) * (g_ref[...] * (1.0 - LAM_INIT))
    o_ref[0] = jnp.transpose(o).astype(BF16)


def _attn_kernel(q_ref, kx_ref, kc_ref, vt_ref, vct_ref, lam_ref, g_ref, o_ref, s0_ref, m0_ref, s1_ref, m1_ref):
    t = pl.program_id(0)
    io = (q_ref, kx_ref, kc_ref, vt_ref, vct_ref, lam_ref, g_ref, o_ref)

    @pl.when(t == 0)
    def _():
        s1_ref[...] = jnp.zeros_like(s1_ref)
        m1_ref[...] = jnp.zeros_like(m1_ref)

    @pl.when(lax.rem(t, 2) == 0)
    def _():
        _attn_step(*io, s0_ref, m0_ref, s1_ref, m1_ref)

    @pl.when(lax.rem(t, 2) == 1)
    def _():
        _attn_step(*io, s1_ref, m1_ref, s0_ref, m0_ref)


def _attn(q, k, kc, vt, vct, lamv, g_col, tq=512):
    b, n, _ = q.shape
    nc = kc.shape[1]
    nq = n // tq
    total = b * N_HEADS * nq

    def cur(t):
        ta = jnp.minimum(t, total - 1)
        bh = ta // nq
        return bh // N_HEADS, bh % N_HEADS, ta % nq

    def prv(t):
        tb = jnp.maximum(t - 1, 0)
        bh = tb // nq
        return bh // N_HEADS, bh % N_HEADS, tb % nq

    def at(fn, order):
        return lambda t: tuple((fn(t) + (0,))[j] for j in order)

    return pl.pallas_call(
        _attn_kernel,
        grid=(total + 1,),
        in_specs=[pl.BlockSpec((1, tq, LANES), at(cur, (0, 2, 1))),
                  pl.BlockSpec((1, n, LANES), at(cur, (0, 3, 1))),
                  pl.BlockSpec((1, nc, LANES), at(cur, (0, 3, 1))),
                  pl.BlockSpec((1, V_DIM, n), at(prv, (0, 1, 3))),
                  pl.BlockSpec((1, V_DIM, nc), at(prv, (0, 1, 3))),
                  pl.BlockSpec((4, QK_DIM), lambda t: (0, 0)),
                  pl.BlockSpec((V_DIM, 1), lambda t: (0, 0))],
        out_specs=pl.BlockSpec((1, tq, LANES), at(prv, (0, 2, 1))),
        out_shape=jax.ShapeDtypeStruct((b, n, ATTN_WIDTH), BF16),
        scratch_shapes=[pltpu.VMEM((2, nc + n, tq), F32), pltpu.VMEM((2, 1, tq), F32),
                        pltpu.VMEM((2, nc + n, tq), F32), pltpu.VMEM((2, 1, tq), F32)],
        compiler_params=_cparams(("arbitrary",)),
        name="attn",
    )(q, k, kc, vt, vct, lamv, g_col)


def _outproj_kernel(att_ref, p_ref, pprev_ref, pnext_ref, x_ref, mod_ref, wpool_ref, bpool_ref,
                    pscale_ref, wout_ref, gpost_ref, gpre_ref, wr_ref,
                    x1_ref, h2_ref, h2t_ref, lg_ref, *, tm, n):
    i = pl.program_id(1)
    p = p_ref[0].astype(F32)
    prev = jnp.where(i > 0, pprev_ref[0].astype(F32), 0.0)
    nxt = jnp.where(i < pl.num_programs(1) - 1, pnext_ref[0].astype(F32), 0.0)
    ext = jnp.concatenate([prev, p, nxt], axis=0)
    rows = tm + 2 * HALO
    trow = (i * tm + lax.broadcasted_iota(jnp.int32, (tm, 1), 0))

    pooled = []
    sums = {}
    acc = ext + pltpu.roll(ext, 1, 0)
    sums[2] = acc
    w = 2
    while w < POOL_WINDOWS[-1]:
        acc = pltpu.roll(acc, rows - w // 2, 0) + pltpu.roll(acc, w // 2, 0)
        w = 2 * w
        sums[w] = acc
    for gi, win in enumerate(POOL_WINDOWS):
        sl = slice(gi * POOL_GROUP, (gi + 1) * POOL_GROUP)
        wsum = sums[win][HALO:HALO + tm, sl]
        hi_c = jnp.minimum(trow + (win - win // 2), n)
        lo_c = jnp.maximum(trow - win // 2, 0)
        cnt = (hi_c - lo_c).astype(F32)
        d = wsum / cnt - p[:, sl]
        y = _dot(d.astype(BF16), wpool_ref[gi]) + bpool_ref[:, sl]
        pooled.append((y * pscale_ref[:, sl]).astype(BF16))
    pool = jnp.concatenate(pooled, axis=1)
    yx = _dot(att_ref[0], wout_ref[0:ATTN_WIDTH, :]) + _dot(pool, wout_ref[ATTN_WIDTH:, :])
    x1 = x_ref[0] + mod_ref[0, 2:3, :] * _rms(yx, gpost_ref[...])
    x1_ref[0] = x1
    h2 = _rms(x1, gpre_ref[...]) * (1.0 + mod_ref[0, 4:5, :]) + mod_ref[0, 3:4, :]
    h2_ref[0] = h2.astype(BF16)
    for k in range(CHUNKS):
        h2t_ref[0, pl.ds(k, tm, stride=CHUNKS), :] = h2[:, k * LANES:(k + 1) * LANES]
    lg_ref[...] = _dot3(wr_ref[...], h2, nt=True)


def _outproj(att, p, x, mod3, w_pool_bf, b_pool, pool_scale, w_out_bf, g_post, g_pre, w_router_t, tm=1024):
    b, n, d = x.shape
    hb = tm // HALO
    nhb = n // HALO
    row = lambda bi, i: (bi, i, 0)
    vec = lambda bi, i: (0, 0)
    kern = functools.partial(_outproj_kernel, tm=tm, n=n)
    return pl.pallas_call(
        kern,
        grid=(b, n // tm),
        in_specs=[pl.BlockSpec((1, tm, ATTN_WIDTH), row),
                  pl.BlockSpec((1, tm, POOL_WIDTH), row),
                  pl.BlockSpec((1, HALO, POOL_WIDTH), lambda bi, i: (bi, jnp.maximum(i * hb - 1, 0), 0)),
                  pl.BlockSpec((1, HALO, POOL_WIDTH), lambda bi, i: (bi, jnp.minimum((i + 1) * hb, nhb - 1), 0)),
                  pl.BlockSpec((1, tm, d), row),
                  pl.BlockSpec((1, 6, d), lambda bi, i: (bi, 0, 0)),
                  pl.BlockSpec((len(POOL_WINDOWS), POOL_GROUP, POOL_GROUP), lambda bi, i: (0, 0, 0)),
                  pl.BlockSpec((1, POOL_WIDTH), vec),
                  pl.BlockSpec((1, POOL_WIDTH), vec),
                  pl.BlockSpec((d, d), vec),
                  pl.BlockSpec((1, d), vec),
                  pl.BlockSpec((1, d), vec),
                  pl.BlockSpec((N_EXPERTS, d), vec)],
        out_specs=[pl.BlockSpec((1, tm, d), row),
                   pl.BlockSpec((1, tm, d), row),
                   pl.BlockSpec((1, tm * CHUNKS, LANES), row),
                   pl.BlockSpec((N_EXPERTS, tm), lambda bi, i: (0, bi * (n // tm) + i))],
        out_shape=[jax.ShapeDtypeStruct((b, n, d), F32),
                   jax.ShapeDtypeStruct((b, n, d), BF16),
                   jax.ShapeDtypeStruct((b, n * CHUNKS, LANES), F32),
                   jax.ShapeDtypeStruct((N_EXPERTS, b * n), F32)],
        compiler_params=_cparams(("arbitrary", "arbitrary")),
        name="outproj",
    )(att, p, p, p, x, mod3, w_pool_bf, b_pool, pool_scale, w_out_bf, g_post, g_pre, w_router_t)


def _beats(a, b, a_first):
    return jnp.where((a >= b) if a_first else (a > b), 1.0, 0.0)


def _route_kernel(lg_ref, bias_ref, w_ref):
    scores = [1.0 / (1.0 + jnp.exp(-lg_ref[e])) for e in range(N_EXPERTS)]
    biased = [scores[e] + bias_ref[e] for e in range(N_EXPERTS)]
    gscore = []
    for g in range(N_GROUPS):
        vals = biased[g * GROUP_SIZE:(g + 1) * GROUP_SIZE]
        m1 = vals[0]
        m2 = jnp.full_like(m1, -jnp.inf)
        for v in vals[1:]:
            m2 = jnp.maximum(m2, jnp.minimum(m1, v))
            m1 = jnp.maximum(m1, v)
        gscore.append(m1 + m2)
    gsel = []
    for g in range(N_GROUPS):
        rank = jnp.zeros_like(gscore[g])
        for g2 in range(N_GROUPS):
            if g2 != g:
                rank = rank + _beats(gscore[g2], gscore[g], g2 < g)
        gsel.append(rank < TOPK_GROUPS)
    vals = [jnp.where(gsel[e // GROUP_SIZE], biased[e], -jnp.inf) for e in range(N_EXPERTS)]
    chosen = [jnp.zeros(vals[0].shape, jnp.bool_) for _ in range(N_EXPERTS)]
    for _ in range(TOP_K):
        top = vals[0]
        for v in vals[1:]:
            top = jnp.maximum(top, v)
        found = jnp.zeros(top.shape, jnp.bool_)
        for e in range(N_EXPERTS):
            hit = (vals[e] == top) & jnp.logical_not(found)
            found = found | hit
            chosen[e] = chosen[e] | hit
            vals[e] = jnp.where(hit, -jnp.inf, vals[e])
    picked = [jnp.where(chosen[e], scores[e], 0.0) for e in range(N_EXPERTS)]
    denom = picked[0]
    for e in range(1, N_EXPERTS):
        denom = denom + picked[e]
    for e in range(N_EXPERTS):
        w_ref[e] = picked[e] / denom * ROUTED_SCALE


def _route(logits3, bias3, rows=SUBLANES):
    e, r, l = logits3.shape
    return pl.pallas_call(
        _route_kernel,
        grid=(r // rows,),
        in_specs=[pl.BlockSpec((e, rows, l), lambda i: (0, i, 0)),
                  pl.BlockSpec((e, 1, l), lambda i: (0, 0, 0))],
        out_specs=pl.BlockSpec((e, rows, l), lambda i: (0, i, 0)),
        out_shape=jax.ShapeDtypeStruct((e, r, l), F32),
        compiler_params=_cparams(("arbitrary",)),
        name="route",
    )(logits3, bias3)


CHUNKS = D_MODEL // LANES
SLAB = SLOT_TILE + 4
ROW_BATCH = 8


def _gather_tile(row_ref, tile, xs_ref, tx_ref):
    for s in range(SLOT_TILE):
        row = pl.multiple_of(row_ref[0, tile, s], CHUNKS)
        tx_ref[pl.ds(s, CHUNKS, stride=SLAB), :] = xs_ref[pl.ds(row, CHUNKS), :]


def _experts_step(j, wslot, row_ref, sw_ref, wg_st, wu_st, wd_st, xs_ref, acc_ref,
                  tx_cur, tx_nxt, ty_cur, ty_prv):
    ntile = row_ref.shape[1]
    _gather_tile(row_ref, jnp.minimum(j + 1, ntile - 1), xs_ref, tx_nxt)
    xb = jnp.concatenate([tx_cur[pl.ds(k * SLAB, SLOT_TILE), :] for k in range(CHUNKS)], axis=1).astype(BF16)
    g = _dot(xb, wg_st[wslot].astype(BF16))
    u = _dot(xb, wu_st[wslot].astype(BF16))
    h = (_silu(g) * u).astype(BF16)
    y = _dot(h, wd_st[wslot].astype(BF16))
    w_rows = jnp.transpose(jnp.broadcast_to(sw_ref[0, pl.ds(j, 1), :], (LANES, SLOT_TILE)))
    for k in range(CHUNKS):
        ty_cur[pl.ds(k * SLAB, SLOT_TILE), :] = y[:, k * LANES:(k + 1) * LANES] * w_rows
    prv = jnp.maximum(j - 1, 0)
    for s0 in range(0, SLOT_TILE, ROW_BATCH):
        new = []
        for s in range(s0, s0 + ROW_BATCH):
            row = pl.multiple_of(row_ref[0, prv, s], CHUNKS)
            new.append((row, acc_ref[pl.ds(row, CHUNKS), :] + ty_prv[pl.ds(s, CHUNKS, stride=SLAB), :]))
        for row, v in new:
            acc_ref[pl.ds(row, CHUNKS), :] = v


def _experts_kernel(be_ref, nt_ref, first_ref, nxt_ref, ws_ref, row_ref, sw_ref, x_hbm,
                    wg_hbm, wu_hbm, wd_hbm, o_hbm,
                    xs_ref, acc_ref, tx0_ref, tx1_ref, ty0_ref, ty1_ref, wg_st, wu_st, wd_st,
                    sem, wsem, *, ntile, n):
    c = pl.program_id(0)
    j = pl.program_id(1)
    t = c * ntile + j
    rows = n * CHUNKS
    wslot = ws_ref[t]

    def weight_copies(e, slot):
        return [pltpu.make_async_copy(src.at[e], dst.at[slot], wsem.at[slot])
                for src, dst in ((wg_hbm, wg_st), (wu_hbm, wu_st), (wd_hbm, wd_st))]

    @pl.when(j == 0)
    def _():
        for wcp in weight_copies(be_ref[t], 0):
            wcp.start()
        cp = pltpu.make_async_copy(x_hbm.at[c], xs_ref.at[pl.ds(0, rows)], sem.at[0])

        @pl.when(c == 0)
        def _():
            cp.start()

        xs_ref[pl.ds(rows, CHUNKS), :] = jnp.zeros((CHUNKS, LANES), F32)
        acc_ref[...] = jnp.zeros_like(acc_ref)
        ty1_ref[...] = jnp.zeros_like(ty1_ref)
        cp.wait()
        _gather_tile(row_ref, 0, xs_ref, tx0_ref)

    @pl.when(first_ref[t] == 1)
    def _():
        for wcp in weight_copies(be_ref[t], wslot):
            wcp.wait()

    @pl.when((first_ref[t] == 1) & (nxt_ref[t] >= 0))
    def _():
        for wcp in weight_copies(nxt_ref[t], 1 - wslot):
            wcp.start()

    live = j <= nt_ref[c]
    args = (j, wslot, row_ref, sw_ref, wg_st, wu_st, wd_st, xs_ref, acc_ref)

    @pl.when(live & (lax.rem(j, 2) == 0))
    def _():
        _experts_step(*args, tx0_ref, tx1_ref, ty0_ref, ty1_ref)

    @pl.when(live & (lax.rem(j, 2) == 1))
    def _():
        _experts_step(*args, tx1_ref, tx0_ref, ty1_ref, ty0_ref)

    @pl.when((j == nt_ref[c]) & (c + 1 < pl.num_programs(0)))
    def _():
        pltpu.make_async_copy(x_hbm.at[c + 1], xs_ref.at[pl.ds(0, rows)], sem.at[0]).start()

    @pl.when(j == ntile - 1)
    def _():
        cp = pltpu.make_async_copy(acc_ref.at[pl.ds(0, rows)], o_hbm.at[c], sem.at[1])
        cp.start()
        cp.wait()


def _experts(block_e, ntiles, first, nxt_e, wslot, slot_row, slot_w, h2t, w_gate, w_up, w_down, ntile, n):
    b = h2t.shape[0]
    d, f = w_gate.shape[1], w_gate.shape[2]
    batch = lambda c, j, *_: (c, 0, 0)
    kern = functools.partial(_experts_kernel, ntile=ntile, n=n)
    slab = pltpu.VMEM((CHUNKS * SLAB, LANES), F32)
    hbm = pl.BlockSpec(memory_space=pl.ANY)
    grid_spec = pltpu.PrefetchScalarGridSpec(
        num_scalar_prefetch=5,
        grid=(b, ntile),
        in_specs=[pl.BlockSpec((1, ntile, SLOT_TILE), batch, memory_space=pltpu.SMEM),
                  pl.BlockSpec((1, ntile, SLOT_TILE), batch),
                  hbm, hbm, hbm, hbm],
        out_specs=hbm,
        scratch_shapes=[pltpu.VMEM(((n + 1) * CHUNKS, LANES), F32),
                        pltpu.VMEM(((n + 1) * CHUNKS, LANES), F32),
                        slab, slab, slab, slab,
                        pltpu.VMEM((2, d, f), F32), pltpu.VMEM((2, d, f), F32), pltpu.VMEM((2, f, d), F32),
                        pltpu.SemaphoreType.DMA((2,)), pltpu.SemaphoreType.DMA((2,))],
    )
    return pl.pallas_call(
        kern,
        grid_spec=grid_spec,
        out_shape=jax.ShapeDtypeStruct((b, n * CHUNKS, LANES), F32),
        compiler_params=_cparams(("arbitrary", "arbitrary")),
        name="experts",
    )(block_e, ntiles, first, nxt_e, wslot, slot_row, slot_w, h2t, w_gate, w_up, w_down)


PLAN_GROUP = 16


def _plan_tables_kernel(w_ref, re_ref, ro_ref, tab_ref, *, rows):
    ne = w_ref.shape[0]
    w2 = w_ref[...].reshape(ne * rows, LANES)
    selb = jnp.where(w2 > 0.0, 1.0, 0.0).astype(BF16)
    i0 = lax.broadcasted_iota(jnp.int32, (LANES, LANES), 0)
    i1 = lax.broadcasted_iota(jnp.int32, (LANES, LANES), 1)
    rowtot = _dot(selb, jnp.ones((LANES, LANES), BF16))
    ridx = lax.broadcasted_iota(jnp.int32, rowtot.shape, 0) & (rows - 1)
    acc = rowtot
    sh = 1
    while sh < rows:
        acc = acc + jnp.where(ridx >= sh, pltpu.roll(acc, sh, 0), 0.0)
        sh *= 2
    re_ref[0] = acc
    ro_ref[0] = acc - rowtot
    incl_t = _dot_nt(jnp.where(i1 <= i0, 1.0, 0.0).astype(BF16), selb)
    eye = jnp.where(i0 == i1, 1.0, 0.0).astype(BF16)
    w_hi = w2.astype(BF16)
    r1 = w2 - w_hi.astype(F32)
    w_mid = r1.astype(BF16)
    w_lo = (r1 - w_mid.astype(F32)).astype(BF16)
    parts = [_dot_nt(eye, p) for p in (w_hi, w_mid, w_lo)]
    for g in range(ne * rows // LANES):
        sl = slice(g * LANES, (g + 1) * LANES)
        tab_ref[0, g, 0:LANES, :] = incl_t[:, sl].astype(BF16)
        for k in range(3):
            tab_ref[0, g, (k + 1) * LANES:(k + 2) * LANES, :] = parts[k][:, sl].astype(BF16)


def _plan_tables(selw3, b):
    ne, r_all, _ = selw3.shape
    rows = r_all // b
    groups = ne * rows // LANES
    kern = functools.partial(_plan_tables_kernel, rows=rows)
    return pl.pallas_call(
        kern,
        grid=(b,),
        in_specs=[pl.BlockSpec((ne, rows, LANES), lambda c: (0, c, 0))],
        out_specs=[pl.BlockSpec((1, ne * rows, LANES), lambda c: (c, 0, 0)),
                   pl.BlockSpec((1, ne * rows, LANES), lambda c: (c, 0, 0)),
                   pl.BlockSpec((1, groups, 4 * LANES, LANES), lambda c: (c, 0, 0, 0))],
        out_shape=[jax.ShapeDtypeStruct((b, ne * rows, LANES), F32),
                   jax.ShapeDtypeStruct((b, ne * rows, LANES), F32),
                   jax.ShapeDtypeStruct((b, groups, 4 * LANES, LANES), BF16)],
        compiler_params=_cparams(("arbitrary",)),
        name="plan_tables",
    )(selw3)


def _plan_slots_kernel(be_ref, q0_ref, nt_ref, re_ref, ro_ref, tab_ref, tok_ref, sw_ref,
                       *, ntile, rows, n):
    c = pl.program_id(0)
    tok_ref[...] = jnp.full(tok_ref.shape, n * CHUNKS, jnp.int32)
    sw_ref[...] = jnp.zeros(sw_ref.shape, F32)
    lane = lax.broadcasted_iota(jnp.int32, (1, SLOT_TILE), 1).astype(F32)
    sub_r = lax.broadcasted_iota(jnp.int32, (rows, SLOT_TILE), 0).astype(F32)
    sub_l = lax.broadcasted_iota(jnp.int32, (LANES, SLOT_TILE), 0).astype(F32)
    reps = SLOT_TILE // LANES

    def tile_body(j):
        e = be_ref[c * ntile + j]
        q = q0_ref[c * ntile + j].astype(F32) + lane
        base = pl.multiple_of(e * rows, rows)
        row_end = jnp.concatenate([re_ref[0, pl.ds(base, rows), :]] * reps, axis=1)
        row_off = jnp.concatenate([ro_ref[0, pl.ds(base, rows), :]] * reps, axis=1)
        r = jnp.sum(jnp.where(row_end <= q, 1.0, 0.0), axis=0, keepdims=True)
        row_start = jnp.sum(jnp.where(sub_r == r, row_off, 0.0), axis=0, keepdims=True)
        per_group = LANES // rows
        g = lax.shift_right_logical(e, per_group.bit_length() - 1)
        col = r + ((e & (per_group - 1)) * rows).astype(F32)
        onehot = jnp.where(sub_l == col, 1.0, 0.0).astype(BF16)
        picked = _dot(tab_ref[0, g], onehot)
        incl = picked[0:LANES]
        lane_idx = jnp.sum(jnp.where(incl <= q - row_start, 1.0, 0.0), axis=0, keepdims=True)
        wrow = (picked[LANES:2 * LANES] + picked[2 * LANES:3 * LANES]) + picked[3 * LANES:]
        wsel = jnp.sum(jnp.where(sub_l == lane_idx, wrow, 0.0), axis=0, keepdims=True)
        valid = (q < row_end[rows - 1:rows, :]) & (j < nt_ref[c])
        tok = jnp.where(valid, r * LANES + lane_idx, float(n)).astype(jnp.int32)
        tok_ref[0, pl.ds(j, 1), :] = tok * CHUNKS
        sw_ref[0, pl.ds(j, 1), :] = jnp.where(valid, wsel, 0.0)

    def tile_group(i, carry):
        for k in range(PLAN_GROUP):
            tile_body(PLAN_GROUP * i + k)
        return carry

    lax.fori_loop(0, (nt_ref[c] + PLAN_GROUP - 1) // PLAN_GROUP, tile_group, 0)


def _plan_slots(block_e, tile_q0, ntiles, row_end, row_off, tables, ntile, n):
    b, er, _ = row_end.shape
    rows = n // LANES
    per_group = LANES // rows
    assert rows * per_group == LANES and per_group & (per_group - 1) == 0 and ntile % PLAN_GROUP == 0
    groups = tables.shape[1]
    kern = functools.partial(_plan_slots_kernel, ntile=ntile, rows=rows, n=n)
    grid_spec = pltpu.PrefetchScalarGridSpec(
        num_scalar_prefetch=3,
        grid=(b,),
        in_specs=[pl.BlockSpec((1, er, LANES), lambda c, *_: (c, 0, 0)),
                  pl.BlockSpec((1, er, LANES), lambda c, *_: (c, 0, 0)),
                  pl.BlockSpec((1, groups, 4 * LANES, LANES), lambda c, *_: (c, 0, 0, 0))],
        out_specs=[pl.BlockSpec((1, ntile, SLOT_TILE), lambda c, *_: (c, 0, 0)),
                   pl.BlockSpec((1, ntile, SLOT_TILE), lambda c, *_: (c, 0, 0))],
    )
    return pl.pallas_call(
        kern,
        grid_spec=grid_spec,
        out_shape=[jax.ShapeDtypeStruct((b, ntile, SLOT_TILE), jnp.int32),
                   jax.ShapeDtypeStruct((b, ntile, SLOT_TILE), F32)],
        compiler_params=_cparams(("arbitrary",)),
        name="plan_slots",
    )(block_e, tile_q0, ntiles, row_end, row_off, tables)


def _tile_meta(row_end, b, n, ntile):
    rows = n // LANES
    cnt = row_end.reshape(b, N_EXPERTS, rows, LANES)[:, :, rows - 1, 0].astype(jnp.int32)
    nt_e = (cnt + SLOT_TILE - 1) // SLOT_TILE
    tend = jnp.cumsum(nt_e, axis=1)
    tstart = tend - nt_e
    ntiles = tend[:, -1]
    tile_id = jnp.arange(ntile, dtype=jnp.int32)
    tid = jnp.minimum(tile_id[None, :], ntiles[:, None] - 1)
    be = jnp.minimum(jnp.sum(tend[:, None, :] <= tid[:, :, None], axis=2).astype(jnp.int32), N_EXPERTS - 1)
    eid = jnp.arange(N_EXPERTS, dtype=jnp.int32)
    is_be = be[:, :, None] == eid[None, None, :]
    pick = lambda per_expert: jnp.sum(jnp.where(is_be, per_expert[:, None, :], 0), axis=2)
    q0 = (tid - pick(tstart)) * SLOT_TILE
    valid = tile_id[None, :] < ntiles[:, None]
    prev_be = jnp.concatenate([jnp.full((b, 1), -1, jnp.int32), be[:, :-1]], axis=1)
    first = (valid & (be != prev_be)).astype(jnp.int32)
    wslot = (jnp.cumsum(first, axis=1) - 1) % 2
    later = (nt_e > 0)[:, None, :] & (eid[None, None, :] > eid[None, :, None])
    nxt = jnp.min(jnp.where(later, eid[None, None, :], N_EXPERTS), axis=2)
    nxt_e = pick(jnp.where(nxt < N_EXPERTS, nxt, -1))
    flat = lambda a: a.reshape(-1).astype(jnp.int32)
    return flat(be), flat(q0), ntiles.astype(jnp.int32), flat(first), flat(nxt_e), flat(wslot)


def _dispatch_plan(selw3, b, n, ntile):
    row_end, row_off, tables = _plan_tables(selw3, b)
    block_e, tile_q0, ntiles, first, nxt_e, wslot = _tile_meta(row_end, b, n, ntile)
    slot_tok, slot_w = _plan_slots(block_e, tile_q0, ntiles, row_end, row_off, tables, ntile, n)
    return block_e, ntiles, first, nxt_e, wslot, slot_tok, slot_w


def _final_kernel(h2_ref, r_ref, x1_ref, mod_ref, wg_ref, wu_ref, wd_ref, g_ref, o_ref):
    hb = h2_ref[0]
    tm = hb.shape[0]
    sh = _dot((_silu(_dot(hb, wg_ref[...])) * _dot(hb, wu_ref[...])).astype(BF16), wd_ref[...])
    routed = jnp.concatenate([r_ref[0, pl.ds(k, tm, stride=CHUNKS), :] for k in range(CHUNKS)], axis=1)
    moe = routed + sh
    o_ref[0] = x1_ref[0] + mod_ref[0, 5:6, :] * _rms(moe, g_ref[...])


def _final(h2, routed, x1, mod3, wsg_bf, wsu_bf, wsd_bf, g_post, tm=512):
    b, n, d = x1.shape
    f = wsg_bf.shape[1]
    row = lambda bi, i: (bi, i, 0)
    vec = lambda bi, i: (0, 0)
    return pl.pallas_call(
        _final_kernel,
        grid=(b, n // tm),
        in_specs=[pl.BlockSpec((1, tm, d), row),
                  pl.BlockSpec((1, tm * CHUNKS, LANES), row),
                  pl.BlockSpec((1, tm, d), row),
                  pl.BlockSpec((1, 6, d), lambda bi, i: (bi, 0, 0)),
                  pl.BlockSpec((d, f), vec),
                  pl.BlockSpec((d, f), vec),
                  pl.BlockSpec((f, d), vec),
                  pl.BlockSpec((1, d), vec)],
        out_specs=pl.BlockSpec((1, tm, d), row),
        out_shape=jax.ShapeDtypeStruct((b, n, d), F32),
        compiler_params=_cparams(("arbitrary", "arbitrary")),
        name="final",
    )(h2, routed, x1, mod3, wsg_bf, wsu_bf, wsd_bf, g_post)


def _rope_tables(n):
    t = np.arange(n)
    row = (t // GRID_W).astype(np.float32)
    col = (t % GRID_W).astype(np.float32)
    freqs = np.float32(ROPE_THETA) ** (-np.arange(0, AXIS_DIM, 2, dtype=np.float32) / np.float32(AXIS_DIM))
    ar = row[:, None] * freqs
    ac = col[:, None] * freqs
    zeros = np.zeros_like(ar)
    cos64 = np.concatenate([np.cos(ar), np.cos(ar), np.cos(ac), np.cos(ac)], axis=1)
    sa64 = np.concatenate([-np.sin(ar), zeros, -np.sin(ac), zeros], axis=1)
    sb64 = np.concatenate([zeros, np.sin(ar), zeros, np.sin(ac)], axis=1)
    two = lambda a: jnp.asarray(np.concatenate([a, a], axis=1).astype(np.float32))
    return two(cos64), two(sa64), two(sb64)


def kernel(x, c, ctx, c_ctx, w_ada, b_ada, g_pre_mix, g_post_mix, g_pre_ffn, g_post_ffn, w_in, lambda_q1, lambda_k1, lambda_q2, lambda_k2, g_subln, w_pool, b_pool, pool_scale, w_out, w_router, router_bias, w_e_gate, w_e_up, w_e_down, w_sh_gate, w_sh_up, w_sh_down):
    b, n, d = x.shape
    l = 0
    mod_rows = SUBLANES
    cin = jnp.concatenate([c, c_ctx[None, :], jnp.zeros((mod_rows - b - 1, d), F32)], axis=0)
    mod3 = _ada(cin, w_ada[l], b_ada[l]).reshape(mod_rows, 6, d)

    w_in_bf = w_in[l].astype(BF16)
    cos, sa, sb = _rope_tables(n)
    q, k, vt, p = _inproj(x, mod3, g_pre_mix[l][None, :], w_in_bf, cos * (QK_DIM ** -0.5 * LOG2E), cos, sa, sb)
    kc, vct = _ctxkv(ctx, mod3, g_pre_mix[l][None, :], w_in_bf, b)

    lamv = jnp.stack([lambda_q1[l], lambda_k1[l], lambda_q2[l], lambda_k2[l]], axis=0)
    att = _attn(q, k, kc, vt, vct, lamv, g_subln[l][:, None])

    x1, h2, h2t, logits_t = _outproj(
        att, p, x, mod3, w_pool[l].astype(BF16), b_pool[l][None, :], pool_scale[l][None, :],
        w_out[l].astype(BF16), g_post_mix[l][None, :], g_pre_ffn[l][None, :], w_router[l].T)

    t = b * n
    selw3 = _route(logits_t.reshape(N_EXPERTS, t // LANES, LANES),
                   jnp.broadcast_to(router_bias[l][:, None, None], (N_EXPERTS, 1, LANES)))

    ntile = (n * TOP_K) // SLOT_TILE + N_EXPERTS
    block_e, ntiles, first, nxt_e, wslot, slot_row, slot_w = _dispatch_plan(selw3, b, n, ntile)
    routed = _experts(block_e, ntiles, first, nxt_e, wslot, slot_row, slot_w, h2t,
                      w_e_gate[l], w_e_up[l], w_e_down[l], ntile, n)

    return _final(h2, routed, x1, mod3, w_sh_gate[l].astype(BF16), w_sh_up[l].astype(BF16),
                  w_sh_down[l].astype(BF16), g_post_ffn[l][None, :])
```

```python
import functools
import math

import jax
import jax.numpy as jnp
import numpy as np
from jax import lax
from jax.experimental import pallas as pl
from jax.experimental.pallas import tpu as pltpu

F32 = jnp.float32
BF16 = jnp.bfloat16

D_MODEL = 1024
GRID_W = 64
N_HEADS = 4
QK_DIM = 64
V_DIM = 128
ATTN_WIDTH = N_HEADS * V_DIM
POOL_WIDTH = D_MODEL - ATTN_WIDTH
POOL_WINDOWS = (2, 4, 8, 16)
POOL_GROUP = POOL_WIDTH // len(POOL_WINDOWS)
AXIS_DIM = QK_DIM // 2
ROPE_THETA = 10000.0
N_EXPERTS = 64
TOP_K = 8
N_GROUPS = 8
GROUP_SIZE = N_EXPERTS // N_GROUPS
TOPK_GROUPS = 4
EXPERT_DIM = 256
ROUTED_SCALE = 2.5
EPS = 1e-6
LAM_INIT = 0.8 - 0.6 * math.exp(-0.3 * 0)

LANES = 128
SUBLANES = 8
SLOT_TILE = 512
HALO = 16
VMEM_LIMIT = 56 * 1024 * 1024


def _cparams(sem, vmem=VMEM_LIMIT):
    return pltpu.CompilerParams(dimension_semantics=sem, vmem_limit_bytes=vmem)


def _split(a):
    hi = a.astype(BF16)
    lo = (a - hi.astype(F32)).astype(BF16)
    return hi, lo


def _dot(a, b):
    return jnp.dot(a, b, preferred_element_type=F32)


def _dot_nt(a, b):
    return lax.dot_general(a, b, (((1,), (1,)), ((), ())), preferred_element_type=F32)


def _dot3(a, b, nt=False):
    d = _dot_nt if nt else _dot
    ah, al = _split(a)
    bh, bl = _split(b)
    return d(ah, bh) + d(ah, bl) + d(al, bh)


def _rms(x, g):
    return x * lax.rsqrt(jnp.mean(x * x, axis=-1, keepdims=True) + EPS) * g


def _silu(x):
    return x * (1.0 / (1.0 + jnp.exp(-x)))


def _ada_kernel(c_ref, w_ref, b_ref, o_ref):
    o_ref[...] = _dot3(_silu(c_ref[...]), w_ref[...]) + b_ref[...]


def _ada(cin, w_ada, b_ada):
    rows, d = cin.shape
    n = w_ada.shape[1]
    tn = 1024
    return pl.pallas_call(
        _ada_kernel,
        grid=(n // tn,),
        in_specs=[pl.BlockSpec((rows, d), lambda j: (0, 0)),
                  pl.BlockSpec((d, tn), lambda j: (0, j)),
                  pl.BlockSpec((1, tn), lambda j: (0, j))],
        out_specs=pl.BlockSpec((rows, tn), lambda j: (0, j)),
        out_shape=jax.ShapeDtypeStruct((rows, n), F32),
        compiler_params=_cparams(("arbitrary",)),
        name="ada",
    )(cin, w_ada, b_ada.reshape(1, n))


def _rope(t, cos, sa, sb):
    return t * cos + pltpu.roll(t, LANES - AXIS_DIM // 2, 1) * sa + pltpu.roll(t, AXIS_DIM // 2, 1) * sb


def _inproj_kernel(x_ref, mod_ref, g_ref, w_ref, cq_ref, ck_ref, sa_ref, sb_ref,
                   q_ref, k_ref, vt_ref, p_ref):
    x = x_ref[0]
    h = _rms(x, g_ref[...]) * (1.0 + mod_ref[0, 1:2, :]) + mod_ref[0, 0:1, :]
    px = _dot(h.astype(BF16), w_ref[...])
    ck = ck_ref[...]
    sa = sa_ref[...]
    sb = sb_ref[...]
    cq = cq_ref[...]
    scale = QK_DIM ** -0.5 * LOG2E
    saq = sa * scale
    sbq = sb * scale
    for hh in range(N_HEADS):
        lo = hh * LANES
        q = px[:, lo:lo + LANES]
        k = px[:, ATTN_WIDTH + lo:ATTN_WIDTH + lo + LANES]
        q_ref[0, :, lo:lo + LANES] = _rope(q, cq, saq, sbq).astype(BF16)
        k_ref[0, :, lo:lo + LANES] = _rope(k, ck, sa, sb).astype(BF16)
    vt_ref[0] = jnp.transpose(px[:, 2 * ATTN_WIDTH:3 * ATTN_WIDTH]).astype(BF16)
    p_ref[0] = px[:, 3 * ATTN_WIDTH:].astype(BF16)


def _inproj(x, mod3, g, w_in_bf, cq, ck, sa, sb, tm=1024):
    b, n, d = x.shape
    width = w_in_bf.shape[1]
    row = lambda bi, i: (bi, i, 0)
    tab = pl.BlockSpec((tm, LANES), lambda bi, i: (i, 0))
    out = jax.ShapeDtypeStruct((b, n, ATTN_WIDTH), BF16)
    return pl.pallas_call(
        _inproj_kernel,
        grid=(b, n // tm),
        in_specs=[pl.BlockSpec((1, tm, d), row),
                  pl.BlockSpec((1, 6, d), lambda bi, i: (bi, 0, 0)),
                  pl.BlockSpec((1, d), lambda bi, i: (0, 0)),
                  pl.BlockSpec((d, width), lambda bi, i: (0, 0)),
                  tab, tab, tab, tab],
        out_specs=[pl.BlockSpec((1, tm, ATTN_WIDTH), row),
                   pl.BlockSpec((1, tm, ATTN_WIDTH), row),
                   pl.BlockSpec((1, ATTN_WIDTH, tm), lambda bi, i: (bi, 0, i)),
                   pl.BlockSpec((1, tm, ATTN_WIDTH), row)],
        out_shape=[out, out, jax.ShapeDtypeStruct((b, ATTN_WIDTH, n), BF16), out],
        compiler_params=_cparams(("arbitrary", "arbitrary")),
        name="inproj",
    )(x, mod3, g, w_in_bf, cq, ck, sa, sb)


def _ctxkv_kernel(x_ref, mod_ref, g_ref, wk_ref, wv_ref, k_ref, vt_ref):
    h = _rms(x_ref[0], g_ref[...]) * (1.0 + mod_ref[0, 1:2, :]) + mod_ref[0, 0:1, :]
    hb = h.astype(BF16)
    k_ref[0] = _dot(hb, wk_ref[...]).astype(BF16)
    vt_ref[0] = jnp.transpose(_dot(hb, wv_ref[...])).astype(BF16)


def _ctxkv(ctx, mod3, g, w_in_bf, ctx_row):
    b, n, d = ctx.shape
    out = jax.ShapeDtypeStruct((b, n, ATTN_WIDTH), BF16)
    return pl.pallas_call(
        _ctxkv_kernel,
        grid=(b,),
        in_specs=[pl.BlockSpec((1, n, d), lambda bi: (bi, 0, 0)),
                  pl.BlockSpec((1, 6, d), lambda bi: (ctx_row, 0, 0)),
                  pl.BlockSpec((1, d), lambda bi: (0, 0)),
                  pl.BlockSpec((d, ATTN_WIDTH), lambda bi: (0, 1)),
                  pl.BlockSpec((d, ATTN_WIDTH), lambda bi: (0, 2))],
        out_specs=[pl.BlockSpec((1, n, ATTN_WIDTH), lambda bi: (bi, 0, 0)),
                   pl.BlockSpec((1, ATTN_WIDTH, n), lambda bi: (bi, 0, 0))],
        out_shape=[out, jax.ShapeDtypeStruct((b, ATTN_WIDTH, n), BF16)],
        compiler_params=_cparams(("arbitrary",)),
        name="ctxkv",
    )(ctx, mod3, g, w_in_bf, w_in_bf)


LOG2E = math.log2(math.e)
KEY_CHUNK = 256


def _attn_step(q_ref, kx_ref, kc_ref, vt_ref, vct_ref, lam_ref, g_ref, o_ref, s_new, m_new, s_old, m_old):
    nc = kc_ref.shape[1]
    tq = q_ref.shape[1]
    nk = s_old.shape[1]
    groups = KEY_CHUNK // SUBLANES
    q = q_ref[0]
    lane = lax.broadcasted_iota(jnp.int32, q.shape, 1)
    zero = jnp.zeros_like(q)
    qms = (jnp.where(lane < QK_DIM, q, zero), jnp.where(lane >= QK_DIM, q, zero))
    m_prev = (m_old[0], m_old[1])
    top = [jnp.full((SUBLANES, tq), -jnp.inf, F32) for _ in range(2)]
    den = [jnp.zeros((SUBLANES, tq), F32) for _ in range(2)]
    acc = [jnp.zeros((V_DIM, tq), F32) for _ in range(2)]
    for c in range(nk // KEY_CHUNK):
        lo = c * KEY_CHUNK
        if lo < nc:
            keys, vt = kc_ref[0, lo:lo + KEY_CHUNK, :], vct_ref[0, :, lo:lo + KEY_CHUNK]
        else:
            keys, vt = kx_ref[0, lo - nc:lo - nc + KEY_CHUNK, :], vt_ref[0, :, lo - nc:lo - nc + KEY_CHUNK]
        for mp in range(2):
            s = _dot_nt(keys, qms[mp])
            s_new[mp, lo:lo + KEY_CHUNK, :] = s
            top[mp] = jnp.maximum(top[mp], jnp.max(s.reshape(groups, SUBLANES, tq), axis=0))
            p = jnp.exp2(s_old[mp, lo:lo + KEY_CHUNK, :] - m_prev[mp])
            den[mp] = den[mp] + jnp.sum(p.reshape(groups, SUBLANES, tq), axis=0)
            acc[mp] = acc[mp] + _dot(vt, p.astype(BF16))
    for mp in range(2):
        m_new[mp] = jnp.max(top[mp], axis=0, keepdims=True)
    lv = lam_ref[...]
    lam = (jnp.exp(jnp.sum(lv[0:1] * lv[1:2], axis=-1, keepdims=True))
           - jnp.exp(jnp.sum(lv[2:3] * lv[3:4], axis=-1, keepdims=True)) + LAM_INIT)
    l1 = jnp.sum(den[0], axis=0, keepdims=True)
    l2 = jnp.sum(den[1], axis=0, keepdims=True)
    o = acc[0] * (1.0 / l1) - acc[1] * (lam / l2)
    o = o * lax.rsqrt(jnp.mean(o * o, axis=0, keepdims=True) + EPS) * (g_ref[...] * (1.0 - LAM_INIT))
    o_ref[0] = jnp.transpose(o).astype(BF16)


def _attn_kernel(q_ref, kx_ref, kc_ref, vt_ref, vct_ref, lam_ref, g_ref, o_ref, s0_ref, m0_ref, s1_ref, m1_ref):
    t = pl.program_id(0)
    io = (q_ref, kx_ref, kc_ref, vt_ref, vct_ref, lam_ref, g_ref, o_ref)

    @pl.when(t == 0)
    def _():
        s1_ref[...] = jnp.zeros_like(s1_ref)
        m1_ref[...] = jnp.zeros_like(m1_ref)

    @pl.when(lax.rem(t, 2) == 0)
    def _():
        _attn_step(*io, s0_ref, m0_ref, s1_ref, m1_ref)

    @pl.when(lax.rem(t, 2) == 1)
    def _():
        _attn_step(*io, s1_ref, m1_ref, s0_ref, m0_ref)


def _attn(q, k, kc, vt, vct, lamv, g_col, tq=512):
    b, n, _ = q.shape
    nc = kc.shape[1]
    nq = n // tq
    total = b * N_HEADS * nq

    def cur(t):
        ta = jnp.minimum(t, total - 1)
        bh = ta // nq
        return bh // N_HEADS, bh % N_HEADS, ta % nq

    def prv(t):
        tb = jnp.maximum(t - 1, 0)
        bh = tb // nq
        return bh // N_HEADS, bh % N_HEADS, tb % nq

    def at(fn, order):
        return lambda t: tuple((fn(t) + (0,))[j] for j in order)

    return pl.pallas_call(
        _attn_kernel,
        grid=(total + 1,),
        in_specs=[pl.BlockSpec((1, tq, LANES), at(cur, (0, 2, 1))),
                  pl.BlockSpec((1, n, LANES), at(cur, (0, 3, 1))),
                  pl.BlockSpec((1, nc, LANES), at(cur, (0, 3, 1))),
                  pl.BlockSpec((1, V_DIM, n), at(prv, (0, 1, 3))),
                  pl.BlockSpec((1, V_DIM, nc), at(prv, (0, 1, 3))),
                  pl.BlockSpec((4, QK_DIM), lambda t: (0, 0)),
                  pl.BlockSpec((V_DIM, 1), lambda t: (0, 0))],
        out_specs=pl.BlockSpec((1, tq, LANES), at(prv, (0, 2, 1))),
        out_shape=jax.ShapeDtypeStruct((b, n, ATTN_WIDTH), BF16),
        scratch_shapes=[pltpu.VMEM((2, nc + n, tq), F32), pltpu.VMEM((2, 1, tq), F32),
                        pltpu.VMEM((2, nc + n, tq), F32), pltpu.VMEM((2, 1, tq), F32)],
        compiler_params=_cparams(("arbitrary",)),
        name="attn",
    )(q, k, kc, vt, vct, lamv, g_col)


def _outproj_kernel(att_ref, p_ref, pprev_ref, pnext_ref, x_ref, mod_ref, wpool_ref, bpool_ref,
                    pscale_ref, wout_ref, gpost_ref, gpre_ref, wr_ref,
                    x1_ref, h2_ref, h2t_ref, lg_ref, *, tm, n):
    i = pl.program_id(1)
    p = p_ref[0].astype(F32)
    prev = jnp.where(i > 0, pprev_ref[0].astype(F32), 0.0)
    nxt = jnp.where(i < pl.num_programs(1) - 1, pnext_ref[0].astype(F32), 0.0)
    ext = jnp.concatenate([prev, p, nxt], axis=0)
    rows = tm + 2 * HALO
    trow = (i * tm + lax.broadcasted_iota(jnp.int32, (tm, 1), 0))

    pooled = []
    sums = {}
    acc = ext + pltpu.roll(ext, 1, 0)
    sums[2] = acc
    w = 2
    while w < POOL_WINDOWS[-1]:
        acc = pltpu.roll(acc, rows - w // 2, 0) + pltpu.roll(acc, w // 2, 0)
        w = 2 * w
        sums[w] = acc
    for gi, win in enumerate(POOL_WINDOWS):
        sl = slice(gi * POOL_GROUP, (gi + 1) * POOL_GROUP)
        wsum = sums[win][HALO:HALO + tm, sl]
        hi_c = jnp.minimum(trow + (win - win // 2), n)
        lo_c = jnp.maximum(trow - win // 2, 0)
        cnt = (hi_c - lo_c).astype(F32)
        d = wsum / cnt - p[:, sl]
        y = _dot(d.astype(BF16), wpool_ref[gi]) + bpool_ref[:, sl]
        pooled.append((y * pscale_ref[:, sl]).astype(BF16))
    pool = jnp.concatenate(pooled, axis=1)
    yx = _dot(att_ref[0], wout_ref[0:ATTN_WIDTH, :]) + _dot(pool, wout_ref[ATTN_WIDTH:, :])
    x1 = x_ref[0] + mod_ref[0, 2:3, :] * _rms(yx, gpost_ref[...])
    x1_ref[0] = x1
    h2 = _rms(x1, gpre_ref[...]) * (1.0 + mod_ref[0, 4:5, :]) + mod_ref[0, 3:4, :]
    h2_ref[0] = h2.astype(BF16)
    for k in range(CHUNKS):
        h2t_ref[0, pl.ds(k, tm, stride=CHUNKS), :] = h2[:, k * LANES:(k + 1) * LANES]
    lg_ref[...] = _dot3(wr_ref[...], h2, nt=True)


def _outproj(att, p, x, mod3, w_pool_bf, b_pool, pool_scale, w_out_bf, g_post, g_pre, w_router_t, tm=1024):
    b, n, d = x.shape
    hb = tm // HALO
    nhb = n // HALO
    row = lambda bi, i: (bi, i, 0)
    vec = lambda bi, i: (0, 0)
    kern = functools.partial(_outproj_kernel, tm=tm, n=n)
    return pl.pallas_call(
        kern,
        grid=(b, n // tm),
        in_specs=[pl.BlockSpec((1, tm, ATTN_WIDTH), row),
                  pl.BlockSpec((1, tm, POOL_WIDTH), row),
                  pl.BlockSpec((1, HALO, POOL_WIDTH), lambda bi, i: (bi, jnp.maximum(i * hb - 1, 0), 0)),
                  pl.BlockSpec((1, HALO, POOL_WIDTH), lambda bi, i: (bi, jnp.minimum((i + 1) * hb, nhb - 1), 0)),
                  pl.BlockSpec((1, tm, d), row),
                  pl.BlockSpec((1, 6, d), lambda bi, i: (bi, 0, 0)),
                  pl.BlockSpec((len(POOL_WINDOWS), POOL_GROUP, POOL_GROUP), lambda bi, i: (0, 0, 0)),
                  pl.BlockSpec((1, POOL_WIDTH), vec),
                  pl.BlockSpec((1, POOL_WIDTH), vec),
                  pl.BlockSpec((d, d), vec),
                  pl.BlockSpec((1, d), vec),
                  pl.BlockSpec((1, d), vec),
                  pl.BlockSpec((N_EXPERTS, d), vec)],
        out_specs=[pl.BlockSpec((1, tm, d), row),
                   pl.BlockSpec((1, tm, d), row),
                   pl.BlockSpec((1, tm * CHUNKS, LANES), row),
                   pl.BlockSpec((N_EXPERTS, tm), lambda bi, i: (0, bi * (n // tm) + i))],
        out_shape=[jax.ShapeDtypeStruct((b, n, d), F32),
                   jax.ShapeDtypeStruct((b, n, d), BF16),
                   jax.ShapeDtypeStruct((b, n * CHUNKS, LANES), F32),
                   jax.ShapeDtypeStruct((N_EXPERTS, b * n), F32)],
        compiler_params=_cparams(("arbitrary", "arbitrary")),
        name="outproj",
    )(att, p, p, p, x, mod3, w_pool_bf, b_pool, pool_scale, w_out_bf, g_post, g_pre, w_router_t)


def _beats(a, b, a_first):
    return jnp.where((a >= b) if a_first else (a > b), 1.0, 0.0)


def _route_kernel(lg_ref, bias_ref, w_ref):
    scores = [1.0 / (1.0 + jnp.exp(-lg_ref[e])) for e in range(N_EXPERTS)]
    biased = [scores[e] + bias_ref[e] for e in range(N_EXPERTS)]
    gscore = []
    for g in range(N_GROUPS):
        vals = biased[g * GROUP_SIZE:(g + 1) * GROUP_SIZE]
        m1 = vals[0]
        m2 = jnp.full_like(m1, -jnp.inf)
        for v in vals[1:]:
            m2 = jnp.maximum(m2, jnp.minimum(m1, v))
            m1 = jnp.maximum(m1, v)
        gscore.append(m1 + m2)
    gsel = []
    for g in range(N_GROUPS):
        rank = jnp.zeros_like(gscore[g])
        for g2 in range(N_GROUPS):
            if g2 != g:
                rank = rank + _beats(gscore[g2], gscore[g], g2 < g)
        gsel.append(rank < TOPK_GROUPS)
    vals = [jnp.where(gsel[e // GROUP_SIZE], biased[e], -jnp.inf) for e in range(N_EXPERTS)]
    chosen = [jnp.zeros(vals[0].shape, jnp.bool_) for _ in range(N_EXPERTS)]
    for _ in range(TOP_K):
        top = vals[0]
        for v in vals[1:]:
            top = jnp.maximum(top, v)
        found = jnp.zeros(top.shape, jnp.bool_)
        for e in range(N_EXPERTS):
            hit = (vals[e] == top) & jnp.logical_not(found)
            found = found | hit
            chosen[e] = chosen[e] | hit
            vals[e] = jnp.where(hit, -jnp.inf, vals[e])
    picked = [jnp.where(chosen[e], scores[e], 0.0) for e in range(N_EXPERTS)]
    denom = picked[0]
    for e in range(1, N_EXPERTS):
        denom = denom + picked[e]
    for e in range(N_EXPERTS):
        w_ref[e] = picked[e] / denom * ROUTED_SCALE


def _route(logits3, bias3, rows=SUBLANES):
    e, r, l = logits3.shape
    return pl.pallas_call(
        _route_kernel,
        grid=(r // rows,),
        in_specs=[pl.BlockSpec((e, rows, l), lambda i: (0, i, 0)),
                  pl.BlockSpec((e, 1, l), lambda i: (0, 0, 0))],
        out_specs=pl.BlockSpec((e, rows, l), lambda i: (0, i, 0)),
        out_shape=jax.ShapeDtypeStruct((e, r, l), F32),
        compiler_params=_cparams(("arbitrary",)),
        name="route",
    )(logits3, bias3)


CHUNKS = D_MODEL // LANES
SLAB = SLOT_TILE + 4
ROW_BATCH = 8


def _gather_tile(row_ref, tile, xs_ref, tx_ref):
    for s in range(SLOT_TILE):
        row = pl.multiple_of(row_ref[0, tile, s], CHUNKS)
        tx_ref[pl.ds(s, CHUNKS, stride=SLAB), :] = xs_ref[pl.ds(row, CHUNKS), :]


def _experts_step(j, wslot, row_ref, sw_ref, wg_st, wu_st, wd_st, xs_ref, acc_ref,
                  tx_cur, tx_nxt, ty_cur, ty_prv):
    ntile = row_ref.shape[1]
    _gather_tile(row_ref, jnp.minimum(j + 1, ntile - 1), xs_ref, tx_nxt)
    xb = jnp.concatenate([tx_cur[pl.ds(k * SLAB, SLOT_TILE), :] for k in range(CHUNKS)], axis=1).astype(BF16)
    g = _dot(xb, wg_st[wslot].astype(BF16))
    u = _dot(xb, wu_st[wslot].astype(BF16))
    h = (_silu(g) * u).astype(BF16)
    y = _dot(h, wd_st[wslot].astype(BF16))
    w_rows = jnp.transpose(jnp.broadcast_to(sw_ref[0, pl.ds(j, 1), :], (LANES, SLOT_TILE)))
    for k in range(CHUNKS):
        ty_cur[pl.ds(k * SLAB, SLOT_TILE), :] = y[:, k * LANES:(k + 1) * LANES] * w_rows
    prv = jnp.maximum(j - 1, 0)
    for s0 in range(0, SLOT_TILE, ROW_BATCH):
        new = []
        for s in range(s0, s0 + ROW_BATCH):
            row = pl.multiple_of(row_ref[0, prv, s], CHUNKS)
            new.append((row, acc_ref[pl.ds(row, CHUNKS), :] + ty_prv[pl.ds(s, CHUNKS, stride=SLAB), :]))
        for row, v in new:
            acc_ref[pl.ds(row, CHUNKS), :] = v


def _experts_kernel(be_ref, nt_ref, first_ref, nxt_ref, ws_ref, row_ref, sw_ref, x_hbm,
                    wg_hbm, wu_hbm, wd_hbm, o_hbm,
                    xs_ref, acc_ref, tx0_ref, tx1_ref, ty0_ref, ty1_ref, wg_st, wu_st, wd_st,
                    sem, wsem, *, ntile, n):
    c = pl.program_id(0)
    j = pl.program_id(1)
    t = c * ntile + j
    rows = n * CHUNKS
    wslot = ws_ref[t]

    def weight_copies(e, slot):
        return [pltpu.make_async_copy(src.at[e], dst.at[slot], wsem.at[slot])
                for src, dst in ((wg_hbm, wg_st), (wu_hbm, wu_st), (wd_hbm, wd_st))]

    @pl.when(j == 0)
    def _():
        for wcp in weight_copies(be_ref[t], 0):
            wcp.start()
        cp = pltpu.make_async_copy(x_hbm.at[c], xs_ref.at[pl.ds(0, rows)], sem.at[0])

        @pl.when(c == 0)
        def _():
            cp.start()

        xs_ref[pl.ds(rows, CHUNKS), :] = jnp.zeros((CHUNKS, LANES), F32)
        acc_ref[...] = jnp.zeros_like(acc_ref)
        ty1_ref[...] = jnp.zeros_like(ty1_ref)
        cp.wait()
        _gather_tile(row_ref, 0, xs_ref, tx0_ref)

    @pl.when(first_ref[t] == 1)
    def _():
        for wcp in weight_copies(be_ref[t], wslot):
            wcp.wait()

    @pl.when((first_ref[t] == 1) & (nxt_ref[t] >= 0))
    def _():
        for wcp in weight_copies(nxt_ref[t], 1 - wslot):
            wcp.start()

    live = j <= nt_ref[c]
    args = (j, wslot, row_ref, sw_ref, wg_st, wu_st, wd_st, xs_ref, acc_ref)

    @pl.when(live & (lax.rem(j, 2) == 0))
    def _():
        _experts_step(*args, tx0_ref, tx1_ref, ty0_ref, ty1_ref)

    @pl.when(live & (lax.rem(j, 2) == 1))
    def _():
        _experts_step(*args, tx1_ref, tx0_ref, ty1_ref, ty0_ref)

    @pl.when((j == nt_ref[c]) & (c + 1 < pl.num_programs(0)))
    def _():
        pltpu.make_async_copy(x_hbm.at[c + 1], xs_ref.at[pl.ds(0, rows)], sem.at[0]).start()

    @pl.when(j == ntile - 1)
    def _():
        cp = pltpu.make_async_copy(acc_ref.at[pl.ds(0, rows)], o_hbm.at[c], sem.at[1])
        cp.start()
        cp.wait()


def _experts(block_e, ntiles, first, nxt_e, wslot, slot_row, slot_w, h2t, w_gate, w_up, w_down, ntile, n):
    b = h2t.shape[0]
    d, f = w_gate.shape[1], w_gate.shape[2]
    batch = lambda c, j, *_: (c, 0, 0)
    kern = functools.partial(_experts_kernel, ntile=ntile, n=n)
    slab = pltpu.VMEM((CHUNKS * SLAB, LANES), F32)
    hbm = pl.BlockSpec(memory_space=pl.ANY)
    grid_spec = pltpu.PrefetchScalarGridSpec(
        num_scalar_prefetch=5,
        grid=(b, ntile),
        in_specs=[pl.BlockSpec((1, ntile, SLOT_TILE), batch, memory_space=pltpu.SMEM),
                  pl.BlockSpec((1, ntile, SLOT_TILE), batch),
                  hbm, hbm, hbm, hbm],
        out_specs=hbm,
        scratch_shapes=[pltpu.VMEM(((n + 1) * CHUNKS, LANES), F32),
                        pltpu.VMEM(((n + 1) * CHUNKS, LANES), F32),
                        slab, slab, slab, slab,
                        pltpu.VMEM((2, d, f), F32), pltpu.VMEM((2, d, f), F32), pltpu.VMEM((2, f, d), F32),
                        pltpu.SemaphoreType.DMA((2,)), pltpu.SemaphoreType.DMA((2,))],
    )
    return pl.pallas_call(
        kern,
        grid_spec=grid_spec,
        out_shape=jax.ShapeDtypeStruct((b, n * CHUNKS, LANES), F32),
        compiler_params=_cparams(("arbitrary", "arbitrary")),
        name="experts",
    )(block_e, ntiles, first, nxt_e, wslot, slot_row, slot_w, h2t, w_gate, w_up, w_down)


PLAN_GROUP = 16


def _plan_tables_kernel(w_ref, re_ref, ro_ref, tab_ref, *, rows):
    ne = w_ref.shape[0]
    w2 = w_ref[...].reshape(ne * rows, LANES)
    selb = jnp.where(w2 > 0.0, 1.0, 0.0).astype(BF16)
    i0 = lax.broadcasted_iota(jnp.int32, (LANES, LANES), 0)
    i1 = lax.broadcasted_iota(jnp.int32, (LANES, LANES), 1)
    rowtot = _dot(selb, jnp.ones((LANES, LANES), BF16))
    ridx = lax.broadcasted_iota(jnp.int32, rowtot.shape, 0) & (rows - 1)
    acc = rowtot
    sh = 1
    while sh < rows:
        acc = acc + jnp.where(ridx >= sh, pltpu.roll(acc, sh, 0), 0.0)
        sh *= 2
    re_ref[0] = acc
    ro_ref[0] = acc - rowtot
    incl_t = _dot_nt(jnp.where(i1 <= i0, 1.0, 0.0).astype(BF16), selb)
    eye = jnp.where(i0 == i1, 1.0, 0.0).astype(BF16)
    w_hi = w2.astype(BF16)
    r1 = w2 - w_hi.astype(F32)
    w_mid = r1.astype(BF16)
    w_lo = (r1 - w_mid.astype(F32)).astype(BF16)
    parts = [_dot_nt(eye, p) for p in (w_hi, w_mid, w_lo)]
    for g in range(ne * rows // LANES):
        sl = slice(g * LANES, (g + 1) * LANES)
        tab_ref[0, g, 0:LANES, :] = incl_t[:, sl].astype(BF16)
        for k in range(3):
            tab_ref[0, g, (k + 1) * LANES:(k + 2) * LANES, :] = parts[k][:, sl].astype(BF16)


def _plan_tables(selw3, b):
    ne, r_all, _ = selw3.shape
    rows = r_all // b
    groups = ne * rows // LANES
    kern = functools.partial(_plan_tables_kernel, rows=rows)
    return pl.pallas_call(
        kern,
        grid=(b,),
        in_specs=[pl.BlockSpec((ne, rows, LANES), lambda c: (0, c, 0))],
        out_specs=[pl.BlockSpec((1, ne * rows, LANES), lambda c: (c, 0, 0)),
                   pl.BlockSpec((1, ne * rows, LANES), lambda c: (c, 0, 0)),
                   pl.BlockSpec((1, groups, 4 * LANES, LANES), lambda c: (c, 0, 0, 0))],
        out_shape=[jax.ShapeDtypeStruct((b, ne * rows, LANES), F32),
                   jax.ShapeDtypeStruct((b, ne * rows, LANES), F32),
                   jax.ShapeDtypeStruct((b, groups, 4 * LANES, LANES), BF16)],
        compiler_params=_cparams(("arbitrary",)),
        name="plan_tables",
    )(selw3)


def _plan_slots_kernel(be_ref, q0_ref, nt_ref, re_ref, ro_ref, tab_ref, tok_ref, sw_ref,
                       *, ntile, rows, n):
    c = pl.program_id(0)
    tok_ref[...] = jnp.full(tok_ref.shape, n * CHUNKS, jnp.int32)
    sw_ref[...] = jnp.zeros(sw_ref.shape, F32)
    lane = lax.broadcasted_iota(jnp.int32, (1, SLOT_TILE), 1).astype(F32)
    sub_r = lax.broadcasted_iota(jnp.int32, (rows, SLOT_TILE), 0).astype(F32)
    sub_l = lax.broadcasted_iota(jnp.int32, (LANES, SLOT_TILE), 0).astype(F32)
    reps = SLOT_TILE // LANES

    def tile_body(j):
        e = be_ref[c * ntile + j]
        q = q0_ref[c * ntile + j].astype(F32) + lane
        base = pl.multiple_of(e * rows, rows)
        row_end = jnp.concatenate([re_ref[0, pl.ds(base, rows), :]] * reps, axis=1)
        row_off = jnp.concatenate([ro_ref[0, pl.ds(base, rows), :]] * reps, axis=1)
        r = jnp.sum(jnp.where(row_end <= q, 1.0, 0.0), axis=0, keepdims=True)
        row_start = jnp.sum(jnp.where(sub_r == r, row_off, 0.0), axis=0, keepdims=True)
        per_group = LANES // rows
        g = lax.shift_right_logical(e, per_group.bit_length() - 1)
        col = r + ((e & (per_group - 1)) * rows).astype(F32)
        onehot = jnp.where(sub_l == col, 1.0, 0.0).astype(BF16)
        picked = _dot(tab_ref[0, g], onehot)
        incl = picked[0:LANES]
        lane_idx = jnp.sum(jnp.where(incl <= q - row_start, 1.0, 0.0), axis=0, keepdims=True)
        wrow = (picked[LANES:2 * LANES] + picked[2 * LANES:3 * LANES]) + picked[3 * LANES:]
        wsel = jnp.sum(jnp.where(sub_l == lane_idx, wrow, 0.0), axis=0, keepdims=True)
        valid = (q < row_end[rows - 1:rows, :]) & (j < nt_ref[c])
        tok = jnp.where(valid, r * LANES + lane_idx, float(n)).astype(jnp.int32)
        tok_ref[0, pl.ds(j, 1), :] = tok * CHUNKS
        sw_ref[0, pl.ds(j, 1), :] = jnp.where(valid, wsel, 0.0)

    def tile_group(i, carry):
        for k in range(PLAN_GROUP):
            tile_body(PLAN_GROUP * i + k)
        return carry

    lax.fori_loop(0, (nt_ref[c] + PLAN_GROUP - 1) // PLAN_GROUP, tile_group, 0)


def _plan_slots(block_e, tile_q0, ntiles, row_end, row_off, tables, ntile, n):
    b, er, _ = row_end.shape
    rows = n // LANES
    per_group = LANES // rows
    assert rows * per_group == LANES and per_group & (per_group - 1) == 0 and ntile % PLAN_GROUP == 0
    groups = tables.shape[1]
    kern = functools.partial(_plan_slots_kernel, ntile=ntile, rows=rows, n=n)
    grid_spec = pltpu.PrefetchScalarGridSpec(
        num_scalar_prefetch=3,
        grid=(b,),
        in_specs=[pl.BlockSpec((1, er, LANES), lambda c, *_: (c, 0, 0)),
                  pl.BlockSpec((1, er, LANES), lambda c, *_: (c, 0, 0)),
                  pl.BlockSpec((1, groups, 4 * LANES, LANES), lambda c, *_: (c, 0, 0, 0))],
        out_specs=[pl.BlockSpec((1, ntile, SLOT_TILE), lambda c, *_: (c, 0, 0)),
                   pl.BlockSpec((1, ntile, SLOT_TILE), lambda c, *_: (c, 0, 0))],
    )
    return pl.pallas_call(
        kern,
        grid_spec=grid_spec,
        out_shape=[jax.ShapeDtypeStruct((b, ntile, SLOT_TILE), jnp.int32),
                   jax.ShapeDtypeStruct((b, ntile, SLOT_TILE), F32)],
        compiler_params=_cparams(("arbitrary",)),
        name="plan_slots",
    )(block_e, tile_q0, ntiles, row_end, row_off, tables)


def _tile_meta(row_end, b, n, ntile):
    rows = n // LANES
    cnt = row_end.reshape(b, N_EXPERTS, rows, LANES)[:, :, rows - 1, 0].astype(jnp.int32)
    nt_e = (cnt + SLOT_TILE - 1) // SLOT_TILE
    tend = jnp.cumsum(nt_e, axis=1)
    tstart = tend - nt_e
    ntiles = tend[:, -1]
    tile_id = jnp.arange(ntile, dtype=jnp.int32)
    tid = jnp.minimum(tile_id[None, :], ntiles[:, None] - 1)
    be = jnp.minimum(jnp.sum(tend[:, None, :] <= tid[:, :, None], axis=2).astype(jnp.int32), N_EXPERTS - 1)
    eid = jnp.arange(N_EXPERTS, dtype=jnp.int32)
    is_be = be[:, :, None] == eid[None, None, :]
    pick = lambda per_expert: jnp.sum(jnp.where(is_be, per_expert[:, None, :], 0), axis=2)
    q0 = (tid - pick(tstart)) * SLOT_TILE
    valid = tile_id[None, :] < ntiles[:, None]
    prev_be = jnp.concatenate([jnp.full((b, 1), -1, jnp.int32), be[:, :-1]], axis=1)
    first = (valid & (be != prev_be)).astype(jnp.int32)
    wslot = (jnp.cumsum(first, axis=1) - 1) % 2
    later = (nt_e > 0)[:, None, :] & (eid[None, None, :] > eid[None, :, None])
    nxt = jnp.min(jnp.where(later, eid[None, None, :], N_EXPERTS), axis=2)
    nxt_e = pick(jnp.where(nxt < N_EXPERTS, nxt, -1))
    flat = lambda a: a.reshape(-1).astype(jnp.int32)
    return flat(be), flat(q0), ntiles.astype(jnp.int32), flat(first), flat(nxt_e), flat(wslot)


def _dispatch_plan(selw3, b, n, ntile):
    row_end, row_off, tables = _plan_tables(selw3, b)
    block_e, tile_q0, ntiles, first, nxt_e, wslot = _tile_meta(row_end, b, n, ntile)
    slot_tok, slot_w = _plan_slots(block_e, tile_q0, ntiles, row_end, row_off, tables, ntile, n)
    return block_e, ntiles, first, nxt_e, wslot, slot_tok, slot_w


def _final_kernel(h2_ref, r_ref, x1_ref, mod_ref, wg_ref, wu_ref, wd_ref, g_ref, o_ref):
    hb = h2_ref[0]
    tm = hb.shape[0]
    sh = _dot((_silu(_dot(hb, wg_ref[...])) * _dot(hb, wu_ref[...])).astype(BF16), wd_ref[...])
    routed = jnp.concatenate([r_ref[0, pl.ds(k, tm, stride=CHUNKS), :] for k in range(CHUNKS)], axis=1)
    moe = routed + sh
    o_ref[0] = x1_ref[0] + mod_ref[0, 5:6, :] * _rms(moe, g_ref[...])


def _final(h2, routed, x1, mod3, wsg_bf, wsu_bf, wsd_bf, g_post, tm=512):
    b, n, d = x1.shape
    f = wsg_bf.shape[1]
    row = lambda bi, i: (bi, i, 0)
    vec = lambda bi, i: (0, 0)
    return pl.pallas_call(
        _final_kernel,
        grid=(b, n // tm),
        in_specs=[pl.BlockSpec((1, tm, d), row),
                  pl.BlockSpec((1, tm * CHUNKS, LANES), row),
                  pl.BlockSpec((1, tm, d), row),
                  pl.BlockSpec((1, 6, d), lambda bi, i: (bi, 0, 0)),
                  pl.BlockSpec((d, f), vec),
                  pl.BlockSpec((d, f), vec),
                  pl.BlockSpec((f, d), vec),
                  pl.BlockSpec((1, d), vec)],
        out_specs=pl.BlockSpec((1, tm, d), row),
        out_shape=jax.ShapeDtypeStruct((b, n, d), F32),
        compiler_params=_cparams(("arbitrary", "arbitrary")),
        name="final",
    )(h2, routed, x1, mod3, wsg_bf, wsu_bf, wsd_bf, g_post)


def _rope_tables(n):
    t = np.arange(n)
    row = (t // GRID_W).astype(np.float32)
    col = (t % GRID_W).astype(np.float32)
    freqs = np.float32(ROPE_THETA) ** (-np.arange(0, AXIS_DIM, 2, dtype=np.float32) / np.float32(AXIS_DIM))
    ar = row[:, None] * freqs
    ac = col[:, None] * freqs
    zeros = np.zeros_like(ar)
    cos64 = np.concatenate([np.cos(ar), np.cos(ar), np.cos(ac), np.cos(ac)], axis=1)
    sa64 = np.concatenate([-np.sin(ar), zeros, -np.sin(ac), zeros], axis=1)
    sb64 = np.concatenate([zeros, np.sin(ar), zeros, np.sin(ac)], axis=1)
    two = lambda a: jnp.asarray(np.concatenate([a, a], axis=1).astype(np.float32))
    return two(cos64), two(sa64), two(sb64)


def kernel(x, c, ctx, c_ctx, w_ada, b_ada, g_pre_mix, g_post_mix, g_pre_ffn, g_post_ffn, w_in, lambda_q1, lambda_k1, lambda_q2, lambda_k2, g_subln, w_pool, b_pool, pool_scale, w_out, w_router, router_bias, w_e_gate, w_e_up, w_e_down, w_sh_gate, w_sh_up, w_sh_down):
    b, n, d = x.shape
    l = 0
    mod_rows = SUBLANES
    cin = jnp.concatenate([c, c_ctx[None, :], jnp.zeros((mod_rows - b - 1, d), F32)], axis=0)
    mod3 = _ada(cin, w_ada[l], b_ada[l]).reshape(mod_rows, 6, d)

    w_in_bf = w_in[l].astype(BF16)
    cos, sa, sb = _rope_tables(n)
    q, k, vt, p = _inproj(x, mod3, g_pre_mix[l][None, :], w_in_bf, cos * (QK_DIM ** -0.5 * LOG2E), cos, sa, sb)
    kc, vct = _ctxkv(ctx, mod3, g_pre_mix[l][None, :], w_in_bf, b)

    lamv = jnp.stack([lambda_q1[l], lambda_k1[l], lambda_q2[l], lambda_k2[l]], axis=0)
    att = _attn(q, k, kc, vt, vct, lamv, g_subln[l][:, None])

    x1, h2, h2t, logits_t = _outproj(
        att, p, x, mod3, w_pool[l].astype(BF16), b_pool[l][None, :], pool_scale[l][None, :],
        w_out[l].astype(BF16), g_post_mix[l][None, :], g_pre_ffn[l][None, :], w_router[l].T)

    t = b * n
    selw3 = _route(logits_t.reshape(N_EXPERTS, t // LANES, LANES),
                   jnp.broadcast_to(router_bias[l][:, None, None], (N_EXPERTS, 1, LANES)))

    ntile = (n * TOP_K) // SLOT_TILE + N_EXPERTS
    block_e, ntiles, first, nxt_e, wslot, slot_row, slot_w = _dispatch_plan(selw3, b, n, ntile)
    routed = _experts(block_e, ntiles, first, nxt_e, wslot, slot_row, slot_w, h2t,
                      w_e_gate[l], w_e_up[l], w_e_down[l], ntile, n)

    return _final(h2, routed, x1, mod3, w_sh_gate[l].astype(BF16), w_sh_up[l].astype(BF16),
                  w_sh_down[l].astype(BF16), g_post_ffn[l][None, :])
```
